```python
import jax, jax.numpy as jnp
from jax import lax
import numpy as np

D_MODEL = 2048
BATCH = 1
SEQ = 8192
DEPTH = 2

CHUNK = 64
EPS = 1e-6
GDN_HEADS = 8
GDN_DK = 128
GDN_DV = 128
GDN_QK_W = GDN_HEADS * GDN_DK
GDN_V_W = GDN_HEADS * GDN_DV
SHORT_CONV = 4
CONF_CH = 1024
CONF_KERNEL = 31
SSM_D_INNER = 2 * D_MODEL
SSM_HEADDIM = 64
SSM_HEADS = SSM_D_INNER // SSM_HEADDIM
SSM_STATE = 128
SSM_GROUPS = 8
SSM_CONV = 4
MOE_GROUPS = 4
MOE_EXPERTS_PER_GROUP = 8
MOE_EXPERTS = MOE_GROUPS * MOE_EXPERTS_PER_GROUP
MOE_TOP_K = 2
MOE_D_FF = 512
MOE_BLOCK = 128

IN0_SPLITS = [GDN_QK_W, 2 * GDN_QK_W, 2 * GDN_QK_W + GDN_V_W, 2 * GDN_QK_W + 2 * GDN_V_W,
              2 * GDN_QK_W + 2 * GDN_V_W + GDN_HEADS, 2 * GDN_QK_W + 2 * GDN_V_W + 2 * GDN_HEADS]
IN0_WIDTH = IN0_SPLITS[-1] + 2 * CONF_CH
MIX0_WIDTH = GDN_V_W + CONF_CH
SSM_XBC = SSM_D_INNER + 2 * SSM_GROUPS * SSM_STATE
IN1_SPLITS = [SSM_D_INNER, SSM_D_INNER + SSM_XBC]
IN1_WIDTH = SSM_D_INNER + SSM_XBC + SSM_HEADS
N_EVEN = (DEPTH + 1) // 2
N_ODD = DEPTH // 2

kernel_name = "hybrid_gdn_conformer_ssd_hmoe_adaln"


def _rmsnorm(x, g):
    xf = x.astype(jnp.float32)
    y = xf * lax.rsqrt(jnp.mean(xf * xf, axis=-1, keepdims=True) + EPS)
    return (y * g.astype(jnp.float32)).astype(x.dtype)


def _layernorm(x, g, b):
    xf = x.astype(jnp.float32)
    mu = jnp.mean(xf, axis=-1, keepdims=True)
    xc = xf - mu
    var = jnp.mean(xc * xc, axis=-1, keepdims=True)
    return (xc * lax.rsqrt(var + EPS) * g.astype(jnp.float32) + b.astype(jnp.float32)).astype(x.dtype)


def _l2norm(x):
    xf = x.astype(jnp.float32)
    return (xf * lax.rsqrt(jnp.sum(xf * xf, axis=-1, keepdims=True) + EPS)).astype(x.dtype)


def _causal_depthwise_conv(x, w):
    k = w.shape[0]
    return lax.conv_general_dilated(
        x, w[:, None, :].astype(x.dtype), window_strides=(1,), padding=((k - 1, 0),),
        dimension_numbers=('NWC', 'WIO', 'NWC'), feature_group_count=x.shape[-1])


def _gated_delta_rule(q, k, v, g, beta):
    out_dtype = v.dtype
    bsz, t_len, h, dk = q.shape
    dv = v.shape[-1]
    nc = t_len // CHUNK

    def chunks(t):
        t = t.astype(jnp.float32).reshape(bsz, nc, CHUNK, h, *t.shape[3:])
        return jnp.moveaxis(t, 3, 2)

    q = chunks(q) * (dk ** -0.5)
    k, v, g, beta = chunks(k), chunks(v), chunks(g), chunks(beta)
    g_cs = jnp.cumsum(g, axis=-1)
    tri_incl = jnp.tril(jnp.ones((CHUNK, CHUNK), bool))
    tri_strict = jnp.tril(jnp.ones((CHUNK, CHUNK), bool), -1)
    decay = jnp.exp(jnp.where(tri_incl, g_cs[..., :, None] - g_cs[..., None, :], -jnp.inf))
    kb = k * beta[..., None]
    a_mat = jnp.where(tri_strict, jnp.einsum('bnhik,bnhjk->bnhij', kb, k) * decay, 0.0)
    eye = jnp.eye(CHUNK, dtype=jnp.float32)
    rhs = jnp.concatenate([v * beta[..., None], kb * jnp.exp(g_cs)[..., None]], axis=-1)
    sol = lax.linalg.triangular_solve(eye + a_mat, rhs, left_side=True, lower=True)
    u_base, k_cumdecay = sol[..., :dv], sol[..., dv:]
    qk = jnp.einsum('bnhik,bnhjk->bnhij', q, k) * decay
    q_dec = q * jnp.exp(g_cs)[..., None]
    k_dec = k * jnp.exp(g_cs[..., -1:] - g_cs)[..., None]
    chunk_decay = jnp.exp(g_cs[..., -1])

    def step(state, inp):
        qk_c, qd_c, kd_c, ub_c, kcd_c, cd_c = inp
        u = ub_c - jnp.einsum('bhik,bhkv->bhiv', kcd_c, state)
        o = jnp.einsum('bhik,bhkv->bhiv', qd_c, state) + jnp.einsum('bhij,bhjv->bhiv', qk_c, u)
        state = state * cd_c[..., None, None] + jnp.einsum('bhjk,bhjv->bhkv', kd_c, u)
        return state, o

    xs = tuple(jnp.moveaxis(t, 1, 0) for t in (qk, q_dec, k_dec, u_base, k_cumdecay, chunk_decay))
    _, o = lax.scan(step, jnp.zeros((bsz, h, dk, dv), jnp.float32), xs)
    o = jnp.transpose(o, (1, 0, 3, 2, 4)).reshape(bsz, t_len, h, dv)
    return o.astype(out_dtype)


def _ssd(x, dt, a, b_mat, c_mat, d_skip):
    out_dtype = x.dtype
    bsz, t_len, h, p = x.shape
    grp, n = b_mat.shape[2], b_mat.shape[3]
    hg = h // grp
    nc = t_len // CHUNK
    xf = x.astype(jnp.float32).reshape(bsz, nc, CHUNK, grp, hg, p)
    dt = dt.astype(jnp.float32).reshape(bsz, nc, CHUNK, grp, hg)
    bm = b_mat.astype(jnp.float32).reshape(bsz, nc, CHUNK, grp, n)
    cm = c_mat.astype(jnp.float32).reshape(bsz, nc, CHUNK, grp, n)
    a_cs = jnp.cumsum(dt * a.astype(jnp.float32).reshape(grp, hg), axis=2)
    tri = jnp.tril(jnp.ones((CHUNK, CHUNK), bool))
    seg = a_cs[:, :, :, None] - a_cs[:, :, None, :]
    l_mat = jnp.exp(jnp.where(tri[:, :, None, None], seg, -jnp.inf))
    xdt = xf * dt[..., None]
    cb = jnp.einsum('bcigs,bcjgs->bcijg', cm, bm)
    y_diag = jnp.einsum('bcijgh,bcjghp->bcighp', cb[..., None] * l_mat, xdt)
    decay_to_end = jnp.exp(a_cs[:, :, -1:] - a_cs)
    chunk_states = jnp.einsum('bcjgs,bcjghp->bcghps', bm, xdt * decay_to_end[..., None])
    chunk_decay = jnp.exp(a_cs[:, :, -1])
    decay_from_start = jnp.exp(a_cs)

    def step(state, inp):
        st_c, dec_c, c_c, dfs_c = inp
        y_off = jnp.einsum('bigs,bghps->bighp', c_c, state) * dfs_c[..., None]
        state = state * dec_c[..., None, None] + st_c
        return state, y_off

    xs = tuple(jnp.moveaxis(t, 1, 0) for t in (chunk_states, chunk_decay, cm, decay_from_start))
    _, y_off = lax.scan(step, jnp.zeros((bsz, grp, hg, p, n), jnp.float32), xs)
    y = y_diag + jnp.moveaxis(y_off, 0, 1) + d_skip.astype(jnp.float32).reshape(grp, hg)[..., None] * xf
    return y.reshape(bsz, t_len, h, p).astype(out_dtype)


def _even_mixer(h, w_in, conv_qkv, a_log, dt_bias, head_norm_g, conf_dw, conf_dw_b, conf_ln_g, conf_ln_b, w_out):
    bsz, t_len, _ = h.shape
    proj = h @ w_in
    qkv, z, b_logit, a_logit, glu = jnp.split(proj, IN0_SPLITS[2:], axis=-1)
    qkv = jax.nn.silu(_causal_depthwise_conv(qkv, conv_qkv))
    q, k, v = jnp.split(qkv, [GDN_QK_W, 2 * GDN_QK_W], axis=-1)
    q = _l2norm(q.reshape(bsz, t_len, GDN_HEADS, GDN_DK))
    k = _l2norm(k.reshape(bsz, t_len, GDN_HEADS, GDN_DK))
    v = v.reshape(bsz, t_len, GDN_HEADS, GDN_DV)
    beta = jax.nn.sigmoid(b_logit.astype(jnp.float32))
    g = -jnp.exp(a_log.astype(jnp.float32)) * jax.nn.softplus(a_logit.astype(jnp.float32) + dt_bias.astype(jnp.float32))
    o = _gated_delta_rule(q, k, v, g, beta)
    o = _rmsnorm(o, head_norm_g) * jax.nn.silu(z.reshape(bsz, t_len, GDN_HEADS, GDN_DV))
    a_out = o.reshape(bsz, t_len, GDN_V_W)
    u = glu[..., :CONF_CH] * jax.nn.sigmoid(glu[..., CONF_CH:])
    u = _causal_depthwise_conv(u, conf_dw) + conf_dw_b
    b_out = jax.nn.silu(_layernorm(u, conf_ln_g, conf_ln_b))
    return jnp.concatenate([a_out, b_out], axis=-1) @ w_out


def _odd_mixer(h, w_in, conv_w, conv_b, dt_bias, a_log, d_skip, norm_g, w_out):
    bsz, t_len, _ = h.shape
    proj = h @ w_in
    z, xbc, dt_raw = jnp.split(proj, IN1_SPLITS, axis=-1)
    xbc = jax.nn.silu(_causal_depthwise_conv(xbc, conv_w) + conv_b)
    xs, bm, cm = jnp.split(xbc, [SSM_D_INNER, SSM_D_INNER + SSM_GROUPS * SSM_STATE], axis=-1)
    dt = jax.nn.softplus(dt_raw.astype(jnp.float32) + dt_bias.astype(jnp.float32))
    a = -jnp.exp(a_log.astype(jnp.float32))
    y = _ssd(xs.reshape(bsz, t_len, SSM_HEADS, SSM_HEADDIM), dt, a,
             bm.reshape(bsz, t_len, SSM_GROUPS, SSM_STATE), cm.reshape(bsz, t_len, SSM_GROUPS, SSM_STATE), d_skip)
    y = _rmsnorm(y.reshape(bsz, t_len, SSM_D_INNER) * jax.nn.silu(z), norm_g)
    return y @ w_out


def _hier_moe(h, w_group, b_group, w_expert, b_expert, w1, w3, w2):
    bsz, t_len, d = h.shape
    m = bsz * t_len
    hf = h.reshape(m, d)
    grp_prob = jax.nn.softmax((hf @ w_group + b_group).astype(jnp.float32), axis=-1)
    grp_p, grp_idx = lax.top_k(grp_prob, 1)
    exp_logits = (hf @ w_expert + b_expert).astype(jnp.float32).reshape(m, MOE_GROUPS, MOE_EXPERTS_PER_GROUP)
    sel = jnp.broadcast_to(grp_idx[:, :, None], (m, 1, MOE_EXPERTS_PER_GROUP))
    in_grp = jnp.take_along_axis(exp_logits, sel, axis=1)[:, 0]
    top_v, top_i = lax.top_k(in_grp, MOE_TOP_K)
    gate = jax.nn.softmax(top_v, axis=-1) * grp_p
    expert_id = grp_idx * MOE_EXPERTS_PER_GROUP + top_i
    n_assign = m * MOE_TOP_K
    flat_e = expert_id.reshape(n_assign)
    flat_gate = gate.reshape(n_assign)
    flat_tok = jnp.repeat(jnp.arange(m, dtype=jnp.int32), MOE_TOP_K)
    order = jnp.argsort(flat_e)
    e_sorted = flat_e[order]
    counts = jnp.bincount(flat_e, length=MOE_EXPERTS)
    padded = (counts + MOE_BLOCK - 1) // MOE_BLOCK * MOE_BLOCK
    pad_end = jnp.cumsum(padded)
    pad_start = pad_end - padded
    raw_start = jnp.cumsum(counts) - counts
    dest = pad_start[e_sorted] + jnp.arange(n_assign) - raw_start[e_sorted]
    n_rows = (n_assign + MOE_EXPERTS * (MOE_BLOCK - 1) + MOE_BLOCK - 1) // MOE_BLOCK * MOE_BLOCK
    n_blocks = n_rows // MOE_BLOCK
    tok_rows = jnp.zeros((n_rows,), jnp.int32).at[dest].set(flat_tok[order])
    gate_rows = jnp.zeros((n_rows,), jnp.float32).at[dest].set(flat_gate[order])
    x_rows = hf[tok_rows].reshape(n_blocks, MOE_BLOCK, d)
    block_expert = jnp.minimum(jnp.searchsorted(pad_end, jnp.arange(n_blocks) * MOE_BLOCK, side='right'), MOE_EXPERTS - 1)

    def expert_block(args):
        xb, e = args
        return (jax.nn.silu(xb @ w1[e]) * (xb @ w3[e])) @ w2[e]

    y_rows = lax.map(expert_block, (x_rows, block_expert)).reshape(n_rows, d)
    y_rows = y_rows * gate_rows[:, None].astype(y_rows.dtype)
    out = jax.ops.segment_sum(y_rows, tok_rows, num_segments=m)
    return out.reshape(bsz, t_len, d)


def _adaln(x, g, shift, scale):
    return _rmsnorm(x, g) * (1 + scale[:, None, :]) + shift[:, None, :]


def setup_inputs(seed: int = 0) -> dict:
    key = jax.random.key(seed)
    ks = jax.random.split(key, 32)
    f32 = jnp.float32
    d = D_MODEL

    def nrm(k, shape, scale):
        return jax.random.normal(k, shape, f32) * scale

    def gain(k, shape):
        return 1.0 + 0.02 * jax.random.normal(k, shape, f32)

    def a_log_init(k, shape):
        return jnp.log(jax.random.uniform(k, shape, f32, 1.0, 16.0))

    def dt_bias_init(k, shape):
        dt = jnp.exp(jax.random.uniform(k, shape, f32, math_log(1e-3), math_log(1e-1)))
        return dt + jnp.log(-jnp.expm1(-dt))

    return {
        'x': nrm(ks[0], (BATCH, SEQ, d), 1.0),
        'c': nrm(ks[1], (BATCH, d), 1.0),
        'w_mod': nrm(ks[2], (2 * DEPTH, d, 3 * d), 0.5 * d ** -0.5),
        'b_mod': nrm(ks[3], (2 * DEPTH, 3 * d), 0.01),
        'norm_g': gain(ks[4], (DEPTH, 2, d)),
        'final_norm_g': gain(ks[5], (d,)),
        'e_w_in': nrm(ks[6], (N_EVEN, d, IN0_WIDTH), d ** -0.5),
        'e_conv_qkv': nrm(ks[7], (N_EVEN, SHORT_CONV, 2 * GDN_QK_W + GDN_V_W), SHORT_CONV ** -0.5),
        'e_a_log': a_log_init(ks[8], (N_EVEN, GDN_HEADS)),
        'e_dt_bias': dt_bias_init(ks[9], (N_EVEN, GDN_HEADS)),
        'e_head_norm_g': gain(ks[10], (N_EVEN, GDN_DV)),
        'e_conf_dw': nrm(ks[11], (N_EVEN, CONF_KERNEL, CONF_CH), CONF_KERNEL ** -0.5),
        'e_conf_dw_b': nrm(ks[12], (N_EVEN, CONF_CH), 0.01),
        'e_conf_ln_g': gain(ks[13], (N_EVEN, CONF_CH)),
        'e_conf_ln_b': nrm(ks[14], (N_EVEN, CONF_CH), 0.01),
        'e_w_out': nrm(ks[15], (N_EVEN, MIX0_WIDTH, d), MIX0_WIDTH ** -0.5),
        'o_w_in': nrm(ks[16], (N_ODD, d, IN1_WIDTH), d ** -0.5),
        'o_conv_w': nrm(ks[17], (N_ODD, SSM_CONV, SSM_XBC), SSM_CONV ** -0.5),
        'o_conv_b': nrm(ks[18], (N_ODD, SSM_XBC), 0.01),
        'o_dt_bias': dt_bias_init(ks[19], (N_ODD, SSM_HEADS)),
        'o_a_log': a_log_init(ks[20], (N_ODD, SSM_HEADS)),
        'o_d_skip': gain(ks[21], (N_ODD, SSM_HEADS)),
        'o_norm_g': gain(ks[22], (N_ODD, SSM_D_INNER)),
        'o_w_out': nrm(ks[23], (N_ODD, SSM_D_INNER, d), SSM_D_INNER ** -0.5),
        'moe_w_group': nrm(ks[24], (DEPTH, d, MOE_GROUPS), d ** -0.5),
        'moe_b_group': nrm(ks[25], (DEPTH, MOE_GROUPS), 0.01),
        'moe_w_expert': nrm(ks[26], (DEPTH, d, MOE_EXPERTS), d ** -0.5),
        'moe_b_expert': nrm(ks[27], (DEPTH, MOE_EXPERTS), 0.01),
        'moe_w1': nrm(ks[28], (DEPTH, MOE_EXPERTS, d, MOE_D_FF), d ** -0.5),
        'moe_w3': nrm(ks[29], (DEPTH, MOE_EXPERTS, d, MOE_D_FF), d ** -0.5),
        'moe_w2': nrm(ks[30], (DEPTH, MOE_EXPERTS, MOE_D_FF, d), MOE_D_FF ** -0.5),
    }


def math_log(v):
    return float(np.log(v))


def reference(x, c, w_mod, b_mod, norm_g, final_norm_g,
              e_w_in, e_conv_qkv, e_a_log, e_dt_bias, e_head_norm_g, e_conf_dw, e_conf_dw_b,
              e_conf_ln_g, e_conf_ln_b, e_w_out,
              o_w_in, o_conv_w, o_conv_b, o_dt_bias, o_a_log, o_d_skip, o_norm_g, o_w_out,
              moe_w_group, moe_b_group, moe_w_expert, moe_b_expert, moe_w1, moe_w3, moe_w2):
    mod = jnp.einsum('bd,lde->lbe', jax.nn.silu(c), w_mod) + b_mod[:, None, :]
    for layer in range(DEPTH):
        li = layer // 2
        shift, scale, gate = jnp.split(mod[2 * layer], 3, axis=-1)
        h = _adaln(x, norm_g[layer, 0], shift, scale)
        if layer % 2 == 0:
            mix = _even_mixer(h, e_w_in[li], e_conv_qkv[li], e_a_log[li], e_dt_bias[li], e_head_norm_g[li],
                              e_conf_dw[li], e_conf_dw_b[li], e_conf_ln_g[li], e_conf_ln_b[li], e_w_out[li])
        else:
            mix = _odd_mixer(h, o_w_in[li], o_conv_w[li], o_conv_b[li], o_dt_bias[li], o_a_log[li],
                             o_d_skip[li], o_norm_g[li], o_w_out[li])
        x = x + gate[:, None, :] * mix
        shift, scale, gate = jnp.split(mod[2 * layer + 1], 3, axis=-1)
        h = _adaln(x, norm_g[layer, 1], shift, scale)
        ffn = _hier_moe(h, moe_w_group[layer], moe_b_group[layer], moe_w_expert[layer], moe_b_expert[layer],
                        moe_w1[layer], moe_w3[layer], moe_w2[layer])
        x = x + gate[:, None, :] * ffn
    return _rmsnorm(x, final_norm_g)
```

```python
import functools

import jax
import jax.numpy as jnp
from jax import lax
from jax.experimental import pallas as pl
from jax.experimental.pallas import tpu as pltpu

F32 = jnp.float32
BF16 = jnp.bfloat16
I32 = jnp.int32
U32 = jnp.uint32

EPS = 1e-6
LANES = 128
CHUNK = 64
CHUNK_LOG2 = 6
GDN_HEADS = 8
HEAD_W = 128
CONF_CH = 1024
CONF_K = 31
SSM_GROUPS = 8
SSM_GROUP_W = 512
SSM_HEADDIM = 64
SSM_STATE = 128
N_EXPERTS = 32
EXPERTS_PER_GROUP = 8
N_GROUPS = 4
MOE_ROWS = 256
TOKEN_TILE_ROWS = 8
Y_TILE_ROWS = 16
VMEM_LIMIT = 56 * 1024 * 1024

_HI = lax.Precision.HIGHEST


def _cparams(sem):
    return pltpu.CompilerParams(dimension_semantics=sem, vmem_limit_bytes=VMEM_LIMIT)


def _dot(a, b, precision=None):
    return jnp.dot(a, b, precision=precision, preferred_element_type=F32)


def _dot_nt(a, b, precision=None):
    return lax.dot_general(a, b, (((1,), (1,)), ((), ())), precision=precision, preferred_element_type=F32)


def _dot_tn(a, b, precision=None):
    return lax.dot_general(a, b, (((0,), (0,)), ((), ())), precision=precision, preferred_element_type=F32)


def _silu(x):
    return x * jax.nn.sigmoid(x)


def _softplus(x):
    return jnp.maximum(x, 0.0) + jnp.log1p(jnp.exp(-jnp.abs(x)))


def _iota(shape, dim):
    return lax.broadcasted_iota(I32, shape, dim)


def _mod_body(c_ref, w_ref, b_ref, o_ref):
    c = c_ref[...]
    o_ref[...] = jnp.sum(w_ref[...] * _silu(c), axis=0, keepdims=True) + b_ref[...]


def _modulation(c, w_mod, b_mod):
    nl, d, n = w_mod.shape
    tn = 768
    return pl.pallas_call(
        _mod_body,
        grid=(nl, n // tn),
        in_specs=[
            pl.BlockSpec((d, 1), lambda l, j: (0, 0)),
            pl.BlockSpec((None, d, tn), lambda l, j: (l, 0, j)),
            pl.BlockSpec((None, 1, tn), lambda l, j: (l, 0, j)),
        ],
        out_specs=pl.BlockSpec((None, 1, tn), lambda l, j: (l, 0, j)),
        out_shape=jax.ShapeDtypeStruct((nl, 1, n), F32),
        compiler_params=_cparams(("arbitrary", "arbitrary")),
        name="adaln_mod",
    )(c.reshape(d, 1), w_mod, b_mod.reshape(nl, 1, n))


def _adaln_rows(x, g, scale, shift):
    r = lax.rsqrt(jnp.mean(x * x, axis=-1, keepdims=True) + EPS)
    return x * r * (g * (1.0 + scale)) + shift


def _norm_mm_body(x_ref, g_ref, sc_ref, sh_ref, w_ref, o_ref, h_ref):
    @pl.when(pl.program_id(1) == 0)
    def _():
        h_ref[...] = _adaln_rows(x_ref[...], g_ref[...], sc_ref[...], sh_ref[...]).astype(BF16)

    o_ref[...] = _dot(h_ref[...], w_ref[...])


def _norm_matmul(x, g, scale, shift, w, tm, tn):
    t, d = x.shape
    n = w.shape[1]
    vec = pl.BlockSpec((1, d), lambda i, j: (0, 0))
    return pl.pallas_call(
        _norm_mm_body,
        grid=(t // tm, n // tn),
        in_specs=[pl.BlockSpec((tm, d), lambda i, j: (i, 0)), vec, vec, vec,
                  pl.BlockSpec((d, tn), lambda i, j: (0, j))],
        out_specs=pl.BlockSpec((tm, tn), lambda i, j: (i, j)),
        out_shape=jax.ShapeDtypeStruct((t, n), F32),
        scratch_shapes=[pltpu.VMEM((tm, d), BF16)],
        compiler_params=_cparams(("arbitrary", "arbitrary")),
        name="adaln_in_proj",
    )(x, g, scale, shift, w)


def _causal_conv(halo, x, w, taps):
    hr = halo.shape[0]
    n = x.shape[0]
    xe = jnp.concatenate([halo, x], axis=0)
    rolled = {0: xe}
    acc = None
    for j in range(taps):
        s = taps - 1 - j
        a, b = divmod(s, 8)
        if b not in rolled:
            rolled[b] = pltpu.roll(xe, b, axis=0)
        term = rolled[b][hr - 8 * a:hr - 8 * a + n] * w[j:j + 1]
        acc = term if acc is None else acc + term
    return acc


def _unit_lower_inverse(a_mat, row, col):
    n = a_mat.shape[0]
    eye = (row == col).astype(F32)
    level = ((row >> 1) == (col >> 1)) & (col < row)
    inv = eye - jnp.where(level, a_mat, 0.0)
    k = 1
    while (1 << k) < n:
        level = ((row >> (k + 1)) == (col >> (k + 1))) & (((row >> k) & 1) == 1) & (((col >> k) & 1) == 0)
        m = jnp.where(level, a_mat, 0.0)
        inv = inv - _dot(_dot(inv, m, _HI), inv, _HI)
        k += 1
    return inv


def _gdn_body(q_ref, k_ref, v_ref, qh_ref, kh_ref, vh_ref, z_ref, ba_ref, cwq_ref, cwk_ref, cwv_ref,
              par_ref, hg_ref, o_ref, s_ref, *, hb, rows):
    hblk = pl.program_id(0)
    ir = pl.program_id(1)

    @pl.when(ir == 0)
    def _():
        s_ref[...] = jnp.zeros_like(s_ref)

    def conv_silu(x_ref, halo_ref, w_ref):
        halo = jnp.where(ir > 0, halo_ref[...], 0.0)
        return _silu(_causal_conv(halo, x_ref[...], w_ref[...], 4))

    qc = conv_silu(q_ref, qh_ref, cwq_ref)
    kc = conv_silu(k_ref, kh_ref, cwk_ref)
    vc = conv_silu(v_ref, vh_ref, cwv_ref)

    ba = ba_ref[...]
    beta_all = jax.nn.sigmoid(ba)
    g_all = -jnp.exp(par_ref[0:1, :]) * _softplus(ba + par_ref[1:2, :])
    rr = _iota((rows, rows), 0)
    cc = _iota((rows, rows), 1)
    chunk_tril = ((rr >> CHUNK_LOG2) == (cc >> CHUNK_LOG2)) & (cc <= rr)
    gcs_all = _dot(chunk_tril.astype(F32), g_all, _HI)

    row = _iota((CHUNK, CHUNK), 0)
    col = _iota((CHUNK, CHUNK), 1)
    eye = row == col
    ones_cc = jnp.ones((CHUNK, CHUNK), F32)
    scale = HEAD_W ** -0.5
    hg = hg_ref[...]

    for h in range(hb):
        hs = slice(h * HEAD_W, (h + 1) * HEAD_W)
        head = hblk * hb + h
        lane = _iota((rows, LANES), 1)
        beta_col = jnp.sum(jnp.where(lane == head, beta_all, 0.0), axis=-1, keepdims=True)
        gcs_colall = jnp.sum(jnp.where(lane == head + GDN_HEADS, gcs_all, 0.0), axis=-1, keepdims=True)
        q = qc[:, hs]
        k = kc[:, hs]
        q = q * lax.rsqrt(jnp.sum(q * q, axis=-1, keepdims=True) + EPS) * scale
        k = k * lax.rsqrt(jnp.sum(k * k, axis=-1, keepdims=True) + EPS)
        v = vc[:, hs]
        zg = _silu(z_ref[:, hs])
        state = s_ref[h]
        for c in range(rows // CHUNK):
            rs = slice(c * CHUNK, (c + 1) * CHUNK)
            qq, kk, vv = q[rs], k[rs], v[rs]
            beta = beta_col[rs]
            gcs = gcs_colall[rs]
            gcs_row = _dot(ones_cc, jnp.where(eye, gcs, 0.0), _HI)
            decay = jnp.where(col <= row, jnp.exp(jnp.minimum(gcs - gcs_row, 0.0)), 0.0)
            kb = kk * beta
            a_mat = jnp.where(col < row, _dot_nt(kb, kk, _HI) * decay, 0.0)
            t_inv = _unit_lower_inverse(a_mat, row, col)
            eg = jnp.exp(gcs)
            rhs = jnp.concatenate([vv * beta, kb * eg], axis=-1)
            sol = _dot(t_inv, rhs, _HI)
            u_base, k_cumdecay = sol[:, :HEAD_W], sol[:, HEAD_W:]
            qk = _dot_nt(qq, kk, _HI) * decay
            g_last = gcs[CHUNK - 1:CHUNK]
            q_dec = qq * eg
            k_dec = kk * jnp.exp(g_last - gcs)
            u = u_base - _dot(k_cumdecay, state, _HI)
            o = _dot(q_dec, state, _HI) + _dot(qk, u, _HI)
            state = state * jnp.exp(g_last) + _dot_tn(k_dec, u, _HI)
            o = o * lax.rsqrt(jnp.mean(o * o, axis=-1, keepdims=True) + EPS) * hg * zg[rs]
            o_ref[rs, hs] = o.astype(o_ref.dtype)
        s_ref[h] = state


def _gdn(proj, conv_w, par, head_g, *, hb=2, rows=256):
    t = proj.shape[0]
    w = hb * HEAD_W
    per = (GDN_HEADS * HEAD_W) // w
    ba_blk = (4 * GDN_HEADS * HEAD_W + 2 * CONF_CH) // LANES

    def sec(k):
        return pl.BlockSpec((rows, w), lambda h, i, k=k: (i, k * per + h))

    def halo(k):
        return pl.BlockSpec((8, w), lambda h, i, k=k: (jnp.maximum(i * (rows // 8) - 1, 0), k * per + h))

    def cw(k):
        return pl.BlockSpec((4, w), lambda h, i, k=k: (0, k * per + h))

    return pl.pallas_call(
        functools.partial(_gdn_body, hb=hb, rows=rows),
        grid=(GDN_HEADS // hb, t // rows),
        in_specs=[sec(0), sec(1), sec(2), halo(0), halo(1), halo(2), sec(3),
                  pl.BlockSpec((rows, LANES), lambda h, i: (i, ba_blk)),
                  cw(0), cw(1), cw(2),
                  pl.BlockSpec((8, LANES), lambda h, i: (0, 0)),
                  pl.BlockSpec((1, HEAD_W), lambda h, i: (0, 0))],
        out_specs=pl.BlockSpec((rows, w), lambda h, i: (i, h)),
        out_shape=jax.ShapeDtypeStruct((t, GDN_HEADS * HEAD_W), BF16),
        scratch_shapes=[pltpu.VMEM((hb, HEAD_W, HEAD_W), F32)],
        compiler_params=_cparams(("arbitrary", "arbitrary")),
        name="gated_deltanet",
    )(proj, proj, proj, proj, proj, proj, proj, proj, conv_w, conv_w, conv_w, par, head_g)


def _conf_body(x_ref, halo_ref, w_ref, b_ref, g_ref, lb_ref, o_ref):
    ir = pl.program_id(0)

    def glu(v):
        return v[:, :CONF_CH] * jax.nn.sigmoid(v[:, CONF_CH:])

    halo = jnp.where(ir > 0, glu(halo_ref[...]), 0.0)
    u = _causal_conv(halo, glu(x_ref[...]), w_ref[...], CONF_K) + b_ref[...]
    mu = jnp.mean(u, axis=-1, keepdims=True)
    uc = u - mu
    var = jnp.mean(uc * uc, axis=-1, keepdims=True)
    y = uc * lax.rsqrt(var + EPS) * g_ref[...] + lb_ref[...]
    o_ref[...] = _silu(y).astype(o_ref.dtype)


def _conformer(proj, dw, dw_b, ln_g, ln_b, *, rows=256):
    t = proj.shape[0]
    glu_blk = (4 * GDN_HEADS * HEAD_W) // (2 * CONF_CH)
    halo_rows = 32
    vec = pl.BlockSpec((1, CONF_CH), lambda i: (0, 0))
    return pl.pallas_call(
        _conf_body,
        grid=(t // rows,),
        in_specs=[pl.BlockSpec((rows, 2 * CONF_CH), lambda i: (i, glu_blk)),
                  pl.BlockSpec((halo_rows, 2 * CONF_CH),
                               lambda i: (jnp.maximum(i * (rows // halo_rows) - 1, 0), glu_blk)),
                  pl.BlockSpec((CONF_K, CONF_CH), lambda i: (0, 0)), vec, vec, vec],
        out_specs=pl.BlockSpec((rows, CONF_CH), lambda i: (i, 0)),
        out_shape=jax.ShapeDtypeStruct((t, CONF_CH), BF16),
        compiler_params=_cparams(("arbitrary",)),
        name="conformer_conv",
    )(proj, proj, dw, dw_b, ln_g, ln_b)


def _out0_body(a_ref, b_ref, wa_ref, wb_ref, x_ref, gate_ref, o_ref):
    mix = _dot(a_ref[...], wa_ref[...]) + _dot(b_ref[...], wb_ref[...])
    o_ref[...] = x_ref[...] + gate_ref[...] * mix


def _out_proj0(a, b, w, x, gate, *, tm=512, tn=1024):
    t, d = x.shape
    ka, kb = a.shape[1], b.shape[1]
    return pl.pallas_call(
        _out0_body,
        grid=(t // tm, d // tn),
        in_specs=[pl.BlockSpec((tm, ka), lambda i, j: (i, 0)),
                  pl.BlockSpec((tm, kb), lambda i, j: (i, 0)),
                  pl.BlockSpec((ka, tn), lambda i, j: (0, j)),
                  pl.BlockSpec((kb, tn), lambda i, j: (ka // kb, j)),
                  pl.BlockSpec((tm, tn), lambda i, j: (i, j)),
                  pl.BlockSpec((1, tn), lambda i, j: (0, j))],
        out_specs=pl.BlockSpec((tm, tn), lambda i, j: (i, j)),
        out_shape=jax.ShapeDtypeStruct((t, d), F32),
        compiler_params=_cparams(("arbitrary", "arbitrary")),
        name="out_proj_even",
    )(a, b, w, w, x, gate)


def _out1_body(y_ref, g_ref, w_ref, x_ref, gate_ref, o_ref, h_ref):
    @pl.when(pl.program_id(1) == 0)
    def _():
        y = y_ref[...]
        r = lax.rsqrt(jnp.mean(y * y, axis=-1, keepdims=True) + EPS)
        h_ref[...] = (y * r * g_ref[...]).astype(BF16)

    o_ref[...] = x_ref[...] + gate_ref[...] * _dot(h_ref[...], w_ref[...])


def _out_proj1(y, norm_g, w, x, gate, *, tm=512, tn=1024):
    t, d = x.shape
    k = y.shape[1]
    return pl.pallas_call(
        _out1_body,
        grid=(t // tm, d // tn),
        in_specs=[pl.BlockSpec((tm, k), lambda i, j: (i, 0)),
                  pl.BlockSpec((1, k), lambda i, j: (0, 0)),
                  pl.BlockSpec((k, tn), lambda i, j: (0, j)),
                  pl.BlockSpec((tm, tn), lambda i, j: (i, j)),
                  pl.BlockSpec((1, tn), lambda i, j: (0, j))],
        out_specs=pl.BlockSpec((tm, tn), lambda i, j: (i, j)),
        out_shape=jax.ShapeDtypeStruct((t, d), F32),
        scratch_shapes=[pltpu.VMEM((tm, k), BF16)],
        compiler_params=_cparams(("arbitrary", "arbitrary")),
        name="out_proj_odd",
    )(y, norm_g, w, x, gate)


def _ssd_body(x_ref, b_ref, c_ref, xh_ref, bh_ref, ch_ref, z_ref, dt_ref, cwx_ref, cwb_ref, cwc_ref,
              cbx_ref, cbb_ref, cbc_ref, par_ref, o_ref, s_ref, *, rows):
    grp = pl.program_id(0)
    ir = pl.program_id(1)
    gw = SSM_GROUP_W
    hpg = gw // SSM_HEADDIM

    @pl.when(ir == 0)
    def _():
        s_ref[...] = jnp.zeros_like(s_ref)

    def conv_silu(x_r, halo_r, w_r, bias_r):
        halo = jnp.where(ir > 0, halo_r[...], 0.0)
        return _silu(_causal_conv(halo, x_r[...], w_r[...], 4) + bias_r[...])

    xs = conv_silu(x_ref, xh_ref, cwx_ref, cbx_ref)
    bm = conv_silu(b_ref, bh_ref, cwb_ref, cbb_ref)
    cm = conv_silu(c_ref, ch_ref, cwc_ref, cbc_ref)

    dt = _softplus(dt_ref[...] + par_ref[0:1, :])
    da = dt * (-jnp.exp(par_ref[1:2, :]))
    rr = _iota((rows, rows), 0)
    cc = _iota((rows, rows), 1)
    chunk_tril = ((rr >> CHUNK_LOG2) == (cc >> CHUNK_LOG2)) & (cc <= rr)
    acs = _dot(chunk_tril.astype(F32), da, _HI)

    sel = (_iota((LANES, gw), 0) == grp * hpg + (_iota((LANES, gw), 1) >> CHUNK_LOG2)).astype(F32)
    dt_e = _dot(dt, sel, _HI)
    acs_e = _dot(acs, sel, _HI)
    dskip_e = _dot(par_ref[...], sel, _HI)[2:3, :]

    row = _iota((CHUNK, gw), 0)
    colm = _iota((CHUNK, gw), 1) & (CHUNK - 1)
    tiled_eye = row == colm
    causal = colm <= row
    ones_cc = jnp.ones((CHUNK, CHUNK), F32)
    blockdiag = (_iota((gw, gw), 0) >> CHUNK_LOG2) == (_iota((gw, gw), 1) >> CHUNK_LOG2)

    state = s_ref[...]
    for c in range(rows // CHUNK):
        rs = slice(c * CHUNK, (c + 1) * CHUNK)
        x_c, b_c, c_c = xs[rs], bm[rs], cm[rs]
        a_e = acs_e[rs]
        a_row = _dot(ones_cc, jnp.where(tiled_eye, a_e, 0.0), _HI)
        l_cat = jnp.where(causal, jnp.exp(jnp.minimum(a_e - a_row, 0.0)), 0.0)
        cb = _dot_nt(c_c.astype(BF16), jnp.concatenate([b_c] * hpg, axis=0).astype(BF16))
        xdt = x_c * dt_e[rs]
        bd = jnp.where(blockdiag, jnp.concatenate([xdt] * hpg, axis=0), 0.0).astype(BF16)
        y = _dot((cb * l_cat).astype(BF16), bd)
        y = y + _dot(c_c.astype(BF16), state.astype(BF16)) * jnp.exp(a_e)
        a_last = a_e[CHUNK - 1:CHUNK]
        state = state * jnp.exp(a_last) + _dot_tn(b_c.astype(BF16), (xdt * jnp.exp(a_last - a_e)).astype(BF16))
        y = y + dskip_e * x_c
        o_ref[rs, :] = y * _silu(z_ref[rs, :])
    s_ref[...] = state


def _ssd(proj, conv_w, conv_b, par, *, rows=256):
    t = proj.shape[0]
    gw = SSM_GROUP_W
    d_inner = SSM_GROUPS * gw
    xo = d_inner // gw
    bo = (2 * d_inner) // LANES
    co = bo + SSM_GROUPS * SSM_STATE // LANES
    dto = (2 * d_inner + 2 * SSM_GROUPS * SSM_STATE) // LANES
    hrow = lambda i: jnp.maximum(i * (rows // 8) - 1, 0)
    cb_x = d_inner // gw
    return pl.pallas_call(
        functools.partial(_ssd_body, rows=rows),
        grid=(SSM_GROUPS, t // rows),
        in_specs=[pl.BlockSpec((rows, gw), lambda g, i: (i, xo + g)),
                  pl.BlockSpec((rows, LANES), lambda g, i: (i, bo + g)),
                  pl.BlockSpec((rows, LANES), lambda g, i: (i, co + g)),
                  pl.BlockSpec((8, gw), lambda g, i: (hrow(i), xo + g)),
                  pl.BlockSpec((8, LANES), lambda g, i: (hrow(i), bo + g)),
                  pl.BlockSpec((8, LANES), lambda g, i: (hrow(i), co + g)),
                  pl.BlockSpec((rows, gw), lambda g, i: (i, g)),
                  pl.BlockSpec((rows, LANES), lambda g, i: (i, dto)),
                  pl.BlockSpec((4, gw), lambda g, i: (0, g)),
                  pl.BlockSpec((4, LANES), lambda g, i: (0, cb_x * (gw // LANES) + g)),
                  pl.BlockSpec((4, LANES), lambda g, i: (0, cb_x * (gw // LANES) + SSM_GROUPS + g)),
                  pl.BlockSpec((1, gw), lambda g, i: (0, g)),
                  pl.BlockSpec((1, LANES), lambda g, i: (0, cb_x * (gw // LANES) + g)),
                  pl.BlockSpec((1, LANES), lambda g, i: (0, cb_x * (gw // LANES) + SSM_GROUPS + g)),
                  pl.BlockSpec((8, LANES), lambda g, i: (0, 0))],
        out_specs=pl.BlockSpec((rows, gw), lambda g, i: (i, g)),
        out_shape=jax.ShapeDtypeStruct((t, d_inner), F32),
        scratch_shapes=[pltpu.VMEM((SSM_STATE, gw), F32)],
        compiler_params=_cparams(("arbitrary", "arbitrary")),
        name="ssd",
    )(proj, proj, proj, proj, proj, proj, proj, proj, conv_w, conv_w, conv_w, conv_b, conv_b, conv_b, par)


def _lane_pick(lane, idx, vals):
    return jnp.sum(jnp.where(lane == idx, vals, 0.0), axis=-1, keepdims=True)


def _router_body(x_ref, g_ref, sc_ref, sh_ref, w_ref, b_ref, hp_ref, route_ref, cnt_ref, carry_ref, *, tm):
    i = pl.program_id(0)

    @pl.when(i == 0)
    def _():
        carry_ref[...] = jnp.zeros_like(carry_ref)

    h = _adaln_rows(x_ref[...], g_ref[...], sc_ref[...], sh_ref[...])
    logits = _dot(h, w_ref[...], _HI) + b_ref[...]
    lane = _iota((tm, LANES), 1).astype(F32)
    neg = -jnp.inf
    gl = jnp.where(lane < N_GROUPS, logits, neg)
    gmax = jnp.max(gl, axis=-1, keepdims=True)
    grp_p = 1.0 / jnp.sum(jnp.exp(gl - gmax), axis=-1, keepdims=True)
    gidx = jnp.min(jnp.where(gl == gmax, lane, float(LANES)), axis=-1, keepdims=True)
    lo = N_GROUPS + EXPERTS_PER_GROUP * gidx
    el = jnp.where((lane >= lo) & (lane < lo + EXPERTS_PER_GROUP), logits, neg)
    m1 = jnp.max(el, axis=-1, keepdims=True)
    i1 = jnp.min(jnp.where(el == m1, lane, float(LANES)), axis=-1, keepdims=True)
    el2 = jnp.where(lane == i1, neg, el)
    m2 = jnp.max(el2, axis=-1, keepdims=True)
    i2 = jnp.min(jnp.where(el2 == m2, lane, float(LANES)), axis=-1, keepdims=True)
    tt = jnp.exp(m2 - m1)
    g1 = grp_p / (1.0 + tt)
    g2 = g1 * tt
    e1 = i1 - N_GROUPS
    e2 = i2 - N_GROUPS

    onehot = ((lane == e1) | (lane == e2)).astype(F32)
    strict = (_iota((tm, tm), 1) < _iota((tm, tm), 0)).astype(BF16)
    rank = _dot(strict, onehot.astype(BF16)) + carry_ref[0:1, :]
    r1 = _lane_pick(lane, e1, rank)
    r2 = _lane_pick(lane, e2, rank)
    carry_ref[...] = carry_ref[...] + jnp.sum(onehot, axis=0, keepdims=True)
    cnt_ref[...] = carry_ref[...]

    route = jnp.where(lane == 0, e1, jnp.where(lane == 1, e2, jnp.where(lane == 2, r1, jnp.where(
        lane == 3, r2, jnp.where(lane == 4, g1, jnp.where(lane == 5, g2, 0.0))))))
    route_ref[...] = route

    hb = h.astype(BF16).astype(F32)
    bits = lax.bitcast_convert_type(hb, U32)
    half = h.shape[1] // 2
    packed = (bits[:, :half] >> 16) | (bits[:, half:] & jnp.uint32(0xFFFF0000))
    for s in range(TOKEN_TILE_ROWS):
        hp_ref[pl.ds(s, tm, stride=TOKEN_TILE_ROWS), :] = packed[:, s * LANES:(s + 1) * LANES]


def _router(x, g, scale, shift, w_route, b_route, *, tm=512):
    t, d = x.shape
    vec = pl.BlockSpec((1, d), lambda i: (0, 0))
    return pl.pallas_call(
        functools.partial(_router_body, tm=tm),
        grid=(t // tm,),
        in_specs=[pl.BlockSpec((tm, d), lambda i: (i, 0)), vec, vec, vec,
                  pl.BlockSpec((d, LANES), lambda i: (0, 0)),
                  pl.BlockSpec((1, LANES), lambda i: (0, 0))],
        out_specs=[pl.BlockSpec((tm * TOKEN_TILE_ROWS, LANES), lambda i: (i, 0)),
                   pl.BlockSpec((tm, LANES), lambda i: (i, 0)),
                   pl.BlockSpec((8, LANES), lambda i: (0, 0))],
        out_shape=[jax.ShapeDtypeStruct((t * TOKEN_TILE_ROWS, LANES), U32),
                   jax.ShapeDtypeStruct((t, LANES), F32),
                   jax.ShapeDtypeStruct((8, LANES), F32)],
        scratch_shapes=[pltpu.VMEM((8, LANES), F32)],
        compiler_params=_cparams(("arbitrary",)),
        name="moe_router",
    )(x, g, scale, shift, w_route, b_route)


def _dest_body(route_ref, cnt_ref, dest_ref, blk_ref, *, tm, n_blocks_pad):
    cnt = cnt_ref[...]
    padded = jnp.floor((cnt + (MOE_ROWS - 1)) * (1.0 / MOE_ROWS)) * MOE_ROWS
    upper = (_iota((LANES, LANES), 0) <= _iota((LANES, LANES), 1)).astype(F32)
    pad_end = _dot(padded, upper, _HI)
    pad_start = (pad_end - padded)[0:1, :]
    route = route_ref[...]
    lane = _iota((tm, LANES), 1).astype(F32)
    e1, e2, r1, r2 = route[:, 0:1], route[:, 1:2], route[:, 2:3], route[:, 3:4]
    d1 = _lane_pick(lane, e1, pad_start) + r1
    d2 = _lane_pick(lane, e2, pad_start) + r2
    dest_ref[...] = jnp.where(lane == 0, d1, jnp.where(lane == 1, d2, 0.0)).astype(I32)

    lane_b = _iota((n_blocks_pad, LANES), 1)
    first_row = (_iota((n_blocks_pad, LANES), 0) * MOE_ROWS).astype(F32)
    done = ((pad_end[0:1, :] <= first_row) & (lane_b < N_EXPERTS)).astype(F32)
    blk_e = jnp.minimum(jnp.sum(done, axis=-1, keepdims=True), float(N_EXPERTS - 1))
    n_valid = pad_end[0:1, N_EXPERTS - 1:N_EXPERTS] * (1.0 / MOE_ROWS)
    blk_ref[...] = jnp.where(lane_b == 0, blk_e, jnp.where(lane_b == 1, n_valid, 0.0)).astype(I32)


def _dest(route, counts, n_blocks_pad, *, tm=512):
    t = route.shape[0]
    assert t % tm == 0
    return pl.pallas_call(
        functools.partial(_dest_body, tm=tm, n_blocks_pad=n_blocks_pad),
        grid=(t // tm,),
        in_specs=[pl.BlockSpec((tm, LANES), lambda i: (i, 0)),
                  pl.BlockSpec((8, LANES), lambda i: (0, 0))],
        out_specs=[pl.BlockSpec((tm, LANES), lambda i: (i, 0)),
                   pl.BlockSpec((n_blocks_pad, LANES), lambda i: (0, 0))],
        out_shape=[jax.ShapeDtypeStruct((t, LANES), I32),
                   jax.ShapeDtypeStruct((n_blocks_pad, LANES), I32)],
        compiler_params=_cparams(("arbitrary",)),
        name="moe_dest",
    )(route, counts)


DISPATCH_BATCH = 256


def _dispatch_body(dest_ref, hp_ref, xz_ref, xr_ref, sem, *, n_assign):
    del xz_ref
    tr = TOKEN_TILE_ROWS
    n_batches = n_assign // DISPATCH_BATCH

    def row_copy(a, slot):
        tok = a // 2
        return pltpu.make_async_copy(hp_ref.at[pl.ds(tok * tr, tr)], xr_ref.at[pl.ds(dest_ref[a] * tr, tr)],
                                     sem.at[slot])

    def batch_wait(slot):
        pltpu.make_async_copy(hp_ref.at[pl.ds(0, DISPATCH_BATCH * tr)], xr_ref.at[pl.ds(0, DISPATCH_BATCH * tr)],
                              sem.at[slot]).wait()

    def batch(bi, carry):
        slot = bi % 2

        def issue(j, c):
            row_copy(bi * DISPATCH_BATCH + j, slot).start()
            return c

        lax.fori_loop(0, DISPATCH_BATCH, issue, 0)

        @pl.when(bi > 0)
        def _():
            batch_wait(1 - slot)

        return carry

    lax.fori_loop(0, n_batches, batch, 0)
    batch_wait((n_batches - 1) % 2)


def _dispatch(dest_flat, h_packed, n_rows):
    n_assign = dest_flat.shape[0]
    zeros = jnp.zeros((n_rows * TOKEN_TILE_ROWS, LANES), U32)
    return pl.pallas_call(
        functools.partial(_dispatch_body, n_assign=n_assign),
        grid_spec=pltpu.PrefetchScalarGridSpec(
            num_scalar_prefetch=1,
            grid=(1,),
            in_specs=[pl.BlockSpec(memory_space=pl.ANY), pl.BlockSpec(memory_space=pl.ANY)],
            out_specs=pl.BlockSpec(memory_space=pl.ANY),
            scratch_shapes=[pltpu.SemaphoreType.DMA((2,))],
        ),
        out_shape=jax.ShapeDtypeStruct(zeros.shape, U32),
        input_output_aliases={2: 0},
        compiler_params=_cparams(("arbitrary",)),
        name="moe_dispatch",
    )(dest_flat, h_packed, zeros)


def _expert_body(be_ref, nv_ref, x_ref, w1_ref, w3_ref, w2_ref, y_ref, w1b_ref, w3b_ref, w2b_ref):
    b = pl.program_id(0)
    valid = b < nv_ref[0]

    @pl.when(valid & ((b == 0) | (be_ref[b] != be_ref[jnp.maximum(b - 1, 0)])))
    def _():
        w1b_ref[...] = w1_ref[...].astype(BF16)
        w3b_ref[...] = w3_ref[...].astype(BF16)
        w2b_ref[...] = w2_ref[...].astype(BF16)

    @pl.when(jnp.logical_not(valid))
    def _():
        y_ref[...] = jnp.zeros_like(y_ref)

    @pl.when(valid)
    def _():
        lo, hi = [], []
        for s in range(TOKEN_TILE_ROWS):
            p = x_ref[pl.ds(s, MOE_ROWS, stride=TOKEN_TILE_ROWS), :]
            lo.append(lax.bitcast_convert_type(p << 16, F32))
            hi.append(lax.bitcast_convert_type(p & jnp.uint32(0xFFFF0000), F32))
        x = jnp.concatenate(lo + hi, axis=1).astype(BF16)
        a = (_silu(_dot(x, w1b_ref[...])) * _dot(x, w3b_ref[...])).astype(BF16)
        y = _dot(a, w2b_ref[...])
        for s in range(Y_TILE_ROWS):
            y_ref[pl.ds(s, MOE_ROWS, stride=Y_TILE_ROWS), :] = y[:, s * LANES:(s + 1) * LANES]


def _experts(block_expert, n_valid, x_rows, w1, w3, w2):
    n_blocks = block_expert.shape[0]
    _, d, f = w1.shape
    return pl.pallas_call(
        _expert_body,
        grid_spec=pltpu.PrefetchScalarGridSpec(
            num_scalar_prefetch=2,
            grid=(n_blocks,),
            in_specs=[pl.BlockSpec((MOE_ROWS * TOKEN_TILE_ROWS, LANES), lambda b, be, nv: (b, 0)),
                      pl.BlockSpec((None, d, f), lambda b, be, nv: (be[b], 0, 0)),
                      pl.BlockSpec((None, d, f), lambda b, be, nv: (be[b], 0, 0)),
                      pl.BlockSpec((None, f, d), lambda b, be, nv: (be[b], 0, 0))],
            out_specs=pl.BlockSpec((MOE_ROWS * Y_TILE_ROWS, LANES), lambda b, be, nv: (b, 0)),
            scratch_shapes=[pltpu.VMEM((d, f), BF16), pltpu.VMEM((d, f), BF16), pltpu.VMEM((f, d), BF16)],
        ),
        out_shape=jax.ShapeDtypeStruct((n_blocks * MOE_ROWS * Y_TILE_ROWS, LANES), F32),
        compiler_params=_cparams(("arbitrary",)),
        name="moe_experts",
    )(block_expert, n_valid, x_rows, w1, w3, w2)


def _combine_body(dest_ref, y_ref, x_ref, route_ref, gate_ref, fg_ref, o_ref, buf_ref, sem, *, tm, final_norm):
    i = pl.program_id(0)
    yr = Y_TILE_ROWS

    def issue(j, c):
        for k in range(2):
            d = dest_ref[(i * tm + j) * 2 + k]
            pltpu.make_async_copy(y_ref.at[pl.ds(d * yr, yr)], buf_ref.at[k, pl.ds(j * yr, yr)], sem.at[k]).start()
        return c

    lax.fori_loop(0, tm, issue, 0)
    for k in range(2):
        pltpu.make_async_copy(y_ref.at[pl.ds(0, tm * yr)], buf_ref.at[k], sem.at[k]).wait()

    route = route_ref[...]
    g1, g2 = route[:, 4:5], route[:, 5:6]
    for s in range(yr):
        cs = slice(s * LANES, (s + 1) * LANES)
        ffn = g1 * buf_ref[0, pl.ds(s, tm, stride=yr), :] + g2 * buf_ref[1, pl.ds(s, tm, stride=yr), :]
        o_ref[:, cs] = x_ref[:, cs] + gate_ref[:, cs] * ffn
    if final_norm:
        xo = o_ref[...]
        o_ref[...] = xo * lax.rsqrt(jnp.mean(xo * xo, axis=-1, keepdims=True) + EPS) * fg_ref[...]


def _combine(dest_flat, y_rows, x, route, gate, final_g, *, final_norm, tm=128):
    t, d = x.shape
    return pl.pallas_call(
        functools.partial(_combine_body, tm=tm, final_norm=final_norm),
        grid_spec=pltpu.PrefetchScalarGridSpec(
            num_scalar_prefetch=1,
            grid=(t // tm,),
            in_specs=[pl.BlockSpec(memory_space=pl.ANY),
                      pl.BlockSpec((tm, d), lambda i, dr: (i, 0)),
                      pl.BlockSpec((tm, LANES), lambda i, dr: (i, 0)),
                      pl.BlockSpec((1, d), lambda i, dr: (0, 0)),
                      pl.BlockSpec((1, d), lambda i, dr: (0, 0))],
            out_specs=pl.BlockSpec((tm, d), lambda i, dr: (i, 0)),
            scratch_shapes=[pltpu.VMEM((2, tm * Y_TILE_ROWS, LANES), F32), pltpu.SemaphoreType.DMA((2,))],
        ),
        out_shape=jax.ShapeDtypeStruct((t, d), F32),
        compiler_params=_cparams(("arbitrary",)),
        name="moe_combine",
    )(dest_flat, y_rows, x, route, gate, final_g)


def _moe(x, g, scale, shift, gate, w_group, b_group, w_expert, b_expert, w1, w3, w2, final_g, *, final_norm):
    t, d = x.shape
    pad = LANES - N_GROUPS - N_EXPERTS
    w_route = jnp.concatenate([w_group, w_expert, jnp.zeros((d, pad), F32)], axis=1)
    b_route = jnp.concatenate([b_group, b_expert, jnp.zeros((pad,), F32)]).reshape(1, LANES)
    h_packed, route, counts = _router(x, g, scale, shift, w_route, b_route)
    n_assign = 2 * t
    n_blocks = -(-(n_assign + N_EXPERTS * (MOE_ROWS - 1)) // MOE_ROWS)
    n_blocks_pad = -(-n_blocks // 8) * 8
    dest, blk = _dest(route, counts, n_blocks_pad)
    dest_flat = dest[:, :2].reshape(n_assign)
    block_expert = blk[:n_blocks, 0]
    n_valid = blk[0, 1:2]
    x_rows = _dispatch(dest_flat, h_packed, n_blocks * MOE_ROWS)
    y_rows = _experts(block_expert, n_valid, x_rows, w1, w3, w2)
    return _combine(dest_flat, y_rows, x, route, gate, final_g, final_norm=final_norm)


def _pad_cols(w, width):
    return jnp.pad(w, ((0, 0), (0, width - w.shape[1])))


def _lane_row(v, offset):
    return jnp.zeros((LANES,), F32).at[offset:offset + v.shape[0]].set(v)


def kernel(x, c, w_mod, b_mod, norm_g, final_norm_g, e_w_in, e_conv_qkv, e_a_log, e_dt_bias, e_head_norm_g, e_conf_dw, e_conf_dw_b, e_conf_ln_g, e_conf_ln_b, e_w_out, o_w_in, o_conv_w, o_conv_b, o_dt_bias, o_a_log, o_d_skip, o_norm_g, o_w_out, moe_w_group, moe_b_group, moe_w_expert, moe_b_expert, moe_w1, moe_w3, moe_w2):
    bsz, t, d = x.shape
    assert bsz == 1 and c.shape == (1, d)
    xt = x[0]
    mod = _modulation(c[0], w_mod, b_mod)

    def mod_parts(l):
        return mod[l, :, :d], mod[l, :, d:2 * d], mod[l, :, 2 * d:]

    row = lambda v: v.reshape(1, -1)
    qkv_w = 3 * GDN_HEADS * HEAD_W
    z_end = qkv_w + GDN_HEADS * HEAD_W

    shift, scale, gate = mod_parts(0)
    w_in = e_w_in[0]
    w_in0 = jnp.concatenate(
        [w_in[:, :z_end], w_in[:, z_end + 2 * GDN_HEADS:], _pad_cols(w_in[:, z_end:z_end + 2 * GDN_HEADS], LANES)],
        axis=1).astype(BF16)
    proj0 = _norm_matmul(xt, row(norm_g[0, 0]), scale, shift, w_in0, tm=512, tn=896)
    gdn_par = jnp.zeros((8, LANES), F32).at[0].set(_lane_row(e_a_log[0], GDN_HEADS)).at[1].set(
        _lane_row(e_dt_bias[0], GDN_HEADS))
    a_out = _gdn(proj0, e_conv_qkv[0], gdn_par, row(e_head_norm_g[0]))
    b_out = _conformer(proj0, e_conf_dw[0], row(e_conf_dw_b[0]), row(e_conf_ln_g[0]), row(e_conf_ln_b[0]))
    xt = _out_proj0(a_out, b_out, e_w_out[0].astype(BF16), xt, gate)
    shift, scale, gate = mod_parts(1)
    xt = _moe(xt, row(norm_g[0, 1]), scale, shift, gate, moe_w_group[0], moe_b_group[0], moe_w_expert[0],
              moe_b_expert[0], moe_w1[0], moe_w3[0], moe_w2[0], row(final_norm_g), final_norm=False)

    shift, scale, gate = mod_parts(2)
    w_in1 = _pad_cols(o_w_in[0], -(-o_w_in.shape[2] // LANES) * LANES).astype(BF16)
    proj1 = _norm_matmul(xt, row(norm_g[1, 0]), scale, shift, w_in1, tm=512, tn=1152)
    ssd_par = jnp.zeros((8, LANES), F32).at[0].set(_lane_row(o_dt_bias[0], 0)).at[1].set(
        _lane_row(o_a_log[0], 0)).at[2].set(_lane_row(o_d_skip[0], 0))
    y = _ssd(proj1, o_conv_w[0], row(o_conv_b[0]), ssd_par)
    xt = _out_proj1(y, row(o_norm_g[0]), o_w_out[0].astype(BF16), xt, gate)
    shift, scale, gate = mod_parts(3)
    xt = _moe(xt, row(norm_g[1, 1]), scale, shift, gate, moe_w_group[1], moe_b_group[1], moe_w_expert[1],
              moe_b_expert[1], moe_w1[1], moe_w3[1], moe_w2[1], row(final_norm_g), final_norm=True)
    return xt[None]
```

```python
import functools

import jax
import jax.numpy as jnp
from jax import lax
from jax.experimental import pallas as pl
from jax.experimental.pallas import tpu as pltpu

F32 = jnp.float32
BF16 = jnp.bfloat16
I32 = jnp.int32
U32 = jnp.uint32

EPS = 1e-6
LANES = 128
CHUNK = 64
CHUNK_LOG2 = 6
GDN_HEADS = 8
HEAD_W = 128
CONF_CH = 1024
CONF_K = 31
SSM_GROUPS = 8
SSM_GROUP_W = 512
SSM_HEADDIM = 64
SSM_STATE = 128
N_EXPERTS = 32
EXPERTS_PER_GROUP = 8
N_GROUPS = 4
MOE_ROWS = 256
TOKEN_TILE_ROWS = 8
Y_TILE_ROWS = 16
VMEM_LIMIT = 56 * 1024 * 1024
PROLOGUE_ROWS = 128

_HI = lax.Precision.HIGHEST


def _cparams(sem):
    return pltpu.CompilerParams(dimension_semantics=sem, vmem_limit_bytes=VMEM_LIMIT)


def _dot(a, b, precision=None):
    return jnp.dot(a, b, precision=precision, preferred_element_type=F32)


def _dot_nt(a, b, precision=None):
    return lax.dot_general(a, b, (((1,), (1,)), ((), ())), precision=precision, preferred_element_type=F32)


def _dot_tn(a, b, precision=None):
    return lax.dot_general(a, b, (((0,), (0,)), ((), ())), precision=precision, preferred_element_type=F32)


def _split_bf16(x):
    hi = x.astype(BF16)
    return hi, (x - hi.astype(F32)).astype(BF16)


def _dot3(a, b):
    ah, al = _split_bf16(a)
    bh, bl = _split_bf16(b)
    return _dot(ah, bh) + (_dot(ah, bl) + _dot(al, bh))


def _silu(x):
    return x * jax.nn.sigmoid(x)


def _softplus(x):
    return jnp.maximum(x, 0.0) + jnp.log1p(jnp.exp(-jnp.abs(x)))


def _iota(shape, dim):
    return lax.broadcasted_iota(I32, shape, dim)


def _mod_body(c_ref, w_ref, b_ref, o_ref):
    c = c_ref[...]
    o_ref[...] = jnp.sum(w_ref[...] * _silu(c), axis=0, keepdims=True) + b_ref[...]


def _modulation(c, w_mod, b_mod):
    nl, d, n = w_mod.shape
    tn = 768
    return pl.pallas_call(
        _mod_body,
        grid=(nl, n // tn),
        in_specs=[
            pl.BlockSpec((d, 1), lambda l, j: (0, 0)),
            pl.BlockSpec((None, d, tn), lambda l, j: (l, 0, j)),
            pl.BlockSpec((None, 1, tn), lambda l, j: (l, 0, j)),
        ],
        out_specs=pl.BlockSpec((None, 1, tn), lambda l, j: (l, 0, j)),
        out_shape=jax.ShapeDtypeStruct((nl, 1, n), F32),
        compiler_params=_cparams(("arbitrary", "arbitrary")),
        name="adaln_mod",
    )(c.reshape(d, 1), w_mod, b_mod.reshape(nl, 1, n))


def _adaln_rows(x, g, scale, shift):
    r = lax.rsqrt(jnp.mean(x * x, axis=-1, keepdims=True) + EPS)
    return x * r * (g * (1.0 + scale)) + shift


def _norm_mm_body(*refs, starts):
    x_ref, g_ref, sc_ref, sh_ref, ws_ref = refs[:5]
    w_refs = refs[5:5 + len(starts)]
    o_ref, os_ref, h_ref = refs[5 + len(starts):]
    j = pl.program_id(1)

    @pl.when(j == 0)
    def _():
        ws = ws_ref[...].astype(BF16)

        def norm_rows(c, carry):
            rs = pl.ds(pl.multiple_of(c * PROLOGUE_ROWS, PROLOGUE_ROWS), PROLOGUE_ROWS)
            h = _adaln_rows(x_ref[rs, :], g_ref[...], sc_ref[...], sh_ref[...]).astype(BF16)
            h_ref[rs, :] = h
            os_ref[rs, :] = _dot(h, ws)
            return carry

        lax.fori_loop(0, x_ref.shape[0] // PROLOGUE_ROWS, norm_rows, 0)

    bounds = list(starts[1:]) + [None]
    for w_ref, lo, hi in zip(w_refs, starts, bounds):
        in_seg = (j >= lo) if hi is None else ((j >= lo) & (j < hi))

        @pl.when(in_seg)
        def _(w_ref=w_ref):
            o_ref[...] = _dot(h_ref[...], w_ref[...].astype(BF16))


def _norm_matmul(x, g, scale, shift, segments, w_small, *, tm=1024, tn=512):
    t, d = x.shape
    tm = min(tm, t)
    starts, specs, n_main = [], [], 0
    for w, nb in segments:
        lo = n_main // tn
        starts.append(lo)
        specs.append(pl.BlockSpec((d, tn), lambda i, j, lo=lo, nb=nb: (0, jnp.clip(j - lo, 0, nb - 1))))
        n_main += nb * tn
    vec = pl.BlockSpec((1, d), lambda i, j: (0, 0))
    return pl.pallas_call(
        functools.partial(_norm_mm_body, starts=tuple(starts)),
        grid=(t // tm, n_main // tn),
        in_specs=[pl.BlockSpec((tm, d), lambda i, j: (i, 0)), vec, vec, vec,
                  pl.BlockSpec((d, LANES), lambda i, j: (0, 0))] + specs,
        out_specs=[pl.BlockSpec((tm, tn), lambda i, j: (i, j)),
                   pl.BlockSpec((tm, LANES), lambda i, j: (i, 0))],
        out_shape=[jax.ShapeDtypeStruct((t, n_main), F32), jax.ShapeDtypeStruct((t, LANES), F32)],
        scratch_shapes=[pltpu.VMEM((tm, d), BF16)],
        compiler_params=_cparams(("arbitrary", "arbitrary")),
        name="adaln_in_proj",
    )(x, g, scale, shift, w_small, *[w for w, _ in segments])


def _causal_conv(halo, x, w, taps):
    hr = halo.shape[0]
    n = x.shape[0]
    xe = jnp.concatenate([halo, x], axis=0)
    rolled = {0: xe}
    acc = None
    for j in range(taps):
        s = taps - 1 - j
        a, b = divmod(s, 8)
        if b not in rolled:
            rolled[b] = pltpu.roll(xe, b, axis=0)
        term = rolled[b][hr - 8 * a:hr - 8 * a + n] * w[j:j + 1]
        acc = term if acc is None else acc + term
    return acc


def _unit_lower_inverses(a_mats, row, col):
    n = a_mats[0].shape[0]
    eye = (row == col).astype(F32)
    level = ((row >> 1) == (col >> 1)) & (col < row)
    invs = [eye - jnp.where(level, a, 0.0) for a in a_mats]
    k = 1
    while (1 << k) < n:
        level = ((row >> (k + 1)) == (col >> (k + 1))) & (((row >> k) & 1) == 1) & (((col >> k) & 1) == 0)
        splits = [_split_bf16(inv) for inv in invs]
        ys = []
        for (ih, il), a in zip(splits, a_mats):
            mh, ml = _split_bf16(jnp.where(level, a, 0.0))
            ys.append(_dot(ih, mh) + (_dot(ih, ml) + _dot(il, mh)))
        new = []
        for inv, (ih, il), y in zip(invs, splits, ys):
            yh, yl = _split_bf16(y)
            new.append(inv - (_dot(yh, ih) + (_dot(yh, il) + _dot(yl, ih))))
        invs = new
        k += 1
    return invs


def _gdn_body(q_ref, k_ref, v_ref, qh_ref, kh_ref, vh_ref, z_ref, ba_ref, cwq_ref, cwk_ref, cwv_ref,
              par_ref, hg_ref, o_ref, s_ref, *, hb, rows):
    hblk = pl.program_id(0)
    ir = pl.program_id(1)

    @pl.when(ir == 0)
    def _():
        s_ref[...] = jnp.zeros_like(s_ref)

    def conv_silu(x_ref, halo_ref, w_ref):
        halo = jnp.where(ir > 0, halo_ref[...], 0.0)
        return _silu(_causal_conv(halo, x_ref[...], w_ref[...], 4))

    qc = conv_silu(q_ref, qh_ref, cwq_ref)
    kc = conv_silu(k_ref, kh_ref, cwk_ref)
    vc = conv_silu(v_ref, vh_ref, cwv_ref)

    ba = ba_ref[...]
    beta_all = jax.nn.sigmoid(ba)
    g_all = -jnp.exp(par_ref[0:1, :]) * _softplus(ba + par_ref[1:2, :])
    rr = _iota((rows, rows), 0)
    cc = _iota((rows, rows), 1)
    chunk_tril = ((rr >> CHUNK_LOG2) == (cc >> CHUNK_LOG2)) & (cc <= rr)
    gcs_all = _dot(chunk_tril.astype(F32), g_all, _HI)
    gcs_t = gcs_all.T

    row = _iota((CHUNK, CHUNK), 0)
    col = _iota((CHUNK, CHUNK), 1)
    scale = HEAD_W ** -0.5
    hg = hg_ref[...]
    bf = lambda a: a.astype(BF16)
    lane = _iota((rows, LANES), 1)
    sub = _iota((LANES, rows), 0)
    nc = rows // CHUNK

    heads = []
    for h in range(hb):
        hs = slice(h * HEAD_W, (h + 1) * HEAD_W)
        head = hblk * hb + h
        q = qc[:, hs]
        k = kc[:, hs]
        heads.append(dict(
            q=q * lax.rsqrt(jnp.sum(q * q, axis=-1, keepdims=True) + EPS) * scale,
            k=k * lax.rsqrt(jnp.sum(k * k, axis=-1, keepdims=True) + EPS),
            v=vc[:, hs],
            beta=jnp.sum(jnp.where(lane == head, beta_all, 0.0), axis=-1, keepdims=True),
            gcs=jnp.sum(jnp.where(lane == head + GDN_HEADS, gcs_all, 0.0), axis=-1, keepdims=True),
            gcs_row=jnp.sum(jnp.where(sub == head + GDN_HEADS, gcs_t, 0.0), axis=0, keepdims=True)))

    items = []
    for c in range(nc):
        rs = slice(c * CHUNK, (c + 1) * CHUNK)
        for hd in heads:
            qq, kk, vv, beta, gcs = hd["q"][rs], hd["k"][rs], hd["v"][rs], hd["beta"][rs], hd["gcs"][rs]
            decay = jnp.where(col <= row, jnp.exp(jnp.minimum(gcs - hd["gcs_row"][:, rs], 0.0)), 0.0)
            kb = kk * beta
            eg = jnp.exp(gcs)
            g_last = gcs[CHUNK - 1:CHUNK]
            items.append(dict(decay=decay, kb16=bf(kb), k16=bf(kk), q16=bf(qq),
                              rhs16=bf(jnp.concatenate([vv * beta, kb * eg], axis=-1)),
                              q_dec16=bf(qq * eg), k_dec16=bf(kk * jnp.exp(g_last - gcs)), cd=jnp.exp(g_last)))
    a_mats = [jnp.where(col < row, _dot_nt(it["kb16"], it["k16"]) * it["decay"], 0.0) for it in items]
    qks = [bf(_dot_nt(it["q16"], it["k16"]) * it["decay"]) for it in items]
    t_invs = _unit_lower_inverses(a_mats, row, col)
    sols = [_dot(bf(t), it["rhs16"]) for t, it in zip(t_invs, items)]

    states = [s_ref[h] for h in range(hb)]
    for c in range(nc):
        rs = slice(c * CHUNK, (c + 1) * CHUNK)
        cur = slice(c * hb, (c + 1) * hb)
        s16 = [bf(s) for s in states]
        us = [so[:, :HEAD_W] - _dot(bf(so[:, HEAD_W:]), s) for so, s in zip(sols[cur], s16)]
        outs = [_dot(it["q_dec16"], s) + _dot(qk, bf(u)) for it, s, qk, u in zip(items[cur], s16, qks[cur], us)]
        states = [s * it["cd"] + _dot_tn(it["k_dec16"], bf(u)) for s, it, u in zip(states, items[cur], us)]
        for h, o in enumerate(outs):
            hs = slice(h * HEAD_W, (h + 1) * HEAD_W)
            o = o * lax.rsqrt(jnp.mean(o * o, axis=-1, keepdims=True) + EPS) * hg * _silu(z_ref[rs, hs])
            o_ref[rs, hs] = o.astype(o_ref.dtype)
    for h in range(hb):
        s_ref[h] = states[h]


def _gdn(proj, ba, conv_w, par, head_g, *, hb=GDN_HEADS, rows=256):
    t = proj.shape[0]
    rows = min(rows, t)
    w = hb * HEAD_W
    per = (GDN_HEADS * HEAD_W) // w

    def sec(k):
        return pl.BlockSpec((rows, w), lambda h, i, k=k: (i, k * per + h))

    def halo(k):
        return pl.BlockSpec((8, w), lambda h, i, k=k: (jnp.maximum(i * (rows // 8) - 1, 0), k * per + h))

    def cw(k):
        return pl.BlockSpec((4, w), lambda h, i, k=k: (0, k * per + h))

    return pl.pallas_call(
        functools.partial(_gdn_body, hb=hb, rows=rows),
        grid=(GDN_HEADS // hb, t // rows),
        in_specs=[sec(0), sec(1), sec(2), halo(0), halo(1), halo(2), sec(3),
                  pl.BlockSpec((rows, LANES), lambda h, i: (i, 0)),
                  cw(0), cw(1), cw(2),
                  pl.BlockSpec((8, LANES), lambda h, i: (0, 0)),
                  pl.BlockSpec((1, HEAD_W), lambda h, i: (0, 0))],
        out_specs=pl.BlockSpec((rows, w), lambda h, i: (i, h)),
        out_shape=jax.ShapeDtypeStruct((t, GDN_HEADS * HEAD_W), BF16),
        scratch_shapes=[pltpu.VMEM((hb, HEAD_W, HEAD_W), F32)],
        compiler_params=_cparams(("arbitrary", "arbitrary")),
        name="gated_deltanet",
    )(proj, proj, proj, proj, proj, proj, proj, ba, conv_w, conv_w, conv_w, par, head_g)


def _conf_body(x_ref, halo_ref, w_ref, b_ref, g_ref, lb_ref, o_ref):
    ir = pl.program_id(0)

    def glu(v):
        return v[:, :CONF_CH] * jax.nn.sigmoid(v[:, CONF_CH:])

    halo = jnp.where(ir > 0, glu(halo_ref[...]), 0.0)
    u = _causal_conv(halo, glu(x_ref[...]), w_ref[...], CONF_K) + b_ref[...]
    mu = jnp.mean(u, axis=-1, keepdims=True)
    uc = u - mu
    var = jnp.mean(uc * uc, axis=-1, keepdims=True)
    y = uc * lax.rsqrt(var + EPS) * g_ref[...] + lb_ref[...]
    o_ref[...] = _silu(y).astype(o_ref.dtype)


def _conformer(proj, dw, dw_b, ln_g, ln_b, *, rows=256):
    t = proj.shape[0]
    glu_blk = (4 * GDN_HEADS * HEAD_W) // (2 * CONF_CH)
    halo_rows = 32
    vec = pl.BlockSpec((1, CONF_CH), lambda i: (0, 0))
    return pl.pallas_call(
        _conf_body,
        grid=(t // rows,),
        in_specs=[pl.BlockSpec((rows, 2 * CONF_CH), lambda i: (i, glu_blk)),
                  pl.BlockSpec((halo_rows, 2 * CONF_CH),
                               lambda i: (jnp.maximum(i * (rows // halo_rows) - 1, 0), glu_blk)),
                  pl.BlockSpec((CONF_K, CONF_CH), lambda i: (0, 0)), vec, vec, vec],
        out_specs=pl.BlockSpec((rows, CONF_CH), lambda i: (i, 0)),
        out_shape=jax.ShapeDtypeStruct((t, CONF_CH), BF16),
        compiler_params=_cparams(("arbitrary",)),
        name="conformer_conv",
    )(proj, proj, dw, dw_b, ln_g, ln_b)


def _out0_body(a_ref, b_ref, wa_ref, wb_ref, x_ref, gate_ref, o_ref):
    mix = _dot(a_ref[...], wa_ref[...].astype(BF16)) + _dot(b_ref[...], wb_ref[...].astype(BF16))
    o_ref[...] = x_ref[...] + gate_ref[...] * mix


def _out_proj0(a, b, w, x, gate, *, tm=1024, tn=1024):
    t, d = x.shape
    tm = min(tm, t)
    ka, kb = a.shape[1], b.shape[1]
    return pl.pallas_call(
        _out0_body,
        grid=(t // tm, d // tn),
        in_specs=[pl.BlockSpec((tm, ka), lambda i, j: (i, 0)),
                  pl.BlockSpec((tm, kb), lambda i, j: (i, 0)),
                  pl.BlockSpec((ka, tn), lambda i, j: (0, j)),
                  pl.BlockSpec((kb, tn), lambda i, j: (ka // kb, j)),
                  pl.BlockSpec((tm, tn), lambda i, j: (i, j)),
                  pl.BlockSpec((1, tn), lambda i, j: (0, j))],
        out_specs=pl.BlockSpec((tm, tn), lambda i, j: (i, j)),
        out_shape=jax.ShapeDtypeStruct((t, d), F32),
        compiler_params=_cparams(("arbitrary", "arbitrary")),
        name="out_proj_even",
    )(a, b, w, w, x, gate)


def _out1_body(y_ref, g_ref, w_ref, x_ref, gate_ref, o_ref, h_ref):
    @pl.when(pl.program_id(1) == 0)
    def _():
        def norm_rows(c, carry):
            rs = pl.ds(pl.multiple_of(c * PROLOGUE_ROWS, PROLOGUE_ROWS), PROLOGUE_ROWS)
            y = y_ref[rs, :].astype(F32)
            r = lax.rsqrt(jnp.mean(y * y, axis=-1, keepdims=True) + EPS)
            h_ref[rs, :] = (y * r * g_ref[...]).astype(BF16)
            return carry

        lax.fori_loop(0, y_ref.shape[0] // PROLOGUE_ROWS, norm_rows, 0)

    o_ref[...] = x_ref[...] + gate_ref[...] * _dot(h_ref[...], w_ref[...].astype(BF16))


def _out_proj1(y, norm_g, w, x, gate, *, tm=1024, tn=512):
    t, d = x.shape
    tm = min(tm, t)
    k = y.shape[1]
    return pl.pallas_call(
        _out1_body,
        grid=(t // tm, d // tn),
        in_specs=[pl.BlockSpec((tm, k), lambda i, j: (i, 0)),
                  pl.BlockSpec((1, k), lambda i, j: (0, 0)),
                  pl.BlockSpec((k, tn), lambda i, j: (0, j)),
                  pl.BlockSpec((tm, tn), lambda i, j: (i, j)),
                  pl.BlockSpec((1, tn), lambda i, j: (0, j))],
        out_specs=pl.BlockSpec((tm, tn), lambda i, j: (i, j)),
        out_shape=jax.ShapeDtypeStruct((t, d), F32),
        scratch_shapes=[pltpu.VMEM((tm, k), BF16)],
        compiler_params=_cparams(("arbitrary", "arbitrary")),
        name="out_proj_odd",
    )(y, norm_g, w, x, gate)


def _ssd_body(x_ref, b_ref, c_ref, xh_ref, bh_ref, ch_ref, z_ref, dt_ref, cwx_ref, cwb_ref, cwc_ref,
              cbx_ref, cbb_ref, cbc_ref, par_ref, o_ref, s_ref, *, rows):
    grp = pl.program_id(0)
    ir = pl.program_id(1)
    gw = SSM_GROUP_W
    hpg = gw // SSM_HEADDIM

    @pl.when(ir == 0)
    def _():
        s_ref[...] = jnp.zeros_like(s_ref)

    def conv_silu(x_r, halo_r, w_r, bias_r):
        halo = jnp.where(ir > 0, halo_r[...], 0.0)
        return _silu(_causal_conv(halo, x_r[...], w_r[...], 4) + bias_r[...])

    xs = conv_silu(x_ref, xh_ref, cwx_ref, cbx_ref)
    bm = conv_silu(b_ref, bh_ref, cwb_ref, cbb_ref)
    cm = conv_silu(c_ref, ch_ref, cwc_ref, cbc_ref)

    dt = _softplus(dt_ref[...] + par_ref[0:1, :])
    da = dt * (-jnp.exp(par_ref[1:2, :]))
    rr = _iota((rows, rows), 0)
    cc = _iota((rows, rows), 1)
    chunk_tril = ((rr >> CHUNK_LOG2) == (cc >> CHUNK_LOG2)) & (cc <= rr)
    acs = _dot(chunk_tril.astype(F32), da, _HI)

    sel = (_iota((LANES, gw), 0) == grp * hpg + (_iota((LANES, gw), 1) >> CHUNK_LOG2)).astype(F32)
    dt_e = _dot(dt, sel, _HI)
    acs_e = _dot(acs, sel, _HI)
    dskip_e = _dot(par_ref[...], sel, _HI)[2:3, :]

    row = _iota((CHUNK, gw), 0)
    colm = _iota((CHUNK, gw), 1) & (CHUNK - 1)
    tiled_eye = row == colm
    causal = colm <= row
    ones_cc = jnp.ones((CHUNK, CHUNK), F32)
    blockdiag = (_iota((gw, gw), 0) >> CHUNK_LOG2) == (_iota((gw, gw), 1) >> CHUNK_LOG2)

    state = s_ref[...]
    for c in range(rows // CHUNK):
        rs = slice(c * CHUNK, (c + 1) * CHUNK)
        x_c, b_c, c_c = xs[rs], bm[rs], cm[rs]
        a_e = acs_e[rs]
        a_row = _dot(ones_cc, jnp.where(tiled_eye, a_e, 0.0), _HI)
        l_cat = jnp.where(causal, jnp.exp(jnp.minimum(a_e - a_row, 0.0)), 0.0)
        cb = _dot_nt(c_c.astype(BF16), jnp.concatenate([b_c] * hpg, axis=0).astype(BF16))
        xdt = x_c * dt_e[rs]
        bd = jnp.where(blockdiag, jnp.concatenate([xdt] * hpg, axis=0), 0.0).astype(BF16)
        y = _dot((cb * l_cat).astype(BF16), bd)
        y = y + _dot(c_c.astype(BF16), state.astype(BF16)) * jnp.exp(a_e)
        a_last = a_e[CHUNK - 1:CHUNK]
        state = state * jnp.exp(a_last) + _dot_tn(b_c.astype(BF16), (xdt * jnp.exp(a_last - a_e)).astype(BF16))
        y = y + dskip_e * x_c
        o_ref[rs, :] = (y * _silu(z_ref[rs, :])).astype(o_ref.dtype)
    s_ref[...] = state


def _ssd(proj, dt_raw, conv_w, conv_b, par, *, rows=256):
    t = proj.shape[0]
    rows = min(rows, t)
    gw = SSM_GROUP_W
    d_inner = SSM_GROUPS * gw
    xo = d_inner // gw
    bo = (2 * d_inner) // LANES
    co = bo + SSM_GROUPS * SSM_STATE // LANES
    hrow = lambda i: jnp.maximum(i * (rows // 8) - 1, 0)
    cb_x = d_inner // gw
    return pl.pallas_call(
        functools.partial(_ssd_body, rows=rows),
        grid=(SSM_GROUPS, t // rows),
        in_specs=[pl.BlockSpec((rows, gw), lambda g, i: (i, xo + g)),
                  pl.BlockSpec((rows, LANES), lambda g, i: (i, bo + g)),
                  pl.BlockSpec((rows, LANES), lambda g, i: (i, co + g)),
                  pl.BlockSpec((8, gw), lambda g, i: (hrow(i), xo + g)),
                  pl.BlockSpec((8, LANES), lambda g, i: (hrow(i), bo + g)),
                  pl.BlockSpec((8, LANES), lambda g, i: (hrow(i), co + g)),
                  pl.BlockSpec((rows, gw), lambda g, i: (i, g)),
                  pl.BlockSpec((rows, LANES), lambda g, i: (i, 0)),
                  pl.BlockSpec((4, gw), lambda g, i: (0, g)),
                  pl.BlockSpec((4, LANES), lambda g, i: (0, cb_x * (gw // LANES) + g)),
                  pl.BlockSpec((4, LANES), lambda g, i: (0, cb_x * (gw // LANES) + SSM_GROUPS + g)),
                  pl.BlockSpec((1, gw), lambda g, i: (0, g)),
                  pl.BlockSpec((1, LANES), lambda g, i: (0, cb_x * (gw // LANES) + g)),
                  pl.BlockSpec((1, LANES), lambda g, i: (0, cb_x * (gw // LANES) + SSM_GROUPS + g)),
                  pl.BlockSpec((8, LANES), lambda g, i: (0, 0))],
        out_specs=pl.BlockSpec((rows, gw), lambda g, i: (i, g)),
        out_shape=jax.ShapeDtypeStruct((t, d_inner), BF16),
        scratch_shapes=[pltpu.VMEM((SSM_STATE, gw), F32)],
        compiler_params=_cparams(("arbitrary", "arbitrary")),
        name="ssd",
    )(proj, proj, proj, proj, proj, proj, proj, dt_raw, conv_w, conv_w, conv_w, conv_b, conv_b, conv_b, par)


def _lane_pick(lane, idx, vals):
    return jnp.sum(jnp.where(lane == idx, vals, 0.0), axis=-1, keepdims=True)


def _router_body(x_ref, g_ref, sc_ref, sh_ref, w_ref, b_ref, hp_ref, route_ref, cnt_ref, carry_ref, *, tm):
    i = pl.program_id(0)

    @pl.when(i == 0)
    def _():
        carry_ref[...] = jnp.zeros_like(carry_ref)

    h = _adaln_rows(x_ref[...], g_ref[...], sc_ref[...], sh_ref[...])
    logits = _dot(h, w_ref[...], _HI) + b_ref[...]
    lane = _iota((tm, LANES), 1).astype(F32)
    neg = -jnp.inf
    gl = jnp.where(lane < N_GROUPS, logits, neg)
    gmax = jnp.max(gl, axis=-1, keepdims=True)
    grp_p = 1.0 / jnp.sum(jnp.exp(gl - gmax), axis=-1, keepdims=True)
    gidx = jnp.min(jnp.where(gl == gmax, lane, float(LANES)), axis=-1, keepdims=True)
    lo = N_GROUPS + EXPERTS_PER_GROUP * gidx
    el = jnp.where((lane >= lo) & (lane < lo + EXPERTS_PER_GROUP), logits, neg)
    m1 = jnp.max(el, axis=-1, keepdims=True)
    i1 = jnp.min(jnp.where(el == m1, lane, float(LANES)), axis=-1, keepdims=True)
    el2 = jnp.where(lane == i1, neg, el)
    m2 = jnp.max(el2, axis=-1, keepdims=True)
    i2 = jnp.min(jnp.where(el2 == m2, lane, float(LANES)), axis=-1, keepdims=True)
    tt = jnp.exp(m2 - m1)
    g1 = grp_p / (1.0 + tt)
    g2 = g1 * tt
    e1 = i1 - N_GROUPS
    e2 = i2 - N_GROUPS

    onehot = ((lane == e1) | (lane == e2)).astype(F32)
    strict = (_iota((tm, tm), 1) < _iota((tm, tm), 0)).astype(BF16)
    rank = _dot(strict, onehot.astype(BF16)) + carry_ref[0:1, :]
    r1 = _lane_pick(lane, e1, rank)
    r2 = _lane_pick(lane, e2, rank)
    carry_ref[...] = carry_ref[...] + jnp.sum(onehot, axis=0, keepdims=True)
    cnt_ref[...] = carry_ref[...]

    route = jnp.where(lane == 0, e1, jnp.where(lane == 1, e2, jnp.where(lane == 2, r1, jnp.where(
        lane == 3, r2, jnp.where(lane == 4, g1, jnp.where(lane == 5, g2, 0.0))))))
    route_ref[...] = route

    hb = h.astype(BF16).astype(F32)
    bits = lax.bitcast_convert_type(hb, U32)
    half = h.shape[1] // 2
    packed = (bits[:, :half] >> 16) | (bits[:, half:] & jnp.uint32(0xFFFF0000))
    for s in range(TOKEN_TILE_ROWS):
        hp_ref[pl.ds(s, tm, stride=TOKEN_TILE_ROWS), :] = packed[:, s * LANES:(s + 1) * LANES]


def _router(x, g, scale, shift, w_route, b_route, *, tm=512):
    t, d = x.shape
    vec = pl.BlockSpec((1, d), lambda i: (0, 0))
    return pl.pallas_call(
        functools.partial(_router_body, tm=tm),
        grid=(t // tm,),
        in_specs=[pl.BlockSpec((tm, d), lambda i: (i, 0)), vec, vec, vec,
                  pl.BlockSpec((d, LANES), lambda i: (0, 0)),
                  pl.BlockSpec((1, LANES), lambda i: (0, 0))],
        out_specs=[pl.BlockSpec((tm * TOKEN_TILE_ROWS, LANES), lambda i: (i, 0)),
                   pl.BlockSpec((tm, LANES), lambda i: (i, 0)),
                   pl.BlockSpec((8, LANES), lambda i: (0, 0))],
        out_shape=[jax.ShapeDtypeStruct((t * TOKEN_TILE_ROWS, LANES), U32),
                   jax.ShapeDtypeStruct((t, LANES), F32),
                   jax.ShapeDtypeStruct((8, LANES), F32)],
        scratch_shapes=[pltpu.VMEM((8, LANES), F32)],
        compiler_params=_cparams(("arbitrary",)),
        name="moe_router",
    )(x, g, scale, shift, w_route, b_route)


def _dest_body(route_ref, cnt_ref, dest_ref, blk_ref, *, tm, n_blocks_pad):
    cnt = cnt_ref[...]
    padded = jnp.floor((cnt + (MOE_ROWS - 1)) * (1.0 / MOE_ROWS)) * MOE_ROWS
    upper = (_iota((LANES, LANES), 0) <= _iota((LANES, LANES), 1)).astype(F32)
    pad_end = _dot(padded, upper, _HI)
    pad_start = (pad_end - padded)[0:1, :]
    route = route_ref[...]
    lane = _iota((tm, LANES), 1).astype(F32)
    e1, e2, r1, r2 = route[:, 0:1], route[:, 1:2], route[:, 2:3], route[:, 3:4]
    d1 = _lane_pick(lane, e1, pad_start) + r1
    d2 = _lane_pick(lane, e2, pad_start) + r2
    dest_ref[...] = jnp.where(lane == 0, d1, jnp.where(lane == 1, d2, 0.0)).astype(I32)

    lane_b = _iota((n_blocks_pad, LANES), 1)
    first_row = (_iota((n_blocks_pad, LANES), 0) * MOE_ROWS).astype(F32)
    done = ((pad_end[0:1, :] <= first_row) & (lane_b < N_EXPERTS)).astype(F32)
    blk_e = jnp.minimum(jnp.sum(done, axis=-1, keepdims=True), float(N_EXPERTS - 1))
    n_valid = pad_end[0:1, N_EXPERTS - 1:N_EXPERTS] * (1.0 / MOE_ROWS)
    blk_ref[...] = jnp.where(lane_b == 0, blk_e, jnp.where(lane_b == 1, n_valid, 0.0)).astype(I32)


def _dest(route, counts, n_blocks_pad, *, tm=512):
    t = route.shape[0]
    assert t % tm == 0
    return pl.pallas_call(
        functools.partial(_dest_body, tm=tm, n_blocks_pad=n_blocks_pad),
        grid=(t // tm,),
        in_specs=[pl.BlockSpec((tm, LANES), lambda i: (i, 0)),
                  pl.BlockSpec((8, LANES), lambda i: (0, 0))],
        out_specs=[pl.BlockSpec((tm, LANES), lambda i: (i, 0)),
                   pl.BlockSpec((n_blocks_pad, LANES), lambda i: (0, 0))],
        out_shape=[jax.ShapeDtypeStruct((t, LANES), I32),
                   jax.ShapeDtypeStruct((n_blocks_pad, LANES), I32)],
        compiler_params=_cparams(("arbitrary",)),
        name="moe_dest",
    )(route, counts)


def _invert_body(dest_ref, tok_ref, *, n_assign, n_rows):
    def fill(r, c):
        tok_ref[r] = 0
        return c

    lax.fori_loop(0, n_rows, fill, 0, unroll=8)

    def scatter(a, c):
        tok_ref[dest_ref[a]] = a >> 1
        return c

    lax.fori_loop(0, n_assign, scatter, 0, unroll=8)


def _invert(dest_flat, n_rows):
    n_assign = dest_flat.shape[0]
    return pl.pallas_call(
        functools.partial(_invert_body, n_assign=n_assign, n_rows=n_rows),
        grid_spec=pltpu.PrefetchScalarGridSpec(
            num_scalar_prefetch=1,
            grid=(1,),
            in_specs=[],
            out_specs=pl.BlockSpec(memory_space=pltpu.SMEM),
        ),
        out_shape=jax.ShapeDtypeStruct((n_rows,), I32),
        compiler_params=_cparams(("arbitrary",)),
        name="moe_invert",
    )(dest_flat)


def _expert_body(be_ref, nv_ref, tok_ref, hp_ref, w1_ref, w3_ref, w2_ref, y_ref, xbuf_ref, sem, w1b_ref, w3b_ref,
                 w2b_ref):
    b = pl.program_id(0)
    n_valid = nv_ref[0]
    valid = b < n_valid
    tr = TOKEN_TILE_ROWS

    def gather_rows(blk, slot):
        def one(j, c):
            tok = tok_ref[blk * MOE_ROWS + j]
            pltpu.make_async_copy(hp_ref.at[pl.ds(tok * tr, tr)], xbuf_ref.at[slot, pl.ds(j * tr, tr)],
                                  sem.at[slot]).start()
            return c

        lax.fori_loop(0, MOE_ROWS, one, 0, unroll=8)

    @pl.when(b == 0)
    def _():
        gather_rows(0, 0)

    @pl.when(b + 1 < n_valid)
    def _():
        gather_rows(b + 1, (b + 1) % 2)

    @pl.when(valid & ((b == 0) | (be_ref[b] != be_ref[jnp.maximum(b - 1, 0)])))
    def _():
        w1b_ref[...] = w1_ref[...].astype(BF16)
        w3b_ref[...] = w3_ref[...].astype(BF16)
        w2b_ref[...] = w2_ref[...].astype(BF16)

    @pl.when(jnp.logical_not(valid))
    def _():
        y_ref[...] = jnp.zeros_like(y_ref)

    @pl.when(valid)
    def _():
        slot = b % 2
        pltpu.make_async_copy(hp_ref.at[pl.ds(0, MOE_ROWS * tr)], xbuf_ref.at[slot], sem.at[slot]).wait()
        lo, hi = [], []
        for s in range(tr):
            p = xbuf_ref[slot, pl.ds(s, MOE_ROWS, stride=tr), :]
            lo.append(lax.bitcast_convert_type(p << 16, F32))
            hi.append(lax.bitcast_convert_type(p & jnp.uint32(0xFFFF0000), F32))
        x = jnp.concatenate(lo + hi, axis=1).astype(BF16)
        a = (_silu(_dot(x, w1b_ref[...])) * _dot(x, w3b_ref[...])).astype(BF16)
        y = _dot(a, w2b_ref[...])
        for s in range(Y_TILE_ROWS):
            y_ref[pl.ds(s, MOE_ROWS, stride=Y_TILE_ROWS), :] = y[:, s * LANES:(s + 1) * LANES]


def _experts(block_expert, n_valid, tok_rows, h_packed, w1, w3, w2):
    n_blocks = block_expert.shape[0]
    _, d, f = w1.shape
    return pl.pallas_call(
        _expert_body,
        grid_spec=pltpu.PrefetchScalarGridSpec(
            num_scalar_prefetch=3,
            grid=(n_blocks,),
            in_specs=[pl.BlockSpec(memory_space=pl.ANY),
                      pl.BlockSpec((None, d, f), lambda b, be, nv, tk: (be[b], 0, 0)),
                      pl.BlockSpec((None, d, f), lambda b, be, nv, tk: (be[b], 0, 0)),
                      pl.BlockSpec((None, f, d), lambda b, be, nv, tk: (be[b], 0, 0))],
            out_specs=pl.BlockSpec((MOE_ROWS * Y_TILE_ROWS, LANES), lambda b, be, nv, tk: (b, 0)),
            scratch_shapes=[pltpu.VMEM((2, MOE_ROWS * TOKEN_TILE_ROWS, LANES), U32), pltpu.SemaphoreType.DMA((2,)),
                            pltpu.VMEM((d, f), BF16), pltpu.VMEM((d, f), BF16), pltpu.VMEM((f, d), BF16)],
        ),
        out_shape=jax.ShapeDtypeStruct((n_blocks * MOE_ROWS * Y_TILE_ROWS, LANES), F32),
        compiler_params=_cparams(("arbitrary",)),
        name="moe_experts",
    )(block_expert, n_valid, tok_rows, h_packed, w1, w3, w2)


def _combine_body(dest_ref, y_ref, x_ref, route_ref, gate_ref, fg_ref, o_ref, buf_ref, sem, *, tm, final_norm):
    i = pl.program_id(0)
    yr = Y_TILE_ROWS

    def issue(j, c):
        for k in range(2):
            d = dest_ref[(i * tm + j) * 2 + k]
            pltpu.make_async_copy(y_ref.at[pl.ds(d * yr, yr)], buf_ref.at[k, pl.ds(j * yr, yr)], sem.at[k]).start()
        return c

    lax.fori_loop(0, tm, issue, 0)
    for k in range(2):
        pltpu.make_async_copy(y_ref.at[pl.ds(0, tm * yr)], buf_ref.at[k], sem.at[k]).wait()

    route = route_ref[...]
    g1, g2 = route[:, 4:5], route[:, 5:6]
    for s in range(yr):
        cs = slice(s * LANES, (s + 1) * LANES)
        ffn = g1 * buf_ref[0, pl.ds(s, tm, stride=yr), :] + g2 * buf_ref[1, pl.ds(s, tm, stride=yr), :]
        o_ref[:, cs] = x_ref[:, cs] + gate_ref[:, cs] * ffn
    if final_norm:
        xo = o_ref[...]
        o_ref[...] = xo * lax.rsqrt(jnp.mean(xo * xo, axis=-1, keepdims=True) + EPS) * fg_ref[...]


def _combine(dest_flat, y_rows, x, route, gate, final_g, *, final_norm, tm=128):
    t, d = x.shape
    return pl.pallas_call(
        functools.partial(_combine_body, tm=tm, final_norm=final_norm),
        grid_spec=pltpu.PrefetchScalarGridSpec(
            num_scalar_prefetch=1,
            grid=(t // tm,),
            in_specs=[pl.BlockSpec(memory_space=pl.ANY),
                      pl.BlockSpec((tm, d), lambda i, dr: (i, 0)),
                      pl.BlockSpec((tm, LANES), lambda i, dr: (i, 0)),
                      pl.BlockSpec((1, d), lambda i, dr: (0, 0)),
                      pl.BlockSpec((1, d), lambda i, dr: (0, 0))],
            out_specs=pl.BlockSpec((tm, d), lambda i, dr: (i, 0)),
            scratch_shapes=[pltpu.VMEM((2, tm * Y_TILE_ROWS, LANES), F32), pltpu.SemaphoreType.DMA((2,))],
        ),
        out_shape=jax.ShapeDtypeStruct((t, d), F32),
        compiler_params=_cparams(("arbitrary",)),
        name="moe_combine",
    )(dest_flat, y_rows, x, route, gate, final_g)


def _moe(x, g, scale, shift, gate, w_group, b_group, w_expert, b_expert, w1, w3, w2, final_g, *, final_norm):
    t, d = x.shape
    pad = LANES - N_GROUPS - N_EXPERTS
    w_route = jnp.concatenate([w_group, w_expert, jnp.zeros((d, pad), F32)], axis=1)
    b_route = jnp.concatenate([b_group, b_expert, jnp.zeros((pad,), F32)]).reshape(1, LANES)
    h_packed, route, counts = _router(x, g, scale, shift, w_route, b_route)
    n_assign = 2 * t
    n_blocks = -(-(n_assign + N_EXPERTS * (MOE_ROWS - 1)) // MOE_ROWS)
    n_blocks_pad = -(-n_blocks // 8) * 8
    dest, blk = _dest(route, counts, n_blocks_pad)
    dest_flat = dest[:, :2].reshape(n_assign)
    block_expert = blk[:n_blocks, 0]
    n_valid = blk[0, 1:2]
    tok_rows = _invert(dest_flat, n_blocks * MOE_ROWS)
    y_rows = _experts(block_expert, n_valid, tok_rows, h_packed, w1, w3, w2)
    return _combine(dest_flat, y_rows, x, route, gate, final_g, final_norm=final_norm)


def _pad_cols(w, width):
    return jnp.pad(w, ((0, 0), (0, width - w.shape[1])))


def _lane_row(v, offset):
    return jnp.zeros((LANES,), F32).at[offset:offset + v.shape[0]].set(v)


def kernel(x, c, w_mod, b_mod, norm_g, final_norm_g, e_w_in, e_conv_qkv, e_a_log, e_dt_bias, e_head_norm_g, e_conf_dw, e_conf_dw_b, e_conf_ln_g, e_conf_ln_b, e_w_out, o_w_in, o_conv_w, o_conv_b, o_dt_bias, o_a_log, o_d_skip, o_norm_g, o_w_out, moe_w_group, moe_b_group, moe_w_expert, moe_b_expert, moe_w1, moe_w3, moe_w2):
    bsz, t, d = x.shape
    assert bsz == 1 and c.shape == (1, d)
    xt = x[0]
    mod = _modulation(c[0], w_mod, b_mod)

    def mod_parts(l):
        return mod[l, :, :d], mod[l, :, d:2 * d], mod[l, :, 2 * d:]

    row = lambda v: v.reshape(1, -1)
    qkv_w = 3 * GDN_HEADS * HEAD_W
    z_end = qkv_w + GDN_HEADS * HEAD_W

    shift, scale, gate = mod_parts(0)
    w_in = e_w_in[0]
    tn = 512
    w_glu = w_in[:, z_end + 2 * GDN_HEADS:]
    w_ba = _pad_cols(w_in[:, z_end:z_end + 2 * GDN_HEADS], LANES)
    proj0, ba = _norm_matmul(xt, row(norm_g[0, 0]), scale, shift,
                             [(w_in, z_end // tn), (w_glu, w_glu.shape[1] // tn)], w_ba, tn=tn)
    gdn_par = jnp.zeros((8, LANES), F32).at[0].set(_lane_row(e_a_log[0], GDN_HEADS)).at[1].set(
        _lane_row(e_dt_bias[0], GDN_HEADS))
    a_out = _gdn(proj0, ba, e_conv_qkv[0], gdn_par, row(e_head_norm_g[0]))
    b_out = _conformer(proj0, e_conf_dw[0], row(e_conf_dw_b[0]), row(e_conf_ln_g[0]), row(e_conf_ln_b[0]))
    xt = _out_proj0(a_out, b_out, e_w_out[0], xt, gate)
    shift, scale, gate = mod_parts(1)
    xt = _moe(xt, row(norm_g[0, 1]), scale, shift, gate, moe_w_group[0], moe_b_group[0], moe_w_expert[0],
              moe_b_expert[0], moe_w1[0], moe_w3[0], moe_w2[0], row(final_norm_g), final_norm=False)

    shift, scale, gate = mod_parts(2)
    w_in = o_w_in[0]
    n_main = 2 * SSM_GROUPS * SSM_GROUP_W + 2 * SSM_GROUPS * SSM_STATE
    proj1, dt_raw = _norm_matmul(xt, row(norm_g[1, 0]), scale, shift, [(w_in, n_main // tn)],
                                 _pad_cols(w_in[:, n_main:], LANES), tn=tn)
    ssd_par = jnp.zeros((8, LANES), F32).at[0].set(_lane_row(o_dt_bias[0], 0)).at[1].set(
        _lane_row(o_a_log[0], 0)).at[2].set(_lane_row(o_d_skip[0], 0))
    y = _ssd(proj1, dt_raw, o_conv_w[0], row(o_conv_b[0]), ssd_par)
    xt = _out_proj1(y, row(o_norm_g[0]), o_w_out[0], xt, gate)
    shift, scale, gate = mod_parts(3)
    xt = _moe(xt, row(norm_g[1, 1]), scale, shift, gate, moe_w_group[1], moe_b_group[1], moe_w_expert[1],
              moe_b_expert[1], moe_w1[1], moe_w3[1], moe_w2[1], row(final_norm_g), final_norm=True)
    return xt[None]
```

```python
import functools

import jax
import jax.numpy as jnp
from jax import lax
from jax.experimental import pallas as pl
from jax.experimental.pallas import tpu as pltpu

F32 = jnp.float32
BF16 = jnp.bfloat16
I32 = jnp.int32
U32 = jnp.uint32

EPS = 1e-6
LANES = 128
CHUNK = 64
CHUNK_LOG2 = 6
GDN_HEADS = 8
HEAD_W = 128
CONF_CH = 1024
CONF_K = 31
SSM_GROUPS = 8
SSM_GROUP_W = 512
SSM_HEADDIM = 64
SSM_STATE = 128
N_EXPERTS = 32
EXPERTS_PER_GROUP = 8
N_GROUPS = 4
MOE_ROWS = 256
TOKEN_TILE_ROWS = 8
Y_TILE_ROWS = 16
VMEM_LIMIT = 56 * 1024 * 1024
PROLOGUE_ROWS = 128

_HI = lax.Precision.HIGHEST


def _cparams(sem):
    return pltpu.CompilerParams(dimension_semantics=sem, vmem_limit_bytes=VMEM_LIMIT)


def _dot(a, b, precision=None):
    return jnp.dot(a, b, precision=precision, preferred_element_type=F32)


def _dot_nt(a, b, precision=None):
    return lax.dot_general(a, b, (((1,), (1,)), ((), ())), precision=precision, preferred_element_type=F32)


def _dot_tn(a, b, precision=None):
    return lax.dot_general(a, b, (((0,), (0,)), ((), ())), precision=precision, preferred_element_type=F32)


def _split_bf16(x):
    hi = x.astype(BF16)
    return hi, (x - hi.astype(F32)).astype(BF16)


def _dot3(a, b):
    ah, al = _split_bf16(a)
    bh, bl = _split_bf16(b)
    return _dot(ah, bh) + (_dot(ah, bl) + _dot(al, bh))


def _select_dot(sel, x, left=False):
    p1 = x.astype(BF16)
    r1 = x - p1.astype(F32)
    p2 = r1.astype(BF16)
    p3 = (r1 - p2.astype(F32)).astype(BF16)
    if left:
        return _dot(sel, p1) + (_dot(sel, p2) + _dot(sel, p3))
    return _dot(p1, sel) + (_dot(p2, sel) + _dot(p3, sel))


def _silu(x):
    return x * jax.nn.sigmoid(x)


def _softplus(x):
    return jnp.maximum(x, 0.0) + jnp.log1p(jnp.exp(-jnp.abs(x)))


def _iota(shape, dim):
    return lax.broadcasted_iota(I32, shape, dim)


def _mod_body(c_ref, w_ref, b_ref, o_ref):
    c = c_ref[...]
    o_ref[...] = jnp.sum(w_ref[...] * _silu(c), axis=0, keepdims=True) + b_ref[...]


def _modulation(c, w_mod, b_mod):
    nl, d, n = w_mod.shape
    tn = 768
    return pl.pallas_call(
        _mod_body,
        grid=(nl, n // tn),
        in_specs=[
            pl.BlockSpec((d, 1), lambda l, j: (0, 0)),
            pl.BlockSpec((None, d, tn), lambda l, j: (l, 0, j)),
            pl.BlockSpec((None, 1, tn), lambda l, j: (l, 0, j)),
        ],
        out_specs=pl.BlockSpec((None, 1, tn), lambda l, j: (l, 0, j)),
        out_shape=jax.ShapeDtypeStruct((nl, 1, n), F32),
        compiler_params=_cparams(("arbitrary", "arbitrary")),
        name="adaln_mod",
    )(c.reshape(d, 1), w_mod, b_mod.reshape(nl, 1, n))


def _adaln_rows(x, g, scale, shift):
    r = lax.rsqrt(jnp.mean(x * x, axis=-1, keepdims=True) + EPS)
    return x * r * (g * (1.0 + scale)) + shift


def _norm_mm_body(*refs, starts):
    x_ref, g_ref, sc_ref, sh_ref, ws_ref = refs[:5]
    w_refs = refs[5:5 + len(starts)]
    o_ref, os_ref, h_ref = refs[5 + len(starts):]
    j = pl.program_id(1)

    @pl.when(j == 0)
    def _():
        ws = ws_ref[...].astype(BF16)

        def norm_rows(c, carry):
            rs = pl.ds(pl.multiple_of(c * PROLOGUE_ROWS, PROLOGUE_ROWS), PROLOGUE_ROWS)
            h = _adaln_rows(x_ref[rs, :], g_ref[...], sc_ref[...], sh_ref[...]).astype(BF16)
            h_ref[rs, :] = h
            os_ref[rs, :] = _dot_nt(h, ws)
            return carry

        lax.fori_loop(0, x_ref.shape[0] // PROLOGUE_ROWS, norm_rows, 0)

    bounds = list(starts[1:]) + [None]
    for w_ref, lo, hi in zip(w_refs, starts, bounds):
        in_seg = (j >= lo) if hi is None else ((j >= lo) & (j < hi))

        @pl.when(in_seg)
        def _(w_ref=w_ref):
            o_ref[...] = _dot_nt(h_ref[...], w_ref[...].astype(BF16))


def _norm_matmul(x, g, scale, shift, segments, w_small, *, tm=1024, tn=512):
    t, d = x.shape
    tm = min(tm, t)
    starts, specs, n_main = [], [], 0
    for w, nb in segments:
        lo = n_main // tn
        starts.append(lo)
        specs.append(pl.BlockSpec((tn, d), lambda i, j, lo=lo, nb=nb: (jnp.clip(j - lo, 0, nb - 1), 0)))
        n_main += nb * tn
    vec = pl.BlockSpec((1, d), lambda i, j: (0, 0))
    return pl.pallas_call(
        functools.partial(_norm_mm_body, starts=tuple(starts)),
        grid=(t // tm, n_main // tn),
        in_specs=[pl.BlockSpec((tm, d), lambda i, j: (i, 0)), vec, vec, vec,
                  pl.BlockSpec((LANES, d), lambda i, j: (0, 0))] + specs,
        out_specs=[pl.BlockSpec((tm, tn), lambda i, j: (i, j)),
                   pl.BlockSpec((tm, LANES), lambda i, j: (i, 0))],
        out_shape=[jax.ShapeDtypeStruct((t, n_main), F32), jax.ShapeDtypeStruct((t, LANES), F32)],
        scratch_shapes=[pltpu.VMEM((tm, d), BF16)],
        compiler_params=_cparams(("arbitrary", "arbitrary")),
        name="adaln_in_proj",
    )(x, g, scale, shift, w_small, *[w for w, _ in segments])


def _causal_conv(halo, x, w, taps):
    hr = halo.shape[0]
    n = x.shape[0]
    xe = jnp.concatenate([halo, x], axis=0)
    rolled = {0: xe}
    acc = None
    for j in range(taps):
        s = taps - 1 - j
        a, b = divmod(s, 8)
        if b not in rolled:
            rolled[b] = pltpu.roll(xe, b, axis=0)
        term = rolled[b][hr - 8 * a:hr - 8 * a + n] * w[j:j + 1]
        acc = term if acc is None else acc + term
    return acc


def _unit_lower_inverses(a_mats, row, col):
    n = a_mats[0].shape[0]
    eye = (row == col).astype(F32)
    level = ((row >> 1) == (col >> 1)) & (col < row)
    invs = [eye - jnp.where(level, a, 0.0) for a in a_mats]
    k = 1
    while (1 << k) < n:
        level = ((row >> (k + 1)) == (col >> (k + 1))) & (((row >> k) & 1) == 1) & (((col >> k) & 1) == 0)
        splits = [_split_bf16(inv) for inv in invs]
        ys = []
        for (ih, il), a in zip(splits, a_mats):
            mh, ml = _split_bf16(jnp.where(level, a, 0.0))
            ys.append(_dot(ih, mh) + (_dot(ih, ml) + _dot(il, mh)))
        new = []
        for inv, (ih, il), y in zip(invs, splits, ys):
            yh, yl = _split_bf16(y)
            new.append(inv - (_dot(yh, ih) + (_dot(yh, il) + _dot(yl, ih))))
        invs = new
        k += 1
    return invs


def _gdn_body(q_ref, k_ref, v_ref, qh_ref, kh_ref, vh_ref, z_ref, ba_ref, cwq_ref, cwk_ref, cwv_ref,
              par_ref, hg_ref, o_ref, s_ref, *, hb, rows):
    hblk = pl.program_id(0)
    ir = pl.program_id(1)

    @pl.when(ir == 0)
    def _():
        s_ref[...] = jnp.zeros_like(s_ref)

    def conv_silu(x_ref, halo_ref, w_ref):
        halo = jnp.where(ir > 0, halo_ref[...], 0.0)
        return _silu(_causal_conv(halo, x_ref[...], w_ref[...], 4))

    qc = conv_silu(q_ref, qh_ref, cwq_ref)
    kc = conv_silu(k_ref, kh_ref, cwk_ref)
    vc = conv_silu(v_ref, vh_ref, cwv_ref)

    ba = ba_ref[...]
    beta_all = jax.nn.sigmoid(ba)
    g_all = -jnp.exp(par_ref[0:1, :]) * _softplus(ba + par_ref[1:2, :])
    rr = _iota((rows, rows), 0)
    cc = _iota((rows, rows), 1)
    chunk_tril = ((rr >> CHUNK_LOG2) == (cc >> CHUNK_LOG2)) & (cc <= rr)
    gcs_all = _select_dot(chunk_tril.astype(BF16), g_all, left=True)
    gcs_t = gcs_all.T

    row = _iota((CHUNK, CHUNK), 0)
    col = _iota((CHUNK, CHUNK), 1)
    scale = HEAD_W ** -0.5
    hg = hg_ref[...]
    bf = lambda a: a.astype(BF16)
    lane = _iota((rows, LANES), 1)
    sub = _iota((LANES, rows), 0)
    nc = rows // CHUNK

    heads = []
    for h in range(hb):
        hs = slice(h * HEAD_W, (h + 1) * HEAD_W)
        head = hblk * hb + h
        q = qc[:, hs]
        k = kc[:, hs]
        heads.append(dict(
            q=q * lax.rsqrt(jnp.sum(q * q, axis=-1, keepdims=True) + EPS) * scale,
            k=k * lax.rsqrt(jnp.sum(k * k, axis=-1, keepdims=True) + EPS),
            v=vc[:, hs],
            beta=jnp.sum(jnp.where(lane == head, beta_all, 0.0), axis=-1, keepdims=True),
            gcs=jnp.sum(jnp.where(lane == head + GDN_HEADS, gcs_all, 0.0), axis=-1, keepdims=True),
            gcs_row=jnp.sum(jnp.where(sub == head + GDN_HEADS, gcs_t, 0.0), axis=0, keepdims=True)))

    items = []
    for c in range(nc):
        rs = slice(c * CHUNK, (c + 1) * CHUNK)
        for hd in heads:
            qq, kk, vv, beta, gcs = hd["q"][rs], hd["k"][rs], hd["v"][rs], hd["beta"][rs], hd["gcs"][rs]
            decay = jnp.where(col <= row, jnp.exp(jnp.minimum(gcs - hd["gcs_row"][:, rs], 0.0)), 0.0)
            kb = kk * beta
            eg = jnp.exp(gcs)
            g_last = gcs[CHUNK - 1:CHUNK]
            items.append(dict(decay=decay, kb16=bf(kb), k16=bf(kk), q16=bf(qq),
                              rhs16=bf(jnp.concatenate([vv * beta, kb * eg], axis=-1)),
                              q_dec16=bf(qq * eg), k_dec16=bf(kk * jnp.exp(g_last - gcs)), cd=jnp.exp(g_last)))
    a_mats = [jnp.where(col < row, _dot_nt(it["kb16"], it["k16"]) * it["decay"], 0.0) for it in items]
    qks = [bf(_dot_nt(it["q16"], it["k16"]) * it["decay"]) for it in items]
    t_invs = _unit_lower_inverses(a_mats, row, col)
    sols = [_dot(bf(t), it["rhs16"]) for t, it in zip(t_invs, items)]

    states = [s_ref[h] for h in range(hb)]
    for c in range(nc):
        rs = slice(c * CHUNK, (c + 1) * CHUNK)
        cur = slice(c * hb, (c + 1) * hb)
        s16 = [bf(s) for s in states]
        us = [so[:, :HEAD_W] - _dot(bf(so[:, HEAD_W:]), s) for so, s in zip(sols[cur], s16)]
        outs = [_dot(it["q_dec16"], s) + _dot(qk, bf(u)) for it, s, qk, u in zip(items[cur], s16, qks[cur], us)]
        states = [s * it["cd"] + _dot_tn(it["k_dec16"], bf(u)) for s, it, u in zip(states, items[cur], us)]
        for h, o in enumerate(outs):
            hs = slice(h * HEAD_W, (h + 1) * HEAD_W)
            o = o * lax.rsqrt(jnp.mean(o * o, axis=-1, keepdims=True) + EPS) * hg * _silu(z_ref[rs, hs])
            o_ref[rs, hs] = o.astype(o_ref.dtype)
    for h in range(hb):
        s_ref[h] = states[h]


def _gdn(proj, ba, conv_w, par, head_g, *, hb=GDN_HEADS, rows=256):
    t = proj.shape[0]
    rows = min(rows, t)
    w = hb * HEAD_W
    per = (GDN_HEADS * HEAD_W) // w

    def sec(k):
        return pl.BlockSpec((rows, w), lambda h, i, k=k: (i, k * per + h))

    def halo(k):
        return pl.BlockSpec((8, w), lambda h, i, k=k: (jnp.maximum(i * (rows // 8) - 1, 0), k * per + h))

    def cw(k):
        return pl.BlockSpec((4, w), lambda h, i, k=k: (0, k * per + h))

    return pl.pallas_call(
        functools.partial(_gdn_body, hb=hb, rows=rows),
        grid=(GDN_HEADS // hb, t // rows),
        in_specs=[sec(0), sec(1), sec(2), halo(0), halo(1), halo(2), sec(3),
                  pl.BlockSpec((rows, LANES), lambda h, i: (i, 0)),
                  cw(0), cw(1), cw(2),
                  pl.BlockSpec((8, LANES), lambda h, i: (0, 0)),
                  pl.BlockSpec((1, HEAD_W), lambda h, i: (0, 0))],
        out_specs=pl.BlockSpec((rows, w), lambda h, i: (i, h)),
        out_shape=jax.ShapeDtypeStruct((t, GDN_HEADS * HEAD_W), BF16),
        scratch_shapes=[pltpu.VMEM((hb, HEAD_W, HEAD_W), F32)],
        compiler_params=_cparams(("arbitrary", "arbitrary")),
        name="gated_deltanet",
    )(proj, proj, proj, proj, proj, proj, proj, ba, conv_w, conv_w, conv_w, par, head_g)


def _conf_body(x_ref, halo_ref, w_ref, b_ref, g_ref, lb_ref, o_ref):
    ir = pl.program_id(0)

    def glu(v):
        return v[:, :CONF_CH] * jax.nn.sigmoid(v[:, CONF_CH:])

    halo = jnp.where(ir > 0, glu(halo_ref[...]), 0.0)
    u = _causal_conv(halo, glu(x_ref[...]), w_ref[...], CONF_K) + b_ref[...]
    mu = jnp.mean(u, axis=-1, keepdims=True)
    uc = u - mu
    var = jnp.mean(uc * uc, axis=-1, keepdims=True)
    y = uc * lax.rsqrt(var + EPS) * g_ref[...] + lb_ref[...]
    o_ref[...] = _silu(y).astype(o_ref.dtype)


def _conformer(proj, dw, dw_b, ln_g, ln_b, *, rows=256):
    t = proj.shape[0]
    glu_blk = (4 * GDN_HEADS * HEAD_W) // (2 * CONF_CH)
    halo_rows = 32
    vec = pl.BlockSpec((1, CONF_CH), lambda i: (0, 0))
    return pl.pallas_call(
        _conf_body,
        grid=(t // rows,),
        in_specs=[pl.BlockSpec((rows, 2 * CONF_CH), lambda i: (i, glu_blk)),
                  pl.BlockSpec((halo_rows, 2 * CONF_CH),
                               lambda i: (jnp.maximum(i * (rows // halo_rows) - 1, 0), glu_blk)),
                  pl.BlockSpec((CONF_K, CONF_CH), lambda i: (0, 0)), vec, vec, vec],
        out_specs=pl.BlockSpec((rows, CONF_CH), lambda i: (i, 0)),
        out_shape=jax.ShapeDtypeStruct((t, CONF_CH), BF16),
        compiler_params=_cparams(("arbitrary",)),
        name="conformer_conv",
    )(proj, proj, dw, dw_b, ln_g, ln_b)


def _out0_body(a_ref, b_ref, wa_ref, wb_ref, x_ref, gate_ref, o_ref):
    mix = _dot(a_ref[...], wa_ref[...].astype(BF16)) + _dot(b_ref[...], wb_ref[...].astype(BF16))
    o_ref[...] = x_ref[...] + gate_ref[...] * mix


def _out_proj0(a, b, w, x, gate, *, tm=1024, tn=1024):
    t, d = x.shape
    tm = min(tm, t)
    ka, kb = a.shape[1], b.shape[1]
    return pl.pallas_call(
        _out0_body,
        grid=(t // tm, d // tn),
        in_specs=[pl.BlockSpec((tm, ka), lambda i, j: (i, 0)),
                  pl.BlockSpec((tm, kb), lambda i, j: (i, 0)),
                  pl.BlockSpec((ka, tn), lambda i, j: (0, j)),
                  pl.BlockSpec((kb, tn), lambda i, j: (ka // kb, j)),
                  pl.BlockSpec((tm, tn), lambda i, j: (i, j)),
                  pl.BlockSpec((1, tn), lambda i, j: (0, j))],
        out_specs=pl.BlockSpec((tm, tn), lambda i, j: (i, j)),
        out_shape=jax.ShapeDtypeStruct((t, d), F32),
        compiler_params=_cparams(("arbitrary", "arbitrary")),
        name="out_proj_even",
    )(a, b, w, w, x, gate)


def _out1_body(y_ref, g_ref, w_ref, x_ref, gate_ref, o_ref, h_ref):
    @pl.when(pl.program_id(1) == 0)
    def _():
        def norm_rows(c, carry):
            rs = pl.ds(pl.multiple_of(c * PROLOGUE_ROWS, PROLOGUE_ROWS), PROLOGUE_ROWS)
            y = y_ref[rs, :].astype(F32)
            r = lax.rsqrt(jnp.mean(y * y, axis=-1, keepdims=True) + EPS)
            h_ref[rs, :] = (y * r * g_ref[...]).astype(BF16)
            return carry

        lax.fori_loop(0, y_ref.shape[0] // PROLOGUE_ROWS, norm_rows, 0)

    o_ref[...] = x_ref[...] + gate_ref[...] * _dot(h_ref[...], w_ref[...].astype(BF16))


def _out_proj1(y, norm_g, w, x, gate, *, tm=1024, tn=512):
    t, d = x.shape
    tm = min(tm, t)
    k = y.shape[1]
    return pl.pallas_call(
        _out1_body,
        grid=(t // tm, d // tn),
        in_specs=[pl.BlockSpec((tm, k), lambda i, j: (i, 0)),
                  pl.BlockSpec((1, k), lambda i, j: (0, 0)),
                  pl.BlockSpec((k, tn), lambda i, j: (0, j)),
                  pl.BlockSpec((tm, tn), lambda i, j: (i, j)),
                  pl.BlockSpec((1, tn), lambda i, j: (0, j))],
        out_specs=pl.BlockSpec((tm, tn), lambda i, j: (i, j)),
        out_shape=jax.ShapeDtypeStruct((t, d), F32),
        scratch_shapes=[pltpu.VMEM((tm, k), BF16)],
        compiler_params=_cparams(("arbitrary", "arbitrary")),
        name="out_proj_odd",
    )(y, norm_g, w, x, gate)


def _ssd_body(x_ref, b_ref, c_ref, xh_ref, bh_ref, ch_ref, z_ref, dt_ref, cwx_ref, cwb_ref, cwc_ref,
              cbx_ref, cbb_ref, cbc_ref, par_ref, o_ref, s_ref, *, rows):
    grp = pl.program_id(0)
    ir = pl.program_id(1)
    gw = SSM_GROUP_W
    hpg = gw // SSM_HEADDIM

    @pl.when(ir == 0)
    def _():
        s_ref[...] = jnp.zeros_like(s_ref)

    def conv_silu(x_r, halo_r, w_r, bias_r):
        halo = jnp.where(ir > 0, halo_r[...], 0.0)
        return _silu(_causal_conv(halo, x_r[...], w_r[...], 4) + bias_r[...])

    xs = conv_silu(x_ref, xh_ref, cwx_ref, cbx_ref)
    bm = conv_silu(b_ref, bh_ref, cwb_ref, cbb_ref)
    cm = conv_silu(c_ref, ch_ref, cwc_ref, cbc_ref)

    dt = _softplus(dt_ref[...] + par_ref[0:1, :])
    da = dt * (-jnp.exp(par_ref[1:2, :]))
    rr = _iota((rows, rows), 0)
    cc = _iota((rows, rows), 1)
    chunk_tril = ((rr >> CHUNK_LOG2) == (cc >> CHUNK_LOG2)) & (cc <= rr)
    acs = _select_dot(chunk_tril.astype(BF16), da, left=True)

    sel = (_iota((LANES, gw), 0) == grp * hpg + (_iota((LANES, gw), 1) >> CHUNK_LOG2)).astype(BF16)
    dt_e = _select_dot(sel, dt)
    acs_e = _select_dot(sel, acs)
    dskip_e = _select_dot(sel, par_ref[...])[2:3, :]

    row = _iota((CHUNK, gw), 0)
    colm = _iota((CHUNK, gw), 1) & (CHUNK - 1)
    tiled_eye = row == colm
    causal = colm <= row
    ones_cc = jnp.ones((CHUNK, CHUNK), BF16)
    blockdiag = (_iota((gw, gw), 0) >> CHUNK_LOG2) == (_iota((gw, gw), 1) >> CHUNK_LOG2)
    bf = lambda a: a.astype(BF16)
    nc = rows // CHUNK
    chunks = [slice(c * CHUNK, (c + 1) * CHUNK) for c in range(nc)]

    a_es = [acs_e[rs] for rs in chunks]
    a_rows = [_select_dot(ones_cc, jnp.where(tiled_eye, a_e, 0.0), left=True) for a_e in a_es]
    l_cats = [jnp.where(causal, jnp.exp(jnp.minimum(a_e - a_row, 0.0)), 0.0) for a_e, a_row in zip(a_es, a_rows)]
    c16 = [bf(cm[rs]) for rs in chunks]
    b16 = [bf(bm[rs]) for rs in chunks]
    cbs = [_dot_nt(c_c, jnp.concatenate([b_c] * hpg, axis=0)) for c_c, b_c in zip(c16, b16)]
    xdts = [xs[rs] * dt_e[rs] for rs in chunks]
    bds = [bf(jnp.where(blockdiag, jnp.concatenate([xdt] * hpg, axis=0), 0.0)) for xdt in xdts]
    y_diags = [_dot(bf(cb * l_cat), bd) for cb, l_cat, bd in zip(cbs, l_cats, bds)]
    a_lasts = [a_e[CHUNK - 1:CHUNK] for a_e in a_es]
    updates = [_dot_tn(b_c, bf(xdt * jnp.exp(a_last - a_e)))
               for b_c, xdt, a_last, a_e in zip(b16, xdts, a_lasts, a_es)]

    states = [s_ref[...]]
    for a_last, upd in zip(a_lasts, updates):
        states.append(states[-1] * jnp.exp(a_last) + upd)
    s_ref[...] = states[-1]

    y_offs = [_dot(c_c, bf(s)) * jnp.exp(a_e) for c_c, s, a_e in zip(c16, states[:-1], a_es)]
    for rs, y_diag, y_off in zip(chunks, y_diags, y_offs):
        y = y_diag + y_off + dskip_e * xs[rs]
        o_ref[rs, :] = (y * _silu(z_ref[rs, :])).astype(o_ref.dtype)


def _ssd(proj, dt_raw, conv_w, conv_b, par, *, rows=256):
    t = proj.shape[0]
    rows = min(rows, t)
    gw = SSM_GROUP_W
    d_inner = SSM_GROUPS * gw
    xo = d_inner // gw
    bo = (2 * d_inner) // LANES
    co = bo + SSM_GROUPS * SSM_STATE // LANES
    hrow = lambda i: jnp.maximum(i * (rows // 8) - 1, 0)
    cb_x = d_inner // gw
    return pl.pallas_call(
        functools.partial(_ssd_body, rows=rows),
        grid=(SSM_GROUPS, t // rows),
        in_specs=[pl.BlockSpec((rows, gw), lambda g, i: (i, xo + g)),
                  pl.BlockSpec((rows, LANES), lambda g, i: (i, bo + g)),
                  pl.BlockSpec((rows, LANES), lambda g, i: (i, co + g)),
                  pl.BlockSpec((8, gw), lambda g, i: (hrow(i), xo + g)),
                  pl.BlockSpec((8, LANES), lambda g, i: (hrow(i), bo + g)),
                  pl.BlockSpec((8, LANES), lambda g, i: (hrow(i), co + g)),
                  pl.BlockSpec((rows, gw), lambda g, i: (i, g)),
                  pl.BlockSpec((rows, LANES), lambda g, i: (i, 0)),
                  pl.BlockSpec((4, gw), lambda g, i: (0, g)),
                  pl.BlockSpec((4, LANES), lambda g, i: (0, cb_x * (gw // LANES) + g)),
                  pl.BlockSpec((4, LANES), lambda g, i: (0, cb_x * (gw // LANES) + SSM_GROUPS + g)),
                  pl.BlockSpec((1, gw), lambda g, i: (0, g)),
                  pl.BlockSpec((1, LANES), lambda g, i: (0, cb_x * (gw // LANES) + g)),
                  pl.BlockSpec((1, LANES), lambda g, i: (0, cb_x * (gw // LANES) + SSM_GROUPS + g)),
                  pl.BlockSpec((8, LANES), lambda g, i: (0, 0))],
        out_specs=pl.BlockSpec((rows, gw), lambda g, i: (i, g)),
        out_shape=jax.ShapeDtypeStruct((t, d_inner), BF16),
        scratch_shapes=[pltpu.VMEM((SSM_STATE, gw), F32)],
        compiler_params=_cparams(("arbitrary", "arbitrary")),
        name="ssd",
    )(proj, proj, proj, proj, proj, proj, proj, dt_raw, conv_w, conv_w, conv_w, conv_b, conv_b, conv_b, par)


def _lane_pick(lane, idx, vals):
    return jnp.sum(jnp.where(lane == idx, vals, 0.0), axis=-1, keepdims=True)


def _router_body(x_ref, g_ref, sc_ref, sh_ref, w_ref, b_ref, hp_ref, route_ref, cnt_ref, carry_ref, *, tm):
    i = pl.program_id(0)

    @pl.when(i == 0)
    def _():
        carry_ref[...] = jnp.zeros_like(carry_ref)

    h = _adaln_rows(x_ref[...], g_ref[...], sc_ref[...], sh_ref[...])
    logits = _dot3(h, w_ref[...]) + b_ref[...]
    lane = _iota((tm, LANES), 1).astype(F32)
    neg = -jnp.inf
    gl = jnp.where(lane < N_GROUPS, logits, neg)
    gmax = jnp.max(gl, axis=-1, keepdims=True)
    grp_p = 1.0 / jnp.sum(jnp.exp(gl - gmax), axis=-1, keepdims=True)
    gidx = jnp.min(jnp.where(gl == gmax, lane, float(LANES)), axis=-1, keepdims=True)
    lo = N_GROUPS + EXPERTS_PER_GROUP * gidx
    el = jnp.where((lane >= lo) & (lane < lo + EXPERTS_PER_GROUP), logits, neg)
    m1 = jnp.max(el, axis=-1, keepdims=True)
    i1 = jnp.min(jnp.where(el == m1, lane, float(LANES)), axis=-1, keepdims=True)
    el2 = jnp.where(lane == i1, neg, el)
    m2 = jnp.max(el2, axis=-1, keepdims=True)
    i2 = jnp.min(jnp.where(el2 == m2, lane, float(LANES)), axis=-1, keepdims=True)
    tt = jnp.exp(m2 - m1)
    g1 = grp_p / (1.0 + tt)
    g2 = g1 * tt
    e1 = i1 - N_GROUPS
    e2 = i2 - N_GROUPS

    onehot = ((lane == e1) | (lane == e2)).astype(F32)
    strict = (_iota((tm, tm), 1) < _iota((tm, tm), 0)).astype(BF16)
    rank = _dot(strict, onehot.astype(BF16)) + carry_ref[0:1, :]
    r1 = _lane_pick(lane, e1, rank)
    r2 = _lane_pick(lane, e2, rank)
    carry_ref[...] = carry_ref[...] + jnp.sum(onehot, axis=0, keepdims=True)
    cnt_ref[...] = carry_ref[...]

    route = jnp.where(lane == 0, e1, jnp.where(lane == 1, e2, jnp.where(lane == 2, r1, jnp.where(
        lane == 3, r2, jnp.where(lane == 4, g1, jnp.where(lane == 5, g2, 0.0))))))
    route_ref[...] = route

    hb = h.astype(BF16).astype(F32)
    bits = lax.bitcast_convert_type(hb, U32)
    half = h.shape[1] // 2
    packed = (bits[:, :half] >> 16) | (bits[:, half:] & jnp.uint32(0xFFFF0000))
    for s in range(TOKEN_TILE_ROWS):
        hp_ref[pl.ds(s, tm, stride=TOKEN_TILE_ROWS), :] = packed[:, s * LANES:(s + 1) * LANES]


def _router(x, g, scale, shift, w_route, b_route, *, tm=512):
    t, d = x.shape
    vec = pl.BlockSpec((1, d), lambda i: (0, 0))
    return pl.pallas_call(
        functools.partial(_router_body, tm=tm),
        grid=(t // tm,),
        in_specs=[pl.BlockSpec((tm, d), lambda i: (i, 0)), vec, vec, vec,
                  pl.BlockSpec((d, LANES), lambda i: (0, 0)),
                  pl.BlockSpec((1, LANES), lambda i: (0, 0))],
        out_specs=[pl.BlockSpec((tm * TOKEN_TILE_ROWS, LANES), lambda i: (i, 0)),
                   pl.BlockSpec((tm, LANES), lambda i: (i, 0)),
                   pl.BlockSpec((8, LANES), lambda i: (0, 0))],
        out_shape=[jax.ShapeDtypeStruct((t * TOKEN_TILE_ROWS, LANES), U32),
                   jax.ShapeDtypeStruct((t, LANES), F32),
                   jax.ShapeDtypeStruct((8, LANES), F32)],
        scratch_shapes=[pltpu.VMEM((8, LANES), F32)],
        compiler_params=_cparams(("arbitrary",)),
        name="moe_router",
    )(x, g, scale, shift, w_route, b_route)


def _dest_body(route_ref, cnt_ref, dest_ref, blk_ref, *, tm, n_blocks_pad):
    cnt = cnt_ref[...]
    padded = jnp.floor((cnt + (MOE_ROWS - 1)) * (1.0 / MOE_ROWS)) * MOE_ROWS
    upper = (_iota((LANES, LANES), 0) <= _iota((LANES, LANES), 1)).astype(F32)
    pad_end = _dot(padded, upper, _HI)
    pad_start = (pad_end - padded)[0:1, :]
    route = route_ref[...]
    lane = _iota((tm, LANES), 1).astype(F32)
    e1, e2, r1, r2 = route[:, 0:1], route[:, 1:2], route[:, 2:3], route[:, 3:4]
    d1 = _lane_pick(lane, e1, pad_start) + r1
    d2 = _lane_pick(lane, e2, pad_start) + r2
    dest_ref[...] = jnp.where(lane == 0, d1, jnp.where(lane == 1, d2, 0.0)).astype(I32)

    lane_b = _iota((n_blocks_pad, LANES), 1)
    first_row = (_iota((n_blocks_pad, LANES), 0) * MOE_ROWS).astype(F32)
    done = ((pad_end[0:1, :] <= first_row) & (lane_b < N_EXPERTS)).astype(F32)
    blk_e = jnp.minimum(jnp.sum(done, axis=-1, keepdims=True), float(N_EXPERTS - 1))
    n_valid = pad_end[0:1, N_EXPERTS - 1:N_EXPERTS] * (1.0 / MOE_ROWS)
    blk_ref[...] = jnp.where(lane_b == 0, blk_e, jnp.where(lane_b == 1, n_valid, 0.0)).astype(I32)


def _dest(route, counts, n_blocks_pad, *, tm=512):
    t = route.shape[0]
    assert t % tm == 0
    return pl.pallas_call(
        functools.partial(_dest_body, tm=tm, n_blocks_pad=n_blocks_pad),
        grid=(t // tm,),
        in_specs=[pl.BlockSpec((tm, LANES), lambda i: (i, 0)),
                  pl.BlockSpec((8, LANES), lambda i: (0, 0))],
        out_specs=[pl.BlockSpec((tm, LANES), lambda i: (i, 0)),
                   pl.BlockSpec((n_blocks_pad, LANES), lambda i: (0, 0))],
        out_shape=[jax.ShapeDtypeStruct((t, LANES), I32),
                   jax.ShapeDtypeStruct((n_blocks_pad, LANES), I32)],
        compiler_params=_cparams(("arbitrary",)),
        name="moe_dest",
    )(route, counts)


def _invert_body(dest_ref, tok_ref, *, n_assign, n_rows):
    def fill(r, c):
        tok_ref[r] = 0
        return c

    lax.fori_loop(0, n_rows, fill, 0, unroll=8)

    def scatter(a, c):
        tok_ref[dest_ref[a]] = a >> 1
        return c

    lax.fori_loop(0, n_assign, scatter, 0, unroll=8)


def _invert(dest_flat, n_rows):
    n_assign = dest_flat.shape[0]
    return pl.pallas_call(
        functools.partial(_invert_body, n_assign=n_assign, n_rows=n_rows),
        grid_spec=pltpu.PrefetchScalarGridSpec(
            num_scalar_prefetch=1,
            grid=(1,),
            in_specs=[],
            out_specs=pl.BlockSpec(memory_space=pltpu.SMEM),
        ),
        out_shape=jax.ShapeDtypeStruct((n_rows,), I32),
        compiler_params=_cparams(("arbitrary",)),
        name="moe_invert",
    )(dest_flat)


def _expert_body(be_ref, nv_ref, tok_ref, hp_ref, w1_ref, w3_ref, w2_ref, y_ref, xbuf_ref, sem, w1b_ref, w3b_ref,
                 w2b_ref):
    b = pl.program_id(0)
    n_valid = nv_ref[0]
    valid = b < n_valid
    tr = TOKEN_TILE_ROWS

    def gather_rows(blk, slot):
        def one(j, c):
            tok = tok_ref[blk * MOE_ROWS + j]
            pltpu.make_async_copy(hp_ref.at[pl.ds(tok * tr, tr)], xbuf_ref.at[slot, pl.ds(j * tr, tr)],
                                  sem.at[slot]).start()
            return c

        lax.fori_loop(0, MOE_ROWS, one, 0, unroll=8)

    @pl.when((b == 0) & valid)
    def _():
        gather_rows(0, 0)

    @pl.when(b + 1 < n_valid)
    def _():
        gather_rows(b + 1, (b + 1) % 2)

    @pl.when(valid & ((b == 0) | (be_ref[b] != be_ref[jnp.maximum(b - 1, 0)])))
    def _():
        w1b_ref[...] = w1_ref[...].astype(BF16)
        w3b_ref[...] = w3_ref[...].astype(BF16)
        w2b_ref[...] = w2_ref[...].astype(BF16)

    @pl.when(jnp.logical_not(valid))
    def _():
        y_ref[...] = jnp.zeros_like(y_ref)

    @pl.when(valid)
    def _():
        slot = b % 2
        pltpu.make_async_copy(hp_ref.at[pl.ds(0, MOE_ROWS * tr)], xbuf_ref.at[slot], sem.at[slot]).wait()
        lo, hi = [], []
        for s in range(tr):
            p = xbuf_ref[slot, pl.ds(s, MOE_ROWS, stride=tr), :]
            lo.append(lax.bitcast_convert_type(p << 16, F32))
            hi.append(lax.bitcast_convert_type(p & jnp.uint32(0xFFFF0000), F32))
        x = jnp.concatenate(lo + hi, axis=1).astype(BF16)
        a = (_silu(_dot(x, w1b_ref[...])) * _dot(x, w3b_ref[...])).astype(BF16)
        y = _dot(a, w2b_ref[...])
        for s in range(Y_TILE_ROWS):
            y_ref[pl.ds(s, MOE_ROWS, stride=Y_TILE_ROWS), :] = y[:, s * LANES:(s + 1) * LANES]


def _experts(block_expert, n_valid, tok_rows, h_packed, w1, w3, w2, *, layer):
    n_blocks = block_expert.shape[0]
    _, _, d, f = w1.shape
    return pl.pallas_call(
        _expert_body,
        grid_spec=pltpu.PrefetchScalarGridSpec(
            num_scalar_prefetch=3,
            grid=(n_blocks,),
            in_specs=[pl.BlockSpec(memory_space=pl.ANY),
                      pl.BlockSpec((None, None, d, f), lambda b, be, nv, tk: (layer, be[b], 0, 0)),
                      pl.BlockSpec((None, None, d, f), lambda b, be, nv, tk: (layer, be[b], 0, 0)),
                      pl.BlockSpec((None, None, f, d), lambda b, be, nv, tk: (layer, be[b], 0, 0))],
            out_specs=pl.BlockSpec((MOE_ROWS * Y_TILE_ROWS, LANES), lambda b, be, nv, tk: (b, 0)),
            scratch_shapes=[pltpu.VMEM((2, MOE_ROWS * TOKEN_TILE_ROWS, LANES), U32), pltpu.SemaphoreType.DMA((2,)),
                            pltpu.VMEM((d, f), BF16), pltpu.VMEM((d, f), BF16), pltpu.VMEM((f, d), BF16)],
        ),
        out_shape=jax.ShapeDtypeStruct((n_blocks * MOE_ROWS * Y_TILE_ROWS, LANES), F32),
        compiler_params=_cparams(("arbitrary",)),
        name="moe_experts",
    )(block_expert, n_valid, tok_rows, h_packed, w1, w3, w2)


def _combine_body(dest_ref, y_ref, x_ref, route_ref, gate_ref, fg_ref, o_ref, buf_ref, sem, *, tm, final_norm):
    i = pl.program_id(0)
    yr = Y_TILE_ROWS

    def issue(j, c):
        for k in range(2):
            d = dest_ref[(i * tm + j) * 2 + k]
            pltpu.make_async_copy(y_ref.at[pl.ds(d * yr, yr)], buf_ref.at[k, pl.ds(j * yr, yr)], sem.at[k]).start()
        return c

    lax.fori_loop(0, tm, issue, 0)
    for k in range(2):
        pltpu.make_async_copy(y_ref.at[pl.ds(0, tm * yr)], buf_ref.at[k], sem.at[k]).wait()

    route = route_ref[...]
    g1, g2 = route[:, 4:5], route[:, 5:6]
    for s in range(yr):
        cs = slice(s * LANES, (s + 1) * LANES)
        ffn = g1 * buf_ref[0, pl.ds(s, tm, stride=yr), :] + g2 * buf_ref[1, pl.ds(s, tm, stride=yr), :]
        o_ref[:, cs] = x_ref[:, cs] + gate_ref[:, cs] * ffn
    if final_norm:
        xo = o_ref[...]
        o_ref[...] = xo * lax.rsqrt(jnp.mean(xo * xo, axis=-1, keepdims=True) + EPS) * fg_ref[...]


def _combine(dest_flat, y_rows, x, route, gate, final_g, *, final_norm, tm=128):
    t, d = x.shape
    return pl.pallas_call(
        functools.partial(_combine_body, tm=tm, final_norm=final_norm),
        grid_spec=pltpu.PrefetchScalarGridSpec(
            num_scalar_prefetch=1,
            grid=(t // tm,),
            in_specs=[pl.BlockSpec(memory_space=pl.ANY),
                      pl.BlockSpec((tm, d), lambda i, dr: (i, 0)),
                      pl.BlockSpec((tm, LANES), lambda i, dr: (i, 0)),
                      pl.BlockSpec((1, d), lambda i, dr: (0, 0)),
                      pl.BlockSpec((1, d), lambda i, dr: (0, 0))],
            out_specs=pl.BlockSpec((tm, d), lambda i, dr: (i, 0)),
            scratch_shapes=[pltpu.VMEM((2, tm * Y_TILE_ROWS, LANES), F32), pltpu.SemaphoreType.DMA((2,))],
        ),
        out_shape=jax.ShapeDtypeStruct((t, d), F32),
        compiler_params=_cparams(("arbitrary",)),
        name="moe_combine",
    )(dest_flat, y_rows, x, route, gate, final_g)


def _moe(x, g, scale, shift, gate, w_group, b_group, w_expert, b_expert, w1, w3, w2, final_g, *, layer, final_norm):
    t, d = x.shape
    pad = LANES - N_GROUPS - N_EXPERTS
    w_route = jnp.concatenate([w_group, w_expert, jnp.zeros((d, pad), F32)], axis=1)
    b_route = jnp.concatenate([b_group, b_expert, jnp.zeros((pad,), F32)]).reshape(1, LANES)
    h_packed, route, counts = _router(x, g, scale, shift, w_route, b_route)
    n_assign = 2 * t
    n_blocks = -(-(n_assign + N_EXPERTS * (MOE_ROWS - 1)) // MOE_ROWS)
    n_blocks_pad = -(-n_blocks // 8) * 8
    dest, blk = _dest(route, counts, n_blocks_pad)
    dest_flat = dest[:, :2].reshape(n_assign)
    block_expert = blk[:n_blocks, 0]
    n_valid = blk[0, 1:2]
    tok_rows = _invert(dest_flat, n_blocks * MOE_ROWS)
    y_rows = _experts(block_expert, n_valid, tok_rows, h_packed, w1, w3, w2, layer=layer)
    return _combine(dest_flat, y_rows, x, route, gate, final_g, final_norm=final_norm)


def _pad_rows(w, rows):
    return jnp.pad(w, ((0, rows - w.shape[0]), (0, 0)))


def _lane_row(v, offset):
    return jnp.zeros((LANES,), F32).at[offset:offset + v.shape[0]].set(v)


def kernel(x, c, w_mod, b_mod, norm_g, final_norm_g, e_w_in, e_conv_qkv, e_a_log, e_dt_bias, e_head_norm_g, e_conf_dw, e_conf_dw_b, e_conf_ln_g, e_conf_ln_b, e_w_out, o_w_in, o_conv_w, o_conv_b, o_dt_bias, o_a_log, o_d_skip, o_norm_g, o_w_out, moe_w_group, moe_b_group, moe_w_expert, moe_b_expert, moe_w1, moe_w3, moe_w2):
    bsz, t, d = x.shape
    assert bsz == 1 and c.shape == (1, d)
    xt = x[0]
    mod = _modulation(c[0], w_mod, b_mod)

    def mod_parts(l):
        return mod[l, :, :d], mod[l, :, d:2 * d], mod[l, :, 2 * d:]

    row = lambda v: v.reshape(1, -1)
    qkv_w = 3 * GDN_HEADS * HEAD_W
    z_end = qkv_w + GDN_HEADS * HEAD_W

    shift, scale, gate = mod_parts(0)
    w_in = e_w_in[0].T
    tn = 512
    w_glu = w_in[z_end + 2 * GDN_HEADS:]
    w_ba = _pad_rows(w_in[z_end:z_end + 2 * GDN_HEADS], LANES)
    proj0, ba = _norm_matmul(xt, row(norm_g[0, 0]), scale, shift,
                             [(w_in, z_end // tn), (w_glu, w_glu.shape[0] // tn)], w_ba, tn=tn)
    gdn_par = jnp.zeros((8, LANES), F32).at[0].set(_lane_row(e_a_log[0], GDN_HEADS)).at[1].set(
        _lane_row(e_dt_bias[0], GDN_HEADS))
    a_out = _gdn(proj0, ba, e_conv_qkv[0], gdn_par, row(e_head_norm_g[0]))
    b_out = _conformer(proj0, e_conf_dw[0], row(e_conf_dw_b[0]), row(e_conf_ln_g[0]), row(e_conf_ln_b[0]))
    xt = _out_proj0(a_out, b_out, e_w_out[0], xt, gate)
    shift, scale, gate = mod_parts(1)
    xt = _moe(xt, row(norm_g[0, 1]), scale, shift, gate, moe_w_group[0], moe_b_group[0], moe_w_expert[0],
              moe_b_expert[0], moe_w1, moe_w3, moe_w2, row(final_norm_g), layer=0, final_norm=False)

    shift, scale, gate = mod_parts(2)
    w_in = o_w_in[0].T
    n_main = 2 * SSM_GROUPS * SSM_GROUP_W + 2 * SSM_GROUPS * SSM_STATE
    proj1, dt_raw = _norm_matmul(xt, row(norm_g[1, 0]), scale, shift, [(w_in, n_main // tn)],
                                 _pad_rows(w_in[n_main:], LANES), tn=tn)
    ssd_par = jnp.zeros((8, LANES), F32).at[0].set(_lane_row(o_dt_bias[0], 0)).at[1].set(
        _lane_row(o_a_log[0], 0)).at[2].set(_lane_row(o_d_skip[0], 0))
    y = _ssd(proj1, dt_raw, o_conv_w[0], row(o_conv_b[0]), ssd_par)
    xt = _out_proj1(y, row(o_norm_g[0]), o_w_out[0], xt, gate)
    shift, scale, gate = mod_parts(3)
    xt = _moe(xt, row(norm_g[1, 1]), scale, shift, gate, moe_w_group[1], moe_b_group[1], moe_w_expert[1],
              moe_b_expert[1], moe_w1, moe_w3, moe_w2, row(final_norm_g), layer=1, final_norm=True)
    return xt[None]
```

```python
import functools

import jax
import jax.numpy as jnp
from jax import lax
from jax.experimental import pallas as pl
from jax.experimental.pallas import tpu as pltpu

F32 = jnp.float32
BF16 = jnp.bfloat16
I32 = jnp.int32
U32 = jnp.uint32

EPS = 1e-6
LANES = 128
CHUNK = 64
CHUNK_LOG2 = 6
GDN_HEADS = 8
HEAD_W = 128
CONF_CH = 1024
CONF_K = 31
SSM_GROUPS = 8
SSM_GROUP_W = 512
SSM_HEADDIM = 64
SSM_STATE = 128
N_EXPERTS = 32
EXPERTS_PER_GROUP = 8
N_GROUPS = 4
MOE_ROWS = 256
TOKEN_TILE_ROWS = 8
Y_TILE_ROWS = 16
VMEM_LIMIT = 56 * 1024 * 1024
PROLOGUE_ROWS = 128
HALO_ROWS = 16
CONF_HALO_ROWS = 32

_HI = lax.Precision.HIGHEST


def _cparams(sem):
    return pltpu.CompilerParams(dimension_semantics=sem, vmem_limit_bytes=VMEM_LIMIT)


def _dot(a, b, precision=None):
    return jnp.dot(a, b, precision=precision, preferred_element_type=F32)


def _dot_nt(a, b, precision=None):
    return lax.dot_general(a, b, (((1,), (1,)), ((), ())), precision=precision, preferred_element_type=F32)


def _dot_tn(a, b, precision=None):
    return lax.dot_general(a, b, (((0,), (0,)), ((), ())), precision=precision, preferred_element_type=F32)


def _split_bf16(x):
    hi = x.astype(BF16)
    return hi, (x - hi.astype(F32)).astype(BF16)


def _dot3(a, b):
    ah, al = _split_bf16(a)
    bh, bl = _split_bf16(b)
    return _dot(ah, bh) + (_dot(ah, bl) + _dot(al, bh))


def _select_dot(sel, x, left=False):
    p1 = x.astype(BF16)
    r1 = x - p1.astype(F32)
    p2 = r1.astype(BF16)
    p3 = (r1 - p2.astype(F32)).astype(BF16)
    if left:
        return _dot(sel, p1) + (_dot(sel, p2) + _dot(sel, p3))
    return _dot(p1, sel) + (_dot(p2, sel) + _dot(p3, sel))


def _silu(x):
    return x * jax.nn.sigmoid(x)


def _softplus(x):
    return jnp.maximum(x, 0.0) + jnp.log1p(jnp.exp(-jnp.abs(x)))


def _iota(shape, dim):
    return lax.broadcasted_iota(I32, shape, dim)


def _mod_body(c_ref, w_ref, b_ref, o_ref):
    c = c_ref[...]
    o_ref[...] = jnp.sum(w_ref[...] * _silu(c), axis=0, keepdims=True) + b_ref[...]


def _modulation(c, w_mod, b_mod):
    nl, d, n = w_mod.shape
    tn = 768
    return pl.pallas_call(
        _mod_body,
        grid=(nl, n // tn),
        in_specs=[
            pl.BlockSpec((d, 1), lambda l, j: (0, 0)),
            pl.BlockSpec((None, d, tn), lambda l, j: (l, 0, j)),
            pl.BlockSpec((None, 1, tn), lambda l, j: (l, 0, j)),
        ],
        out_specs=pl.BlockSpec((None, 1, tn), lambda l, j: (l, 0, j)),
        out_shape=jax.ShapeDtypeStruct((nl, 1, n), F32),
        compiler_params=_cparams(("arbitrary", "arbitrary")),
        name="adaln_mod",
    )(c.reshape(d, 1), w_mod, b_mod.reshape(nl, 1, n))


def _adaln_rows(x, g, scale, shift):
    r = lax.rsqrt(jnp.mean(x * x, axis=-1, keepdims=True) + EPS)
    return x * r * (g * (1.0 + scale)) + shift


def _norm_mm_body(*refs, starts):
    x_ref, g_ref, sc_ref, sh_ref, ws_ref = refs[:5]
    w_refs = refs[5:5 + len(starts)]
    o_ref, os_ref, h_ref = refs[5 + len(starts):]
    j = pl.program_id(1)

    @pl.when(j == 0)
    def _():
        ws = ws_ref[...].astype(BF16)

        def norm_rows(c, carry):
            rs = pl.ds(pl.multiple_of(c * PROLOGUE_ROWS, PROLOGUE_ROWS), PROLOGUE_ROWS)
            h = _adaln_rows(x_ref[rs, :], g_ref[...], sc_ref[...], sh_ref[...]).astype(BF16)
            h_ref[rs, :] = h
            os_ref[rs, :] = _dot_nt(h, ws)
            return carry

        lax.fori_loop(0, x_ref.shape[0] // PROLOGUE_ROWS, norm_rows, 0)

    bounds = list(starts[1:]) + [None]
    for w_ref, lo, hi in zip(w_refs, starts, bounds):
        in_seg = (j >= lo) if hi is None else ((j >= lo) & (j < hi))

        @pl.when(in_seg)
        def _(w_ref=w_ref):
            o_ref[...] = _dot_nt(h_ref[...], w_ref[...].astype(BF16)).astype(o_ref.dtype)


def _norm_matmul(x, g, scale, shift, segments, w_small, *, tm=1024, tn=512):
    t, d = x.shape
    tm = min(tm, t)
    starts, specs, n_main = [], [], 0
    for w, nb in segments:
        lo = n_main // tn
        starts.append(lo)
        specs.append(pl.BlockSpec((tn, d), lambda i, j, lo=lo, nb=nb: (jnp.clip(j - lo, 0, nb - 1), 0)))
        n_main += nb * tn
    vec = pl.BlockSpec((1, d), lambda i, j: (0, 0))
    return pl.pallas_call(
        functools.partial(_norm_mm_body, starts=tuple(starts)),
        grid=(t // tm, n_main // tn),
        in_specs=[pl.BlockSpec((tm, d), lambda i, j: (i, 0)), vec, vec, vec,
                  pl.BlockSpec((LANES, d), lambda i, j: (0, 0))] + specs,
        out_specs=[pl.BlockSpec((tm, tn), lambda i, j: (i, j)),
                   pl.BlockSpec((tm, LANES), lambda i, j: (i, 0))],
        out_shape=[jax.ShapeDtypeStruct((t, n_main), BF16), jax.ShapeDtypeStruct((t, LANES), F32)],
        scratch_shapes=[pltpu.VMEM((tm, d), BF16)],
        compiler_params=_cparams(("arbitrary", "arbitrary")),
        name="adaln_in_proj",
    )(x, g, scale, shift, w_small, *[w for w, _ in segments])


def _causal_conv(halo, x, w, taps):
    hr = halo.shape[0]
    n = x.shape[0]
    xe = jnp.concatenate([halo, x], axis=0)
    rolled = {0: xe}
    acc = None
    for j in range(taps):
        s = taps - 1 - j
        a, b = divmod(s, 8)
        if b not in rolled:
            rolled[b] = pltpu.roll(xe, b, axis=0)
        term = rolled[b][hr - 8 * a:hr - 8 * a + n] * w[j:j + 1]
        acc = term if acc is None else acc + term
    return acc


def _unit_lower_inverses(a_mats, row, col):
    n = a_mats[0].shape[0]
    eye = (row == col).astype(F32)
    level = ((row >> 1) == (col >> 1)) & (col < row)
    invs = [eye - jnp.where(level, a, 0.0) for a in a_mats]
    k = 1
    while (1 << k) < n:
        level = ((row >> (k + 1)) == (col >> (k + 1))) & (((row >> k) & 1) == 1) & (((col >> k) & 1) == 0)
        splits = [_split_bf16(inv) for inv in invs]
        ys = []
        for (ih, il), a in zip(splits, a_mats):
            mh, ml = _split_bf16(jnp.where(level, a, 0.0))
            ys.append(_dot(ih, mh) + (_dot(ih, ml) + _dot(il, mh)))
        new = []
        for inv, (ih, il), y in zip(invs, splits, ys):
            yh, yl = _split_bf16(y)
            new.append(inv - (_dot(yh, ih) + (_dot(yh, il) + _dot(yl, ih))))
        invs = new
        k += 1
    return invs


def _gdn_body(q_ref, k_ref, v_ref, qh_ref, kh_ref, vh_ref, z_ref, ba_ref, cwq_ref, cwk_ref, cwv_ref,
              par_ref, hg_ref, o_ref, s_ref, *, hb, rows):
    hblk = pl.program_id(0)
    ir = pl.program_id(1)

    @pl.when(ir == 0)
    def _():
        s_ref[...] = jnp.zeros_like(s_ref)

    def conv_silu(x_ref, halo_ref, w_ref):
        halo = jnp.where(ir > 0, halo_ref[...].astype(F32), 0.0)
        return _silu(_causal_conv(halo, x_ref[...].astype(F32), w_ref[...], 4))

    qc = conv_silu(q_ref, qh_ref, cwq_ref)
    kc = conv_silu(k_ref, kh_ref, cwk_ref)
    vc = conv_silu(v_ref, vh_ref, cwv_ref)

    ba = ba_ref[...]
    beta_all = jax.nn.sigmoid(ba)
    g_all = -jnp.exp(par_ref[0:1, :]) * _softplus(ba + par_ref[1:2, :])
    rr = _iota((rows, rows), 0)
    cc = _iota((rows, rows), 1)
    chunk_tril = ((rr >> CHUNK_LOG2) == (cc >> CHUNK_LOG2)) & (cc <= rr)
    gcs_all = _select_dot(chunk_tril.astype(BF16), g_all, left=True)
    gcs_t = gcs_all.T

    row = _iota((CHUNK, CHUNK), 0)
    col = _iota((CHUNK, CHUNK), 1)
    scale = HEAD_W ** -0.5
    hg = hg_ref[...]
    bf = lambda a: a.astype(BF16)
    lane = _iota((rows, LANES), 1)
    sub = _iota((LANES, rows), 0)
    nc = rows // CHUNK

    heads = []
    for h in range(hb):
        hs = slice(h * HEAD_W, (h + 1) * HEAD_W)
        head = hblk * hb + h
        q = qc[:, hs]
        k = kc[:, hs]
        heads.append(dict(
            q=q * lax.rsqrt(jnp.sum(q * q, axis=-1, keepdims=True) + EPS) * scale,
            k=k * lax.rsqrt(jnp.sum(k * k, axis=-1, keepdims=True) + EPS),
            v=vc[:, hs],
            beta=jnp.sum(jnp.where(lane == head, beta_all, 0.0), axis=-1, keepdims=True),
            gcs=jnp.sum(jnp.where(lane == head + GDN_HEADS, gcs_all, 0.0), axis=-1, keepdims=True),
            gcs_row=jnp.sum(jnp.where(sub == head + GDN_HEADS, gcs_t, 0.0), axis=0, keepdims=True)))

    items = []
    for c in range(nc):
        rs = slice(c * CHUNK, (c + 1) * CHUNK)
        for hd in heads:
            qq, kk, vv, beta, gcs = hd["q"][rs], hd["k"][rs], hd["v"][rs], hd["beta"][rs], hd["gcs"][rs]
            decay = jnp.where(col <= row, jnp.exp(jnp.minimum(gcs - hd["gcs_row"][:, rs], 0.0)), 0.0)
            kb = kk * beta
            eg = jnp.exp(gcs)
            g_last = gcs[CHUNK - 1:CHUNK]
            items.append(dict(decay=decay, kb16=bf(kb), k16=bf(kk), q16=bf(qq),
                              rhs16=bf(jnp.concatenate([vv * beta, kb * eg], axis=-1)),
                              q_dec16=bf(qq * eg), k_dec16=bf(kk * jnp.exp(g_last - gcs)), cd=jnp.exp(g_last)))
    a_mats = [jnp.where(col < row, _dot_nt(it["kb16"], it["k16"]) * it["decay"], 0.0) for it in items]
    qks = [bf(_dot_nt(it["q16"], it["k16"]) * it["decay"]) for it in items]
    t_invs = _unit_lower_inverses(a_mats, row, col)
    sols = [_dot(bf(t), it["rhs16"]) for t, it in zip(t_invs, items)]

    states = [s_ref[h] for h in range(hb)]
    for c in range(nc):
        rs = slice(c * CHUNK, (c + 1) * CHUNK)
        cur = slice(c * hb, (c + 1) * hb)
        s16 = [bf(s) for s in states]
        us = [so[:, :HEAD_W] - _dot(bf(so[:, HEAD_W:]), s) for so, s in zip(sols[cur], s16)]
        outs = [_dot(it["q_dec16"], s) + _dot(qk, bf(u)) for it, s, qk, u in zip(items[cur], s16, qks[cur], us)]
        states = [s * it["cd"] + _dot_tn(it["k_dec16"], bf(u)) for s, it, u in zip(states, items[cur], us)]
        for h, o in enumerate(outs):
            hs = slice(h * HEAD_W, (h + 1) * HEAD_W)
            o = o * lax.rsqrt(jnp.mean(o * o, axis=-1, keepdims=True) + EPS) * hg * _silu(z_ref[rs, hs].astype(F32))
            o_ref[rs, hs] = o.astype(o_ref.dtype)
    for h in range(hb):
        s_ref[h] = states[h]


def _gdn(proj, ba, conv_w, par, head_g, *, hb=GDN_HEADS, rows=256):
    t = proj.shape[0]
    rows = min(rows, t)
    w = hb * HEAD_W
    per = (GDN_HEADS * HEAD_W) // w

    def sec(k):
        return pl.BlockSpec((rows, w), lambda h, i, k=k: (i, k * per + h))

    def halo(k):
        return pl.BlockSpec((HALO_ROWS, w),
                            lambda h, i, k=k: (jnp.maximum(i * (rows // HALO_ROWS) - 1, 0), k * per + h))

    def cw(k):
        return pl.BlockSpec((4, w), lambda h, i, k=k: (0, k * per + h))

    return pl.pallas_call(
        functools.partial(_gdn_body, hb=hb, rows=rows),
        grid=(GDN_HEADS // hb, t // rows),
        in_specs=[sec(0), sec(1), sec(2), halo(0), halo(1), halo(2), sec(3),
                  pl.BlockSpec((rows, LANES), lambda h, i: (i, 0)),
                  cw(0), cw(1), cw(2),
                  pl.BlockSpec((8, LANES), lambda h, i: (0, 0)),
                  pl.BlockSpec((1, HEAD_W), lambda h, i: (0, 0))],
        out_specs=pl.BlockSpec((rows, w), lambda h, i: (i, h)),
        out_shape=jax.ShapeDtypeStruct((t, GDN_HEADS * HEAD_W), BF16),
        scratch_shapes=[pltpu.VMEM((hb, HEAD_W, HEAD_W), F32)],
        compiler_params=_cparams(("arbitrary", "arbitrary")),
        name="gated_deltanet",
    )(proj, proj, proj, proj, proj, proj, proj, ba, conv_w, conv_w, conv_w, par, head_g)


def _conf_body(x_ref, halo_ref, w_ref, b_ref, g_ref, lb_ref, o_ref):
    ir = pl.program_id(0)

    def glu(v):
        v = v.astype(F32)
        return v[:, :CONF_CH] * jax.nn.sigmoid(v[:, CONF_CH:])

    halo = jnp.where(ir > 0, glu(halo_ref[...]), 0.0)
    u = _causal_conv(halo, glu(x_ref[...]), w_ref[...], CONF_K) + b_ref[...]
    mu = jnp.mean(u, axis=-1, keepdims=True)
    uc = u - mu
    var = jnp.mean(uc * uc, axis=-1, keepdims=True)
    y = uc * lax.rsqrt(var + EPS) * g_ref[...] + lb_ref[...]
    o_ref[...] = _silu(y).astype(o_ref.dtype)


def _conformer(proj, dw, dw_b, ln_g, ln_b, *, rows=256):
    t = proj.shape[0]
    rows = min(rows, t)
    glu_blk = (4 * GDN_HEADS * HEAD_W) // (2 * CONF_CH)
    halo_rows = CONF_HALO_ROWS
    vec = pl.BlockSpec((1, CONF_CH), lambda i: (0, 0))
    return pl.pallas_call(
        _conf_body,
        grid=(t // rows,),
        in_specs=[pl.BlockSpec((rows, 2 * CONF_CH), lambda i: (i, glu_blk)),
                  pl.BlockSpec((halo_rows, 2 * CONF_CH),
                               lambda i: (jnp.maximum(i * (rows // halo_rows) - 1, 0), glu_blk)),
                  pl.BlockSpec((CONF_K, CONF_CH), lambda i: (0, 0)), vec, vec, vec],
        out_specs=pl.BlockSpec((rows, CONF_CH), lambda i: (i, 0)),
        out_shape=jax.ShapeDtypeStruct((t, CONF_CH), BF16),
        compiler_params=_cparams(("arbitrary",)),
        name="conformer_conv",
    )(proj, proj, dw, dw_b, ln_g, ln_b)


def _out0_body(a_ref, b_ref, wa_ref, wb_ref, x_ref, gate_ref, o_ref):
    mix = _dot(a_ref[...], wa_ref[...].astype(BF16)) + _dot(b_ref[...], wb_ref[...].astype(BF16))
    o_ref[...] = x_ref[...] + gate_ref[...] * mix


def _out_proj0(a, b, w, x, gate, *, tm=1024, tn=1024):
    t, d = x.shape
    tm = min(tm, t)
    ka, kb = a.shape[1], b.shape[1]
    return pl.pallas_call(
        _out0_body,
        grid=(t // tm, d // tn),
        in_specs=[pl.BlockSpec((tm, ka), lambda i, j: (i, 0)),
                  pl.BlockSpec((tm, kb), lambda i, j: (i, 0)),
                  pl.BlockSpec((ka, tn), lambda i, j: (0, j)),
                  pl.BlockSpec((kb, tn), lambda i, j: (ka // kb, j)),
                  pl.BlockSpec((tm, tn), lambda i, j: (i, j)),
                  pl.BlockSpec((1, tn), lambda i, j: (0, j))],
        out_specs=pl.BlockSpec((tm, tn), lambda i, j: (i, j)),
        out_shape=jax.ShapeDtypeStruct((t, d), F32),
        compiler_params=_cparams(("arbitrary", "arbitrary")),
        name="out_proj_even",
    )(a, b, w, w, x, gate)


def _out1_body(y_ref, g_ref, w_ref, x_ref, gate_ref, o_ref, h_ref):
    @pl.when(pl.program_id(1) == 0)
    def _():
        def norm_rows(c, carry):
            rs = pl.ds(pl.multiple_of(c * PROLOGUE_ROWS, PROLOGUE_ROWS), PROLOGUE_ROWS)
            y = y_ref[rs, :].astype(F32)
            r = lax.rsqrt(jnp.mean(y * y, axis=-1, keepdims=True) + EPS)
            h_ref[rs, :] = (y * r * g_ref[...]).astype(BF16)
            return carry

        lax.fori_loop(0, y_ref.shape[0] // PROLOGUE_ROWS, norm_rows, 0)

    o_ref[...] = x_ref[...] + gate_ref[...] * _dot(h_ref[...], w_ref[...].astype(BF16))


def _out_proj1(y, norm_g, w, x, gate, *, tm=1024, tn=512):
    t, d = x.shape
    tm = min(tm, t)
    k = y.shape[1]
    return pl.pallas_call(
        _out1_body,
        grid=(t // tm, d // tn),
        in_specs=[pl.BlockSpec((tm, k), lambda i, j: (i, 0)),
                  pl.BlockSpec((1, k), lambda i, j: (0, 0)),
                  pl.BlockSpec((k, tn), lambda i, j: (0, j)),
                  pl.BlockSpec((tm, tn), lambda i, j: (i, j)),
                  pl.BlockSpec((1, tn), lambda i, j: (0, j))],
        out_specs=pl.BlockSpec((tm, tn), lambda i, j: (i, j)),
        out_shape=jax.ShapeDtypeStruct((t, d), F32),
        scratch_shapes=[pltpu.VMEM((tm, k), BF16)],
        compiler_params=_cparams(("arbitrary", "arbitrary")),
        name="out_proj_odd",
    )(y, norm_g, w, x, gate)


def _ssd_body(x_ref, b_ref, c_ref, xh_ref, bh_ref, ch_ref, z_ref, dt_ref, cwx_ref, cwb_ref, cwc_ref,
              cbx_ref, cbb_ref, cbc_ref, par_ref, o_ref, s_ref, at_ref, *, rows):
    grp = pl.program_id(0)
    ir = pl.program_id(1)
    gw = SSM_GROUP_W
    hpg = gw // SSM_HEADDIM

    @pl.when(ir == 0)
    def _():
        s_ref[...] = jnp.zeros_like(s_ref)

    def conv_silu(x_r, halo_r, w_r, bias_r):
        halo = jnp.where(ir > 0, halo_r[...].astype(F32), 0.0)
        return _silu(_causal_conv(halo, x_r[...].astype(F32), w_r[...], 4) + bias_r[...])

    xs = conv_silu(x_ref, xh_ref, cwx_ref, cbx_ref)
    bm = conv_silu(b_ref, bh_ref, cwb_ref, cbb_ref)
    cm = conv_silu(c_ref, ch_ref, cwc_ref, cbc_ref)

    dt = _softplus(dt_ref[...] + par_ref[0:1, :])
    da = dt * (-jnp.exp(par_ref[1:2, :]))
    rr = _iota((rows, rows), 0)
    cc = _iota((rows, rows), 1)
    chunk_tril = ((rr >> CHUNK_LOG2) == (cc >> CHUNK_LOG2)) & (cc <= rr)
    acs = _select_dot(chunk_tril.astype(BF16), da, left=True)
    at_ref[...] = acs.T
    acs_rows = at_ref[pl.ds(pl.multiple_of(grp * hpg, hpg), hpg), :]

    sel = (_iota((LANES, gw), 0) == grp * hpg + (_iota((LANES, gw), 1) >> CHUNK_LOG2)).astype(BF16)
    dt_e = _select_dot(sel, dt)
    acs_e = _select_dot(sel, acs)
    dskip_e = _select_dot(sel, par_ref[...])[2:3, :]

    causal = (_iota((CHUNK, gw), 1) & (CHUNK - 1)) <= _iota((CHUNK, gw), 0)
    same_head = (_iota((LANES, LANES), 0) >> CHUNK_LOG2) == (_iota((LANES, LANES), 1) >> CHUNK_LOG2)
    bf = lambda a: a.astype(BF16)
    nc = rows // CHUNK
    chunks = [slice(c * CHUNK, (c + 1) * CHUNK) for c in range(nc)]
    tiles = [slice(k * LANES, (k + 1) * LANES) for k in range(gw // LANES)]

    a_es = [acs_e[rs] for rs in chunks]
    a_rows = [jnp.concatenate([acs_rows[h:h + 1, rs] for h in range(hpg)], axis=1) for rs in chunks]
    l_cats = [jnp.where(causal, jnp.exp(jnp.minimum(a_e - a_row, 0.0)), 0.0) for a_e, a_row in zip(a_es, a_rows)]
    c16 = [bf(cm[rs]) for rs in chunks]
    b16 = [bf(bm[rs]) for rs in chunks]
    cbs = [_dot_nt(c_c, jnp.concatenate([b_c] * hpg, axis=0)) for c_c, b_c in zip(c16, b16)]
    xdts = [xs[rs] * dt_e[rs] for rs in chunks]
    m16 = [bf(cb * l_cat) for cb, l_cat in zip(cbs, l_cats)]
    y_diags = []
    for m, xdt in zip(m16, xdts):
        parts = [_dot(m[:, ts], bf(jnp.where(same_head, jnp.concatenate([xdt[:, ts]] * 2, axis=0), 0.0)))
                 for ts in tiles]
        y_diags.append(jnp.concatenate(parts, axis=1))
    a_lasts = [a_e[CHUNK - 1:CHUNK] for a_e in a_es]
    updates = [_dot_tn(b_c, bf(xdt * jnp.exp(a_last - a_e)))
               for b_c, xdt, a_last, a_e in zip(b16, xdts, a_lasts, a_es)]

    states = [s_ref[...]]
    for a_last, upd in zip(a_lasts, updates):
        states.append(states[-1] * jnp.exp(a_last) + upd)
    s_ref[...] = states[-1]

    y_offs = [_dot(c_c, bf(s)) * jnp.exp(a_e) for c_c, s, a_e in zip(c16, states[:-1], a_es)]
    for rs, y_diag, y_off in zip(chunks, y_diags, y_offs):
        y = y_diag + y_off + dskip_e * xs[rs]
        o_ref[rs, :] = (y * _silu(z_ref[rs, :].astype(F32))).astype(o_ref.dtype)


def _ssd(proj, dt_raw, conv_w, conv_b, par, *, rows=256):
    t = proj.shape[0]
    rows = min(rows, t)
    gw = SSM_GROUP_W
    d_inner = SSM_GROUPS * gw
    xo = d_inner // gw
    bo = (2 * d_inner) // LANES
    co = bo + SSM_GROUPS * SSM_STATE // LANES
    hrow = lambda i: jnp.maximum(i * (rows // HALO_ROWS) - 1, 0)
    cb_x = d_inner // gw
    return pl.pallas_call(
        functools.partial(_ssd_body, rows=rows),
        grid=(SSM_GROUPS, t // rows),
        in_specs=[pl.BlockSpec((rows, gw), lambda g, i: (i, xo + g)),
                  pl.BlockSpec((rows, LANES), lambda g, i: (i, bo + g)),
                  pl.BlockSpec((rows, LANES), lambda g, i: (i, co + g)),
                  pl.BlockSpec((HALO_ROWS, gw), lambda g, i: (hrow(i), xo + g)),
                  pl.BlockSpec((HALO_ROWS, LANES), lambda g, i: (hrow(i), bo + g)),
                  pl.BlockSpec((HALO_ROWS, LANES), lambda g, i: (hrow(i), co + g)),
                  pl.BlockSpec((rows, gw), lambda g, i: (i, g)),
                  pl.BlockSpec((rows, LANES), lambda g, i: (i, 0)),
                  pl.BlockSpec((4, gw), lambda g, i: (0, g)),
                  pl.BlockSpec((4, LANES), lambda g, i: (0, cb_x * (gw // LANES) + g)),
                  pl.BlockSpec((4, LANES), lambda g, i: (0, cb_x * (gw // LANES) + SSM_GROUPS + g)),
                  pl.BlockSpec((1, gw), lambda g, i: (0, g)),
                  pl.BlockSpec((1, LANES), lambda g, i: (0, cb_x * (gw // LANES) + g)),
                  pl.BlockSpec((1, LANES), lambda g, i: (0, cb_x * (gw // LANES) + SSM_GROUPS + g)),
                  pl.BlockSpec((8, LANES), lambda g, i: (0, 0))],
        out_specs=pl.BlockSpec((rows, gw), lambda g, i: (i, g)),
        out_shape=jax.ShapeDtypeStruct((t, d_inner), BF16),
        scratch_shapes=[pltpu.VMEM((SSM_STATE, gw), F32), pltpu.VMEM((LANES, rows), F32)],
        compiler_params=_cparams(("arbitrary", "arbitrary")),
        name="ssd",
    )(proj, proj, proj, proj, proj, proj, proj, dt_raw, conv_w, conv_w, conv_w, conv_b, conv_b, conv_b, par)


def _lane_pick(lane, idx, vals):
    return jnp.sum(jnp.where(lane == idx, vals, 0.0), axis=-1, keepdims=True)


def _router_body(x_ref, g_ref, sc_ref, sh_ref, w_ref, b_ref, hp_ref, route_ref, cnt_ref, carry_ref, *, tm):
    i = pl.program_id(0)

    @pl.when(i == 0)
    def _():
        carry_ref[...] = jnp.zeros_like(carry_ref)

    h = _adaln_rows(x_ref[...], g_ref[...], sc_ref[...], sh_ref[...])
    logits = _dot3(h, w_ref[...]) + b_ref[...]
    lane = _iota((tm, LANES), 1).astype(F32)
    neg = -jnp.inf
    gl = jnp.where(lane < N_GROUPS, logits, neg)
    gmax = jnp.max(gl, axis=-1, keepdims=True)
    grp_p = 1.0 / jnp.sum(jnp.exp(gl - gmax), axis=-1, keepdims=True)
    gidx = jnp.min(jnp.where(gl == gmax, lane, float(LANES)), axis=-1, keepdims=True)
    lo = N_GROUPS + EXPERTS_PER_GROUP * gidx
    el = jnp.where((lane >= lo) & (lane < lo + EXPERTS_PER_GROUP), logits, neg)
    m1 = jnp.max(el, axis=-1, keepdims=True)
    i1 = jnp.min(jnp.where(el == m1, lane, float(LANES)), axis=-1, keepdims=True)
    el2 = jnp.where(lane == i1, neg, el)
    m2 = jnp.max(el2, axis=-1, keepdims=True)
    i2 = jnp.min(jnp.where(el2 == m2, lane, float(LANES)), axis=-1, keepdims=True)
    tt = jnp.exp(m2 - m1)
    g1 = grp_p / (1.0 + tt)
    g2 = g1 * tt
    e1 = i1 - N_GROUPS
    e2 = i2 - N_GROUPS

    onehot = ((lane == e1) | (lane == e2)).astype(F32)
    strict = (_iota((tm, tm), 1) < _iota((tm, tm), 0)).astype(BF16)
    rank = _dot(strict, onehot.astype(BF16)) + carry_ref[0:1, :]
    r1 = _lane_pick(lane, e1, rank)
    r2 = _lane_pick(lane, e2, rank)
    carry_ref[...] = carry_ref[...] + jnp.sum(onehot, axis=0, keepdims=True)
    cnt_ref[...] = carry_ref[...]

    route = jnp.where(lane == 0, e1, jnp.where(lane == 1, e2, jnp.where(lane == 2, r1, jnp.where(
        lane == 3, r2, jnp.where(lane == 4, g1, jnp.where(lane == 5, g2, 0.0))))))
    route_ref[...] = route

    hb = h.astype(BF16).astype(F32)
    bits = lax.bitcast_convert_type(hb, U32)
    half = h.shape[1] // 2
    packed = (bits[:, :half] >> 16) | (bits[:, half:] & jnp.uint32(0xFFFF0000))
    for s in range(TOKEN_TILE_ROWS):
        hp_ref[pl.ds(s, tm, stride=TOKEN_TILE_ROWS), :] = packed[:, s * LANES:(s + 1) * LANES]


def _router(x, g, scale, shift, w_route, b_route, *, tm=512):
    t, d = x.shape
    vec = pl.BlockSpec((1, d), lambda i: (0, 0))
    return pl.pallas_call(
        functools.partial(_router_body, tm=tm),
        grid=(t // tm,),
        in_specs=[pl.BlockSpec((tm, d), lambda i: (i, 0)), vec, vec, vec,
                  pl.BlockSpec((d, LANES), lambda i: (0, 0)),
                  pl.BlockSpec((1, LANES), lambda i: (0, 0))],
        out_specs=[pl.BlockSpec((tm * TOKEN_TILE_ROWS, LANES), lambda i: (i, 0)),
                   pl.BlockSpec((tm, LANES), lambda i: (i, 0)),
                   pl.BlockSpec((8, LANES), lambda i: (0, 0))],
        out_shape=[jax.ShapeDtypeStruct((t * TOKEN_TILE_ROWS, LANES), U32),
                   jax.ShapeDtypeStruct((t, LANES), F32),
                   jax.ShapeDtypeStruct((8, LANES), F32)],
        scratch_shapes=[pltpu.VMEM((8, LANES), F32)],
        compiler_params=_cparams(("arbitrary",)),
        name="moe_router",
    )(x, g, scale, shift, w_route, b_route)


def _dest_body(route_ref, cnt_ref, dest_ref, blk_ref, *, tm, n_blocks_pad):
    cnt = cnt_ref[...]
    padded = jnp.floor((cnt + (MOE_ROWS - 1)) * (1.0 / MOE_ROWS)) * MOE_ROWS
    upper = (_iota((LANES, LANES), 0) <= _iota((LANES, LANES), 1)).astype(F32)
    pad_end = _dot(padded, upper, _HI)
    pad_start = (pad_end - padded)[0:1, :]
    route = route_ref[...]
    lane = _iota((tm, LANES), 1).astype(F32)
    e1, e2, r1, r2 = route[:, 0:1], route[:, 1:2], route[:, 2:3], route[:, 3:4]
    d1 = _lane_pick(lane, e1, pad_start) + r1
    d2 = _lane_pick(lane, e2, pad_start) + r2
    dest_ref[...] = jnp.where(lane == 0, d1, jnp.where(lane == 1, d2, 0.0)).astype(I32)

    lane_b = _iota((n_blocks_pad, LANES), 1)
    first_row = (_iota((n_blocks_pad, LANES), 0) * MOE_ROWS).astype(F32)
    done = ((pad_end[0:1, :] <= first_row) & (lane_b < N_EXPERTS)).astype(F32)
    blk_e = jnp.minimum(jnp.sum(done, axis=-1, keepdims=True), float(N_EXPERTS - 1))
    n_valid = pad_end[0:1, N_EXPERTS - 1:N_EXPERTS] * (1.0 / MOE_ROWS)
    blk_ref[...] = jnp.where(lane_b == 0, blk_e, jnp.where(lane_b == 1, n_valid, 0.0)).astype(I32)


def _dest(route, counts, n_blocks_pad, *, tm=512):
    t = route.shape[0]
    assert t % tm == 0
    return pl.pallas_call(
        functools.partial(_dest_body, tm=tm, n_blocks_pad=n_blocks_pad),
        grid=(t // tm,),
        in_specs=[pl.BlockSpec((tm, LANES), lambda i: (i, 0)),
                  pl.BlockSpec((8, LANES), lambda i: (0, 0))],
        out_specs=[pl.BlockSpec((tm, LANES), lambda i: (i, 0)),
                   pl.BlockSpec((n_blocks_pad, LANES), lambda i: (0, 0))],
        out_shape=[jax.ShapeDtypeStruct((t, LANES), I32),
                   jax.ShapeDtypeStruct((n_blocks_pad, LANES), I32)],
        compiler_params=_cparams(("arbitrary",)),
        name="moe_dest",
    )(route, counts)


def _invert_body(dest_ref, tok_ref, *, n_assign, n_rows):
    def fill(r, c):
        tok_ref[r] = 0
        return c

    lax.fori_loop(0, n_rows, fill, 0, unroll=8)

    def scatter(a, c):
        tok_ref[dest_ref[a]] = a >> 1
        return c

    lax.fori_loop(0, n_assign, scatter, 0, unroll=8)


def _invert(dest_flat, n_rows):
    n_assign = dest_flat.shape[0]
    return pl.pallas_call(
        functools.partial(_invert_body, n_assign=n_assign, n_rows=n_rows),
        grid_spec=pltpu.PrefetchScalarGridSpec(
            num_scalar_prefetch=1,
            grid=(1,),
            in_specs=[],
            out_specs=pl.BlockSpec(memory_space=pltpu.SMEM),
        ),
        out_shape=jax.ShapeDtypeStruct((n_rows,), I32),
        compiler_params=_cparams(("arbitrary",)),
        name="moe_invert",
    )(dest_flat)


def _expert_body(be_ref, nv_ref, tok_ref, hp_ref, w1_ref, w3_ref, w2_ref, y_ref, xbuf_ref, sem, w1b_ref, w3b_ref,
                 w2b_ref):
    b = pl.program_id(0)
    n_valid = nv_ref[0]
    valid = b < n_valid
    tr = TOKEN_TILE_ROWS

    def gather_rows(blk, slot):
        def one(j, c):
            tok = tok_ref[blk * MOE_ROWS + j]
            pltpu.make_async_copy(hp_ref.at[pl.ds(tok * tr, tr)], xbuf_ref.at[slot, pl.ds(j * tr, tr)],
                                  sem.at[slot]).start()
            return c

        lax.fori_loop(0, MOE_ROWS, one, 0, unroll=8)

    @pl.when((b == 0) & valid)
    def _():
        gather_rows(0, 0)

    @pl.when(b + 1 < n_valid)
    def _():
        gather_rows(b + 1, (b + 1) % 2)

    @pl.when(valid & ((b == 0) | (be_ref[b] != be_ref[jnp.maximum(b - 1, 0)])))
    def _():
        w1b_ref[...] = w1_ref[...].astype(BF16)
        w3b_ref[...] = w3_ref[...].astype(BF16)
        w2b_ref[...] = w2_ref[...].astype(BF16)

    @pl.when(jnp.logical_not(valid))
    def _():
        y_ref[...] = jnp.zeros_like(y_ref)

    @pl.when(valid)
    def _():
        slot = b % 2
        pltpu.make_async_copy(hp_ref.at[pl.ds(0, MOE_ROWS * tr)], xbuf_ref.at[slot], sem.at[slot]).wait()
        lo, hi = [], []
        for s in range(tr):
            p = xbuf_ref[slot, pl.ds(s, MOE_ROWS, stride=tr), :]
            lo.append(lax.bitcast_convert_type(p << 16, F32))
            hi.append(lax.bitcast_convert_type(p & jnp.uint32(0xFFFF0000), F32))
        x = jnp.concatenate(lo + hi, axis=1).astype(BF16)
        a = (_silu(_dot(x, w1b_ref[...])) * _dot(x, w3b_ref[...])).astype(BF16)
        y = _dot(a, w2b_ref[...])
        for s in range(Y_TILE_ROWS):
            y_ref[pl.ds(s, MOE_ROWS, stride=Y_TILE_ROWS), :] = y[:, s * LANES:(s + 1) * LANES]


def _experts(block_expert, n_valid, tok_rows, h_packed, w1, w3, w2, *, layer):
    n_blocks = block_expert.shape[0]
    _, _, d, f = w1.shape
    return pl.pallas_call(
        _expert_body,
        grid_spec=pltpu.PrefetchScalarGridSpec(
            num_scalar_prefetch=3,
            grid=(n_blocks,),
            in_specs=[pl.BlockSpec(memory_space=pl.ANY),
                      pl.BlockSpec((None, None, d, f), lambda b, be, nv, tk: (layer, be[b], 0, 0)),
                      pl.BlockSpec((None, None, d, f), lambda b, be, nv, tk: (layer, be[b], 0, 0)),
                      pl.BlockSpec((None, None, f, d), lambda b, be, nv, tk: (layer, be[b], 0, 0))],
            out_specs=pl.BlockSpec((MOE_ROWS * Y_TILE_ROWS, LANES), lambda b, be, nv, tk: (b, 0)),
            scratch_shapes=[pltpu.VMEM((2, MOE_ROWS * TOKEN_TILE_ROWS, LANES), U32), pltpu.SemaphoreType.DMA((2,)),
                            pltpu.VMEM((d, f), BF16), pltpu.VMEM((d, f), BF16), pltpu.VMEM((f, d), BF16)],
        ),
        out_shape=jax.ShapeDtypeStruct((n_blocks * MOE_ROWS * Y_TILE_ROWS, LANES), F32),
        compiler_params=_cparams(("arbitrary",)),
        name="moe_experts",
    )(block_expert, n_valid, tok_rows, h_packed, w1, w3, w2)


def _combine_body(dest_ref, y_ref, x_ref, route_ref, gate_ref, fg_ref, o_ref, buf_ref, sem, *, tm, final_norm):
    i = pl.program_id(0)
    yr = Y_TILE_ROWS

    def gather_rows(blk, slot):
        def one(j, c):
            for k in range(2):
                d = dest_ref[(blk * tm + j) * 2 + k]
                pltpu.make_async_copy(y_ref.at[pl.ds(d * yr, yr)], buf_ref.at[slot, k, pl.ds(j * yr, yr)],
                                      sem.at[slot, k]).start()
            return c

        lax.fori_loop(0, tm, one, 0, unroll=4)

    @pl.when(i == 0)
    def _():
        gather_rows(0, 0)

    @pl.when(i + 1 < pl.num_programs(0))
    def _():
        gather_rows(i + 1, (i + 1) % 2)

    slot = i % 2
    for k in range(2):
        pltpu.make_async_copy(y_ref.at[pl.ds(0, tm * yr)], buf_ref.at[slot, k], sem.at[slot, k]).wait()

    route = route_ref[...]
    g1, g2 = route[:, 4:5], route[:, 5:6]
    for s in range(yr):
        cs = slice(s * LANES, (s + 1) * LANES)
        ffn = (g1 * buf_ref[slot, 0, pl.ds(s, tm, stride=yr), :]
               + g2 * buf_ref[slot, 1, pl.ds(s, tm, stride=yr), :])
        o_ref[:, cs] = x_ref[:, cs] + gate_ref[:, cs] * ffn
    if final_norm:
        xo = o_ref[...]
        o_ref[...] = xo * lax.rsqrt(jnp.mean(xo * xo, axis=-1, keepdims=True) + EPS) * fg_ref[...]


def _combine(dest_flat, y_rows, x, route, gate, final_g, *, final_norm, tm=128):
    t, d = x.shape
    return pl.pallas_call(
        functools.partial(_combine_body, tm=tm, final_norm=final_norm),
        grid_spec=pltpu.PrefetchScalarGridSpec(
            num_scalar_prefetch=1,
            grid=(t // tm,),
            in_specs=[pl.BlockSpec(memory_space=pl.ANY),
                      pl.BlockSpec((tm, d), lambda i, dr: (i, 0)),
                      pl.BlockSpec((tm, LANES), lambda i, dr: (i, 0)),
                      pl.BlockSpec((1, d), lambda i, dr: (0, 0)),
                      pl.BlockSpec((1, d), lambda i, dr: (0, 0))],
            out_specs=pl.BlockSpec((tm, d), lambda i, dr: (i, 0)),
            scratch_shapes=[pltpu.VMEM((2, 2, tm * Y_TILE_ROWS, LANES), F32), pltpu.SemaphoreType.DMA((2, 2))],
        ),
        out_shape=jax.ShapeDtypeStruct((t, d), F32),
        compiler_params=_cparams(("arbitrary",)),
        name="moe_combine",
    )(dest_flat, y_rows, x, route, gate, final_g)


def _moe(x, g, scale, shift, gate, w_group, b_group, w_expert, b_expert, w1, w3, w2, final_g, *, layer, final_norm):
    t, d = x.shape
    pad = LANES - N_GROUPS - N_EXPERTS
    w_route = jnp.concatenate([w_group, w_expert, jnp.zeros((d, pad), F32)], axis=1)
    b_route = jnp.concatenate([b_group, b_expert, jnp.zeros((pad,), F32)]).reshape(1, LANES)
    h_packed, route, counts = _router(x, g, scale, shift, w_route, b_route)
    n_assign = 2 * t
    n_blocks = -(-(n_assign + N_EXPERTS * (MOE_ROWS - 1)) // MOE_ROWS)
    n_blocks_pad = -(-n_blocks // 8) * 8
    dest, blk = _dest(route, counts, n_blocks_pad)
    dest_flat = dest[:, :2].reshape(n_assign)
    block_expert = blk[:n_blocks, 0]
    n_valid = blk[0, 1:2]
    tok_rows = _invert(dest_flat, n_blocks * MOE_ROWS)
    y_rows = _experts(block_expert, n_valid, tok_rows, h_packed, w1, w3, w2, layer=layer)
    return _combine(dest_flat, y_rows, x, route, gate, final_g, final_norm=final_norm)


def _pad_rows(w, rows):
    return jnp.pad(w, ((0, rows - w.shape[0]), (0, 0)))


def _lane_row(v, offset):
    return jnp.zeros((LANES,), F32).at[offset:offset + v.shape[0]].set(v)


def kernel(x, c, w_mod, b_mod, norm_g, final_norm_g, e_w_in, e_conv_qkv, e_a_log, e_dt_bias, e_head_norm_g, e_conf_dw, e_conf_dw_b, e_conf_ln_g, e_conf_ln_b, e_w_out, o_w_in, o_conv_w, o_conv_b, o_dt_bias, o_a_log, o_d_skip, o_norm_g, o_w_out, moe_w_group, moe_b_group, moe_w_expert, moe_b_expert, moe_w1, moe_w3, moe_w2):
    bsz, t, d = x.shape
    assert bsz == 1 and c.shape == (1, d)
    xt = x[0]
    mod = _modulation(c[0], w_mod, b_mod)

    def mod_parts(l):
        return mod[l, :, :d], mod[l, :, d:2 * d], mod[l, :, 2 * d:]

    row = lambda v: v.reshape(1, -1)
    qkv_w = 3 * GDN_HEADS * HEAD_W
    z_end = qkv_w + GDN_HEADS * HEAD_W

    shift, scale, gate = mod_parts(0)
    w_in = e_w_in[0].T
    tn = 512
    w_glu = w_in[z_end + 2 * GDN_HEADS:]
    w_ba = _pad_rows(w_in[z_end:z_end + 2 * GDN_HEADS], LANES)
    proj0, ba = _norm_matmul(xt, row(norm_g[0, 0]), scale, shift,
                             [(w_in, z_end // tn), (w_glu, w_glu.shape[0] // tn)], w_ba, tn=tn)
    gdn_par = jnp.zeros((8, LANES), F32).at[0].set(_lane_row(e_a_log[0], GDN_HEADS)).at[1].set(
        _lane_row(e_dt_bias[0], GDN_HEADS))
    a_out = _gdn(proj0, ba, e_conv_qkv[0], gdn_par, row(e_head_norm_g[0]))
    b_out = _conformer(proj0, e_conf_dw[0], row(e_conf_dw_b[0]), row(e_conf_ln_g[0]), row(e_conf_ln_b[0]))
    xt = _out_proj0(a_out, b_out, e_w_out[0], xt, gate)
    shift, scale, gate = mod_parts(1)
    xt = _moe(xt, row(norm_g[0, 1]), scale, shift, gate, moe_w_group[0], moe_b_group[0], moe_w_expert[0],
              moe_b_expert[0], moe_w1, moe_w3, moe_w2, row(final_norm_g), layer=0, final_norm=False)

    shift, scale, gate = mod_parts(2)
    w_in = o_w_in[0].T
    n_main = 2 * SSM_GROUPS * SSM_GROUP_W + 2 * SSM_GROUPS * SSM_STATE
    proj1, dt_raw = _norm_matmul(xt, row(norm_g[1, 0]), scale, shift, [(w_in, n_main // tn)],
                                 _pad_rows(w_in[n_main:], LANES), tn=tn)
    ssd_par = jnp.zeros((8, LANES), F32).at[0].set(_lane_row(o_dt_bias[0], 0)).at[1].set(
        _lane_row(o_a_log[0], 0)).at[2].set(_lane_row(o_d_skip[0], 0))
    y = _ssd(proj1, dt_raw, o_conv_w[0], row(o_conv_b[0]), ssd_par)
    xt = _out_proj1(y, row(o_norm_g[0]), o_w_out[0], xt, gate)
    shift, scale, gate = mod_parts(3)
    xt = _moe(xt, row(norm_g[1, 1]), scale, shift, gate, moe_w_group[1], moe_b_group[1], moe_w_expert[1],
              moe_b_expert[1], moe_w1, moe_w3, moe_w2, row(final_norm_g), layer=1, final_norm=True)
    return xt[None]
```

```python
import functools

import jax
import jax.numpy as jnp
from jax import lax
from jax.experimental import pallas as pl
from jax.experimental.pallas import tpu as pltpu

F32 = jnp.float32
BF16 = jnp.bfloat16
I32 = jnp.int32
U32 = jnp.uint32

EPS = 1e-6
LANES = 128
CHUNK = 64
CHUNK_LOG2 = 6
GDN_HEADS = 8
HEAD_W = 128
CONF_CH = 1024
CONF_K = 31
SSM_GROUPS = 8
SSM_GROUP_W = 512
SSM_HEADDIM = 64
SSM_STATE = 128
N_EXPERTS = 32
EXPERTS_PER_GROUP = 8
N_GROUPS = 4
MOE_ROWS = 256
TOKEN_TILE_ROWS = 8
VMEM_LIMIT = 56 * 1024 * 1024
PROLOGUE_ROWS = 128
HALO_ROWS = 16
CONF_HALO_ROWS = 32

_HI = lax.Precision.HIGHEST


def _cparams(sem):
    return pltpu.CompilerParams(dimension_semantics=sem, vmem_limit_bytes=VMEM_LIMIT)


def _dot(a, b, precision=None):
    return jnp.dot(a, b, precision=precision, preferred_element_type=F32)


def _dot_nt(a, b, precision=None):
    return lax.dot_general(a, b, (((1,), (1,)), ((), ())), precision=precision, preferred_element_type=F32)


def _dot_tn(a, b, precision=None):
    return lax.dot_general(a, b, (((0,), (0,)), ((), ())), precision=precision, preferred_element_type=F32)


def _split_bf16(x):
    hi = x.astype(BF16)
    return hi, (x - hi.astype(F32)).astype(BF16)


def _dot3(a, b):
    ah, al = _split_bf16(a)
    bh, bl = _split_bf16(b)
    return _dot(ah, bh) + (_dot(ah, bl) + _dot(al, bh))


def _select_dot(sel, x, left=False):
    p1 = x.astype(BF16)
    r1 = x - p1.astype(F32)
    p2 = r1.astype(BF16)
    p3 = (r1 - p2.astype(F32)).astype(BF16)
    if left:
        return _dot(sel, p1) + (_dot(sel, p2) + _dot(sel, p3))
    return _dot(p1, sel) + (_dot(p2, sel) + _dot(p3, sel))


def _silu(x):
    return x * jax.nn.sigmoid(x)


def _softplus(x):
    return jnp.maximum(x, 0.0) + jnp.log1p(jnp.exp(-jnp.abs(x)))


def _iota(shape, dim):
    return lax.broadcasted_iota(I32, shape, dim)


def _mod_body(c_ref, w_ref, b_ref, o_ref):
    c = c_ref[...]
    o_ref[...] = jnp.sum(w_ref[...] * _silu(c), axis=0, keepdims=True) + b_ref[...]


def _modulation(c, w_mod, b_mod):
    nl, d, n = w_mod.shape
    tn = 768
    return pl.pallas_call(
        _mod_body,
        grid=(nl, n // tn),
        in_specs=[
            pl.BlockSpec((d, 1), lambda l, j: (0, 0)),
            pl.BlockSpec((None, d, tn), lambda l, j: (l, 0, j)),
            pl.BlockSpec((None, 1, tn), lambda l, j: (l, 0, j)),
        ],
        out_specs=pl.BlockSpec((None, 1, tn), lambda l, j: (l, 0, j)),
        out_shape=jax.ShapeDtypeStruct((nl, 1, n), F32),
        compiler_params=_cparams(("arbitrary", "arbitrary")),
        name="adaln_mod",
    )(c.reshape(d, 1), w_mod, b_mod.reshape(nl, 1, n))


def _adaln_rows(x, g, scale, shift):
    r = lax.rsqrt(jnp.mean(x * x, axis=-1, keepdims=True) + EPS)
    return x * r * (g * (1.0 + scale)) + shift


def _norm_mm_body(*refs, starts):
    x_ref, g_ref, sc_ref, sh_ref, ws_ref = refs[:5]
    w_refs = refs[5:5 + len(starts)]
    o_ref, os_ref, h_ref = refs[5 + len(starts):]
    j = pl.program_id(1)

    @pl.when(j == 0)
    def _():
        ws = ws_ref[...].astype(BF16)

        def norm_rows(c, carry):
            rs = pl.ds(pl.multiple_of(c * PROLOGUE_ROWS, PROLOGUE_ROWS), PROLOGUE_ROWS)
            h = _adaln_rows(x_ref[rs, :], g_ref[...], sc_ref[...], sh_ref[...]).astype(BF16)
            h_ref[rs, :] = h
            os_ref[rs, :] = _dot_nt(h, ws)
            return carry

        lax.fori_loop(0, x_ref.shape[0] // PROLOGUE_ROWS, norm_rows, 0)

    bounds = list(starts[1:]) + [None]
    for w_ref, lo, hi in zip(w_refs, starts, bounds):
        in_seg = (j >= lo) if hi is None else ((j >= lo) & (j < hi))

        @pl.when(in_seg)
        def _(w_ref=w_ref):
            o_ref[...] = _dot_nt(h_ref[...], w_ref[...].astype(BF16)).astype(o_ref.dtype)


def _norm_matmul(x, g, scale, shift, segments, w_small, *, tm=1024, tn=512):
    t, d = x.shape
    tm = min(tm, t)
    starts, specs, n_main = [], [], 0
    for w, nb in segments:
        lo = n_main // tn
        starts.append(lo)
        specs.append(pl.BlockSpec((tn, d), lambda i, j, lo=lo, nb=nb: (jnp.clip(j - lo, 0, nb - 1), 0)))
        n_main += nb * tn
    vec = pl.BlockSpec((1, d), lambda i, j: (0, 0))
    return pl.pallas_call(
        functools.partial(_norm_mm_body, starts=tuple(starts)),
        grid=(t // tm, n_main // tn),
        in_specs=[pl.BlockSpec((tm, d), lambda i, j: (i, 0)), vec, vec, vec,
                  pl.BlockSpec((LANES, d), lambda i, j: (0, 0))] + specs,
        out_specs=[pl.BlockSpec((tm, tn), lambda i, j: (i, j)),
                   pl.BlockSpec((tm, LANES), lambda i, j: (i, 0))],
        out_shape=[jax.ShapeDtypeStruct((t, n_main), BF16), jax.ShapeDtypeStruct((t, LANES), F32)],
        scratch_shapes=[pltpu.VMEM((tm, d), BF16)],
        compiler_params=_cparams(("arbitrary", "arbitrary")),
        name="adaln_in_proj",
    )(x, g, scale, shift, w_small, *[w for w, _ in segments])


def _causal_conv(halo, x, w, taps):
    hr = halo.shape[0]
    n = x.shape[0]
    xe = jnp.concatenate([halo, x], axis=0)
    rolled = {0: xe}
    acc = None
    for j in range(taps):
        s = taps - 1 - j
        a, b = divmod(s, 8)
        if b not in rolled:
            rolled[b] = pltpu.roll(xe, b, axis=0)
        term = rolled[b][hr - 8 * a:hr - 8 * a + n] * w[j:j + 1]
        acc = term if acc is None else acc + term
    return acc


def _unit_lower_inverses(a_mats, row, col):
    n = a_mats[0].shape[0]
    eye = (row == col).astype(F32)
    level = ((row >> 1) == (col >> 1)) & (col < row)
    invs = [eye - jnp.where(level, a, 0.0) for a in a_mats]
    k = 1
    while (1 << k) < n:
        level = ((row >> (k + 1)) == (col >> (k + 1))) & (((row >> k) & 1) == 1) & (((col >> k) & 1) == 0)
        splits = [_split_bf16(inv) for inv in invs]
        ys = []
        for (ih, il), a in zip(splits, a_mats):
            mh, ml = _split_bf16(jnp.where(level, a, 0.0))
            ys.append(_dot(ih, mh) + (_dot(ih, ml) + _dot(il, mh)))
        new = []
        for inv, (ih, il), y in zip(invs, splits, ys):
            yh, yl = _split_bf16(y)
            new.append(inv - (_dot(yh, ih) + (_dot(yh, il) + _dot(yl, ih))))
        invs = new
        k += 1
    return invs


def _gdn_body(q_ref, k_ref, v_ref, qh_ref, kh_ref, vh_ref, z_ref, ba_ref, cwq_ref, cwk_ref, cwv_ref,
              par_ref, hg_ref, o_ref, s_ref, *, hb, rows):
    hblk = pl.program_id(0)
    ir = pl.program_id(1)

    @pl.when(ir == 0)
    def _():
        s_ref[...] = jnp.zeros_like(s_ref)

    def conv_silu(x_ref, halo_ref, w_ref):
        halo = jnp.where(ir > 0, halo_ref[...].astype(F32), 0.0)
        return _silu(_causal_conv(halo, x_ref[...].astype(F32), w_ref[...], 4))

    qc = conv_silu(q_ref, qh_ref, cwq_ref)
    kc = conv_silu(k_ref, kh_ref, cwk_ref)
    vc = conv_silu(v_ref, vh_ref, cwv_ref)

    ba = ba_ref[...]
    beta_all = jax.nn.sigmoid(ba)
    g_all = -jnp.exp(par_ref[0:1, :]) * _softplus(ba + par_ref[1:2, :])
    rr = _iota((rows, rows), 0)
    cc = _iota((rows, rows), 1)
    chunk_tril = ((rr >> CHUNK_LOG2) == (cc >> CHUNK_LOG2)) & (cc <= rr)
    gcs_all = _select_dot(chunk_tril.astype(BF16), g_all, left=True)
    gcs_t = gcs_all.T

    row = _iota((CHUNK, CHUNK), 0)
    col = _iota((CHUNK, CHUNK), 1)
    scale = HEAD_W ** -0.5
    hg = hg_ref[...]
    bf = lambda a: a.astype(BF16)
    lane = _iota((rows, LANES), 1)
    sub = _iota((LANES, rows), 0)
    nc = rows // CHUNK

    heads = []
    for h in range(hb):
        hs = slice(h * HEAD_W, (h + 1) * HEAD_W)
        head = hblk * hb + h
        q = qc[:, hs]
        k = kc[:, hs]
        heads.append(dict(
            q=q * lax.rsqrt(jnp.sum(q * q, axis=-1, keepdims=True) + EPS) * scale,
            k=k * lax.rsqrt(jnp.sum(k * k, axis=-1, keepdims=True) + EPS),
            v=vc[:, hs],
            beta=jnp.sum(jnp.where(lane == head, beta_all, 0.0), axis=-1, keepdims=True),
            gcs=jnp.sum(jnp.where(lane == head + GDN_HEADS, gcs_all, 0.0), axis=-1, keepdims=True),
            gcs_row=jnp.sum(jnp.where(sub == head + GDN_HEADS, gcs_t, 0.0), axis=0, keepdims=True)))

    items = []
    for c in range(nc):
        rs = slice(c * CHUNK, (c + 1) * CHUNK)
        for hd in heads:
            qq, kk, vv, beta, gcs = hd["q"][rs], hd["k"][rs], hd["v"][rs], hd["beta"][rs], hd["gcs"][rs]
            decay = jnp.where(col <= row, jnp.exp(jnp.minimum(gcs - hd["gcs_row"][:, rs], 0.0)), 0.0)
            kb = kk * beta
            eg = jnp.exp(gcs)
            g_last = gcs[CHUNK - 1:CHUNK]
            items.append(dict(decay=decay, kb16=bf(kb), k16=bf(kk), q16=bf(qq),
                              rhs16=bf(jnp.concatenate([vv * beta, kb * eg], axis=-1)),
                              q_dec16=bf(qq * eg), k_dec16=bf(kk * jnp.exp(g_last - gcs)), cd=jnp.exp(g_last)))
    a_mats = [jnp.where(col < row, _dot_nt(it["kb16"], it["k16"]) * it["decay"], 0.0) for it in items]
    qks = [bf(_dot_nt(it["q16"], it["k16"]) * it["decay"]) for it in items]
    t_invs = _unit_lower_inverses(a_mats, row, col)
    sols = [_dot(bf(t), it["rhs16"]) for t, it in zip(t_invs, items)]

    states = [s_ref[h] for h in range(hb)]
    for c in range(nc):
        rs = slice(c * CHUNK, (c + 1) * CHUNK)
        cur = slice(c * hb, (c + 1) * hb)
        s16 = [bf(s) for s in states]
        us = [so[:, :HEAD_W] - _dot(bf(so[:, HEAD_W:]), s) for so, s in zip(sols[cur], s16)]
        outs = [_dot(it["q_dec16"], s) + _dot(qk, bf(u)) for it, s, qk, u in zip(items[cur], s16, qks[cur], us)]
        states = [s * it["cd"] + _dot_tn(it["k_dec16"], bf(u)) for s, it, u in zip(states, items[cur], us)]
        for h, o in enumerate(outs):
            hs = slice(h * HEAD_W, (h + 1) * HEAD_W)
            o = o * lax.rsqrt(jnp.mean(o * o, axis=-1, keepdims=True) + EPS) * hg * _silu(z_ref[rs, hs].astype(F32))
            o_ref[rs, hs] = o.astype(o_ref.dtype)
    for h in range(hb):
        s_ref[h] = states[h]


def _gdn(proj, ba, conv_w, par, head_g, *, hb=GDN_HEADS, rows=256):
    t = proj.shape[0]
    rows = min(rows, t)
    w = hb * HEAD_W
    per = (GDN_HEADS * HEAD_W) // w

    def sec(k):
        return pl.BlockSpec((rows, w), lambda h, i, k=k: (i, k * per + h))

    def halo(k):
        return pl.BlockSpec((HALO_ROWS, w),
                            lambda h, i, k=k: (jnp.maximum(i * (rows // HALO_ROWS) - 1, 0), k * per + h))

    def cw(k):
        return pl.BlockSpec((4, w), lambda h, i, k=k: (0, k * per + h))

    return pl.pallas_call(
        functools.partial(_gdn_body, hb=hb, rows=rows),
        grid=(GDN_HEADS // hb, t // rows),
        in_specs=[sec(0), sec(1), sec(2), halo(0), halo(1), halo(2), sec(3),
                  pl.BlockSpec((rows, LANES), lambda h, i: (i, 0)),
                  cw(0), cw(1), cw(2),
                  pl.BlockSpec((8, LANES), lambda h, i: (0, 0)),
                  pl.BlockSpec((1, HEAD_W), lambda h, i: (0, 0))],
        out_specs=pl.BlockSpec((rows, w), lambda h, i: (i, h)),
        out_shape=jax.ShapeDtypeStruct((t, GDN_HEADS * HEAD_W), BF16),
        scratch_shapes=[pltpu.VMEM((hb, HEAD_W, HEAD_W), F32)],
        compiler_params=_cparams(("arbitrary", "arbitrary")),
        name="gated_deltanet",
    )(proj, proj, proj, proj, proj, proj, proj, ba, conv_w, conv_w, conv_w, par, head_g)


def _conf_body(x_ref, halo_ref, w_ref, b_ref, g_ref, lb_ref, o_ref):
    ir = pl.program_id(0)

    def glu(v):
        v = v.astype(F32)
        return v[:, :CONF_CH] * jax.nn.sigmoid(v[:, CONF_CH:])

    halo = jnp.where(ir > 0, glu(halo_ref[...]), 0.0)
    u = _causal_conv(halo, glu(x_ref[...]), w_ref[...], CONF_K) + b_ref[...]
    mu = jnp.mean(u, axis=-1, keepdims=True)
    uc = u - mu
    var = jnp.mean(uc * uc, axis=-1, keepdims=True)
    y = uc * lax.rsqrt(var + EPS) * g_ref[...] + lb_ref[...]
    o_ref[...] = _silu(y).astype(o_ref.dtype)


def _conformer(proj, dw, dw_b, ln_g, ln_b, *, rows=256):
    t = proj.shape[0]
    rows = min(rows, t)
    glu_blk = (4 * GDN_HEADS * HEAD_W) // (2 * CONF_CH)
    halo_rows = CONF_HALO_ROWS
    vec = pl.BlockSpec((1, CONF_CH), lambda i: (0, 0))
    return pl.pallas_call(
        _conf_body,
        grid=(t // rows,),
        in_specs=[pl.BlockSpec((rows, 2 * CONF_CH), lambda i: (i, glu_blk)),
                  pl.BlockSpec((halo_rows, 2 * CONF_CH),
                               lambda i: (jnp.maximum(i * (rows // halo_rows) - 1, 0), glu_blk)),
                  pl.BlockSpec((CONF_K, CONF_CH), lambda i: (0, 0)), vec, vec, vec],
        out_specs=pl.BlockSpec((rows, CONF_CH), lambda i: (i, 0)),
        out_shape=jax.ShapeDtypeStruct((t, CONF_CH), BF16),
        compiler_params=_cparams(("arbitrary",)),
        name="conformer_conv",
    )(proj, proj, dw, dw_b, ln_g, ln_b)


def _out0_body(a_ref, b_ref, wa_ref, wb_ref, x_ref, gate_ref, o_ref):
    mix = _dot(a_ref[...], wa_ref[...].astype(BF16)) + _dot(b_ref[...], wb_ref[...].astype(BF16))
    o_ref[...] = x_ref[...] + gate_ref[...] * mix


def _out_proj0(a, b, w, x, gate, *, tm=1024, tn=1024):
    t, d = x.shape
    tm = min(tm, t)
    ka, kb = a.shape[1], b.shape[1]
    return pl.pallas_call(
        _out0_body,
        grid=(t // tm, d // tn),
        in_specs=[pl.BlockSpec((tm, ka), lambda i, j: (i, 0)),
                  pl.BlockSpec((tm, kb), lambda i, j: (i, 0)),
                  pl.BlockSpec((ka, tn), lambda i, j: (0, j)),
                  pl.BlockSpec((kb, tn), lambda i, j: (ka // kb, j)),
                  pl.BlockSpec((tm, tn), lambda i, j: (i, j)),
                  pl.BlockSpec((1, tn), lambda i, j: (0, j))],
        out_specs=pl.BlockSpec((tm, tn), lambda i, j: (i, j)),
        out_shape=jax.ShapeDtypeStruct((t, d), F32),
        compiler_params=_cparams(("arbitrary", "arbitrary")),
        name="out_proj_even",
    )(a, b, w, w, x, gate)


def _out1_body(y_ref, g_ref, w_ref, x_ref, gate_ref, o_ref, h_ref):
    @pl.when(pl.program_id(1) == 0)
    def _():
        def norm_rows(c, carry):
            rs = pl.ds(pl.multiple_of(c * PROLOGUE_ROWS, PROLOGUE_ROWS), PROLOGUE_ROWS)
            y = y_ref[rs, :].astype(F32)
            r = lax.rsqrt(jnp.mean(y * y, axis=-1, keepdims=True) + EPS)
            h_ref[rs, :] = (y * r * g_ref[...]).astype(BF16)
            return carry

        lax.fori_loop(0, y_ref.shape[0] // PROLOGUE_ROWS, norm_rows, 0)

    o_ref[...] = x_ref[...] + gate_ref[...] * _dot(h_ref[...], w_ref[...].astype(BF16))


def _out_proj1(y, norm_g, w, x, gate, *, tm=1024, tn=512):
    t, d = x.shape
    tm = min(tm, t)
    k = y.shape[1]
    return pl.pallas_call(
        _out1_body,
        grid=(t // tm, d // tn),
        in_specs=[pl.BlockSpec((tm, k), lambda i, j: (i, 0)),
                  pl.BlockSpec((1, k), lambda i, j: (0, 0)),
                  pl.BlockSpec((k, tn), lambda i, j: (0, j)),
                  pl.BlockSpec((tm, tn), lambda i, j: (i, j)),
                  pl.BlockSpec((1, tn), lambda i, j: (0, j))],
        out_specs=pl.BlockSpec((tm, tn), lambda i, j: (i, j)),
        out_shape=jax.ShapeDtypeStruct((t, d), F32),
        scratch_shapes=[pltpu.VMEM((tm, k), BF16)],
        compiler_params=_cparams(("arbitrary", "arbitrary")),
        name="out_proj_odd",
    )(y, norm_g, w, x, gate)


def _ssd_body(x_ref, b_ref, c_ref, xh_ref, bh_ref, ch_ref, z_ref, dt_ref, cwx_ref, cwb_ref, cwc_ref,
              cbx_ref, cbb_ref, cbc_ref, par_ref, o_ref, s_ref, at_ref, *, rows):
    grp = pl.program_id(0)
    ir = pl.program_id(1)
    gw = SSM_GROUP_W
    hpg = gw // SSM_HEADDIM

    @pl.when(ir == 0)
    def _():
        s_ref[...] = jnp.zeros_like(s_ref)

    def conv_silu(x_r, halo_r, w_r, bias_r):
        halo = jnp.where(ir > 0, halo_r[...].astype(F32), 0.0)
        return _silu(_causal_conv(halo, x_r[...].astype(F32), w_r[...], 4) + bias_r[...])

    xs = conv_silu(x_ref, xh_ref, cwx_ref, cbx_ref)
    bm = conv_silu(b_ref, bh_ref, cwb_ref, cbb_ref)
    cm = conv_silu(c_ref, ch_ref, cwc_ref, cbc_ref)

    dt = _softplus(dt_ref[...] + par_ref[0:1, :])
    da = dt * (-jnp.exp(par_ref[1:2, :]))
    rr = _iota((rows, rows), 0)
    cc = _iota((rows, rows), 1)
    chunk_tril = ((rr >> CHUNK_LOG2) == (cc >> CHUNK_LOG2)) & (cc <= rr)
    acs = _select_dot(chunk_tril.astype(BF16), da, left=True)
    at_ref[...] = acs.T
    acs_rows = at_ref[pl.ds(pl.multiple_of(grp * hpg, hpg), hpg), :]

    sel = (_iota((LANES, gw), 0) == grp * hpg + (_iota((LANES, gw), 1) >> CHUNK_LOG2)).astype(BF16)
    dt_e = _select_dot(sel, dt)
    acs_e = _select_dot(sel, acs)
    dskip_e = _select_dot(sel, par_ref[...])[2:3, :]

    causal = (_iota((CHUNK, gw), 1) & (CHUNK - 1)) <= _iota((CHUNK, gw), 0)
    same_head = (_iota((LANES, LANES), 0) >> CHUNK_LOG2) == (_iota((LANES, LANES), 1) >> CHUNK_LOG2)
    bf = lambda a: a.astype(BF16)
    nc = rows // CHUNK
    chunks = [slice(c * CHUNK, (c + 1) * CHUNK) for c in range(nc)]
    tiles = [slice(k * LANES, (k + 1) * LANES) for k in range(gw // LANES)]

    a_es = [acs_e[rs] for rs in chunks]
    a_rows = [jnp.concatenate([acs_rows[h:h + 1, rs] for h in range(hpg)], axis=1) for rs in chunks]
    l_cats = [jnp.where(causal, jnp.exp(jnp.minimum(a_e - a_row, 0.0)), 0.0) for a_e, a_row in zip(a_es, a_rows)]
    c16 = [bf(cm[rs]) for rs in chunks]
    b16 = [bf(bm[rs]) for rs in chunks]
    cbs = [_dot_nt(c_c, jnp.concatenate([b_c] * hpg, axis=0)) for c_c, b_c in zip(c16, b16)]
    xdts = [xs[rs] * dt_e[rs] for rs in chunks]
    m16 = [bf(cb * l_cat) for cb, l_cat in zip(cbs, l_cats)]
    y_diags = []
    for m, xdt in zip(m16, xdts):
        parts = [_dot(m[:, ts], bf(jnp.where(same_head, jnp.concatenate([xdt[:, ts]] * 2, axis=0), 0.0)))
                 for ts in tiles]
        y_diags.append(jnp.concatenate(parts, axis=1))
    a_lasts = [a_e[CHUNK - 1:CHUNK] for a_e in a_es]
    updates = [_dot_tn(b_c, bf(xdt * jnp.exp(a_last - a_e)))
               for b_c, xdt, a_last, a_e in zip(b16, xdts, a_lasts, a_es)]

    states = [s_ref[...]]
    for a_last, upd in zip(a_lasts, updates):
        states.append(states[-1] * jnp.exp(a_last) + upd)
    s_ref[...] = states[-1]

    y_offs = [_dot(c_c, bf(s)) * jnp.exp(a_e) for c_c, s, a_e in zip(c16, states[:-1], a_es)]
    for rs, y_diag, y_off in zip(chunks, y_diags, y_offs):
        y = y_diag + y_off + dskip_e * xs[rs]
        o_ref[rs, :] = (y * _silu(z_ref[rs, :].astype(F32))).astype(o_ref.dtype)


def _ssd(proj, dt_raw, conv_w, conv_b, par, *, rows=256):
    t = proj.shape[0]
    rows = min(rows, t)
    gw = SSM_GROUP_W
    d_inner = SSM_GROUPS * gw
    xo = d_inner // gw
    bo = (2 * d_inner) // LANES
    co = bo + SSM_GROUPS * SSM_STATE // LANES
    hrow = lambda i: jnp.maximum(i * (rows // HALO_ROWS) - 1, 0)
    cb_x = d_inner // gw
    return pl.pallas_call(
        functools.partial(_ssd_body, rows=rows),
        grid=(SSM_GROUPS, t // rows),
        in_specs=[pl.BlockSpec((rows, gw), lambda g, i: (i, xo + g)),
                  pl.BlockSpec((rows, LANES), lambda g, i: (i, bo + g)),
                  pl.BlockSpec((rows, LANES), lambda g, i: (i, co + g)),
                  pl.BlockSpec((HALO_ROWS, gw), lambda g, i: (hrow(i), xo + g)),
                  pl.BlockSpec((HALO_ROWS, LANES), lambda g, i: (hrow(i), bo + g)),
                  pl.BlockSpec((HALO_ROWS, LANES), lambda g, i: (hrow(i), co + g)),
                  pl.BlockSpec((rows, gw), lambda g, i: (i, g)),
                  pl.BlockSpec((rows, LANES), lambda g, i: (i, 0)),
                  pl.BlockSpec((4, gw), lambda g, i: (0, g)),
                  pl.BlockSpec((4, LANES), lambda g, i: (0, cb_x * (gw // LANES) + g)),
                  pl.BlockSpec((4, LANES), lambda g, i: (0, cb_x * (gw // LANES) + SSM_GROUPS + g)),
                  pl.BlockSpec((1, gw), lambda g, i: (0, g)),
                  pl.BlockSpec((1, LANES), lambda g, i: (0, cb_x * (gw // LANES) + g)),
                  pl.BlockSpec((1, LANES), lambda g, i: (0, cb_x * (gw // LANES) + SSM_GROUPS + g)),
                  pl.BlockSpec((8, LANES), lambda g, i: (0, 0))],
        out_specs=pl.BlockSpec((rows, gw), lambda g, i: (i, g)),
        out_shape=jax.ShapeDtypeStruct((t, d_inner), BF16),
        scratch_shapes=[pltpu.VMEM((SSM_STATE, gw), F32), pltpu.VMEM((LANES, rows), F32)],
        compiler_params=_cparams(("arbitrary", "arbitrary")),
        name="ssd",
    )(proj, proj, proj, proj, proj, proj, proj, dt_raw, conv_w, conv_w, conv_w, conv_b, conv_b, conv_b, par)


def _store_token_tiles(tile_ref, x):
    rows, d = x.shape
    bits = lax.bitcast_convert_type(x.astype(BF16).astype(F32), U32)
    packed = (bits[:, :d // 2] >> 16) | (bits[:, d // 2:] & jnp.uint32(0xFFFF0000))
    for s in range(TOKEN_TILE_ROWS):
        tile_ref[pl.ds(s, rows, stride=TOKEN_TILE_ROWS), :] = packed[:, s * LANES:(s + 1) * LANES]


def _load_token_tiles(tile_ref, rows):
    lo, hi = [], []
    for s in range(TOKEN_TILE_ROWS):
        p = tile_ref[pl.ds(s, rows, stride=TOKEN_TILE_ROWS), :]
        lo.append(lax.bitcast_convert_type(p << 16, F32))
        hi.append(lax.bitcast_convert_type(p & jnp.uint32(0xFFFF0000), F32))
    return lo, hi


def _lane_pick(lane, idx, vals):
    return jnp.sum(jnp.where(lane == idx, vals, 0.0), axis=-1, keepdims=True)


def _router_body(x_ref, g_ref, sc_ref, sh_ref, w_ref, b_ref, hp_ref, route_ref, cnt_ref, carry_ref, *, tm):
    i = pl.program_id(0)

    @pl.when(i == 0)
    def _():
        carry_ref[...] = jnp.zeros_like(carry_ref)

    h = _adaln_rows(x_ref[...], g_ref[...], sc_ref[...], sh_ref[...])
    logits = _dot3(h, w_ref[...]) + b_ref[...]
    lane = _iota((tm, LANES), 1).astype(F32)
    neg = -jnp.inf
    gl = jnp.where(lane < N_GROUPS, logits, neg)
    gmax = jnp.max(gl, axis=-1, keepdims=True)
    grp_p = 1.0 / jnp.sum(jnp.exp(gl - gmax), axis=-1, keepdims=True)
    gidx = jnp.min(jnp.where(gl == gmax, lane, float(LANES)), axis=-1, keepdims=True)
    lo = N_GROUPS + EXPERTS_PER_GROUP * gidx
    el = jnp.where((lane >= lo) & (lane < lo + EXPERTS_PER_GROUP), logits, neg)
    m1 = jnp.max(el, axis=-1, keepdims=True)
    i1 = jnp.min(jnp.where(el == m1, lane, float(LANES)), axis=-1, keepdims=True)
    el2 = jnp.where(lane == i1, neg, el)
    m2 = jnp.max(el2, axis=-1, keepdims=True)
    i2 = jnp.min(jnp.where(el2 == m2, lane, float(LANES)), axis=-1, keepdims=True)
    tt = jnp.exp(m2 - m1)
    g1 = grp_p / (1.0 + tt)
    g2 = g1 * tt
    e1 = i1 - N_GROUPS
    e2 = i2 - N_GROUPS

    onehot = ((lane == e1) | (lane == e2)).astype(F32)
    strict = (_iota((tm, tm), 1) < _iota((tm, tm), 0)).astype(BF16)
    rank = _dot(strict, onehot.astype(BF16)) + carry_ref[0:1, :]
    r1 = _lane_pick(lane, e1, rank)
    r2 = _lane_pick(lane, e2, rank)
    carry_ref[...] = carry_ref[...] + jnp.sum(onehot, axis=0, keepdims=True)
    cnt_ref[...] = carry_ref[...]

    route = jnp.where(lane == 0, e1, jnp.where(lane == 1, e2, jnp.where(lane == 2, r1, jnp.where(
        lane == 3, r2, jnp.where(lane == 4, g1, jnp.where(lane == 5, g2, 0.0))))))
    route_ref[...] = route

    _store_token_tiles(hp_ref, h)


def _router(x, g, scale, shift, w_route, b_route, *, tm=512):
    t, d = x.shape
    vec = pl.BlockSpec((1, d), lambda i: (0, 0))
    return pl.pallas_call(
        functools.partial(_router_body, tm=tm),
        grid=(t // tm,),
        in_specs=[pl.BlockSpec((tm, d), lambda i: (i, 0)), vec, vec, vec,
                  pl.BlockSpec((d, LANES), lambda i: (0, 0)),
                  pl.BlockSpec((1, LANES), lambda i: (0, 0))],
        out_specs=[pl.BlockSpec((tm * TOKEN_TILE_ROWS, LANES), lambda i: (i, 0)),
                   pl.BlockSpec((tm, LANES), lambda i: (i, 0)),
                   pl.BlockSpec((8, LANES), lambda i: (0, 0))],
        out_shape=[jax.ShapeDtypeStruct((t * TOKEN_TILE_ROWS, LANES), U32),
                   jax.ShapeDtypeStruct((t, LANES), F32),
                   jax.ShapeDtypeStruct((8, LANES), F32)],
        scratch_shapes=[pltpu.VMEM((8, LANES), F32)],
        compiler_params=_cparams(("arbitrary",)),
        name="moe_router",
    )(x, g, scale, shift, w_route, b_route)


def _dest_body(route_ref, cnt_ref, dest_ref, blk_ref, *, tm, n_blocks_pad):
    cnt = cnt_ref[...]
    padded = jnp.floor((cnt + (MOE_ROWS - 1)) * (1.0 / MOE_ROWS)) * MOE_ROWS
    upper = (_iota((LANES, LANES), 0) <= _iota((LANES, LANES), 1)).astype(F32)
    pad_end = _dot(padded, upper, _HI)
    pad_start = (pad_end - padded)[0:1, :]
    route = route_ref[...]
    lane = _iota((tm, LANES), 1).astype(F32)
    e1, e2, r1, r2 = route[:, 0:1], route[:, 1:2], route[:, 2:3], route[:, 3:4]
    d1 = _lane_pick(lane, e1, pad_start) + r1
    d2 = _lane_pick(lane, e2, pad_start) + r2
    dest_ref[...] = jnp.where(lane == 0, d1, jnp.where(lane == 1, d2, 0.0)).astype(I32)

    lane_b = _iota((n_blocks_pad, LANES), 1)
    first_row = (_iota((n_blocks_pad, LANES), 0) * MOE_ROWS).astype(F32)
    done = ((pad_end[0:1, :] <= first_row) & (lane_b < N_EXPERTS)).astype(F32)
    blk_e = jnp.minimum(jnp.sum(done, axis=-1, keepdims=True), float(N_EXPERTS - 1))
    n_valid = pad_end[0:1, N_EXPERTS - 1:N_EXPERTS] * (1.0 / MOE_ROWS)
    blk_ref[...] = jnp.where(lane_b == 0, blk_e, jnp.where(lane_b == 1, n_valid, 0.0)).astype(I32)


def _dest(route, counts, n_blocks_pad, *, tm=512):
    t = route.shape[0]
    assert t % tm == 0
    return pl.pallas_call(
        functools.partial(_dest_body, tm=tm, n_blocks_pad=n_blocks_pad),
        grid=(t // tm,),
        in_specs=[pl.BlockSpec((tm, LANES), lambda i: (i, 0)),
                  pl.BlockSpec((8, LANES), lambda i: (0, 0))],
        out_specs=[pl.BlockSpec((tm, LANES), lambda i: (i, 0)),
                   pl.BlockSpec((n_blocks_pad, LANES), lambda i: (0, 0))],
        out_shape=[jax.ShapeDtypeStruct((t, LANES), I32),
                   jax.ShapeDtypeStruct((n_blocks_pad, LANES), I32)],
        compiler_params=_cparams(("arbitrary",)),
        name="moe_dest",
    )(route, counts)


def _invert_body(dest_ref, tok_ref, *, n_assign, n_rows):
    def fill(r, c):
        tok_ref[r] = 0
        return c

    lax.fori_loop(0, n_rows, fill, 0, unroll=8)

    def scatter(a, c):
        tok_ref[dest_ref[a]] = a >> 1
        return c

    lax.fori_loop(0, n_assign, scatter, 0, unroll=8)


def _invert(dest_flat, n_rows):
    n_assign = dest_flat.shape[0]
    return pl.pallas_call(
        functools.partial(_invert_body, n_assign=n_assign, n_rows=n_rows),
        grid_spec=pltpu.PrefetchScalarGridSpec(
            num_scalar_prefetch=1,
            grid=(1,),
            in_specs=[],
            out_specs=pl.BlockSpec(memory_space=pltpu.SMEM),
        ),
        out_shape=jax.ShapeDtypeStruct((n_rows,), I32),
        compiler_params=_cparams(("arbitrary",)),
        name="moe_invert",
    )(dest_flat)


def _expert_body(be_ref, nv_ref, tok_ref, hp_ref, w1_ref, w3_ref, w2_ref, y_ref, xbuf_ref, sem, wst1_ref, wst3_ref,
                 wst2_ref, wsem, w1b_ref, w3b_ref, w2b_ref, *, layer):
    b = pl.program_id(0)
    n_blocks = pl.num_programs(0)
    wst_refs = (wst1_ref, wst3_ref, wst2_ref)
    n_valid = nv_ref[0]
    valid = b < n_valid
    tr = TOKEN_TILE_ROWS
    expert = be_ref[b]
    first_of_expert = valid & ((b == 0) | (expert != be_ref[jnp.maximum(b - 1, 0)]))

    def weight_copies(e):
        return [pltpu.make_async_copy(w_ref.at[layer, e], st_ref, wsem.at[i])
                for i, (w_ref, st_ref) in enumerate(zip((w1_ref, w3_ref, w2_ref), wst_refs))]

    @pl.when((b == 0) & valid)
    def _():
        for c in weight_copies(expert):
            c.start()

    def gather_rows(blk, slot):
        def one(j, c):
            tok = tok_ref[blk * MOE_ROWS + j]
            pltpu.make_async_copy(hp_ref.at[pl.ds(tok * tr, tr)], xbuf_ref.at[slot, pl.ds(j * tr, tr)],
                                  sem.at[slot]).start()
            return c

        lax.fori_loop(0, MOE_ROWS, one, 0, unroll=8)

    @pl.when((b == 0) & valid)
    def _():
        gather_rows(0, 0)

    @pl.when(b + 1 < n_valid)
    def _():
        gather_rows(b + 1, (b + 1) % 2)

    @pl.when(first_of_expert)
    def _():
        for c in weight_copies(expert):
            c.wait()
        for wb_ref, st_ref in zip((w1b_ref, w3b_ref, w2b_ref), wst_refs):
            wb_ref[...] = st_ref[...].astype(BF16)
        nxt = lax.while_loop(lambda j: (j < n_valid) & (be_ref[jnp.minimum(j, n_blocks - 1)] == expert),
                             lambda j: j + 1, b + 1)

        @pl.when(nxt < n_valid)
        def _():
            for c in weight_copies(be_ref[jnp.minimum(nxt, n_blocks - 1)]):
                c.start()

    @pl.when(jnp.logical_not(valid))
    def _():
        y_ref[...] = jnp.zeros_like(y_ref)

    @pl.when(valid)
    def _():
        slot = b % 2
        pltpu.make_async_copy(hp_ref.at[pl.ds(0, MOE_ROWS * tr)], xbuf_ref.at[slot], sem.at[slot]).wait()
        lo, hi = _load_token_tiles(xbuf_ref.at[slot], MOE_ROWS)
        x = jnp.concatenate(lo + hi, axis=1).astype(BF16)
        a = (_silu(_dot(x, w1b_ref[...])) * _dot(x, w3b_ref[...])).astype(BF16)
        _store_token_tiles(y_ref, _dot(a, w2b_ref[...]))


def _experts(block_expert, n_valid, tok_rows, h_packed, w1, w3, w2, *, layer):
    n_blocks = block_expert.shape[0]
    _, _, d, f = w1.shape
    anywhere = pl.BlockSpec(memory_space=pl.ANY)
    return pl.pallas_call(
        functools.partial(_expert_body, layer=layer),
        grid_spec=pltpu.PrefetchScalarGridSpec(
            num_scalar_prefetch=3,
            grid=(n_blocks,),
            in_specs=[anywhere, anywhere, anywhere, anywhere],
            out_specs=pl.BlockSpec((MOE_ROWS * TOKEN_TILE_ROWS, LANES), lambda b, be, nv, tk: (b, 0)),
            scratch_shapes=[pltpu.VMEM((2, MOE_ROWS * TOKEN_TILE_ROWS, LANES), U32), pltpu.SemaphoreType.DMA((2,)),
                            pltpu.VMEM((d, f), F32), pltpu.VMEM((d, f), F32), pltpu.VMEM((f, d), F32),
                            pltpu.SemaphoreType.DMA((3,)),
                            pltpu.VMEM((d, f), BF16), pltpu.VMEM((d, f), BF16), pltpu.VMEM((f, d), BF16)],
        ),
        out_shape=jax.ShapeDtypeStruct((n_blocks * MOE_ROWS * TOKEN_TILE_ROWS, LANES), U32),
        compiler_params=_cparams(("arbitrary",)),
        name="moe_experts",
    )(block_expert, n_valid, tok_rows, h_packed, w1, w3, w2)


def _combine_body(dest_ref, y_ref, x_ref, route_ref, gate_ref, fg_ref, o_ref, buf_ref, sem, *, tm, final_norm):
    i = pl.program_id(0)
    yr = TOKEN_TILE_ROWS

    def gather_rows(blk, slot):
        def one(j, c):
            for k in range(2):
                d = dest_ref[(blk * tm + j) * 2 + k]
                pltpu.make_async_copy(y_ref.at[pl.ds(d * yr, yr)], buf_ref.at[slot, k, pl.ds(j * yr, yr)],
                                      sem.at[slot, k]).start()
            return c

        lax.fori_loop(0, tm, one, 0, unroll=4)

    @pl.when(i == 0)
    def _():
        gather_rows(0, 0)

    @pl.when(i + 1 < pl.num_programs(0))
    def _():
        gather_rows(i + 1, (i + 1) % 2)

    slot = i % 2
    for k in range(2):
        pltpu.make_async_copy(y_ref.at[pl.ds(0, tm * yr)], buf_ref.at[slot, k], sem.at[slot, k]).wait()

    route = route_ref[...]
    g1, g2 = route[:, 4:5], route[:, 5:6]
    lo1, hi1 = _load_token_tiles(buf_ref.at[slot, 0], tm)
    lo2, hi2 = _load_token_tiles(buf_ref.at[slot, 1], tm)
    half = x_ref.shape[1] // 2
    for s in range(yr):
        for base, y1, y2 in ((0, lo1[s], lo2[s]), (half, hi1[s], hi2[s])):
            cs = slice(base + s * LANES, base + (s + 1) * LANES)
            o_ref[:, cs] = x_ref[:, cs] + gate_ref[:, cs] * (g1 * y1 + g2 * y2)
    if final_norm:
        xo = o_ref[...]
        o_ref[...] = xo * lax.rsqrt(jnp.mean(xo * xo, axis=-1, keepdims=True) + EPS) * fg_ref[...]


def _combine(dest_flat, y_rows, x, route, gate, final_g, *, final_norm, tm=128):
    t, d = x.shape
    return pl.pallas_call(
        functools.partial(_combine_body, tm=tm, final_norm=final_norm),
        grid_spec=pltpu.PrefetchScalarGridSpec(
            num_scalar_prefetch=1,
            grid=(t // tm,),
            in_specs=[pl.BlockSpec(memory_space=pl.ANY),
                      pl.BlockSpec((tm, d), lambda i, dr: (i, 0)),
                      pl.BlockSpec((tm, LANES), lambda i, dr: (i, 0)),
                      pl.BlockSpec((1, d), lambda i, dr: (0, 0)),
                      pl.BlockSpec((1, d), lambda i, dr: (0, 0))],
            out_specs=pl.BlockSpec((tm, d), lambda i, dr: (i, 0)),
            scratch_shapes=[pltpu.VMEM((2, 2, tm * TOKEN_TILE_ROWS, LANES), U32), pltpu.SemaphoreType.DMA((2, 2))],
        ),
        out_shape=jax.ShapeDtypeStruct((t, d), F32),
        compiler_params=_cparams(("arbitrary",)),
        name="moe_combine",
    )(dest_flat, y_rows, x, route, gate, final_g)


def _moe(x, g, scale, shift, gate, w_group, b_group, w_expert, b_expert, w1, w3, w2, final_g, *, layer, final_norm):
    t, d = x.shape
    pad = LANES - N_GROUPS - N_EXPERTS
    w_route = jnp.concatenate([w_group, w_expert, jnp.zeros((d, pad), F32)], axis=1)
    b_route = jnp.concatenate([b_group, b_expert, jnp.zeros((pad,), F32)]).reshape(1, LANES)
    h_packed, route, counts = _router(x, g, scale, shift, w_route, b_route)
    n_assign = 2 * t
    n_blocks = -(-(n_assign + N_EXPERTS * (MOE_ROWS - 1)) // MOE_ROWS)
    n_blocks_pad = -(-n_blocks // 8) * 8
    dest, blk = _dest(route, counts, n_blocks_pad)
    dest_flat = dest[:, :2].reshape(n_assign)
    block_expert = blk[:n_blocks, 0]
    n_valid = blk[0, 1:2]
    tok_rows = _invert(dest_flat, n_blocks * MOE_ROWS)
    y_rows = _experts(block_expert, n_valid, tok_rows, h_packed, w1, w3, w2, layer=layer)
    return _combine(dest_flat, y_rows, x, route, gate, final_g, final_norm=final_norm)


def _pad_rows(w, rows):
    return jnp.pad(w, ((0, rows - w.shape[0]), (0, 0)))


def _lane_row(v, offset):
    return jnp.zeros((LANES,), F32).at[offset:offset + v.shape[0]].set(v)


def kernel(x, c, w_mod, b_mod, norm_g, final_norm_g, e_w_in, e_conv_qkv, e_a_log, e_dt_bias, e_head_norm_g, e_conf_dw, e_conf_dw_b, e_conf_ln_g, e_conf_ln_b, e_w_out, o_w_in, o_conv_w, o_conv_b, o_dt_bias, o_a_log, o_d_skip, o_norm_g, o_w_out, moe_w_group, moe_b_group, moe_w_expert, moe_b_expert, moe_w1, moe_w3, moe_w2):
    bsz, t, d = x.shape
    assert bsz == 1 and c.shape == (1, d)
    xt = x[0]
    mod = _modulation(c[0], w_mod, b_mod)

    def mod_parts(l):
        return mod[l, :, :d], mod[l, :, d:2 * d], mod[l, :, 2 * d:]

    row = lambda v: v.reshape(1, -1)
    qkv_w = 3 * GDN_HEADS * HEAD_W
    z_end = qkv_w + GDN_HEADS * HEAD_W

    shift, scale, gate = mod_parts(0)
    w_in = e_w_in[0].T
    tn = 512
    w_glu = w_in[z_end + 2 * GDN_HEADS:]
    w_ba = _pad_rows(w_in[z_end:z_end + 2 * GDN_HEADS], LANES)
    proj0, ba = _norm_matmul(xt, row(norm_g[0, 0]), scale, shift,
                             [(w_in, z_end // tn), (w_glu, w_glu.shape[0] // tn)], w_ba, tn=tn)
    gdn_par = jnp.zeros((8, LANES), F32).at[0].set(_lane_row(e_a_log[0], GDN_HEADS)).at[1].set(
        _lane_row(e_dt_bias[0], GDN_HEADS))
    a_out = _gdn(proj0, ba, e_conv_qkv[0], gdn_par, row(e_head_norm_g[0]))
    b_out = _conformer(proj0, e_conf_dw[0], row(e_conf_dw_b[0]), row(e_conf_ln_g[0]), row(e_conf_ln_b[0]))
    xt = _out_proj0(a_out, b_out, e_w_out[0], xt, gate)
    shift, scale, gate = mod_parts(1)
    xt = _moe(xt, row(norm_g[0, 1]), scale, shift, gate, moe_w_group[0], moe_b_group[0], moe_w_expert[0],
              moe_b_expert[0], moe_w1, moe_w3, moe_w2, row(final_norm_g), layer=0, final_norm=False)

    shift, scale, gate = mod_parts(2)
    w_in = o_w_in[0].T
    n_main = 2 * SSM_GROUPS * SSM_GROUP_W + 2 * SSM_GROUPS * SSM_STATE
    proj1, dt_raw = _norm_matmul(xt, row(norm_g[1, 0]), scale, shift, [(w_in, n_main // tn)],
                                 _pad_rows(w_in[n_main:], LANES), tn=tn)
    ssd_par = jnp.zeros((8, LANES), F32).at[0].set(_lane_row(o_dt_bias[0], 0)).at[1].set(
        _lane_row(o_a_log[0], 0)).at[2].set(_lane_row(o_d_skip[0], 0))
    y = _ssd(proj1, dt_raw, o_conv_w[0], row(o_conv_b[0]), ssd_par)
    xt = _out_proj1(y, row(o_norm_g[0]), o_w_out[0], xt, gate)
    shift, scale, gate = mod_parts(3)
    xt = _moe(xt, row(norm_g[1, 1]), scale, shift, gate, moe_w_group[1], moe_b_group[1], moe_w_expert[1],
              moe_b_expert[1], moe_w1, moe_w3, moe_w2, row(final_norm_g), layer=1, final_norm=True)
    return xt[None]
```

```python
import functools

import jax
import jax.numpy as jnp
from jax import lax
from jax.experimental import pallas as pl
from jax.experimental.pallas import tpu as pltpu

F32 = jnp.float32
BF16 = jnp.bfloat16
I32 = jnp.int32
U32 = jnp.uint32

EPS = 1e-6
LANES = 128
CHUNK = 64
CHUNK_LOG2 = 6
GDN_HEADS = 8
HEAD_W = 128
CONF_CH = 1024
CONF_K = 31
SSM_GROUPS = 8
SSM_GROUP_W = 512
SSM_HEADDIM = 64
SSM_STATE = 128
N_EXPERTS = 32
EXPERTS_PER_GROUP = 8
N_GROUPS = 4
MOE_ROWS = 256
TOKEN_TILE_ROWS = 8
VMEM_LIMIT = 56 * 1024 * 1024
PROLOGUE_ROWS = 128
WEIGHT_DMA_PRIORITY = 1
HALO_ROWS = 16
CONF_HALO_ROWS = 32

_HI = lax.Precision.HIGHEST


def _cparams(sem):
    return pltpu.CompilerParams(dimension_semantics=sem, vmem_limit_bytes=VMEM_LIMIT)


def _dot(a, b, precision=None):
    return jnp.dot(a, b, precision=precision, preferred_element_type=F32)


def _dot_nt(a, b, precision=None):
    return lax.dot_general(a, b, (((1,), (1,)), ((), ())), precision=precision, preferred_element_type=F32)


def _dot_tn(a, b, precision=None):
    return lax.dot_general(a, b, (((0,), (0,)), ((), ())), precision=precision, preferred_element_type=F32)


def _split_bf16(x):
    hi = x.astype(BF16)
    return hi, (x - hi.astype(F32)).astype(BF16)


def _dot3(a, b):
    ah, al = _split_bf16(a)
    bh, bl = _split_bf16(b)
    return _dot(ah, bh) + (_dot(ah, bl) + _dot(al, bh))


def _select_dot(sel, x, left=False):
    p1 = x.astype(BF16)
    r1 = x - p1.astype(F32)
    p2 = r1.astype(BF16)
    p3 = (r1 - p2.astype(F32)).astype(BF16)
    if left:
        return _dot(sel, p1) + (_dot(sel, p2) + _dot(sel, p3))
    return _dot(p1, sel) + (_dot(p2, sel) + _dot(p3, sel))


def _silu(x):
    return x * jax.nn.sigmoid(x)


def _softplus(x):
    return jnp.maximum(x, 0.0) + jnp.log1p(jnp.exp(-jnp.abs(x)))


def _iota(shape, dim):
    return lax.broadcasted_iota(I32, shape, dim)


def _mod_body(c_ref, w_ref, b_ref, o_ref):
    c = c_ref[...]
    o_ref[...] = jnp.sum(w_ref[...] * _silu(c), axis=0, keepdims=True) + b_ref[...]


def _modulation(c, w_mod, b_mod):
    nl, d, n = w_mod.shape
    tn = 768
    return pl.pallas_call(
        _mod_body,
        grid=(nl, n // tn),
        in_specs=[
            pl.BlockSpec((d, 1), lambda l, j: (0, 0)),
            pl.BlockSpec((None, d, tn), lambda l, j: (l, 0, j)),
            pl.BlockSpec((None, 1, tn), lambda l, j: (l, 0, j)),
        ],
        out_specs=pl.BlockSpec((None, 1, tn), lambda l, j: (l, 0, j)),
        out_shape=jax.ShapeDtypeStruct((nl, 1, n), F32),
        compiler_params=_cparams(("arbitrary", "arbitrary")),
        name="adaln_mod",
    )(c.reshape(d, 1), w_mod, b_mod.reshape(nl, 1, n))


def _adaln_rows(x, g, scale, shift):
    r = lax.rsqrt(jnp.mean(x * x, axis=-1, keepdims=True) + EPS)
    return x * r * (g * (1.0 + scale)) + shift


def _norm_mm_body(*refs, starts):
    x_ref, g_ref, sc_ref, sh_ref, ws_ref = refs[:5]
    w_refs = refs[5:5 + len(starts)]
    o_ref, os_ref, h_ref = refs[5 + len(starts):]
    j = pl.program_id(1)

    @pl.when(j == 0)
    def _():
        ws = ws_ref[...].astype(BF16)

        def norm_rows(c, carry):
            rs = pl.ds(pl.multiple_of(c * PROLOGUE_ROWS, PROLOGUE_ROWS), PROLOGUE_ROWS)
            h = _adaln_rows(x_ref[rs, :], g_ref[...], sc_ref[...], sh_ref[...]).astype(BF16)
            h_ref[rs, :] = h
            os_ref[rs, :] = _dot_nt(h, ws)
            return carry

        lax.fori_loop(0, x_ref.shape[0] // PROLOGUE_ROWS, norm_rows, 0)

    bounds = list(starts[1:]) + [None]
    for w_ref, lo, hi in zip(w_refs, starts, bounds):
        in_seg = (j >= lo) if hi is None else ((j >= lo) & (j < hi))

        @pl.when(in_seg)
        def _(w_ref=w_ref):
            o_ref[...] = _dot_nt(h_ref[...], w_ref[...].astype(BF16)).astype(o_ref.dtype)


def _norm_matmul(x, g, scale, shift, segments, w_small, *, tm=1024, tn=512):
    t, d = x.shape
    tm = min(tm, t)
    starts, specs, n_main = [], [], 0
    for w, nb in segments:
        lo = n_main // tn
        starts.append(lo)
        specs.append(pl.BlockSpec((tn, d), lambda i, j, lo=lo, nb=nb: (jnp.clip(j - lo, 0, nb - 1), 0)))
        n_main += nb * tn
    vec = pl.BlockSpec((1, d), lambda i, j: (0, 0))
    return pl.pallas_call(
        functools.partial(_norm_mm_body, starts=tuple(starts)),
        grid=(t // tm, n_main // tn),
        in_specs=[pl.BlockSpec((tm, d), lambda i, j: (i, 0)), vec, vec, vec,
                  pl.BlockSpec((LANES, d), lambda i, j: (0, 0))] + specs,
        out_specs=[pl.BlockSpec((tm, tn), lambda i, j: (i, j)),
                   pl.BlockSpec((tm, LANES), lambda i, j: (i, 0))],
        out_shape=[jax.ShapeDtypeStruct((t, n_main), BF16), jax.ShapeDtypeStruct((t, LANES), F32)],
        scratch_shapes=[pltpu.VMEM((tm, d), BF16)],
        compiler_params=_cparams(("arbitrary", "arbitrary")),
        name="adaln_in_proj",
    )(x, g, scale, shift, w_small, *[w for w, _ in segments])


def _causal_conv(halo, x, w, taps):
    hr = halo.shape[0]
    n = x.shape[0]
    xe = jnp.concatenate([halo, x], axis=0)
    rolled = {0: xe}
    acc = None
    for j in range(taps):
        s = taps - 1 - j
        a, b = divmod(s, 8)
        if b not in rolled:
            rolled[b] = pltpu.roll(xe, b, axis=0)
        term = rolled[b][hr - 8 * a:hr - 8 * a + n] * w[j:j + 1]
        acc = term if acc is None else acc + term
    return acc


def _unit_lower_inverses(a_mats, row, col):
    n = a_mats[0].shape[0]
    eye = (row == col).astype(F32)
    level = ((row >> 1) == (col >> 1)) & (col < row)
    invs = [eye - jnp.where(level, a, 0.0) for a in a_mats]
    k = 1
    while (1 << k) < n:
        level = ((row >> (k + 1)) == (col >> (k + 1))) & (((row >> k) & 1) == 1) & (((col >> k) & 1) == 0)
        splits = [_split_bf16(inv) for inv in invs]
        ys = []
        for (ih, il), a in zip(splits, a_mats):
            mh, ml = _split_bf16(jnp.where(level, a, 0.0))
            ys.append(_dot(ih, mh) + (_dot(ih, ml) + _dot(il, mh)))
        new = []
        for inv, (ih, il), y in zip(invs, splits, ys):
            yh, yl = _split_bf16(y)
            new.append(inv - (_dot(yh, ih) + (_dot(yh, il) + _dot(yl, ih))))
        invs = new
        k += 1
    return invs


def _gdn_body(q_ref, k_ref, v_ref, qh_ref, kh_ref, vh_ref, z_ref, ba_ref, cwq_ref, cwk_ref, cwv_ref,
              par_ref, hg_ref, o_ref, s_ref, *, hb, rows):
    hblk = pl.program_id(0)
    ir = pl.program_id(1)

    @pl.when(ir == 0)
    def _():
        s_ref[...] = jnp.zeros_like(s_ref)

    def conv_silu(x_ref, halo_ref, w_ref):
        halo = jnp.where(ir > 0, halo_ref[...].astype(F32), 0.0)
        return _silu(_causal_conv(halo, x_ref[...].astype(F32), w_ref[...], 4))

    qc = conv_silu(q_ref, qh_ref, cwq_ref)
    kc = conv_silu(k_ref, kh_ref, cwk_ref)
    vc = conv_silu(v_ref, vh_ref, cwv_ref)

    ba = ba_ref[...]
    beta_all = jax.nn.sigmoid(ba)
    g_all = -jnp.exp(par_ref[0:1, :]) * _softplus(ba + par_ref[1:2, :])
    rr = _iota((rows, rows), 0)
    cc = _iota((rows, rows), 1)
    chunk_tril = ((rr >> CHUNK_LOG2) == (cc >> CHUNK_LOG2)) & (cc <= rr)
    gcs_all = _select_dot(chunk_tril.astype(BF16), g_all, left=True)
    gcs_t = gcs_all.T

    row = _iota((CHUNK, CHUNK), 0)
    col = _iota((CHUNK, CHUNK), 1)
    scale = HEAD_W ** -0.5
    hg = hg_ref[...]
    bf = lambda a: a.astype(BF16)
    lane = _iota((rows, LANES), 1)
    sub = _iota((LANES, rows), 0)
    nc = rows // CHUNK

    heads = []
    for h in range(hb):
        hs = slice(h * HEAD_W, (h + 1) * HEAD_W)
        head = hblk * hb + h
        q = qc[:, hs]
        k = kc[:, hs]
        heads.append(dict(
            q=q * lax.rsqrt(jnp.sum(q * q, axis=-1, keepdims=True) + EPS) * scale,
            k=k * lax.rsqrt(jnp.sum(k * k, axis=-1, keepdims=True) + EPS),
            v=vc[:, hs],
            beta=jnp.sum(jnp.where(lane == head, beta_all, 0.0), axis=-1, keepdims=True),
            gcs=jnp.sum(jnp.where(lane == head + GDN_HEADS, gcs_all, 0.0), axis=-1, keepdims=True),
            gcs_row=jnp.sum(jnp.where(sub == head + GDN_HEADS, gcs_t, 0.0), axis=0, keepdims=True)))

    items = []
    for c in range(nc):
        rs = slice(c * CHUNK, (c + 1) * CHUNK)
        for hd in heads:
            qq, kk, vv, beta, gcs = hd["q"][rs], hd["k"][rs], hd["v"][rs], hd["beta"][rs], hd["gcs"][rs]
            decay = jnp.where(col <= row, jnp.exp(jnp.minimum(gcs - hd["gcs_row"][:, rs], 0.0)), 0.0)
            kb = kk * beta
            eg = jnp.exp(gcs)
            g_last = gcs[CHUNK - 1:CHUNK]
            items.append(dict(decay=decay, kb16=bf(kb), k16=bf(kk), q16=bf(qq),
                              rhs16=bf(jnp.concatenate([vv * beta, kb * eg], axis=-1)),
                              q_dec16=bf(qq * eg), k_dec16=bf(kk * jnp.exp(g_last - gcs)), cd=jnp.exp(g_last)))
    a_mats = [jnp.where(col < row, _dot_nt(it["kb16"], it["k16"]) * it["decay"], 0.0) for it in items]
    qks = [bf(_dot_nt(it["q16"], it["k16"]) * it["decay"]) for it in items]
    t_invs = _unit_lower_inverses(a_mats, row, col)
    sols = [_dot(bf(t), it["rhs16"]) for t, it in zip(t_invs, items)]

    states = [s_ref[h] for h in range(hb)]
    for c in range(nc):
        rs = slice(c * CHUNK, (c + 1) * CHUNK)
        cur = slice(c * hb, (c + 1) * hb)
        s16 = [bf(s) for s in states]
        us = [so[:, :HEAD_W] - _dot(bf(so[:, HEAD_W:]), s) for so, s in zip(sols[cur], s16)]
        outs = [_dot(it["q_dec16"], s) + _dot(qk, bf(u)) for it, s, qk, u in zip(items[cur], s16, qks[cur], us)]
        states = [s * it["cd"] + _dot_tn(it["k_dec16"], bf(u)) for s, it, u in zip(states, items[cur], us)]
        for h, o in enumerate(outs):
            hs = slice(h * HEAD_W, (h + 1) * HEAD_W)
            o = o * lax.rsqrt(jnp.mean(o * o, axis=-1, keepdims=True) + EPS) * hg * _silu(z_ref[rs, hs].astype(F32))
            o_ref[rs, hs] = o.astype(o_ref.dtype)
    for h in range(hb):
        s_ref[h] = states[h]


def _gdn(proj, ba, conv_w, par, head_g, *, hb=GDN_HEADS, rows=256):
    t = proj.shape[0]
    rows = min(rows, t)
    w = hb * HEAD_W
    per = (GDN_HEADS * HEAD_W) // w

    def sec(k):
        return pl.BlockSpec((rows, w), lambda h, i, k=k: (i, k * per + h))

    def halo(k):
        return pl.BlockSpec((HALO_ROWS, w),
                            lambda h, i, k=k: (jnp.maximum(i * (rows // HALO_ROWS) - 1, 0), k * per + h))

    def cw(k):
        return pl.BlockSpec((4, w), lambda h, i, k=k: (0, k * per + h))

    return pl.pallas_call(
        functools.partial(_gdn_body, hb=hb, rows=rows),
        grid=(GDN_HEADS // hb, t // rows),
        in_specs=[sec(0), sec(1), sec(2), halo(0), halo(1), halo(2), sec(3),
                  pl.BlockSpec((rows, LANES), lambda h, i: (i, 0)),
                  cw(0), cw(1), cw(2),
                  pl.BlockSpec((8, LANES), lambda h, i: (0, 0)),
                  pl.BlockSpec((1, HEAD_W), lambda h, i: (0, 0))],
        out_specs=pl.BlockSpec((rows, w), lambda h, i: (i, h)),
        out_shape=jax.ShapeDtypeStruct((t, GDN_HEADS * HEAD_W), BF16),
        scratch_shapes=[pltpu.VMEM((hb, HEAD_W, HEAD_W), F32)],
        compiler_params=_cparams(("arbitrary", "arbitrary")),
        name="gated_deltanet",
    )(proj, proj, proj, proj, proj, proj, proj, ba, conv_w, conv_w, conv_w, par, head_g)


def _conf_body(x_ref, halo_ref, w_ref, b_ref, g_ref, lb_ref, o_ref):
    ir = pl.program_id(0)

    def glu(v):
        v = v.astype(F32)
        return v[:, :CONF_CH] * jax.nn.sigmoid(v[:, CONF_CH:])

    halo = jnp.where(ir > 0, glu(halo_ref[...]), 0.0)
    u = _causal_conv(halo, glu(x_ref[...]), w_ref[...], CONF_K) + b_ref[...]
    mu = jnp.mean(u, axis=-1, keepdims=True)
    uc = u - mu
    var = jnp.mean(uc * uc, axis=-1, keepdims=True)
    y = uc * lax.rsqrt(var + EPS) * g_ref[...] + lb_ref[...]
    o_ref[...] = _silu(y).astype(o_ref.dtype)


def _conformer(proj, dw, dw_b, ln_g, ln_b, *, rows=256):
    t = proj.shape[0]
    rows = min(rows, t)
    glu_blk = (4 * GDN_HEADS * HEAD_W) // (2 * CONF_CH)
    halo_rows = CONF_HALO_ROWS
    vec = pl.BlockSpec((1, CONF_CH), lambda i: (0, 0))
    return pl.pallas_call(
        _conf_body,
        grid=(t // rows,),
        in_specs=[pl.BlockSpec((rows, 2 * CONF_CH), lambda i: (i, glu_blk)),
                  pl.BlockSpec((halo_rows, 2 * CONF_CH),
                               lambda i: (jnp.maximum(i * (rows // halo_rows) - 1, 0), glu_blk)),
                  pl.BlockSpec((CONF_K, CONF_CH), lambda i: (0, 0)), vec, vec, vec],
        out_specs=pl.BlockSpec((rows, CONF_CH), lambda i: (i, 0)),
        out_shape=jax.ShapeDtypeStruct((t, CONF_CH), BF16),
        compiler_params=_cparams(("arbitrary",)),
        name="conformer_conv",
    )(proj, proj, dw, dw_b, ln_g, ln_b)


def _out0_body(a_ref, b_ref, wa_ref, wb_ref, x_ref, gate_ref, o_ref):
    mix = _dot(a_ref[...], wa_ref[...].astype(BF16)) + _dot(b_ref[...], wb_ref[...].astype(BF16))
    o_ref[...] = x_ref[...] + gate_ref[...] * mix


def _out_proj0(a, b, w, x, gate, *, tm=1024, tn=1024):
    t, d = x.shape
    tm = min(tm, t)
    ka, kb = a.shape[1], b.shape[1]
    return pl.pallas_call(
        _out0_body,
        grid=(t // tm, d // tn),
        in_specs=[pl.BlockSpec((tm, ka), lambda i, j: (i, 0)),
                  pl.BlockSpec((tm, kb), lambda i, j: (i, 0)),
                  pl.BlockSpec((ka, tn), lambda i, j: (0, j)),
                  pl.BlockSpec((kb, tn), lambda i, j: (ka // kb, j)),
                  pl.BlockSpec((tm, tn), lambda i, j: (i, j)),
                  pl.BlockSpec((1, tn), lambda i, j: (0, j))],
        out_specs=pl.BlockSpec((tm, tn), lambda i, j: (i, j)),
        out_shape=jax.ShapeDtypeStruct((t, d), F32),
        compiler_params=_cparams(("arbitrary", "arbitrary")),
        name="out_proj_even",
    )(a, b, w, w, x, gate)


def _out1_body(y_ref, g_ref, w_ref, x_ref, gate_ref, o_ref, h_ref):
    @pl.when(pl.program_id(1) == 0)
    def _():
        def norm_rows(c, carry):
            rs = pl.ds(pl.multiple_of(c * PROLOGUE_ROWS, PROLOGUE_ROWS), PROLOGUE_ROWS)
            y = y_ref[rs, :].astype(F32)
            r = lax.rsqrt(jnp.mean(y * y, axis=-1, keepdims=True) + EPS)
            h_ref[rs, :] = (y * r * g_ref[...]).astype(BF16)
            return carry

        lax.fori_loop(0, y_ref.shape[0] // PROLOGUE_ROWS, norm_rows, 0)

    o_ref[...] = x_ref[...] + gate_ref[...] * _dot(h_ref[...], w_ref[...].astype(BF16))


def _out_proj1(y, norm_g, w, x, gate, *, tm=1024, tn=512):
    t, d = x.shape
    tm = min(tm, t)
    k = y.shape[1]
    return pl.pallas_call(
        _out1_body,
        grid=(t // tm, d // tn),
        in_specs=[pl.BlockSpec((tm, k), lambda i, j: (i, 0)),
                  pl.BlockSpec((1, k), lambda i, j: (0, 0)),
                  pl.BlockSpec((k, tn), lambda i, j: (0, j)),
                  pl.BlockSpec((tm, tn), lambda i, j: (i, j)),
                  pl.BlockSpec((1, tn), lambda i, j: (0, j))],
        out_specs=pl.BlockSpec((tm, tn), lambda i, j: (i, j)),
        out_shape=jax.ShapeDtypeStruct((t, d), F32),
        scratch_shapes=[pltpu.VMEM((tm, k), BF16)],
        compiler_params=_cparams(("arbitrary", "arbitrary")),
        name="out_proj_odd",
    )(y, norm_g, w, x, gate)


def _ssd_body(x_ref, b_ref, c_ref, xh_ref, bh_ref, ch_ref, z_ref, dt_ref, cwx_ref, cwb_ref, cwc_ref,
              cbx_ref, cbb_ref, cbc_ref, par_ref, o_ref, s_ref, at_ref, *, rows):
    grp = pl.program_id(0)
    ir = pl.program_id(1)
    gw = SSM_GROUP_W
    hpg = gw // SSM_HEADDIM

    @pl.when(ir == 0)
    def _():
        s_ref[...] = jnp.zeros_like(s_ref)

    def conv_silu(x_r, halo_r, w_r, bias_r):
        halo = jnp.where(ir > 0, halo_r[...].astype(F32), 0.0)
        return _silu(_causal_conv(halo, x_r[...].astype(F32), w_r[...], 4) + bias_r[...])

    xs = conv_silu(x_ref, xh_ref, cwx_ref, cbx_ref)
    bm = conv_silu(b_ref, bh_ref, cwb_ref, cbb_ref)
    cm = conv_silu(c_ref, ch_ref, cwc_ref, cbc_ref)

    dt = _softplus(dt_ref[...] + par_ref[0:1, :])
    da = dt * (-jnp.exp(par_ref[1:2, :]))
    rr = _iota((rows, rows), 0)
    cc = _iota((rows, rows), 1)
    chunk_tril = ((rr >> CHUNK_LOG2) == (cc >> CHUNK_LOG2)) & (cc <= rr)
    acs = _select_dot(chunk_tril.astype(BF16), da, left=True)
    at_ref[...] = acs.T
    acs_rows = at_ref[pl.ds(pl.multiple_of(grp * hpg, hpg), hpg), :]

    sel = (_iota((LANES, gw), 0) == grp * hpg + (_iota((LANES, gw), 1) >> CHUNK_LOG2)).astype(BF16)
    dt_e = _select_dot(sel, dt)
    acs_e = _select_dot(sel, acs)
    dskip_e = _select_dot(sel, par_ref[...])[2:3, :]

    causal = (_iota((CHUNK, gw), 1) & (CHUNK - 1)) <= _iota((CHUNK, gw), 0)
    same_head = (_iota((LANES, LANES), 0) >> CHUNK_LOG2) == (_iota((LANES, LANES), 1) >> CHUNK_LOG2)
    bf = lambda a: a.astype(BF16)
    nc = rows // CHUNK
    chunks = [slice(c * CHUNK, (c + 1) * CHUNK) for c in range(nc)]
    tiles = [slice(k * LANES, (k + 1) * LANES) for k in range(gw // LANES)]

    a_es = [acs_e[rs] for rs in chunks]
    a_rows = [jnp.concatenate([acs_rows[h:h + 1, rs] for h in range(hpg)], axis=1) for rs in chunks]
    l_cats = [jnp.where(causal, jnp.exp(jnp.minimum(a_e - a_row, 0.0)), 0.0) for a_e, a_row in zip(a_es, a_rows)]
    c16 = [bf(cm[rs]) for rs in chunks]
    b16 = [bf(bm[rs]) for rs in chunks]
    cbs = [_dot_nt(c_c, jnp.concatenate([b_c] * hpg, axis=0)) for c_c, b_c in zip(c16, b16)]
    xdts = [xs[rs] * dt_e[rs] for rs in chunks]
    m16 = [bf(cb * l_cat) for cb, l_cat in zip(cbs, l_cats)]
    y_diags = []
    for m, xdt in zip(m16, xdts):
        parts = [_dot(m[:, ts], bf(jnp.where(same_head, jnp.concatenate([xdt[:, ts]] * 2, axis=0), 0.0)))
                 for ts in tiles]
        y_diags.append(jnp.concatenate(parts, axis=1))
    a_lasts = [a_e[CHUNK - 1:CHUNK] for a_e in a_es]
    updates = [_dot_tn(b_c, bf(xdt * jnp.exp(a_last - a_e)))
               for b_c, xdt, a_last, a_e in zip(b16, xdts, a_lasts, a_es)]

    states = [s_ref[...]]
    for a_last, upd in zip(a_lasts, updates):
        states.append(states[-1] * jnp.exp(a_last) + upd)
    s_ref[...] = states[-1]

    y_offs = [_dot(c_c, bf(s)) * jnp.exp(a_e) for c_c, s, a_e in zip(c16, states[:-1], a_es)]
    for rs, y_diag, y_off in zip(chunks, y_diags, y_offs):
        y = y_diag + y_off + dskip_e * xs[rs]
        o_ref[rs, :] = (y * _silu(z_ref[rs, :].astype(F32))).astype(o_ref.dtype)


def _ssd(proj, dt_raw, conv_w, conv_b, par, *, rows=256):
    t = proj.shape[0]
    rows = min(rows, t)
    gw = SSM_GROUP_W
    d_inner = SSM_GROUPS * gw
    xo = d_inner // gw
    bo = (2 * d_inner) // LANES
    co = bo + SSM_GROUPS * SSM_STATE // LANES
    hrow = lambda i: jnp.maximum(i * (rows // HALO_ROWS) - 1, 0)
    cb_x = d_inner // gw
    return pl.pallas_call(
        functools.partial(_ssd_body, rows=rows),
        grid=(SSM_GROUPS, t // rows),
        in_specs=[pl.BlockSpec((rows, gw), lambda g, i: (i, xo + g)),
                  pl.BlockSpec((rows, LANES), lambda g, i: (i, bo + g)),
                  pl.BlockSpec((rows, LANES), lambda g, i: (i, co + g)),
                  pl.BlockSpec((HALO_ROWS, gw), lambda g, i: (hrow(i), xo + g)),
                  pl.BlockSpec((HALO_ROWS, LANES), lambda g, i: (hrow(i), bo + g)),
                  pl.BlockSpec((HALO_ROWS, LANES), lambda g, i: (hrow(i), co + g)),
                  pl.BlockSpec((rows, gw), lambda g, i: (i, g)),
                  pl.BlockSpec((rows, LANES), lambda g, i: (i, 0)),
                  pl.BlockSpec((4, gw), lambda g, i: (0, g)),
                  pl.BlockSpec((4, LANES), lambda g, i: (0, cb_x * (gw // LANES) + g)),
                  pl.BlockSpec((4, LANES), lambda g, i: (0, cb_x * (gw // LANES) + SSM_GROUPS + g)),
                  pl.BlockSpec((1, gw), lambda g, i: (0, g)),
                  pl.BlockSpec((1, LANES), lambda g, i: (0, cb_x * (gw // LANES) + g)),
                  pl.BlockSpec((1, LANES), lambda g, i: (0, cb_x * (gw // LANES) + SSM_GROUPS + g)),
                  pl.BlockSpec((8, LANES), lambda g, i: (0, 0))],
        out_specs=pl.BlockSpec((rows, gw), lambda g, i: (i, g)),
        out_shape=jax.ShapeDtypeStruct((t, d_inner), BF16),
        scratch_shapes=[pltpu.VMEM((SSM_STATE, gw), F32), pltpu.VMEM((LANES, rows), F32)],
        compiler_params=_cparams(("arbitrary", "arbitrary")),
        name="ssd",
    )(proj, proj, proj, proj, proj, proj, proj, dt_raw, conv_w, conv_w, conv_w, conv_b, conv_b, conv_b, par)


def _store_token_tiles(tile_ref, x):
    rows, d = x.shape
    bits = lax.bitcast_convert_type(x.astype(BF16).astype(F32), U32)
    packed = (bits[:, :d // 2] >> 16) | (bits[:, d // 2:] & jnp.uint32(0xFFFF0000))
    for s in range(TOKEN_TILE_ROWS):
        tile_ref[pl.ds(s, rows, stride=TOKEN_TILE_ROWS), :] = packed[:, s * LANES:(s + 1) * LANES]


def _load_token_tiles(tile_ref, rows):
    lo, hi = [], []
    for s in range(TOKEN_TILE_ROWS):
        p = tile_ref[pl.ds(s, rows, stride=TOKEN_TILE_ROWS), :]
        lo.append(lax.bitcast_convert_type(p << 16, F32))
        hi.append(lax.bitcast_convert_type(p & jnp.uint32(0xFFFF0000), F32))
    return lo, hi


def _lane_pick(lane, idx, vals):
    return jnp.sum(jnp.where(lane == idx, vals, 0.0), axis=-1, keepdims=True)


def _router_body(x_ref, g_ref, sc_ref, sh_ref, w_ref, b_ref, hp_ref, route_ref, cnt_ref, carry_ref, *, tm):
    i = pl.program_id(0)

    @pl.when(i == 0)
    def _():
        carry_ref[...] = jnp.zeros_like(carry_ref)

    h = _adaln_rows(x_ref[...], g_ref[...], sc_ref[...], sh_ref[...])
    logits = _dot3(h, w_ref[...]) + b_ref[...]
    lane = _iota((tm, LANES), 1).astype(F32)
    neg = -jnp.inf
    gl = jnp.where(lane < N_GROUPS, logits, neg)
    gmax = jnp.max(gl, axis=-1, keepdims=True)
    grp_p = 1.0 / jnp.sum(jnp.exp(gl - gmax), axis=-1, keepdims=True)
    gidx = jnp.min(jnp.where(gl == gmax, lane, float(LANES)), axis=-1, keepdims=True)
    lo = N_GROUPS + EXPERTS_PER_GROUP * gidx
    el = jnp.where((lane >= lo) & (lane < lo + EXPERTS_PER_GROUP), logits, neg)
    m1 = jnp.max(el, axis=-1, keepdims=True)
    i1 = jnp.min(jnp.where(el == m1, lane, float(LANES)), axis=-1, keepdims=True)
    el2 = jnp.where(lane == i1, neg, el)
    m2 = jnp.max(el2, axis=-1, keepdims=True)
    i2 = jnp.min(jnp.where(el2 == m2, lane, float(LANES)), axis=-1, keepdims=True)
    tt = jnp.exp(m2 - m1)
    g1 = grp_p / (1.0 + tt)
    g2 = g1 * tt
    e1 = i1 - N_GROUPS
    e2 = i2 - N_GROUPS

    onehot = ((lane == e1) | (lane == e2)).astype(F32)
    strict = (_iota((tm, tm), 1) < _iota((tm, tm), 0)).astype(BF16)
    rank = _dot(strict, onehot.astype(BF16)) + carry_ref[0:1, :]
    r1 = _lane_pick(lane, e1, rank)
    r2 = _lane_pick(lane, e2, rank)
    carry_ref[...] = carry_ref[...] + jnp.sum(onehot, axis=0, keepdims=True)
    cnt_ref[...] = carry_ref[...]

    route = jnp.where(lane == 0, e1, jnp.where(lane == 1, e2, jnp.where(lane == 2, r1, jnp.where(
        lane == 3, r2, jnp.where(lane == 4, g1, jnp.where(lane == 5, g2, 0.0))))))
    route_ref[...] = route

    _store_token_tiles(hp_ref, h)


def _router(x, g, scale, shift, w_route, b_route, *, tm=512):
    t, d = x.shape
    vec = pl.BlockSpec((1, d), lambda i: (0, 0))
    return pl.pallas_call(
        functools.partial(_router_body, tm=tm),
        grid=(t // tm,),
        in_specs=[pl.BlockSpec((tm, d), lambda i: (i, 0)), vec, vec, vec,
                  pl.BlockSpec((d, LANES), lambda i: (0, 0)),
                  pl.BlockSpec((1, LANES), lambda i: (0, 0))],
        out_specs=[pl.BlockSpec((tm * TOKEN_TILE_ROWS, LANES), lambda i: (i, 0)),
                   pl.BlockSpec((tm, LANES), lambda i: (i, 0)),
                   pl.BlockSpec((8, LANES), lambda i: (0, 0))],
        out_shape=[jax.ShapeDtypeStruct((t * TOKEN_TILE_ROWS, LANES), U32),
                   jax.ShapeDtypeStruct((t, LANES), F32),
                   jax.ShapeDtypeStruct((8, LANES), F32)],
        scratch_shapes=[pltpu.VMEM((8, LANES), F32)],
        compiler_params=_cparams(("arbitrary",)),
        name="moe_router",
    )(x, g, scale, shift, w_route, b_route)


def _dest_body(route_ref, cnt_ref, dest_ref, blk_ref, *, tm, n_blocks_pad):
    cnt = cnt_ref[...]
    padded = jnp.floor((cnt + (MOE_ROWS - 1)) * (1.0 / MOE_ROWS)) * MOE_ROWS
    upper = (_iota((LANES, LANES), 0) <= _iota((LANES, LANES), 1)).astype(F32)
    pad_end = _dot(padded, upper, _HI)
    pad_start = (pad_end - padded)[0:1, :]
    route = route_ref[...]
    lane = _iota((tm, LANES), 1).astype(F32)
    e1, e2, r1, r2 = route[:, 0:1], route[:, 1:2], route[:, 2:3], route[:, 3:4]
    d1 = _lane_pick(lane, e1, pad_start) + r1
    d2 = _lane_pick(lane, e2, pad_start) + r2
    dest_ref[...] = jnp.where(lane == 0, d1, jnp.where(lane == 1, d2, 0.0)).astype(I32)

    lane_b = _iota((n_blocks_pad, LANES), 1)
    first_row = (_iota((n_blocks_pad, LANES), 0) * MOE_ROWS).astype(F32)
    done = ((pad_end[0:1, :] <= first_row) & (lane_b < N_EXPERTS)).astype(F32)
    blk_e = jnp.minimum(jnp.sum(done, axis=-1, keepdims=True), float(N_EXPERTS - 1))
    n_valid = pad_end[0:1, N_EXPERTS - 1:N_EXPERTS] * (1.0 / MOE_ROWS)
    on_diag = lane_b == _iota((n_blocks_pad, LANES), 0)
    seg_fill = jnp.sum(jnp.where(on_diag, pad_start + cnt[0:1, :], 0.0), axis=-1, keepdims=True)
    seg_end = jnp.sum(jnp.where(on_diag, pad_end[0:1, :], 0.0), axis=-1, keepdims=True)
    blk_ref[...] = jnp.where(lane_b == 0, blk_e, jnp.where(lane_b == 1, n_valid, jnp.where(
        lane_b == 2, seg_fill, jnp.where(lane_b == 3, seg_end, 0.0)))).astype(I32)


def _dest(route, counts, n_blocks_pad, *, tm=512):
    t = route.shape[0]
    assert t % tm == 0
    return pl.pallas_call(
        functools.partial(_dest_body, tm=tm, n_blocks_pad=n_blocks_pad),
        grid=(t // tm,),
        in_specs=[pl.BlockSpec((tm, LANES), lambda i: (i, 0)),
                  pl.BlockSpec((8, LANES), lambda i: (0, 0))],
        out_specs=[pl.BlockSpec((tm, LANES), lambda i: (i, 0)),
                   pl.BlockSpec((n_blocks_pad, LANES), lambda i: (0, 0))],
        out_shape=[jax.ShapeDtypeStruct((t, LANES), I32),
                   jax.ShapeDtypeStruct((n_blocks_pad, LANES), I32)],
        compiler_params=_cparams(("arbitrary",)),
        name="moe_dest",
    )(route, counts)


def _invert_body(dest_ref, fill_ref, end_ref, tok_ref, *, n_assign, n_rows):
    def fill(r, c):
        tok_ref[r] = 0
        return c

    def fill_expert(e, c):
        return lax.fori_loop(fill_ref[e], end_ref[e], fill, c)

    lax.fori_loop(0, N_EXPERTS, fill_expert, 0)
    lax.fori_loop(end_ref[N_EXPERTS - 1], n_rows, fill, 0)

    def scatter(a, c):
        tok_ref[dest_ref[a]] = a >> 1
        return c

    lax.fori_loop(0, n_assign, scatter, 0, unroll=8)


def _invert(dest_flat, seg_fill, seg_end, n_rows):
    n_assign = dest_flat.shape[0]
    return pl.pallas_call(
        functools.partial(_invert_body, n_assign=n_assign, n_rows=n_rows),
        grid_spec=pltpu.PrefetchScalarGridSpec(
            num_scalar_prefetch=3,
            grid=(1,),
            in_specs=[],
            out_specs=pl.BlockSpec(memory_space=pltpu.SMEM),
        ),
        out_shape=jax.ShapeDtypeStruct((n_rows,), I32),
        compiler_params=_cparams(("arbitrary",)),
        name="moe_invert",
    )(dest_flat, seg_fill, seg_end)


def _expert_body(be_ref, nv_ref, tok_ref, hp_ref, w1_ref, w3_ref, w2_ref, y_ref, xbuf_ref, sem, wst1_ref, wst3_ref,
                 wst2_ref, wsem, w1b_ref, w3b_ref, w2b_ref, *, layer):
    b = pl.program_id(0)
    n_blocks = pl.num_programs(0)
    wst_refs = (wst1_ref, wst3_ref, wst2_ref)
    n_valid = nv_ref[0]
    valid = b < n_valid
    tr = TOKEN_TILE_ROWS
    expert = be_ref[b]
    first_of_expert = valid & ((b == 0) | (expert != be_ref[jnp.maximum(b - 1, 0)]))

    def weight_copies(e):
        return [pltpu.make_async_copy(w_ref.at[layer, e], st_ref, wsem.at[i])
                for i, (w_ref, st_ref) in enumerate(zip((w1_ref, w3_ref, w2_ref), wst_refs))]

    @pl.when((b == 0) & valid)
    def _():
        for c in weight_copies(expert):
            c.start(priority=WEIGHT_DMA_PRIORITY)

    def gather_rows(blk, slot):
        def one(j, c):
            tok = tok_ref[blk * MOE_ROWS + j]
            pltpu.make_async_copy(hp_ref.at[pl.ds(tok * tr, tr)], xbuf_ref.at[slot, pl.ds(j * tr, tr)],
                                  sem.at[slot]).start()
            return c

        lax.fori_loop(0, MOE_ROWS, one, 0, unroll=8)

    @pl.when((b == 0) & valid)
    def _():
        gather_rows(0, 0)

    @pl.when(first_of_expert)
    def _():
        for c in weight_copies(expert):
            c.wait()
        for wb_ref, st_ref in zip((w1b_ref, w3b_ref, w2b_ref), wst_refs):
            wb_ref[...] = st_ref[...].astype(BF16)
        nxt = lax.while_loop(lambda j: (j < n_valid) & (be_ref[jnp.minimum(j, n_blocks - 1)] == expert),
                             lambda j: j + 1, b + 1)

        @pl.when(nxt < n_valid)
        def _():
            for c in weight_copies(be_ref[jnp.minimum(nxt, n_blocks - 1)]):
                c.start(priority=WEIGHT_DMA_PRIORITY)

    @pl.when(jnp.logical_not(valid))
    def _():
        y_ref[...] = jnp.zeros_like(y_ref)

    def ffn_block(prefetch_next):
        slot = b % 2
        pltpu.make_async_copy(hp_ref.at[pl.ds(0, MOE_ROWS * tr)], xbuf_ref.at[slot], sem.at[slot]).wait()
        lo, hi = _load_token_tiles(xbuf_ref.at[slot], MOE_ROWS)
        x = jnp.concatenate(lo + hi, axis=1).astype(BF16)
        if prefetch_next:
            for j in range(MOE_ROWS):
                tok = tok_ref[(b + 1) * MOE_ROWS + j]
                pltpu.make_async_copy(hp_ref.at[pl.ds(tok * tr, tr)], xbuf_ref.at[1 - slot, pl.ds(j * tr, tr)],
                                      sem.at[1 - slot]).start()
        a = (_silu(_dot(x, w1b_ref[...])) * _dot(x, w3b_ref[...])).astype(BF16)
        _store_token_tiles(y_ref, _dot(a, w2b_ref[...]))

    has_next = b + 1 < n_valid
    pl.when(valid & has_next)(functools.partial(ffn_block, True))
    pl.when(valid & jnp.logical_not(has_next))(functools.partial(ffn_block, False))


def _experts(block_expert, n_valid, tok_rows, h_packed, w1, w3, w2, *, layer):
    n_blocks = block_expert.shape[0]
    _, _, d, f = w1.shape
    anywhere = pl.BlockSpec(memory_space=pl.ANY)
    return pl.pallas_call(
        functools.partial(_expert_body, layer=layer),
        grid_spec=pltpu.PrefetchScalarGridSpec(
            num_scalar_prefetch=3,
            grid=(n_blocks,),
            in_specs=[anywhere, anywhere, anywhere, anywhere],
            out_specs=pl.BlockSpec((MOE_ROWS * TOKEN_TILE_ROWS, LANES), lambda b, be, nv, tk: (b, 0)),
            scratch_shapes=[pltpu.VMEM((2, MOE_ROWS * TOKEN_TILE_ROWS, LANES), U32), pltpu.SemaphoreType.DMA((2,)),
                            pltpu.VMEM((d, f), F32), pltpu.VMEM((d, f), F32), pltpu.VMEM((f, d), F32),
                            pltpu.SemaphoreType.DMA((3,)),
                            pltpu.VMEM((d, f), BF16), pltpu.VMEM((d, f), BF16), pltpu.VMEM((f, d), BF16)],
        ),
        out_shape=jax.ShapeDtypeStruct((n_blocks * MOE_ROWS * TOKEN_TILE_ROWS, LANES), U32),
        compiler_params=_cparams(("arbitrary",)),
        name="moe_experts",
    )(block_expert, n_valid, tok_rows, h_packed, w1, w3, w2)


def _combine_body(dest_ref, y_ref, x_ref, route_ref, gate_ref, fg_ref, o_ref, buf_ref, sem, *, tm, final_norm):
    i = pl.program_id(0)
    yr = TOKEN_TILE_ROWS

    def gather_rows(blk, slot):
        def one(j, c):
            for k in range(2):
                d = dest_ref[(blk * tm + j) * 2 + k]
                pltpu.make_async_copy(y_ref.at[pl.ds(d * yr, yr)], buf_ref.at[slot, k, pl.ds(j * yr, yr)],
                                      sem.at[slot, k]).start(priority=k)
            return c

        lax.fori_loop(0, tm, one, 0, unroll=4)

    @pl.when(i == 0)
    def _():
        gather_rows(0, 0)

    @pl.when(i + 1 < pl.num_programs(0))
    def _():
        gather_rows(i + 1, (i + 1) % 2)

    slot = i % 2
    for k in range(2):
        pltpu.make_async_copy(y_ref.at[pl.ds(0, tm * yr)], buf_ref.at[slot, k], sem.at[slot, k]).wait()

    route = route_ref[...]
    g1, g2 = route[:, 4:5], route[:, 5:6]
    lo1, hi1 = _load_token_tiles(buf_ref.at[slot, 0], tm)
    lo2, hi2 = _load_token_tiles(buf_ref.at[slot, 1], tm)
    half = x_ref.shape[1] // 2
    for s in range(yr):
        for base, y1, y2 in ((0, lo1[s], lo2[s]), (half, hi1[s], hi2[s])):
            cs = slice(base + s * LANES, base + (s + 1) * LANES)
            o_ref[:, cs] = x_ref[:, cs] + gate_ref[:, cs] * (g1 * y1 + g2 * y2)
    if final_norm:
        xo = o_ref[...]
        o_ref[...] = xo * lax.rsqrt(jnp.mean(xo * xo, axis=-1, keepdims=True) + EPS) * fg_ref[...]


def _combine(dest_flat, y_rows, x, route, gate, final_g, *, final_norm, tm=128):
    t, d = x.shape
    return pl.pallas_call(
        functools.partial(_combine_body, tm=tm, final_norm=final_norm),
        grid_spec=pltpu.PrefetchScalarGridSpec(
            num_scalar_prefetch=1,
            grid=(t // tm,),
            in_specs=[pl.BlockSpec(memory_space=pl.ANY),
                      pl.BlockSpec((tm, d), lambda i, dr: (i, 0)),
                      pl.BlockSpec((tm, LANES), lambda i, dr: (i, 0)),
                      pl.BlockSpec((1, d), lambda i, dr: (0, 0)),
                      pl.BlockSpec((1, d), lambda i, dr: (0, 0))],
            out_specs=pl.BlockSpec((tm, d), lambda i, dr: (i, 0)),
            scratch_shapes=[pltpu.VMEM((2, 2, tm * TOKEN_TILE_ROWS, LANES), U32), pltpu.SemaphoreType.DMA((2, 2))],
        ),
        out_shape=jax.ShapeDtypeStruct((t, d), F32),
        compiler_params=_cparams(("arbitrary",)),
        name="moe_combine",
    )(dest_flat, y_rows, x, route, gate, final_g)


def _moe(x, g, scale, shift, gate, w_group, b_group, w_expert, b_expert, w1, w3, w2, final_g, *, layer, final_norm):
    t, d = x.shape
    pad = LANES - N_GROUPS - N_EXPERTS
    w_route = jnp.concatenate([w_group, w_expert, jnp.zeros((d, pad), F32)], axis=1)
    b_route = jnp.concatenate([b_group, b_expert, jnp.zeros((pad,), F32)]).reshape(1, LANES)
    h_packed, route, counts = _router(x, g, scale, shift, w_route, b_route)
    n_assign = 2 * t
    n_blocks = -(-(n_assign + N_EXPERTS * (MOE_ROWS - 1)) // MOE_ROWS)
    n_blocks_pad = -(-n_blocks // 8) * 8
    dest, blk = _dest(route, counts, n_blocks_pad)
    dest_flat = dest[:, :2].reshape(n_assign)
    block_expert = blk[:n_blocks, 0]
    n_valid = blk[0, 1:2]
    tok_rows = _invert(dest_flat, blk[:N_EXPERTS, 2], blk[:N_EXPERTS, 3], n_blocks * MOE_ROWS)
    y_rows = _experts(block_expert, n_valid, tok_rows, h_packed, w1, w3, w2, layer=layer)
    return _combine(dest_flat, y_rows, x, route, gate, final_g, final_norm=final_norm)


def _pad_rows(w, rows):
    return jnp.pad(w, ((0, rows - w.shape[0]), (0, 0)))


def _lane_row(v, offset):
    return jnp.zeros((LANES,), F32).at[offset:offset + v.shape[0]].set(v)


def kernel(x, c, w_mod, b_mod, norm_g, final_norm_g, e_w_in, e_conv_qkv, e_a_log, e_dt_bias, e_head_norm_g, e_conf_dw, e_conf_dw_b, e_conf_ln_g, e_conf_ln_b, e_w_out, o_w_in, o_conv_w, o_conv_b, o_dt_bias, o_a_log, o_d_skip, o_norm_g, o_w_out, moe_w_group, moe_b_group, moe_w_expert, moe_b_expert, moe_w1, moe_w3, moe_w2):
    bsz, t, d = x.shape
    assert bsz == 1 and c.shape == (1, d)
    xt = x[0]
    mod = _modulation(c[0], w_mod, b_mod)

    def mod_parts(l):
        return mod[l, :, :d], mod[l, :, d:2 * d], mod[l, :, 2 * d:]

    row = lambda v: v.reshape(1, -1)
    qkv_w = 3 * GDN_HEADS * HEAD_W
    z_end = qkv_w + GDN_HEADS * HEAD_W

    shift, scale, gate = mod_parts(0)
    w_in = e_w_in[0].T
    tn = 512
    w_glu = w_in[z_end + 2 * GDN_HEADS:]
    w_ba = _pad_rows(w_in[z_end:z_end + 2 * GDN_HEADS], LANES)
    proj0, ba = _norm_matmul(xt, row(norm_g[0, 0]), scale, shift,
                             [(w_in, z_end // tn), (w_glu, w_glu.shape[0] // tn)], w_ba, tn=tn)
    gdn_par = jnp.zeros((8, LANES), F32).at[0].set(_lane_row(e_a_log[0], GDN_HEADS)).at[1].set(
        _lane_row(e_dt_bias[0], GDN_HEADS))
    a_out = _gdn(proj0, ba, e_conv_qkv[0], gdn_par, row(e_head_norm_g[0]))
    b_out = _conformer(proj0, e_conf_dw[0], row(e_conf_dw_b[0]), row(e_conf_ln_g[0]), row(e_conf_ln_b[0]))
    xt = _out_proj0(a_out, b_out, e_w_out[0], xt, gate)
    shift, scale, gate = mod_parts(1)
    xt = _moe(xt, row(norm_g[0, 1]), scale, shift, gate, moe_w_group[0], moe_b_group[0], moe_w_expert[0],
              moe_b_expert[0], moe_w1, moe_w3, moe_w2, row(final_norm_g), layer=0, final_norm=False)

    shift, scale, gate = mod_parts(2)
    w_in = o_w_in[0].T
    n_main = 2 * SSM_GROUPS * SSM_GROUP_W + 2 * SSM_GROUPS * SSM_STATE
    proj1, dt_raw = _norm_matmul(xt, row(norm_g[1, 0]), scale, shift, [(w_in, n_main // tn)],
                                 _pad_rows(w_in[n_main:], LANES), tn=tn)
    ssd_par = jnp.zeros((8, LANES), F32).at[0].set(_lane_row(o_dt_bias[0], 0)).at[1].set(
        _lane_row(o_a_log[0], 0)).at[2].set(_lane_row(o_d_skip[0], 0))
    y = _ssd(proj1, dt_raw, o_conv_w[0], row(o_conv_b[0]), ssd_par)
    xt = _out_proj1(y, row(o_norm_g[0]), o_w_out[0], xt, gate)
    shift, scale, gate = mod_parts(3)
    xt = _moe(xt, row(norm_g[1, 1]), scale, shift, gate, moe_w_group[1], moe_b_group[1], moe_w_expert[1],
              moe_b_expert[1], moe_w1, moe_w3, moe_w2, row(final_norm_g), layer=1, final_norm=True)
    return xt[None]
```

```python
import functools

import jax
import jax.numpy as jnp
from jax import lax
from jax.experimental import pallas as pl
from jax.experimental.pallas import tpu as pltpu

F32 = jnp.float32
BF16 = jnp.bfloat16
I32 = jnp.int32
U32 = jnp.uint32

EPS = 1e-6
LANES = 128
CHUNK = 64
CHUNK_LOG2 = 6
GDN_HEADS = 8
HEAD_W = 128
CONF_CH = 1024
CONF_K = 31
SSM_GROUPS = 8
SSM_GROUP_W = 512
SSM_HEADDIM = 64
SSM_STATE = 128
N_EXPERTS = 32
EXPERTS_PER_GROUP = 8
N_GROUPS = 4
MOE_ROWS = 256
TOKEN_TILE_ROWS = 8
VMEM_LIMIT = 56 * 1024 * 1024
PROLOGUE_ROWS = 128
HALO_ROWS = 16
CONF_HALO_ROWS = 32

_HI = lax.Precision.HIGHEST


def _cparams(sem):
    return pltpu.CompilerParams(dimension_semantics=sem, vmem_limit_bytes=VMEM_LIMIT)


def _dot(a, b, precision=None):
    return jnp.dot(a, b, precision=precision, preferred_element_type=F32)


def _dot_nt(a, b, precision=None):
    return lax.dot_general(a, b, (((1,), (1,)), ((), ())), precision=precision, preferred_element_type=F32)


def _dot_tn(a, b, precision=None):
    return lax.dot_general(a, b, (((0,), (0,)), ((), ())), precision=precision, preferred_element_type=F32)


def _split_bf16(x):
    hi = x.astype(BF16)
    return hi, (x - hi.astype(F32)).astype(BF16)


def _dot3(a, b):
    ah, al = _split_bf16(a)
    bh, bl = _split_bf16(b)
    return _dot(ah, bh) + (_dot(ah, bl) + _dot(al, bh))


def _select_dot(sel, x, left=False):
    p1 = x.astype(BF16)
    r1 = x - p1.astype(F32)
    p2 = r1.astype(BF16)
    p3 = (r1 - p2.astype(F32)).astype(BF16)
    if left:
        return _dot(sel, p1) + (_dot(sel, p2) + _dot(sel, p3))
    return _dot(p1, sel) + (_dot(p2, sel) + _dot(p3, sel))


def _silu(x):
    return x * jax.nn.sigmoid(x)


def _softplus(x):
    return jnp.maximum(x, 0.0) + jnp.log1p(jnp.exp(-jnp.abs(x)))


def _iota(shape, dim):
    return lax.broadcasted_iota(I32, shape, dim)


def _mod_body(c_ref, w_ref, b_ref, o_ref):
    c = c_ref[...]
    o_ref[...] = jnp.sum(w_ref[...] * _silu(c), axis=0, keepdims=True) + b_ref[...]


def _modulation(c, w_mod, b_mod):
    nl, d, n = w_mod.shape
    tn = 768
    return pl.pallas_call(
        _mod_body,
        grid=(nl, n // tn),
        in_specs=[
            pl.BlockSpec((d, 1), lambda l, j: (0, 0)),
            pl.BlockSpec((None, d, tn), lambda l, j: (l, 0, j)),
            pl.BlockSpec((None, 1, tn), lambda l, j: (l, 0, j)),
        ],
        out_specs=pl.BlockSpec((None, 1, tn), lambda l, j: (l, 0, j)),
        out_shape=jax.ShapeDtypeStruct((nl, 1, n), F32),
        compiler_params=_cparams(("arbitrary", "arbitrary")),
        name="adaln_mod",
    )(c.reshape(d, 1), w_mod, b_mod.reshape(nl, 1, n))


def _adaln_rows(x, g, scale, shift):
    r = lax.rsqrt(jnp.mean(x * x, axis=-1, keepdims=True) + EPS)
    return x * r * (g * (1.0 + scale)) + shift


def _norm_mm_body(*refs, starts):
    x_ref, g_ref, sc_ref, sh_ref, ws_ref = refs[:5]
    w_refs = refs[5:5 + len(starts)]
    o_ref, os_ref, h_ref = refs[5 + len(starts):]
    j = pl.program_id(1)

    @pl.when(j == 0)
    def _():
        ws = ws_ref[...].astype(BF16)

        def norm_rows(c, carry):
            rs = pl.ds(pl.multiple_of(c * PROLOGUE_ROWS, PROLOGUE_ROWS), PROLOGUE_ROWS)
            h = _adaln_rows(x_ref[rs, :], g_ref[...], sc_ref[...], sh_ref[...]).astype(BF16)
            h_ref[rs, :] = h
            os_ref[rs, :] = _dot_nt(h, ws)
            return carry

        lax.fori_loop(0, x_ref.shape[0] // PROLOGUE_ROWS, norm_rows, 0)

    bounds = list(starts[1:]) + [None]
    for w_ref, lo, hi in zip(w_refs, starts, bounds):
        in_seg = (j >= lo) if hi is None else ((j >= lo) & (j < hi))

        @pl.when(in_seg)
        def _(w_ref=w_ref):
            o_ref[...] = _dot_nt(h_ref[...], w_ref[...].astype(BF16)).astype(o_ref.dtype)


def _norm_matmul(x, g, scale, shift, segments, w_small, *, tm=1024, tn=512):
    t, d = x.shape
    tm = min(tm, t)
    starts, specs, n_main = [], [], 0
    for w, nb in segments:
        lo = n_main // tn
        starts.append(lo)
        specs.append(pl.BlockSpec((tn, d), lambda i, j, lo=lo, nb=nb: (jnp.clip(j - lo, 0, nb - 1), 0)))
        n_main += nb * tn
    vec = pl.BlockSpec((1, d), lambda i, j: (0, 0))
    return pl.pallas_call(
        functools.partial(_norm_mm_body, starts=tuple(starts)),
        grid=(t // tm, n_main // tn),
        in_specs=[pl.BlockSpec((tm, d), lambda i, j: (i, 0)), vec, vec, vec,
                  pl.BlockSpec((LANES, d), lambda i, j: (0, 0))] + specs,
        out_specs=[pl.BlockSpec((tm, tn), lambda i, j: (i, j)),
                   pl.BlockSpec((tm, LANES), lambda i, j: (i, 0))],
        out_shape=[jax.ShapeDtypeStruct((t, n_main), BF16), jax.ShapeDtypeStruct((t, LANES), F32)],
        scratch_shapes=[pltpu.VMEM((tm, d), BF16)],
        compiler_params=_cparams(("arbitrary", "arbitrary")),
        name="adaln_in_proj",
    )(x, g, scale, shift, w_small, *[w for w, _ in segments])


def _causal_conv(halo, x, w, taps):
    hr = halo.shape[0]
    n = x.shape[0]
    xe = jnp.concatenate([halo, x], axis=0)
    rolled = {0: xe}
    acc = None
    for j in range(taps):
        s = taps - 1 - j
        a, b = divmod(s, 8)
        if b not in rolled:
            rolled[b] = pltpu.roll(xe, b, axis=0)
        term = rolled[b][hr - 8 * a:hr - 8 * a + n] * w[j:j + 1]
        acc = term if acc is None else acc + term
    return acc


def _unit_lower_inverses(a_mats, row, col):
    n = a_mats[0].shape[0]
    eye = (row == col).astype(F32)
    level = ((row >> 1) == (col >> 1)) & (col < row)
    invs = [eye - jnp.where(level, a, 0.0) for a in a_mats]
    k = 1
    while (1 << k) < n:
        level = ((row >> (k + 1)) == (col >> (k + 1))) & (((row >> k) & 1) == 1) & (((col >> k) & 1) == 0)
        inv16 = [inv.astype(BF16) for inv in invs]
        ys = [_dot(i16, jnp.where(level, a, 0.0).astype(BF16)) for i16, a in zip(inv16, a_mats)]
        invs = [inv - _dot(y.astype(BF16), i16) for inv, y, i16 in zip(invs, ys, inv16)]
        k += 1
    return invs


def _gdn_body(q_ref, k_ref, v_ref, qh_ref, kh_ref, vh_ref, z_ref, ba_ref, cwq_ref, cwk_ref, cwv_ref,
              par_ref, hg_ref, o_ref, s_ref, *, hb, rows):
    hblk = pl.program_id(0)
    ir = pl.program_id(1)

    @pl.when(ir == 0)
    def _():
        s_ref[...] = jnp.zeros_like(s_ref)

    def conv_silu(x_ref, halo_ref, w_ref):
        halo = jnp.where(ir > 0, halo_ref[...].astype(F32), 0.0)
        return _silu(_causal_conv(halo, x_ref[...].astype(F32), w_ref[...], 4))

    qc = conv_silu(q_ref, qh_ref, cwq_ref)
    kc = conv_silu(k_ref, kh_ref, cwk_ref)
    vc = conv_silu(v_ref, vh_ref, cwv_ref)

    ba = ba_ref[...]
    beta_all = jax.nn.sigmoid(ba)
    g_all = -jnp.exp(par_ref[0:1, :]) * _softplus(ba + par_ref[1:2, :])
    rr = _iota((rows, rows), 0)
    cc = _iota((rows, rows), 1)
    chunk_tril = ((rr >> CHUNK_LOG2) == (cc >> CHUNK_LOG2)) & (cc <= rr)
    gcs_all = _select_dot(chunk_tril.astype(BF16), g_all, left=True)
    gcs_t = gcs_all.T

    row = _iota((CHUNK, CHUNK), 0)
    col = _iota((CHUNK, CHUNK), 1)
    scale = HEAD_W ** -0.5
    hg = hg_ref[...]
    bf = lambda a: a.astype(BF16)
    lane = _iota((rows, LANES), 1)
    sub = _iota((LANES, rows), 0)
    nc = rows // CHUNK

    heads = []
    for h in range(hb):
        hs = slice(h * HEAD_W, (h + 1) * HEAD_W)
        head = hblk * hb + h
        q = qc[:, hs]
        k = kc[:, hs]
        heads.append(dict(
            q=q * lax.rsqrt(jnp.sum(q * q, axis=-1, keepdims=True) + EPS) * scale,
            k=k * lax.rsqrt(jnp.sum(k * k, axis=-1, keepdims=True) + EPS),
            v=vc[:, hs],
            beta=jnp.sum(jnp.where(lane == head, beta_all, 0.0), axis=-1, keepdims=True),
            gcs=jnp.sum(jnp.where(lane == head + GDN_HEADS, gcs_all, 0.0), axis=-1, keepdims=True),
            gcs_row=jnp.sum(jnp.where(sub == head + GDN_HEADS, gcs_t, 0.0), axis=0, keepdims=True)))

    items = []
    for c in range(nc):
        rs = slice(c * CHUNK, (c + 1) * CHUNK)
        for hd in heads:
            qq, kk, vv, beta, gcs = hd["q"][rs], hd["k"][rs], hd["v"][rs], hd["beta"][rs], hd["gcs"][rs]
            decay = jnp.where(col <= row, jnp.exp(jnp.minimum(gcs - hd["gcs_row"][:, rs], 0.0)), 0.0)
            kb = kk * beta
            eg = jnp.exp(gcs)
            g_last = gcs[CHUNK - 1:CHUNK]
            items.append(dict(decay=decay, kb16=bf(kb), k16=bf(kk), q16=bf(qq),
                              rhs16=bf(jnp.concatenate([vv * beta, kb * eg], axis=-1)),
                              q_dec16=bf(qq * eg), k_dec16=bf(kk * jnp.exp(g_last - gcs)), cd=jnp.exp(g_last)))
    a_mats = [jnp.where(col < row, _dot_nt(it["kb16"], it["k16"]) * it["decay"], 0.0) for it in items]
    qks = [bf(_dot_nt(it["q16"], it["k16"]) * it["decay"]) for it in items]
    t_invs = _unit_lower_inverses(a_mats, row, col)
    sols = [_dot(bf(t), it["rhs16"]) for t, it in zip(t_invs, items)]

    states = [s_ref[h] for h in range(hb)]
    for c in range(nc):
        rs = slice(c * CHUNK, (c + 1) * CHUNK)
        cur = slice(c * hb, (c + 1) * hb)
        s16 = [bf(s) for s in states]
        us = [so[:, :HEAD_W] - _dot(bf(so[:, HEAD_W:]), s) for so, s in zip(sols[cur], s16)]
        outs = [_dot(it["q_dec16"], s) + _dot(qk, bf(u)) for it, s, qk, u in zip(items[cur], s16, qks[cur], us)]
        states = [s * it["cd"] + _dot_tn(it["k_dec16"], bf(u)) for s, it, u in zip(states, items[cur], us)]
        for h, o in enumerate(outs):
            hs = slice(h * HEAD_W, (h + 1) * HEAD_W)
            o = o * lax.rsqrt(jnp.mean(o * o, axis=-1, keepdims=True) + EPS) * hg * _silu(z_ref[rs, hs].astype(F32))
            o_ref[rs, hs] = o.astype(o_ref.dtype)
    for h in range(hb):
        s_ref[h] = states[h]


def _gdn(proj, ba, conv_w, par, head_g, *, hb=GDN_HEADS, rows=256):
    t = proj.shape[0]
    rows = min(rows, t)
    w = hb * HEAD_W
    per = (GDN_HEADS * HEAD_W) // w

    def sec(k):
        return pl.BlockSpec((rows, w), lambda h, i, k=k: (i, k * per + h))

    def halo(k):
        return pl.BlockSpec((HALO_ROWS, w),
                            lambda h, i, k=k: (jnp.maximum(i * (rows // HALO_ROWS) - 1, 0), k * per + h))

    def cw(k):
        return pl.BlockSpec((4, w), lambda h, i, k=k: (0, k * per + h))

    return pl.pallas_call(
        functools.partial(_gdn_body, hb=hb, rows=rows),
        grid=(GDN_HEADS // hb, t // rows),
        in_specs=[sec(0), sec(1), sec(2), halo(0), halo(1), halo(2), sec(3),
                  pl.BlockSpec((rows, LANES), lambda h, i: (i, 0)),
                  cw(0), cw(1), cw(2),
                  pl.BlockSpec((8, LANES), lambda h, i: (0, 0)),
                  pl.BlockSpec((1, HEAD_W), lambda h, i: (0, 0))],
        out_specs=pl.BlockSpec((rows, w), lambda h, i: (i, h)),
        out_shape=jax.ShapeDtypeStruct((t, GDN_HEADS * HEAD_W), BF16),
        scratch_shapes=[pltpu.VMEM((hb, HEAD_W, HEAD_W), F32)],
        compiler_params=_cparams(("arbitrary", "arbitrary")),
        name="gated_deltanet",
    )(proj, proj, proj, proj, proj, proj, proj, ba, conv_w, conv_w, conv_w, par, head_g)


def _conf_body(x_ref, halo_ref, w_ref, b_ref, g_ref, lb_ref, o_ref):
    ir = pl.program_id(0)

    def glu(v):
        v = v.astype(F32)
        return v[:, :CONF_CH] * jax.nn.sigmoid(v[:, CONF_CH:])

    halo = jnp.where(ir > 0, glu(halo_ref[...]), 0.0)
    u = _causal_conv(halo, glu(x_ref[...]), w_ref[...], CONF_K) + b_ref[...]
    mu = jnp.mean(u, axis=-1, keepdims=True)
    uc = u - mu
    var = jnp.mean(uc * uc, axis=-1, keepdims=True)
    y = uc * lax.rsqrt(var + EPS) * g_ref[...] + lb_ref[...]
    o_ref[...] = _silu(y).astype(o_ref.dtype)


def _conformer(proj, dw, dw_b, ln_g, ln_b, *, rows=256):
    t = proj.shape[0]
    rows = min(rows, t)
    glu_blk = (4 * GDN_HEADS * HEAD_W) // (2 * CONF_CH)
    halo_rows = CONF_HALO_ROWS
    vec = pl.BlockSpec((1, CONF_CH), lambda i: (0, 0))
    return pl.pallas_call(
        _conf_body,
        grid=(t // rows,),
        in_specs=[pl.BlockSpec((rows, 2 * CONF_CH), lambda i: (i, glu_blk)),
                  pl.BlockSpec((halo_rows, 2 * CONF_CH),
                               lambda i: (jnp.maximum(i * (rows // halo_rows) - 1, 0), glu_blk)),
                  pl.BlockSpec((CONF_K, CONF_CH), lambda i: (0, 0)), vec, vec, vec],
        out_specs=pl.BlockSpec((rows, CONF_CH), lambda i: (i, 0)),
        out_shape=jax.ShapeDtypeStruct((t, CONF_CH), BF16),
        compiler_params=_cparams(("arbitrary",)),
        name="conformer_conv",
    )(proj, proj, dw, dw_b, ln_g, ln_b)


def _out0_body(a_ref, b_ref, wa_ref, wb_ref, x_ref, gate_ref, o_ref):
    mix = _dot(a_ref[...], wa_ref[...].astype(BF16)) + _dot(b_ref[...], wb_ref[...].astype(BF16))
    o_ref[...] = x_ref[...] + gate_ref[...] * mix


def _out_proj0(a, b, w, x, gate, *, tm=1024, tn=1024):
    t, d = x.shape
    tm = min(tm, t)
    ka, kb = a.shape[1], b.shape[1]
    return pl.pallas_call(
        _out0_body,
        grid=(t // tm, d // tn),
        in_specs=[pl.BlockSpec((tm, ka), lambda i, j: (i, 0)),
                  pl.BlockSpec((tm, kb), lambda i, j: (i, 0)),
                  pl.BlockSpec((ka, tn), lambda i, j: (0, j)),
                  pl.BlockSpec((kb, tn), lambda i, j: (ka // kb, j)),
                  pl.BlockSpec((tm, tn), lambda i, j: (i, j)),
                  pl.BlockSpec((1, tn), lambda i, j: (0, j))],
        out_specs=pl.BlockSpec((tm, tn), lambda i, j: (i, j)),
        out_shape=jax.ShapeDtypeStruct((t, d), F32),
        compiler_params=_cparams(("arbitrary", "arbitrary")),
        name="out_proj_even",
    )(a, b, w, w, x, gate)


def _out1_body(y_ref, g_ref, w_ref, x_ref, gate_ref, o_ref, h_ref):
    @pl.when(pl.program_id(1) == 0)
    def _():
        def norm_rows(c, carry):
            rs = pl.ds(pl.multiple_of(c * PROLOGUE_ROWS, PROLOGUE_ROWS), PROLOGUE_ROWS)
            y = y_ref[rs, :].astype(F32)
            r = lax.rsqrt(jnp.mean(y * y, axis=-1, keepdims=True) + EPS)
            h_ref[rs, :] = (y * r * g_ref[...]).astype(BF16)
            return carry

        lax.fori_loop(0, y_ref.shape[0] // PROLOGUE_ROWS, norm_rows, 0)

    o_ref[...] = x_ref[...] + gate_ref[...] * _dot(h_ref[...], w_ref[...].astype(BF16))


def _out_proj1(y, norm_g, w, x, gate, *, tm=1024, tn=512):
    t, d = x.shape
    tm = min(tm, t)
    k = y.shape[1]
    return pl.pallas_call(
        _out1_body,
        grid=(t // tm, d // tn),
        in_specs=[pl.BlockSpec((tm, k), lambda i, j: (i, 0)),
                  pl.BlockSpec((1, k), lambda i, j: (0, 0)),
                  pl.BlockSpec((k, tn), lambda i, j: (0, j)),
                  pl.BlockSpec((tm, tn), lambda i, j: (i, j)),
                  pl.BlockSpec((1, tn), lambda i, j: (0, j))],
        out_specs=pl.BlockSpec((tm, tn), lambda i, j: (i, j)),
        out_shape=jax.ShapeDtypeStruct((t, d), F32),
        scratch_shapes=[pltpu.VMEM((tm, k), BF16)],
        compiler_params=_cparams(("arbitrary", "arbitrary")),
        name="out_proj_odd",
    )(y, norm_g, w, x, gate)


def _ssd_body(x_ref, b_ref, c_ref, xh_ref, bh_ref, ch_ref, z_ref, dt_ref, cwx_ref, cwb_ref, cwc_ref,
              cbx_ref, cbb_ref, cbc_ref, par_ref, o_ref, s_ref, at_ref, *, rows):
    grp = pl.program_id(0)
    ir = pl.program_id(1)
    gw = SSM_GROUP_W
    hpg = gw // SSM_HEADDIM

    @pl.when(ir == 0)
    def _():
        s_ref[...] = jnp.zeros_like(s_ref)

    def conv_silu(x_r, halo_r, w_r, bias_r):
        halo = jnp.where(ir > 0, halo_r[...].astype(F32), 0.0)
        return _silu(_causal_conv(halo, x_r[...].astype(F32), w_r[...], 4) + bias_r[...])

    xs = conv_silu(x_ref, xh_ref, cwx_ref, cbx_ref)
    bm = conv_silu(b_ref, bh_ref, cwb_ref, cbb_ref)
    cm = conv_silu(c_ref, ch_ref, cwc_ref, cbc_ref)

    dt = _softplus(dt_ref[...] + par_ref[0:1, :])
    da = dt * (-jnp.exp(par_ref[1:2, :]))
    rr = _iota((rows, rows), 0)
    cc = _iota((rows, rows), 1)
    chunk_tril = ((rr >> CHUNK_LOG2) == (cc >> CHUNK_LOG2)) & (cc <= rr)
    acs = _select_dot(chunk_tril.astype(BF16), da, left=True)
    at_ref[...] = acs.T
    acs_rows = at_ref[pl.ds(pl.multiple_of(grp * hpg, hpg), hpg), :]

    sel = (_iota((LANES, gw), 0) == grp * hpg + (_iota((LANES, gw), 1) >> CHUNK_LOG2)).astype(BF16)
    dt_e = _select_dot(sel, dt)
    acs_e = _select_dot(sel, acs)
    dskip_e = _select_dot(sel, par_ref[...])[2:3, :]

    causal = (_iota((CHUNK, gw), 1) & (CHUNK - 1)) <= _iota((CHUNK, gw), 0)
    same_head = (_iota((LANES, LANES), 0) >> CHUNK_LOG2) == (_iota((LANES, LANES), 1) >> CHUNK_LOG2)
    bf = lambda a: a.astype(BF16)
    nc = rows // CHUNK
    chunks = [slice(c * CHUNK, (c + 1) * CHUNK) for c in range(nc)]
    tiles = [slice(k * LANES, (k + 1) * LANES) for k in range(gw // LANES)]

    a_es = [acs_e[rs] for rs in chunks]
    a_rows = [jnp.concatenate([acs_rows[h:h + 1, rs] for h in range(hpg)], axis=1) for rs in chunks]
    l_cats = [jnp.where(causal, jnp.exp(jnp.minimum(a_e - a_row, 0.0)), 0.0) for a_e, a_row in zip(a_es, a_rows)]
    c16 = [bf(cm[rs]) for rs in chunks]
    b16 = [bf(bm[rs]) for rs in chunks]
    cbs = [_dot_nt(c_c, jnp.concatenate([b_c] * hpg, axis=0)) for c_c, b_c in zip(c16, b16)]
    xdts = [xs[rs] * dt_e[rs] for rs in chunks]
    m16 = [bf(cb * l_cat) for cb, l_cat in zip(cbs, l_cats)]
    y_diags = []
    for m, xdt in zip(m16, xdts):
        parts = [_dot(m[:, ts], bf(jnp.where(same_head, jnp.concatenate([xdt[:, ts]] * 2, axis=0), 0.0)))
                 for ts in tiles]
        y_diags.append(jnp.concatenate(parts, axis=1))
    a_lasts = [a_e[CHUNK - 1:CHUNK] for a_e in a_es]
    updates = [_dot_tn(b_c, bf(xdt * jnp.exp(a_last - a_e)))
               for b_c, xdt, a_last, a_e in zip(b16, xdts, a_lasts, a_es)]

    states = [s_ref[...]]
    for a_last, upd in zip(a_lasts, updates):
        states.append(states[-1] * jnp.exp(a_last) + upd)
    s_ref[...] = states[-1]

    y_offs = [_dot(c_c, bf(s)) * jnp.exp(a_e) for c_c, s, a_e in zip(c16, states[:-1], a_es)]
    for rs, y_diag, y_off in zip(chunks, y_diags, y_offs):
        y = y_diag + y_off + dskip_e * xs[rs]
        o_ref[rs, :] = (y * _silu(z_ref[rs, :].astype(F32))).astype(o_ref.dtype)


def _ssd(proj, dt_raw, conv_w, conv_b, par, *, rows=256):
    t = proj.shape[0]
    rows = min(rows, t)
    gw = SSM_GROUP_W
    d_inner = SSM_GROUPS * gw
    xo = d_inner // gw
    bo = (2 * d_inner) // LANES
    co = bo + SSM_GROUPS * SSM_STATE // LANES
    hrow = lambda i: jnp.maximum(i * (rows // HALO_ROWS) - 1, 0)
    cb_x = d_inner // gw
    return pl.pallas_call(
        functools.partial(_ssd_body, rows=rows),
        grid=(SSM_GROUPS, t // rows),
        in_specs=[pl.BlockSpec((rows, gw), lambda g, i: (i, xo + g)),
                  pl.BlockSpec((rows, LANES), lambda g, i: (i, bo + g)),
                  pl.BlockSpec((rows, LANES), lambda g, i: (i, co + g)),
                  pl.BlockSpec((HALO_ROWS, gw), lambda g, i: (hrow(i), xo + g)),
                  pl.BlockSpec((HALO_ROWS, LANES), lambda g, i: (hrow(i), bo + g)),
                  pl.BlockSpec((HALO_ROWS, LANES), lambda g, i: (hrow(i), co + g)),
                  pl.BlockSpec((rows, gw), lambda g, i: (i, g)),
                  pl.BlockSpec((rows, LANES), lambda g, i: (i, 0)),
                  pl.BlockSpec((4, gw), lambda g, i: (0, g)),
                  pl.BlockSpec((4, LANES), lambda g, i: (0, cb_x * (gw // LANES) + g)),
                  pl.BlockSpec((4, LANES), lambda g, i: (0, cb_x * (gw // LANES) + SSM_GROUPS + g)),
                  pl.BlockSpec((1, gw), lambda g, i: (0, g)),
                  pl.BlockSpec((1, LANES), lambda g, i: (0, cb_x * (gw // LANES) + g)),
                  pl.BlockSpec((1, LANES), lambda g, i: (0, cb_x * (gw // LANES) + SSM_GROUPS + g)),
                  pl.BlockSpec((8, LANES), lambda g, i: (0, 0))],
        out_specs=pl.BlockSpec((rows, gw), lambda g, i: (i, g)),
        out_shape=jax.ShapeDtypeStruct((t, d_inner), BF16),
        scratch_shapes=[pltpu.VMEM((SSM_STATE, gw), F32), pltpu.VMEM((LANES, rows), F32)],
        compiler_params=_cparams(("arbitrary", "arbitrary")),
        name="ssd",
    )(proj, proj, proj, proj, proj, proj, proj, dt_raw, conv_w, conv_w, conv_w, conv_b, conv_b, conv_b, par)


def _store_token_tiles(tile_ref, x):
    rows, d = x.shape
    bits = lax.bitcast_convert_type(x.astype(BF16).astype(F32), U32)
    packed = (bits[:, :d // 2] >> 16) | (bits[:, d // 2:] & jnp.uint32(0xFFFF0000))
    for s in range(TOKEN_TILE_ROWS):
        tile_ref[pl.ds(s, rows, stride=TOKEN_TILE_ROWS), :] = packed[:, s * LANES:(s + 1) * LANES]


def _load_token_tiles(tile_ref, rows):
    lo, hi = [], []
    for s in range(TOKEN_TILE_ROWS):
        p = tile_ref[pl.ds(s, rows, stride=TOKEN_TILE_ROWS), :]
        lo.append(lax.bitcast_convert_type(p << 16, F32))
        hi.append(lax.bitcast_convert_type(p & jnp.uint32(0xFFFF0000), F32))
    return lo, hi


def _lane_pick(lane, idx, vals):
    return jnp.sum(jnp.where(lane == idx, vals, 0.0), axis=-1, keepdims=True)


def _router_body(x_ref, g_ref, sc_ref, sh_ref, w_ref, b_ref, hp_ref, route_ref, cnt_ref, carry_ref, *, tm):
    i = pl.program_id(0)

    @pl.when(i == 0)
    def _():
        carry_ref[...] = jnp.zeros_like(carry_ref)

    h = _adaln_rows(x_ref[...], g_ref[...], sc_ref[...], sh_ref[...])
    logits = _dot3(h, w_ref[...]) + b_ref[...]
    lane = _iota((tm, LANES), 1).astype(F32)
    neg = -jnp.inf
    gl = jnp.where(lane < N_GROUPS, logits, neg)
    gmax = jnp.max(gl, axis=-1, keepdims=True)
    grp_p = 1.0 / jnp.sum(jnp.exp(gl - gmax), axis=-1, keepdims=True)
    gidx = jnp.min(jnp.where(gl == gmax, lane, float(LANES)), axis=-1, keepdims=True)
    lo = N_GROUPS + EXPERTS_PER_GROUP * gidx
    el = jnp.where((lane >= lo) & (lane < lo + EXPERTS_PER_GROUP), logits, neg)
    m1 = jnp.max(el, axis=-1, keepdims=True)
    i1 = jnp.min(jnp.where(el == m1, lane, float(LANES)), axis=-1, keepdims=True)
    el2 = jnp.where(lane == i1, neg, el)
    m2 = jnp.max(el2, axis=-1, keepdims=True)
    i2 = jnp.min(jnp.where(el2 == m2, lane, float(LANES)), axis=-1, keepdims=True)
    tt = jnp.exp(m2 - m1)
    g1 = grp_p / (1.0 + tt)
    g2 = g1 * tt
    e1 = i1 - N_GROUPS
    e2 = i2 - N_GROUPS

    onehot = ((lane == e1) | (lane == e2)).astype(F32)
    strict = (_iota((tm, tm), 1) < _iota((tm, tm), 0)).astype(BF16)
    rank = _dot(strict, onehot.astype(BF16)) + carry_ref[0:1, :]
    r1 = _lane_pick(lane, e1, rank)
    r2 = _lane_pick(lane, e2, rank)
    carry_ref[...] = carry_ref[...] + jnp.sum(onehot, axis=0, keepdims=True)
    cnt_ref[...] = carry_ref[...]

    route = jnp.where(lane == 0, e1, jnp.where(lane == 1, e2, jnp.where(lane == 2, r1, jnp.where(
        lane == 3, r2, jnp.where(lane == 4, g1, jnp.where(lane == 5, g2, 0.0))))))
    route_ref[...] = route

    _store_token_tiles(hp_ref, h)


def _router(x, g, scale, shift, w_route, b_route, *, tm=512):
    t, d = x.shape
    vec = pl.BlockSpec((1, d), lambda i: (0, 0))
    return pl.pallas_call(
        functools.partial(_router_body, tm=tm),
        grid=(t // tm,),
        in_specs=[pl.BlockSpec((tm, d), lambda i: (i, 0)), vec, vec, vec,
                  pl.BlockSpec((d, LANES), lambda i: (0, 0)),
                  pl.BlockSpec((1, LANES), lambda i: (0, 0))],
        out_specs=[pl.BlockSpec((tm * TOKEN_TILE_ROWS, LANES), lambda i: (i, 0)),
                   pl.BlockSpec((tm, LANES), lambda i: (i, 0)),
                   pl.BlockSpec((8, LANES), lambda i: (0, 0))],
        out_shape=[jax.ShapeDtypeStruct((t * TOKEN_TILE_ROWS, LANES), U32),
                   jax.ShapeDtypeStruct((t, LANES), F32),
                   jax.ShapeDtypeStruct((8, LANES), F32)],
        scratch_shapes=[pltpu.VMEM((8, LANES), F32)],
        compiler_params=_cparams(("arbitrary",)),
        name="moe_router",
    )(x, g, scale, shift, w_route, b_route)


def _dest_body(route_ref, cnt_ref, dest_ref, blk_ref, *, tm, n_blocks_pad):
    cnt = cnt_ref[...]
    padded = jnp.floor((cnt + (MOE_ROWS - 1)) * (1.0 / MOE_ROWS)) * MOE_ROWS
    upper = (_iota((LANES, LANES), 0) <= _iota((LANES, LANES), 1)).astype(F32)
    pad_end = _dot(padded, upper, _HI)
    pad_start = (pad_end - padded)[0:1, :]
    route = route_ref[...]
    lane = _iota((tm, LANES), 1).astype(F32)
    e1, e2, r1, r2 = route[:, 0:1], route[:, 1:2], route[:, 2:3], route[:, 3:4]
    d1 = _lane_pick(lane, e1, pad_start) + r1
    d2 = _lane_pick(lane, e2, pad_start) + r2
    dest_ref[...] = jnp.where(lane == 0, d1, jnp.where(lane == 1, d2, 0.0)).astype(I32)

    lane_b = _iota((n_blocks_pad, LANES), 1)
    first_row = (_iota((n_blocks_pad, LANES), 0) * MOE_ROWS).astype(F32)
    done = ((pad_end[0:1, :] <= first_row) & (lane_b < N_EXPERTS)).astype(F32)
    blk_e = jnp.minimum(jnp.sum(done, axis=-1, keepdims=True), float(N_EXPERTS - 1))
    n_valid = pad_end[0:1, N_EXPERTS - 1:N_EXPERTS] * (1.0 / MOE_ROWS)
    on_diag = lane_b == _iota((n_blocks_pad, LANES), 0)
    seg_fill = jnp.sum(jnp.where(on_diag, pad_start + cnt[0:1, :], 0.0), axis=-1, keepdims=True)
    seg_end = jnp.sum(jnp.where(on_diag, pad_end[0:1, :], 0.0), axis=-1, keepdims=True)
    blk_ref[...] = jnp.where(lane_b == 0, blk_e, jnp.where(lane_b == 1, n_valid, jnp.where(
        lane_b == 2, seg_fill, jnp.where(lane_b == 3, seg_end, 0.0)))).astype(I32)


def _dest(route, counts, n_blocks_pad, *, tm=512):
    t = route.shape[0]
    assert t % tm == 0
    return pl.pallas_call(
        functools.partial(_dest_body, tm=tm, n_blocks_pad=n_blocks_pad),
        grid=(t // tm,),
        in_specs=[pl.BlockSpec((tm, LANES), lambda i: (i, 0)),
                  pl.BlockSpec((8, LANES), lambda i: (0, 0))],
        out_specs=[pl.BlockSpec((tm, LANES), lambda i: (i, 0)),
                   pl.BlockSpec((n_blocks_pad, LANES), lambda i: (0, 0))],
        out_shape=[jax.ShapeDtypeStruct((t, LANES), I32),
                   jax.ShapeDtypeStruct((n_blocks_pad, LANES), I32)],
        compiler_params=_cparams(("arbitrary",)),
        name="moe_dest",
    )(route, counts)


def _invert_body(dest_ref, fill_ref, end_ref, tok_ref, *, n_assign, n_rows):
    def fill8(g, c):
        for r in range(8):
            tok_ref[g * 8 + r] = 0
        return c

    def fill_expert(e, c):
        return lax.fori_loop(fill_ref[e] >> 3, end_ref[e] >> 3, fill8, c)

    lax.fori_loop(0, N_EXPERTS, fill_expert, 0)
    lax.fori_loop(end_ref[N_EXPERTS - 1] >> 3, n_rows // 8, fill8, 0)

    def scatter(a, c):
        tok_ref[dest_ref[a]] = a >> 1
        return c

    lax.fori_loop(0, n_assign, scatter, 0, unroll=8)


def _invert(dest_flat, seg_fill, seg_end, n_rows):
    n_assign = dest_flat.shape[0]
    return pl.pallas_call(
        functools.partial(_invert_body, n_assign=n_assign, n_rows=n_rows),
        grid_spec=pltpu.PrefetchScalarGridSpec(
            num_scalar_prefetch=3,
            grid=(1,),
            in_specs=[],
            out_specs=pl.BlockSpec(memory_space=pltpu.SMEM),
        ),
        out_shape=jax.ShapeDtypeStruct((n_rows,), I32),
        compiler_params=_cparams(("arbitrary",)),
        name="moe_invert",
    )(dest_flat, seg_fill, seg_end)


def _expert_body(be_ref, nv_ref, tok_ref, hp_ref, w1_ref, w3_ref, w2_ref, y_ref, xbuf_ref, sem, wst1_ref, wst3_ref,
                 wst2_ref, wsem, w1b_ref, w3b_ref, w2b_ref, *, layer, inline_prefetch, weight_priority):
    b = pl.program_id(0)
    n_blocks = pl.num_programs(0)
    wst_refs = (wst1_ref, wst3_ref, wst2_ref)
    n_valid = nv_ref[0]
    valid = b < n_valid
    tr = TOKEN_TILE_ROWS
    expert = be_ref[b]
    first_of_expert = valid & ((b == 0) | (expert != be_ref[jnp.maximum(b - 1, 0)]))

    def weight_copies(e):
        return [pltpu.make_async_copy(w_ref.at[layer, e], st_ref, wsem.at[i])
                for i, (w_ref, st_ref) in enumerate(zip((w1_ref, w3_ref, w2_ref), wst_refs))]

    @pl.when((b == 0) & valid)
    def _():
        for c in weight_copies(expert):
            c.start(priority=weight_priority)

    def gather_rows(blk, slot):
        def one(j, c):
            tok = tok_ref[blk * MOE_ROWS + j]
            pltpu.make_async_copy(hp_ref.at[pl.ds(tok * tr, tr)], xbuf_ref.at[slot, pl.ds(j * tr, tr)],
                                  sem.at[slot]).start()
            return c

        lax.fori_loop(0, MOE_ROWS, one, 0, unroll=8)

    @pl.when((b == 0) & valid)
    def _():
        gather_rows(0, 0)

    if not inline_prefetch:
        @pl.when(b + 1 < n_valid)
        def _():
            gather_rows(b + 1, (b + 1) % 2)

    @pl.when(first_of_expert)
    def _():
        for c in weight_copies(expert):
            c.wait()
        for wb_ref, st_ref in zip((w1b_ref, w3b_ref, w2b_ref), wst_refs):
            wb_ref[...] = st_ref[...].astype(BF16)
        nxt = lax.while_loop(lambda j: (j < n_valid) & (be_ref[jnp.minimum(j, n_blocks - 1)] == expert),
                             lambda j: j + 1, b + 1)

        @pl.when(nxt < n_valid)
        def _():
            for c in weight_copies(be_ref[jnp.minimum(nxt, n_blocks - 1)]):
                c.start(priority=weight_priority)

    @pl.when(jnp.logical_not(valid))
    def _():
        y_ref[...] = jnp.zeros_like(y_ref)

    def ffn_block(prefetch_next):
        slot = b % 2
        pltpu.make_async_copy(hp_ref.at[pl.ds(0, MOE_ROWS * tr)], xbuf_ref.at[slot], sem.at[slot]).wait()
        lo, hi = _load_token_tiles(xbuf_ref.at[slot], MOE_ROWS)
        x = jnp.concatenate(lo + hi, axis=1).astype(BF16)
        if prefetch_next:
            for j in range(MOE_ROWS):
                tok = tok_ref[(b + 1) * MOE_ROWS + j]
                pltpu.make_async_copy(hp_ref.at[pl.ds(tok * tr, tr)], xbuf_ref.at[1 - slot, pl.ds(j * tr, tr)],
                                      sem.at[1 - slot]).start()
        a = (_silu(_dot(x, w1b_ref[...])) * _dot(x, w3b_ref[...])).astype(BF16)
        _store_token_tiles(y_ref, _dot(a, w2b_ref[...]))

    has_next = b + 1 < n_valid
    if inline_prefetch:
        pl.when(valid & has_next)(functools.partial(ffn_block, True))
        pl.when(valid & jnp.logical_not(has_next))(functools.partial(ffn_block, False))
    else:
        pl.when(valid)(functools.partial(ffn_block, False))


def _experts(block_expert, n_valid, tok_rows, h_packed, w1, w3, w2, *, layer):
    n_blocks = block_expert.shape[0]
    _, _, d, f = w1.shape
    anywhere = pl.BlockSpec(memory_space=pl.ANY)
    return pl.pallas_call(
        functools.partial(_expert_body, layer=layer, inline_prefetch=(layer == 1), weight_priority=1 - layer),
        grid_spec=pltpu.PrefetchScalarGridSpec(
            num_scalar_prefetch=3,
            grid=(n_blocks,),
            in_specs=[anywhere, anywhere, anywhere, anywhere],
            out_specs=pl.BlockSpec((MOE_ROWS * TOKEN_TILE_ROWS, LANES), lambda b, be, nv, tk: (b, 0)),
            scratch_shapes=[pltpu.VMEM((2, MOE_ROWS * TOKEN_TILE_ROWS, LANES), U32), pltpu.SemaphoreType.DMA((2,)),
                            pltpu.VMEM((d, f), F32), pltpu.VMEM((d, f), F32), pltpu.VMEM((f, d), F32),
                            pltpu.SemaphoreType.DMA((3,)),
                            pltpu.VMEM((d, f), BF16), pltpu.VMEM((d, f), BF16), pltpu.VMEM((f, d), BF16)],
        ),
        out_shape=jax.ShapeDtypeStruct((n_blocks * MOE_ROWS * TOKEN_TILE_ROWS, LANES), U32),
        compiler_params=_cparams(("arbitrary",)),
        name="moe_experts",
    )(block_expert, n_valid, tok_rows, h_packed, w1, w3, w2)


def _combine_body(dest_ref, y_ref, x_ref, route_ref, gate_ref, fg_ref, o_ref, buf_ref, sem, *, tm, final_norm):
    i = pl.program_id(0)
    yr = TOKEN_TILE_ROWS

    def gather_rows(blk, slot):
        def one(j, c):
            for k in range(2):
                d = dest_ref[(blk * tm + j) * 2 + k]
                pltpu.make_async_copy(y_ref.at[pl.ds(d * yr, yr)], buf_ref.at[slot, k, pl.ds(j * yr, yr)],
                                      sem.at[slot, k]).start()
            return c

        lax.fori_loop(0, tm, one, 0, unroll=4)

    @pl.when(i == 0)
    def _():
        gather_rows(0, 0)

    @pl.when(i + 1 < pl.num_programs(0))
    def _():
        gather_rows(i + 1, (i + 1) % 2)

    slot = i % 2
    for k in range(2):
        pltpu.make_async_copy(y_ref.at[pl.ds(0, tm * yr)], buf_ref.at[slot, k], sem.at[slot, k]).wait()

    route = route_ref[...]
    g1, g2 = route[:, 4:5], route[:, 5:6]
    lo1, hi1 = _load_token_tiles(buf_ref.at[slot, 0], tm)
    lo2, hi2 = _load_token_tiles(buf_ref.at[slot, 1], tm)
    half = x_ref.shape[1] // 2
    for s in range(yr):
        for base, y1, y2 in ((0, lo1[s], lo2[s]), (half, hi1[s], hi2[s])):
            cs = slice(base + s * LANES, base + (s + 1) * LANES)
            o_ref[:, cs] = x_ref[:, cs] + gate_ref[:, cs] * (g1 * y1 + g2 * y2)
    if final_norm:
        xo = o_ref[...]
        o_ref[...] = xo * lax.rsqrt(jnp.mean(xo * xo, axis=-1, keepdims=True) + EPS) * fg_ref[...]


def _combine(dest_flat, y_rows, x, route, gate, final_g, *, final_norm, tm=128):
    t, d = x.shape
    return pl.pallas_call(
        functools.partial(_combine_body, tm=tm, final_norm=final_norm),
        grid_spec=pltpu.PrefetchScalarGridSpec(
            num_scalar_prefetch=1,
            grid=(t // tm,),
            in_specs=[pl.BlockSpec(memory_space=pl.ANY),
                      pl.BlockSpec((tm, d), lambda i, dr: (i, 0)),
                      pl.BlockSpec((tm, LANES), lambda i, dr: (i, 0)),
                      pl.BlockSpec((1, d), lambda i, dr: (0, 0)),
                      pl.BlockSpec((1, d), lambda i, dr: (0, 0))],
            out_specs=pl.BlockSpec((tm, d), lambda i, dr: (i, 0)),
            scratch_shapes=[pltpu.VMEM((2, 2, tm * TOKEN_TILE_ROWS, LANES), U32), pltpu.SemaphoreType.DMA((2, 2))],
        ),
        out_shape=jax.ShapeDtypeStruct((t, d), F32),
        compiler_params=_cparams(("arbitrary",)),
        name="moe_combine",
    )(dest_flat, y_rows, x, route, gate, final_g)


def _moe(x, g, scale, shift, gate, w_group, b_group, w_expert, b_expert, w1, w3, w2, final_g, *, layer, final_norm):
    t, d = x.shape
    pad = LANES - N_GROUPS - N_EXPERTS
    w_route = jnp.concatenate([w_group, w_expert, jnp.zeros((d, pad), F32)], axis=1)
    b_route = jnp.concatenate([b_group, b_expert, jnp.zeros((pad,), F32)]).reshape(1, LANES)
    h_packed, route, counts = _router(x, g, scale, shift, w_route, b_route)
    n_assign = 2 * t
    n_blocks = -(-(n_assign + N_EXPERTS * (MOE_ROWS - 1)) // MOE_ROWS)
    n_blocks_pad = -(-n_blocks // 8) * 8
    dest, blk = _dest(route, counts, n_blocks_pad)
    dest_flat = dest[:, :2].reshape(n_assign)
    block_expert = blk[:n_blocks, 0]
    n_valid = blk[0, 1:2]
    tok_rows = _invert(dest_flat, blk[:N_EXPERTS, 2], blk[:N_EXPERTS, 3], n_blocks * MOE_ROWS)
    y_rows = _experts(block_expert, n_valid, tok_rows, h_packed, w1, w3, w2, layer=layer)
    return _combine(dest_flat, y_rows, x, route, gate, final_g, final_norm=final_norm)


def _pad_rows(w, rows):
    return jnp.pad(w, ((0, rows - w.shape[0]), (0, 0)))


def _lane_row(v, offset):
    return jnp.zeros((LANES,), F32).at[offset:offset + v.shape[0]].set(v)


def kernel(x, c, w_mod, b_mod, norm_g, final_norm_g, e_w_in, e_conv_qkv, e_a_log, e_dt_bias, e_head_norm_g, e_conf_dw, e_conf_dw_b, e_conf_ln_g, e_conf_ln_b, e_w_out, o_w_in, o_conv_w, o_conv_b, o_dt_bias, o_a_log, o_d_skip, o_norm_g, o_w_out, moe_w_group, moe_b_group, moe_w_expert, moe_b_expert, moe_w1, moe_w3, moe_w2):
    bsz, t, d = x.shape
    assert bsz == 1 and c.shape == (1, d)
    xt = x[0]
    mod = _modulation(c[0], w_mod, b_mod)

    def mod_parts(l):
        return mod[l, :, :d], mod[l, :, d:2 * d], mod[l, :, 2 * d:]

    row = lambda v: v.reshape(1, -1)
    qkv_w = 3 * GDN_HEADS * HEAD_W
    z_end = qkv_w + GDN_HEADS * HEAD_W

    shift, scale, gate = mod_parts(0)
    w_in = e_w_in[0].T
    tn = 512
    w_glu = w_in[z_end + 2 * GDN_HEADS:]
    w_ba = _pad_rows(w_in[z_end:z_end + 2 * GDN_HEADS], LANES)
    proj0, ba = _norm_matmul(xt, row(norm_g[0, 0]), scale, shift,
                             [(w_in, z_end // tn), (w_glu, w_glu.shape[0] // tn)], w_ba, tn=tn)
    gdn_par = jnp.zeros((8, LANES), F32).at[0].set(_lane_row(e_a_log[0], GDN_HEADS)).at[1].set(
        _lane_row(e_dt_bias[0], GDN_HEADS))
    a_out = _gdn(proj0, ba, e_conv_qkv[0], gdn_par, row(e_head_norm_g[0]))
    b_out = _conformer(proj0, e_conf_dw[0], row(e_conf_dw_b[0]), row(e_conf_ln_g[0]), row(e_conf_ln_b[0]))
    xt = _out_proj0(a_out, b_out, e_w_out[0], xt, gate)
    shift, scale, gate = mod_parts(1)
    xt = _moe(xt, row(norm_g[0, 1]), scale, shift, gate, moe_w_group[0], moe_b_group[0], moe_w_expert[0],
              moe_b_expert[0], moe_w1, moe_w3, moe_w2, row(final_norm_g), layer=0, final_norm=False)

    shift, scale, gate = mod_parts(2)
    w_in = o_w_in[0].T
    n_main = 2 * SSM_GROUPS * SSM_GROUP_W + 2 * SSM_GROUPS * SSM_STATE
    tn = 1024
    proj1, dt_raw = _norm_matmul(xt, row(norm_g[1, 0]), scale, shift, [(w_in, n_main // tn)],
                                 _pad_rows(w_in[n_main:], LANES), tn=tn)
    ssd_par = jnp.zeros((8, LANES), F32).at[0].set(_lane_row(o_dt_bias[0], 0)).at[1].set(
        _lane_row(o_a_log[0], 0)).at[2].set(_lane_row(o_d_skip[0], 0))
    y = _ssd(proj1, dt_raw, o_conv_w[0], row(o_conv_b[0]), ssd_par)
    xt = _out_proj1(y, row(o_norm_g[0]), o_w_out[0], xt, gate)
    shift, scale, gate = mod_parts(3)
    xt = _moe(xt, row(norm_g[1, 1]), scale, shift, gate, moe_w_group[1], moe_b_group[1], moe_w_expert[1],
              moe_b_expert[1], moe_w1, moe_w3, moe_w2, row(final_norm_g), layer=1, final_norm=True)
    return xt[None]
```

```python
import functools

import jax
import jax.numpy as jnp
from jax import lax
from jax.experimental import pallas as pl
from jax.experimental.pallas import tpu as pltpu

F32 = jnp.float32
BF16 = jnp.bfloat16
I32 = jnp.int32
U32 = jnp.uint32

EPS = 1e-6
LANES = 128
CHUNK = 64
CHUNK_LOG2 = 6
GDN_HEADS = 8
HEAD_W = 128
CONF_CH = 1024
CONF_K = 31
SSM_GROUPS = 8
SSM_GROUP_W = 512
SSM_HEADDIM = 64
SSM_STATE = 128
N_EXPERTS = 32
EXPERTS_PER_GROUP = 8
N_GROUPS = 4
MOE_ROWS = 256
TOKEN_TILE_ROWS = 8
VMEM_LIMIT = 56 * 1024 * 1024
PROLOGUE_ROWS = 128
HALO_ROWS = 16
CONF_HALO_ROWS = 32

_HI = lax.Precision.HIGHEST


def _cparams(sem):
    return pltpu.CompilerParams(dimension_semantics=sem, vmem_limit_bytes=VMEM_LIMIT)


def _dot(a, b, precision=None):
    return jnp.dot(a, b, precision=precision, preferred_element_type=F32)


def _dot_nt(a, b, precision=None):
    return lax.dot_general(a, b, (((1,), (1,)), ((), ())), precision=precision, preferred_element_type=F32)


def _dot_tn(a, b, precision=None):
    return lax.dot_general(a, b, (((0,), (0,)), ((), ())), precision=precision, preferred_element_type=F32)


def _split_bf16(x):
    hi = x.astype(BF16)
    return hi, (x - hi.astype(F32)).astype(BF16)


def _dot3(a, b):
    ah, al = _split_bf16(a)
    bh, bl = _split_bf16(b)
    return _dot(ah, bh) + (_dot(ah, bl) + _dot(al, bh))


def _select_dot(sel, x, left=False):
    p1 = x.astype(BF16)
    r1 = x - p1.astype(F32)
    p2 = r1.astype(BF16)
    p3 = (r1 - p2.astype(F32)).astype(BF16)
    if left:
        return _dot(sel, p1) + (_dot(sel, p2) + _dot(sel, p3))
    return _dot(p1, sel) + (_dot(p2, sel) + _dot(p3, sel))


def _silu(x):
    return x * jax.nn.sigmoid(x)


def _softplus(x):
    return jnp.maximum(x, 0.0) + jnp.log1p(jnp.exp(-jnp.abs(x)))


def _iota(shape, dim):
    return lax.broadcasted_iota(I32, shape, dim)


def _mod_body(c_ref, w_ref, b_ref, o_ref):
    c = c_ref[...]
    o_ref[...] = jnp.sum(w_ref[...] * _silu(c), axis=0, keepdims=True) + b_ref[...]


def _modulation(c, w_mod, b_mod):
    nl, d, n = w_mod.shape
    tn = 768
    return pl.pallas_call(
        _mod_body,
        grid=(nl, n // tn),
        in_specs=[
            pl.BlockSpec((d, 1), lambda l, j: (0, 0)),
            pl.BlockSpec((None, d, tn), lambda l, j: (l, 0, j)),
            pl.BlockSpec((None, 1, tn), lambda l, j: (l, 0, j)),
        ],
        out_specs=pl.BlockSpec((None, 1, tn), lambda l, j: (l, 0, j)),
        out_shape=jax.ShapeDtypeStruct((nl, 1, n), F32),
        compiler_params=_cparams(("arbitrary", "arbitrary")),
        name="adaln_mod",
    )(c.reshape(d, 1), w_mod, b_mod.reshape(nl, 1, n))


def _adaln_rows(x, g, scale, shift):
    r = lax.rsqrt(jnp.mean(x * x, axis=-1, keepdims=True) + EPS)
    return x * r * (g * (1.0 + scale)) + shift


def _norm_mm_body(*refs, starts):
    x_ref, g_ref, sc_ref, sh_ref, ws_ref = refs[:5]
    w_refs = refs[5:5 + len(starts)]
    o_ref, os_ref, h_ref = refs[5 + len(starts):]
    j = pl.program_id(1)

    @pl.when(j == 0)
    def _():
        ws = ws_ref[...].astype(BF16)

        def norm_rows(c, carry):
            rs = pl.ds(pl.multiple_of(c * PROLOGUE_ROWS, PROLOGUE_ROWS), PROLOGUE_ROWS)
            h = _adaln_rows(x_ref[rs, :], g_ref[...], sc_ref[...], sh_ref[...]).astype(BF16)
            h_ref[rs, :] = h
            os_ref[rs, :] = _dot_nt(h, ws)
            return carry

        lax.fori_loop(0, x_ref.shape[0] // PROLOGUE_ROWS, norm_rows, 0)

    bounds = list(starts[1:]) + [None]
    for w_ref, lo, hi in zip(w_refs, starts, bounds):
        in_seg = (j >= lo) if hi is None else ((j >= lo) & (j < hi))

        @pl.when(in_seg)
        def _(w_ref=w_ref):
            o_ref[...] = _dot_nt(h_ref[...], w_ref[...].astype(BF16)).astype(o_ref.dtype)


def _norm_matmul(x, g, scale, shift, segments, w_small, *, tm=1024, tn=512):
    t, d = x.shape
    tm = min(tm, t)
    starts, specs, n_main = [], [], 0
    for w, nb in segments:
        lo = n_main // tn
        starts.append(lo)
        specs.append(pl.BlockSpec((tn, d), lambda i, j, lo=lo, nb=nb: (jnp.clip(j - lo, 0, nb - 1), 0)))
        n_main += nb * tn
    vec = pl.BlockSpec((1, d), lambda i, j: (0, 0))
    return pl.pallas_call(
        functools.partial(_norm_mm_body, starts=tuple(starts)),
        grid=(t // tm, n_main // tn),
        in_specs=[pl.BlockSpec((tm, d), lambda i, j: (i, 0)), vec, vec, vec,
                  pl.BlockSpec((LANES, d), lambda i, j: (0, 0))] + specs,
        out_specs=[pl.BlockSpec((tm, tn), lambda i, j: (i, j)),
                   pl.BlockSpec((tm, LANES), lambda i, j: (i, 0))],
        out_shape=[jax.ShapeDtypeStruct((t, n_main), BF16), jax.ShapeDtypeStruct((t, LANES), F32)],
        scratch_shapes=[pltpu.VMEM((tm, d), BF16)],
        compiler_params=_cparams(("arbitrary", "arbitrary")),
        name="adaln_in_proj",
    )(x, g, scale, shift, w_small, *[w for w, _ in segments])


def _causal_conv(halo, x, w, taps):
    hr = halo.shape[0]
    n = x.shape[0]
    xe = jnp.concatenate([halo, x], axis=0)
    rolled = {0: xe}
    acc = None
    for j in range(taps):
        s = taps - 1 - j
        a, b = divmod(s, 8)
        if b not in rolled:
            rolled[b] = pltpu.roll(xe, b, axis=0)
        term = rolled[b][hr - 8 * a:hr - 8 * a + n] * w[j:j + 1]
        acc = term if acc is None else acc + term
    return acc


def _unit_lower_inverses(a_mats, row, col):
    n = a_mats[0].shape[0]
    eye = (row == col).astype(F32)
    level = ((row >> 1) == (col >> 1)) & (col < row)
    invs = [eye - jnp.where(level, a, 0.0) for a in a_mats]
    k = 1
    while (1 << k) < n:
        level = ((row >> (k + 1)) == (col >> (k + 1))) & (((row >> k) & 1) == 1) & (((col >> k) & 1) == 0)
        inv16 = [inv.astype(BF16) for inv in invs]
        ys = [_dot(i16, jnp.where(level, a, 0.0).astype(BF16)) for i16, a in zip(inv16, a_mats)]
        invs = [inv - _dot(y.astype(BF16), i16) for inv, y, i16 in zip(invs, ys, inv16)]
        k += 1
    return invs


def _gdn_body(q_ref, k_ref, v_ref, qh_ref, kh_ref, vh_ref, z_ref, ba_ref, cwq_ref, cwk_ref, cwv_ref,
              par_ref, hg_ref, o_ref, s_ref, *, hb, rows):
    hblk = pl.program_id(0)
    ir = pl.program_id(1)

    @pl.when(ir == 0)
    def _():
        s_ref[...] = jnp.zeros_like(s_ref)

    def conv_silu(x_ref, halo_ref, w_ref):
        halo = jnp.where(ir > 0, halo_ref[...].astype(F32), 0.0)
        return _silu(_causal_conv(halo, x_ref[...].astype(F32), w_ref[...], 4))

    qc = conv_silu(q_ref, qh_ref, cwq_ref)
    kc = conv_silu(k_ref, kh_ref, cwk_ref)
    vc = conv_silu(v_ref, vh_ref, cwv_ref)

    ba = ba_ref[...]
    beta_all = jax.nn.sigmoid(ba)
    g_all = -jnp.exp(par_ref[0:1, :]) * _softplus(ba + par_ref[1:2, :])
    rr = _iota((rows, rows), 0)
    cc = _iota((rows, rows), 1)
    chunk_tril = ((rr >> CHUNK_LOG2) == (cc >> CHUNK_LOG2)) & (cc <= rr)
    gcs_all = _select_dot(chunk_tril.astype(BF16), g_all, left=True)
    gcs_t = gcs_all.T

    row = _iota((CHUNK, CHUNK), 0)
    col = _iota((CHUNK, CHUNK), 1)
    scale = HEAD_W ** -0.5
    hg = hg_ref[...]
    bf = lambda a: a.astype(BF16)
    lane = _iota((rows, LANES), 1)
    sub = _iota((LANES, rows), 0)
    nc = rows // CHUNK

    heads = []
    for h in range(hb):
        hs = slice(h * HEAD_W, (h + 1) * HEAD_W)
        head = hblk * hb + h
        q = qc[:, hs]
        k = kc[:, hs]
        heads.append(dict(
            q=q * lax.rsqrt(jnp.sum(q * q, axis=-1, keepdims=True) + EPS) * scale,
            k=k * lax.rsqrt(jnp.sum(k * k, axis=-1, keepdims=True) + EPS),
            v=vc[:, hs],
            beta=jnp.sum(jnp.where(lane == head, beta_all, 0.0), axis=-1, keepdims=True),
            gcs=jnp.sum(jnp.where(lane == head + GDN_HEADS, gcs_all, 0.0), axis=-1, keepdims=True),
            gcs_row=jnp.sum(jnp.where(sub == head + GDN_HEADS, gcs_t, 0.0), axis=0, keepdims=True)))

    items = []
    for c in range(nc):
        rs = slice(c * CHUNK, (c + 1) * CHUNK)
        for hd in heads:
            qq, kk, vv, beta, gcs = hd["q"][rs], hd["k"][rs], hd["v"][rs], hd["beta"][rs], hd["gcs"][rs]
            decay = jnp.where(col <= row, jnp.exp(jnp.minimum(gcs - hd["gcs_row"][:, rs], 0.0)), 0.0)
            kb = kk * beta
            eg = jnp.exp(gcs)
            g_last = gcs[CHUNK - 1:CHUNK]
            items.append(dict(decay=decay, kb16=bf(kb), k16=bf(kk), q16=bf(qq),
                              rhs16=bf(jnp.concatenate([vv * beta, kb * eg], axis=-1)),
                              q_dec16=bf(qq * eg), k_dec16=bf(kk * jnp.exp(g_last - gcs)), cd=jnp.exp(g_last)))
    a_mats = [jnp.where(col < row, _dot_nt(it["kb16"], it["k16"]) * it["decay"], 0.0) for it in items]
    qks = [bf(_dot_nt(it["q16"], it["k16"]) * it["decay"]) for it in items]
    t_invs = _unit_lower_inverses(a_mats, row, col)
    sols = [_dot(bf(t), it["rhs16"]) for t, it in zip(t_invs, items)]

    states = [s_ref[h] for h in range(hb)]
    for c in range(nc):
        rs = slice(c * CHUNK, (c + 1) * CHUNK)
        cur = slice(c * hb, (c + 1) * hb)
        s16 = [bf(s) for s in states]
        us = [so[:, :HEAD_W] - _dot(bf(so[:, HEAD_W:]), s) for so, s in zip(sols[cur], s16)]
        outs = [_dot(it["q_dec16"], s) + _dot(qk, bf(u)) for it, s, qk, u in zip(items[cur], s16, qks[cur], us)]
        states = [s * it["cd"] + _dot_tn(it["k_dec16"], bf(u)) for s, it, u in zip(states, items[cur], us)]
        for h, o in enumerate(outs):
            hs = slice(h * HEAD_W, (h + 1) * HEAD_W)
            o = o * lax.rsqrt(jnp.mean(o * o, axis=-1, keepdims=True) + EPS) * hg * _silu(z_ref[rs, hs].astype(F32))
            o_ref[rs, hs] = o.astype(o_ref.dtype)
    for h in range(hb):
        s_ref[h] = states[h]


def _gdn(proj, ba, conv_w, par, head_g, *, hb=GDN_HEADS, rows=256):
    t = proj.shape[0]
    rows = min(rows, t)
    w = hb * HEAD_W
    per = (GDN_HEADS * HEAD_W) // w

    def sec(k):
        return pl.BlockSpec((rows, w), lambda h, i, k=k: (i, k * per + h))

    def halo(k):
        return pl.BlockSpec((HALO_ROWS, w),
                            lambda h, i, k=k: (jnp.maximum(i * (rows // HALO_ROWS) - 1, 0), k * per + h))

    def cw(k):
        return pl.BlockSpec((4, w), lambda h, i, k=k: (0, k * per + h))

    return pl.pallas_call(
        functools.partial(_gdn_body, hb=hb, rows=rows),
        grid=(GDN_HEADS // hb, t // rows),
        in_specs=[sec(0), sec(1), sec(2), halo(0), halo(1), halo(2), sec(3),
                  pl.BlockSpec((rows, LANES), lambda h, i: (i, 0)),
                  cw(0), cw(1), cw(2),
                  pl.BlockSpec((8, LANES), lambda h, i: (0, 0)),
                  pl.BlockSpec((1, HEAD_W), lambda h, i: (0, 0))],
        out_specs=pl.BlockSpec((rows, w), lambda h, i: (i, h)),
        out_shape=jax.ShapeDtypeStruct((t, GDN_HEADS * HEAD_W), BF16),
        scratch_shapes=[pltpu.VMEM((hb, HEAD_W, HEAD_W), F32)],
        compiler_params=_cparams(("arbitrary", "arbitrary")),
        name="gated_deltanet",
    )(proj, proj, proj, proj, proj, proj, proj, ba, conv_w, conv_w, conv_w, par, head_g)


def _conf_body(x_ref, halo_ref, w_ref, b_ref, g_ref, lb_ref, o_ref):
    ir = pl.program_id(0)

    def glu(v):
        v = v.astype(F32)
        return v[:, :CONF_CH] * jax.nn.sigmoid(v[:, CONF_CH:])

    halo = jnp.where(ir > 0, glu(halo_ref[...]), 0.0)
    u = _causal_conv(halo, glu(x_ref[...]), w_ref[...], CONF_K) + b_ref[...]
    mu = jnp.mean(u, axis=-1, keepdims=True)
    uc = u - mu
    var = jnp.mean(uc * uc, axis=-1, keepdims=True)
    y = uc * lax.rsqrt(var + EPS) * g_ref[...] + lb_ref[...]
    o_ref[...] = _silu(y).astype(o_ref.dtype)


def _conformer(proj, dw, dw_b, ln_g, ln_b, *, rows=256):
    t = proj.shape[0]
    rows = min(rows, t)
    glu_blk = (4 * GDN_HEADS * HEAD_W) // (2 * CONF_CH)
    halo_rows = CONF_HALO_ROWS
    vec = pl.BlockSpec((1, CONF_CH), lambda i: (0, 0))
    return pl.pallas_call(
        _conf_body,
        grid=(t // rows,),
        in_specs=[pl.BlockSpec((rows, 2 * CONF_CH), lambda i: (i, glu_blk)),
                  pl.BlockSpec((halo_rows, 2 * CONF_CH),
                               lambda i: (jnp.maximum(i * (rows // halo_rows) - 1, 0), glu_blk)),
                  pl.BlockSpec((CONF_K, CONF_CH), lambda i: (0, 0)), vec, vec, vec],
        out_specs=pl.BlockSpec((rows, CONF_CH), lambda i: (i, 0)),
        out_shape=jax.ShapeDtypeStruct((t, CONF_CH), BF16),
        compiler_params=_cparams(("arbitrary",)),
        name="conformer_conv",
    )(proj, proj, dw, dw_b, ln_g, ln_b)


def _out0_body(a_ref, b_ref, wa_ref, wb_ref, x_ref, gate_ref, o_ref):
    mix = _dot(a_ref[...], wa_ref[...].astype(BF16)) + _dot(b_ref[...], wb_ref[...].astype(BF16))
    o_ref[...] = x_ref[...] + gate_ref[...] * mix


def _out_proj0(a, b, w, x, gate, *, tm=1024, tn=1024):
    t, d = x.shape
    tm = min(tm, t)
    ka, kb = a.shape[1], b.shape[1]
    return pl.pallas_call(
        _out0_body,
        grid=(t // tm, d // tn),
        in_specs=[pl.BlockSpec((tm, ka), lambda i, j: (i, 0)),
                  pl.BlockSpec((tm, kb), lambda i, j: (i, 0)),
                  pl.BlockSpec((ka, tn), lambda i, j: (0, j)),
                  pl.BlockSpec((kb, tn), lambda i, j: (ka // kb, j)),
                  pl.BlockSpec((tm, tn), lambda i, j: (i, j)),
                  pl.BlockSpec((1, tn), lambda i, j: (0, j))],
        out_specs=pl.BlockSpec((tm, tn), lambda i, j: (i, j)),
        out_shape=jax.ShapeDtypeStruct((t, d), F32),
        compiler_params=_cparams(("arbitrary", "arbitrary")),
        name="out_proj_even",
    )(a, b, w, w, x, gate)


def _out1_body(y_ref, g_ref, w_ref, x_ref, gate_ref, o_ref, h_ref):
    @pl.when(pl.program_id(1) == 0)
    def _():
        def norm_rows(c, carry):
            rs = pl.ds(pl.multiple_of(c * PROLOGUE_ROWS, PROLOGUE_ROWS), PROLOGUE_ROWS)
            y = y_ref[rs, :].astype(F32)
            r = lax.rsqrt(jnp.mean(y * y, axis=-1, keepdims=True) + EPS)
            h_ref[rs, :] = (y * r * g_ref[...]).astype(BF16)
            return carry

        lax.fori_loop(0, y_ref.shape[0] // PROLOGUE_ROWS, norm_rows, 0)

    o_ref[...] = x_ref[...] + gate_ref[...] * _dot(h_ref[...], w_ref[...].astype(BF16))


def _out_proj1(y, norm_g, w, x, gate, *, tm=1024, tn=512):
    t, d = x.shape
    tm = min(tm, t)
    k = y.shape[1]
    return pl.pallas_call(
        _out1_body,
        grid=(t // tm, d // tn),
        in_specs=[pl.BlockSpec((tm, k), lambda i, j: (i, 0)),
                  pl.BlockSpec((1, k), lambda i, j: (0, 0)),
                  pl.BlockSpec((k, tn), lambda i, j: (0, j)),
                  pl.BlockSpec((tm, tn), lambda i, j: (i, j)),
                  pl.BlockSpec((1, tn), lambda i, j: (0, j))],
        out_specs=pl.BlockSpec((tm, tn), lambda i, j: (i, j)),
        out_shape=jax.ShapeDtypeStruct((t, d), F32),
        scratch_shapes=[pltpu.VMEM((tm, k), BF16)],
        compiler_params=_cparams(("arbitrary", "arbitrary")),
        name="out_proj_odd",
    )(y, norm_g, w, x, gate)


def _ssd_body(x_ref, b_ref, c_ref, xh_ref, bh_ref, ch_ref, z_ref, dt_ref, cwx_ref, cwb_ref, cwc_ref,
              cbx_ref, cbb_ref, cbc_ref, par_ref, o_ref, s_ref, at_ref, *, rows, gps):
    ir = pl.program_id(1)
    gw = SSM_GROUP_W
    hpg = gw // SSM_HEADDIM

    @pl.when(ir == 0)
    def _():
        s_ref[...] = jnp.zeros_like(s_ref)

    def conv_silu(x_r, halo_r, w_r, bias_r):
        halo = jnp.where(ir > 0, halo_r[...].astype(F32), 0.0)
        return _silu(_causal_conv(halo, x_r[...].astype(F32), w_r[...], 4) + bias_r[...])

    xs_all = conv_silu(x_ref, xh_ref, cwx_ref, cbx_ref)
    bm_all = conv_silu(b_ref, bh_ref, cwb_ref, cbb_ref)
    cm_all = conv_silu(c_ref, ch_ref, cwc_ref, cbc_ref)

    dt = _softplus(dt_ref[...] + par_ref[0:1, :])
    da = dt * (-jnp.exp(par_ref[1:2, :]))
    rr = _iota((rows, rows), 0)
    cc = _iota((rows, rows), 1)
    chunk_tril = ((rr >> CHUNK_LOG2) == (cc >> CHUNK_LOG2)) & (cc <= rr)
    acs = _select_dot(chunk_tril.astype(BF16), da, left=True)
    at_ref[...] = acs.T

    causal = (_iota((CHUNK, gw), 1) & (CHUNK - 1)) <= _iota((CHUNK, gw), 0)
    same_head = (_iota((LANES, LANES), 0) >> CHUNK_LOG2) == (_iota((LANES, LANES), 1) >> CHUNK_LOG2)
    bf = lambda a: a.astype(BF16)
    nc = rows // CHUNK
    chunks = [slice(c * CHUNK, (c + 1) * CHUNK) for c in range(nc)]
    tiles = [slice(k * LANES, (k + 1) * LANES) for k in range(gw // LANES)]

    for gi in range(gps):
        grp = pl.program_id(0) * gps + gi
        cols = slice(gi * gw, (gi + 1) * gw)
        xs = xs_all[:, cols]
        bm = bm_all[:, gi * SSM_STATE:(gi + 1) * SSM_STATE]
        cm = cm_all[:, gi * SSM_STATE:(gi + 1) * SSM_STATE]
        acs_rows = at_ref[pl.ds(pl.multiple_of(grp * hpg, hpg), hpg), :]

        sel = (_iota((LANES, gw), 0) == grp * hpg + (_iota((LANES, gw), 1) >> CHUNK_LOG2)).astype(BF16)
        dt_e = _select_dot(sel, dt)
        acs_e = _select_dot(sel, acs)
        dskip_e = _select_dot(sel, par_ref[...])[2:3, :]

        a_es = [acs_e[rs] for rs in chunks]
        a_rows = [jnp.concatenate([acs_rows[h:h + 1, rs] for h in range(hpg)], axis=1) for rs in chunks]
        l_cats = [jnp.where(causal, jnp.exp(jnp.minimum(a_e - a_row, 0.0)), 0.0) for a_e, a_row in zip(a_es, a_rows)]
        c16 = [bf(cm[rs]) for rs in chunks]
        b16 = [bf(bm[rs]) for rs in chunks]
        cbs = [_dot_nt(c_c, jnp.concatenate([b_c] * hpg, axis=0)) for c_c, b_c in zip(c16, b16)]
        xdts = [xs[rs] * dt_e[rs] for rs in chunks]
        m16 = [bf(cb * l_cat) for cb, l_cat in zip(cbs, l_cats)]
        y_diags = []
        for m, xdt in zip(m16, xdts):
            parts = [_dot(m[:, ts], bf(jnp.where(same_head, jnp.concatenate([xdt[:, ts]] * 2, axis=0), 0.0)))
                     for ts in tiles]
            y_diags.append(jnp.concatenate(parts, axis=1))
        a_lasts = [a_e[CHUNK - 1:CHUNK] for a_e in a_es]
        updates = [_dot_tn(b_c, bf(xdt * jnp.exp(a_last - a_e)))
                   for b_c, xdt, a_last, a_e in zip(b16, xdts, a_lasts, a_es)]

        states = [s_ref[gi]]
        for a_last, upd in zip(a_lasts, updates):
            states.append(states[-1] * jnp.exp(a_last) + upd)
        s_ref[gi] = states[-1]

        y_offs = [_dot(c_c, bf(s)) * jnp.exp(a_e) for c_c, s, a_e in zip(c16, states[:-1], a_es)]
        for rs, y_diag, y_off in zip(chunks, y_diags, y_offs):
            y = y_diag + y_off + dskip_e * xs[rs]
            o_ref[rs, cols] = (y * _silu(z_ref[rs, cols].astype(F32))).astype(o_ref.dtype)


def _ssd(proj, dt_raw, conv_w, conv_b, par, *, rows=256, gps=8):
    t = proj.shape[0]
    rows = min(rows, t)
    xw = gps * SSM_GROUP_W
    bw = gps * SSM_STATE
    d_inner = SSM_GROUPS * SSM_GROUP_W
    xo = d_inner // xw
    bo = (2 * d_inner) // bw
    co = bo + SSM_GROUPS * SSM_STATE // bw
    hrow = lambda i: jnp.maximum(i * (rows // HALO_ROWS) - 1, 0)
    wbo = d_inner // bw
    wco = wbo + SSM_GROUPS * SSM_STATE // bw
    return pl.pallas_call(
        functools.partial(_ssd_body, rows=rows, gps=gps),
        grid=(SSM_GROUPS // gps, t // rows),
        in_specs=[pl.BlockSpec((rows, xw), lambda g, i: (i, xo + g)),
                  pl.BlockSpec((rows, bw), lambda g, i: (i, bo + g)),
                  pl.BlockSpec((rows, bw), lambda g, i: (i, co + g)),
                  pl.BlockSpec((HALO_ROWS, xw), lambda g, i: (hrow(i), xo + g)),
                  pl.BlockSpec((HALO_ROWS, bw), lambda g, i: (hrow(i), bo + g)),
                  pl.BlockSpec((HALO_ROWS, bw), lambda g, i: (hrow(i), co + g)),
                  pl.BlockSpec((rows, xw), lambda g, i: (i, g)),
                  pl.BlockSpec((rows, LANES), lambda g, i: (i, 0)),
                  pl.BlockSpec((4, xw), lambda g, i: (0, g)),
                  pl.BlockSpec((4, bw), lambda g, i: (0, wbo + g)),
                  pl.BlockSpec((4, bw), lambda g, i: (0, wco + g)),
                  pl.BlockSpec((1, xw), lambda g, i: (0, g)),
                  pl.BlockSpec((1, bw), lambda g, i: (0, wbo + g)),
                  pl.BlockSpec((1, bw), lambda g, i: (0, wco + g)),
                  pl.BlockSpec((8, LANES), lambda g, i: (0, 0))],
        out_specs=pl.BlockSpec((rows, xw), lambda g, i: (i, g)),
        out_shape=jax.ShapeDtypeStruct((t, d_inner), BF16),
        scratch_shapes=[pltpu.VMEM((gps, SSM_STATE, SSM_GROUP_W), F32), pltpu.VMEM((LANES, rows), F32)],
        compiler_params=_cparams(("arbitrary", "arbitrary")),
        name="ssd",
    )(proj, proj, proj, proj, proj, proj, proj, dt_raw, conv_w, conv_w, conv_w, conv_b, conv_b, conv_b, par)


def _store_token_tiles(tile_ref, x):
    rows, d = x.shape
    bits = lax.bitcast_convert_type(x.astype(BF16).astype(F32), U32)
    packed = (bits[:, :d // 2] >> 16) | (bits[:, d // 2:] & jnp.uint32(0xFFFF0000))
    for s in range(TOKEN_TILE_ROWS):
        tile_ref[pl.ds(s, rows, stride=TOKEN_TILE_ROWS), :] = packed[:, s * LANES:(s + 1) * LANES]


def _load_token_tiles(tile_ref, rows):
    lo, hi = [], []
    for s in range(TOKEN_TILE_ROWS):
        p = tile_ref[pl.ds(s, rows, stride=TOKEN_TILE_ROWS), :]
        lo.append(lax.bitcast_convert_type(p << 16, F32))
        hi.append(lax.bitcast_convert_type(p & jnp.uint32(0xFFFF0000), F32))
    return lo, hi


def _lane_pick(lane, idx, vals):
    return jnp.sum(jnp.where(lane == idx, vals, 0.0), axis=-1, keepdims=True)


def _router_body(x_ref, g_ref, sc_ref, sh_ref, w_ref, b_ref, hp_ref, route_ref, cnt_ref, carry_ref, *, tm):
    i = pl.program_id(0)

    @pl.when(i == 0)
    def _():
        carry_ref[...] = jnp.zeros_like(carry_ref)

    h = _adaln_rows(x_ref[...], g_ref[...], sc_ref[...], sh_ref[...])
    logits = _dot3(h, w_ref[...]) + b_ref[...]
    lane = _iota((tm, LANES), 1).astype(F32)
    neg = -jnp.inf
    gl = jnp.where(lane < N_GROUPS, logits, neg)
    gmax = jnp.max(gl, axis=-1, keepdims=True)
    grp_p = 1.0 / jnp.sum(jnp.exp(gl - gmax), axis=-1, keepdims=True)
    gidx = jnp.min(jnp.where(gl == gmax, lane, float(LANES)), axis=-1, keepdims=True)
    lo = N_GROUPS + EXPERTS_PER_GROUP * gidx
    el = jnp.where((lane >= lo) & (lane < lo + EXPERTS_PER_GROUP), logits, neg)
    m1 = jnp.max(el, axis=-1, keepdims=True)
    i1 = jnp.min(jnp.where(el == m1, lane, float(LANES)), axis=-1, keepdims=True)
    el2 = jnp.where(lane == i1, neg, el)
    m2 = jnp.max(el2, axis=-1, keepdims=True)
    i2 = jnp.min(jnp.where(el2 == m2, lane, float(LANES)), axis=-1, keepdims=True)
    tt = jnp.exp(m2 - m1)
    g1 = grp_p / (1.0 + tt)
    g2 = g1 * tt
    e1 = i1 - N_GROUPS
    e2 = i2 - N_GROUPS

    onehot = ((lane == e1) | (lane == e2)).astype(F32)
    strict = (_iota((tm, tm), 1) < _iota((tm, tm), 0)).astype(BF16)
    rank = _dot(strict, onehot.astype(BF16)) + carry_ref[0:1, :]
    r1 = _lane_pick(lane, e1, rank)
    r2 = _lane_pick(lane, e2, rank)
    carry_ref[...] = carry_ref[...] + jnp.sum(onehot, axis=0, keepdims=True)
    cnt_ref[...] = carry_ref[...]

    route = jnp.where(lane == 0, e1, jnp.where(lane == 1, e2, jnp.where(lane == 2, r1, jnp.where(
        lane == 3, r2, jnp.where(lane == 4, g1, jnp.where(lane == 5, g2, 0.0))))))
    route_ref[...] = route

    _store_token_tiles(hp_ref, h)


def _router(x, g, scale, shift, w_route, b_route, *, tm=512):
    t, d = x.shape
    vec = pl.BlockSpec((1, d), lambda i: (0, 0))
    return pl.pallas_call(
        functools.partial(_router_body, tm=tm),
        grid=(t // tm,),
        in_specs=[pl.BlockSpec((tm, d), lambda i: (i, 0)), vec, vec, vec,
                  pl.BlockSpec((d, LANES), lambda i: (0, 0)),
                  pl.BlockSpec((1, LANES), lambda i: (0, 0))],
        out_specs=[pl.BlockSpec((tm * TOKEN_TILE_ROWS, LANES), lambda i: (i, 0)),
                   pl.BlockSpec((tm, LANES), lambda i: (i, 0)),
                   pl.BlockSpec((8, LANES), lambda i: (0, 0))],
        out_shape=[jax.ShapeDtypeStruct((t * TOKEN_TILE_ROWS, LANES), U32),
                   jax.ShapeDtypeStruct((t, LANES), F32),
                   jax.ShapeDtypeStruct((8, LANES), F32)],
        scratch_shapes=[pltpu.VMEM((8, LANES), F32)],
        compiler_params=_cparams(("arbitrary",)),
        name="moe_router",
    )(x, g, scale, shift, w_route, b_route)


def _dest_body(route_ref, cnt_ref, dest_ref, blk_ref, *, tm, n_blocks_pad):
    cnt = cnt_ref[...]
    padded = jnp.floor((cnt + (MOE_ROWS - 1)) * (1.0 / MOE_ROWS)) * MOE_ROWS
    upper = (_iota((LANES, LANES), 0) <= _iota((LANES, LANES), 1)).astype(F32)
    pad_end = _dot(padded, upper, _HI)
    pad_start = (pad_end - padded)[0:1, :]
    route = route_ref[...]
    lane = _iota((tm, LANES), 1).astype(F32)
    e1, e2, r1, r2 = route[:, 0:1], route[:, 1:2], route[:, 2:3], route[:, 3:4]
    d1 = _lane_pick(lane, e1, pad_start) + r1
    d2 = _lane_pick(lane, e2, pad_start) + r2
    dest_ref[...] = jnp.where(lane == 0, d1, jnp.where(lane == 1, d2, 0.0)).astype(I32)

    lane_b = _iota((n_blocks_pad, LANES), 1)
    first_row = (_iota((n_blocks_pad, LANES), 0) * MOE_ROWS).astype(F32)
    done = ((pad_end[0:1, :] <= first_row) & (lane_b < N_EXPERTS)).astype(F32)
    blk_e = jnp.minimum(jnp.sum(done, axis=-1, keepdims=True), float(N_EXPERTS - 1))
    n_valid = pad_end[0:1, N_EXPERTS - 1:N_EXPERTS] * (1.0 / MOE_ROWS)
    on_diag = lane_b == _iota((n_blocks_pad, LANES), 0)
    seg_fill = jnp.sum(jnp.where(on_diag, pad_start + cnt[0:1, :], 0.0), axis=-1, keepdims=True)
    seg_end = jnp.sum(jnp.where(on_diag, pad_end[0:1, :], 0.0), axis=-1, keepdims=True)
    blk_ref[...] = jnp.where(lane_b == 0, blk_e, jnp.where(lane_b == 1, n_valid, jnp.where(
        lane_b == 2, seg_fill, jnp.where(lane_b == 3, seg_end, 0.0)))).astype(I32)


def _dest(route, counts, n_blocks_pad, *, tm=512):
    t = route.shape[0]
    assert t % tm == 0
    return pl.pallas_call(
        functools.partial(_dest_body, tm=tm, n_blocks_pad=n_blocks_pad),
        grid=(t // tm,),
        in_specs=[pl.BlockSpec((tm, LANES), lambda i: (i, 0)),
                  pl.BlockSpec((8, LANES), lambda i: (0, 0))],
        out_specs=[pl.BlockSpec((tm, LANES), lambda i: (i, 0)),
                   pl.BlockSpec((n_blocks_pad, LANES), lambda i: (0, 0))],
        out_shape=[jax.ShapeDtypeStruct((t, LANES), I32),
                   jax.ShapeDtypeStruct((n_blocks_pad, LANES), I32)],
        compiler_params=_cparams(("arbitrary",)),
        name="moe_dest",
    )(route, counts)


def _invert_body(dest_ref, fill_ref, end_ref, tok_ref, *, n_assign, n_rows):
    def fill8(g, c):
        for r in range(8):
            tok_ref[g * 8 + r] = 0
        return c

    def fill_expert(e, c):
        return lax.fori_loop(fill_ref[e] >> 3, end_ref[e] >> 3, fill8, c)

    lax.fori_loop(0, N_EXPERTS, fill_expert, 0)
    lax.fori_loop(end_ref[N_EXPERTS - 1] >> 3, n_rows // 8, fill8, 0)

    def scatter(a, c):
        tok_ref[dest_ref[a]] = a >> 1
        return c

    lax.fori_loop(0, n_assign, scatter, 0, unroll=8)


def _invert(dest_flat, seg_fill, seg_end, n_rows):
    n_assign = dest_flat.shape[0]
    return pl.pallas_call(
        functools.partial(_invert_body, n_assign=n_assign, n_rows=n_rows),
        grid_spec=pltpu.PrefetchScalarGridSpec(
            num_scalar_prefetch=3,
            grid=(1,),
            in_specs=[],
            out_specs=pl.BlockSpec(memory_space=pltpu.SMEM),
        ),
        out_shape=jax.ShapeDtypeStruct((n_rows,), I32),
        compiler_params=_cparams(("arbitrary",)),
        name="moe_invert",
    )(dest_flat, seg_fill, seg_end)


def _expert_body(be_ref, nv_ref, tok_ref, hp_ref, w1_ref, w3_ref, w2_ref, y_ref, xbuf_ref, sem, wst1_ref, wst3_ref,
                 wst2_ref, wsem, w1b_ref, w3b_ref, w2b_ref, *, layer):
    b = pl.program_id(0)
    n_blocks = pl.num_programs(0)
    wst_refs = (wst1_ref, wst3_ref, wst2_ref)
    n_valid = nv_ref[0]
    valid = b < n_valid
    tr = TOKEN_TILE_ROWS
    expert = be_ref[b]
    first_of_expert = valid & ((b == 0) | (expert != be_ref[jnp.maximum(b - 1, 0)]))

    def weight_copies(e):
        return [pltpu.make_async_copy(w_ref.at[layer, e], st_ref, wsem.at[i])
                for i, (w_ref, st_ref) in enumerate(zip((w1_ref, w3_ref, w2_ref), wst_refs))]

    @pl.when((b == 0) & valid)
    def _():
        for c in weight_copies(expert):
            c.start()

    def gather_rows(blk, slot):
        def one(j, c):
            tok = tok_ref[blk * MOE_ROWS + j]
            pltpu.make_async_copy(hp_ref.at[pl.ds(tok * tr, tr)], xbuf_ref.at[slot, pl.ds(j * tr, tr)],
                                  sem.at[slot]).start()
            return c

        lax.fori_loop(0, MOE_ROWS, one, 0, unroll=8)

    @pl.when((b == 0) & valid)
    def _():
        gather_rows(0, 0)

    @pl.when(b + 1 < n_valid)
    def _():
        gather_rows(b + 1, (b + 1) % 2)

    @pl.when(first_of_expert)
    def _():
        for c in weight_copies(expert):
            c.wait()
        for wb_ref, st_ref in zip((w1b_ref, w3b_ref, w2b_ref), wst_refs):
            wb_ref[...] = st_ref[...].astype(BF16)
        nxt = lax.while_loop(lambda j: (j < n_valid) & (be_ref[jnp.minimum(j, n_blocks - 1)] == expert),
                             lambda j: j + 1, b + 1)

        @pl.when(nxt < n_valid)
        def _():
            for c in weight_copies(be_ref[jnp.minimum(nxt, n_blocks - 1)]):
                c.start()

    @pl.when(jnp.logical_not(valid))
    def _():
        y_ref[...] = jnp.zeros_like(y_ref)

    @pl.when(valid)
    def _():
        slot = b % 2
        pltpu.make_async_copy(hp_ref.at[pl.ds(0, MOE_ROWS * tr)], xbuf_ref.at[slot], sem.at[slot]).wait()
        lo, hi = _load_token_tiles(xbuf_ref.at[slot], MOE_ROWS)
        x = jnp.concatenate(lo + hi, axis=1).astype(BF16)
        a = (_silu(_dot(x, w1b_ref[...])) * _dot(x, w3b_ref[...])).astype(BF16)
        _store_token_tiles(y_ref, _dot(a, w2b_ref[...]))


def _experts(block_expert, n_valid, tok_rows, h_packed, w1, w3, w2, *, layer):
    n_blocks = block_expert.shape[0]
    _, _, d, f = w1.shape
    anywhere = pl.BlockSpec(memory_space=pl.ANY)
    return pl.pallas_call(
        functools.partial(_expert_body, layer=layer),
        grid_spec=pltpu.PrefetchScalarGridSpec(
            num_scalar_prefetch=3,
            grid=(n_blocks,),
            in_specs=[anywhere, anywhere, anywhere, anywhere],
            out_specs=pl.BlockSpec((MOE_ROWS * TOKEN_TILE_ROWS, LANES), lambda b, be, nv, tk: (b, 0)),
            scratch_shapes=[pltpu.VMEM((2, MOE_ROWS * TOKEN_TILE_ROWS, LANES), U32), pltpu.SemaphoreType.DMA((2,)),
                            pltpu.VMEM((d, f), F32), pltpu.VMEM((d, f), F32), pltpu.VMEM((f, d), F32),
                            pltpu.SemaphoreType.DMA((3,)),
                            pltpu.VMEM((d, f), BF16), pltpu.VMEM((d, f), BF16), pltpu.VMEM((f, d), BF16)],
        ),
        out_shape=jax.ShapeDtypeStruct((n_blocks * MOE_ROWS * TOKEN_TILE_ROWS, LANES), U32),
        compiler_params=_cparams(("arbitrary",)),
        name="moe_experts",
    )(block_expert, n_valid, tok_rows, h_packed, w1, w3, w2)


def _combine_body(dest_ref, y_ref, x_ref, route_ref, gate_ref, fg_ref, o_ref, buf_ref, sem, *, tm, final_norm):
    i = pl.program_id(0)
    yr = TOKEN_TILE_ROWS

    def gather_rows(blk, slot):
        def one(j, c):
            for k in range(2):
                d = dest_ref[(blk * tm + j) * 2 + k]
                pltpu.make_async_copy(y_ref.at[pl.ds(d * yr, yr)], buf_ref.at[slot, k, pl.ds(j * yr, yr)],
                                      sem.at[slot, k]).start()
            return c

        lax.fori_loop(0, tm, one, 0, unroll=4)

    @pl.when(i == 0)
    def _():
        gather_rows(0, 0)

    @pl.when(i + 1 < pl.num_programs(0))
    def _():
        gather_rows(i + 1, (i + 1) % 2)

    slot = i % 2
    for k in range(2):
        pltpu.make_async_copy(y_ref.at[pl.ds(0, tm * yr)], buf_ref.at[slot, k], sem.at[slot, k]).wait()

    route = route_ref[...]
    g1, g2 = route[:, 4:5], route[:, 5:6]
    lo1, hi1 = _load_token_tiles(buf_ref.at[slot, 0], tm)
    lo2, hi2 = _load_token_tiles(buf_ref.at[slot, 1], tm)
    half = x_ref.shape[1] // 2
    for s in range(yr):
        for base, y1, y2 in ((0, lo1[s], lo2[s]), (half, hi1[s], hi2[s])):
            cs = slice(base + s * LANES, base + (s + 1) * LANES)
            o_ref[:, cs] = x_ref[:, cs] + gate_ref[:, cs] * (g1 * y1 + g2 * y2)
    if final_norm:
        xo = o_ref[...]
        o_ref[...] = xo * lax.rsqrt(jnp.mean(xo * xo, axis=-1, keepdims=True) + EPS) * fg_ref[...]


def _combine(dest_flat, y_rows, x, route, gate, final_g, *, final_norm, tm=128):
    t, d = x.shape
    return pl.pallas_call(
        functools.partial(_combine_body, tm=tm, final_norm=final_norm),
        grid_spec=pltpu.PrefetchScalarGridSpec(
            num_scalar_prefetch=1,
            grid=(t // tm,),
            in_specs=[pl.BlockSpec(memory_space=pl.ANY),
                      pl.BlockSpec((tm, d), lambda i, dr: (i, 0)),
                      pl.BlockSpec((tm, LANES), lambda i, dr: (i, 0)),
                      pl.BlockSpec((1, d), lambda i, dr: (0, 0)),
                      pl.BlockSpec((1, d), lambda i, dr: (0, 0))],
            out_specs=pl.BlockSpec((tm, d), lambda i, dr: (i, 0)),
            scratch_shapes=[pltpu.VMEM((2, 2, tm * TOKEN_TILE_ROWS, LANES), U32), pltpu.SemaphoreType.DMA((2, 2))],
        ),
        out_shape=jax.ShapeDtypeStruct((t, d), F32),
        compiler_params=_cparams(("arbitrary",)),
        name="moe_combine",
    )(dest_flat, y_rows, x, route, gate, final_g)


def _moe(x, g, scale, shift, gate, w_group, b_group, w_expert, b_expert, w1, w3, w2, final_g, *, layer, final_norm):
    t, d = x.shape
    pad = LANES - N_GROUPS - N_EXPERTS
    w_route = jnp.concatenate([w_group, w_expert, jnp.zeros((d, pad), F32)], axis=1)
    b_route = jnp.concatenate([b_group, b_expert, jnp.zeros((pad,), F32)]).reshape(1, LANES)
    h_packed, route, counts = _router(x, g, scale, shift, w_route, b_route)
    n_assign = 2 * t
    n_blocks = -(-(n_assign + N_EXPERTS * (MOE_ROWS - 1)) // MOE_ROWS)
    n_blocks_pad = -(-n_blocks // 8) * 8
    dest, blk = _dest(route, counts, n_blocks_pad)
    dest_flat = dest[:, :2].reshape(n_assign)
    block_expert = blk[:n_blocks, 0]
    n_valid = blk[0, 1:2]
    tok_rows = _invert(dest_flat, blk[:N_EXPERTS, 2], blk[:N_EXPERTS, 3], n_blocks * MOE_ROWS)
    y_rows = _experts(block_expert, n_valid, tok_rows, h_packed, w1, w3, w2, layer=layer)
    return _combine(dest_flat, y_rows, x, route, gate, final_g, final_norm=final_norm)


def _pad_rows(w, rows):
    return jnp.pad(w, ((0, rows - w.shape[0]), (0, 0)))


def _lane_row(v, offset):
    return jnp.zeros((LANES,), F32).at[offset:offset + v.shape[0]].set(v)


def kernel(x, c, w_mod, b_mod, norm_g, final_norm_g, e_w_in, e_conv_qkv, e_a_log, e_dt_bias, e_head_norm_g, e_conf_dw, e_conf_dw_b, e_conf_ln_g, e_conf_ln_b, e_w_out, o_w_in, o_conv_w, o_conv_b, o_dt_bias, o_a_log, o_d_skip, o_norm_g, o_w_out, moe_w_group, moe_b_group, moe_w_expert, moe_b_expert, moe_w1, moe_w3, moe_w2):
    bsz, t, d = x.shape
    assert bsz == 1 and c.shape == (1, d)
    xt = x[0]
    mod = _modulation(c[0], w_mod, b_mod)

    def mod_parts(l):
        return mod[l, :, :d], mod[l, :, d:2 * d], mod[l, :, 2 * d:]

    row = lambda v: v.reshape(1, -1)
    qkv_w = 3 * GDN_HEADS * HEAD_W
    z_end = qkv_w + GDN_HEADS * HEAD_W

    shift, scale, gate = mod_parts(0)
    w_in = e_w_in[0].T
    tn = 512
    w_glu = w_in[z_end + 2 * GDN_HEADS:]
    w_ba = _pad_rows(w_in[z_end:z_end + 2 * GDN_HEADS], LANES)
    proj0, ba = _norm_matmul(xt, row(norm_g[0, 0]), scale, shift,
                             [(w_in, z_end // tn), (w_glu, w_glu.shape[0] // tn)], w_ba, tn=tn)
    gdn_par = jnp.zeros((8, LANES), F32).at[0].set(_lane_row(e_a_log[0], GDN_HEADS)).at[1].set(
        _lane_row(e_dt_bias[0], GDN_HEADS))
    a_out = _gdn(proj0, ba, e_conv_qkv[0], gdn_par, row(e_head_norm_g[0]))
    b_out = _conformer(proj0, e_conf_dw[0], row(e_conf_dw_b[0]), row(e_conf_ln_g[0]), row(e_conf_ln_b[0]))
    xt = _out_proj0(a_out, b_out, e_w_out[0], xt, gate)
    shift, scale, gate = mod_parts(1)
    xt = _moe(xt, row(norm_g[0, 1]), scale, shift, gate, moe_w_group[0], moe_b_group[0], moe_w_expert[0],
              moe_b_expert[0], moe_w1, moe_w3, moe_w2, row(final_norm_g), layer=0, final_norm=False)

    shift, scale, gate = mod_parts(2)
    w_in = o_w_in[0].T
    n_main = 2 * SSM_GROUPS * SSM_GROUP_W + 2 * SSM_GROUPS * SSM_STATE
    tn = 1024
    proj1, dt_raw = _norm_matmul(xt, row(norm_g[1, 0]), scale, shift, [(w_in, n_main // tn)],
                                 _pad_rows(w_in[n_main:], LANES), tn=tn)
    ssd_par = jnp.zeros((8, LANES), F32).at[0].set(_lane_row(o_dt_bias[0], 0)).at[1].set(
        _lane_row(o_a_log[0], 0)).at[2].set(_lane_row(o_d_skip[0], 0))
    y = _ssd(proj1, dt_raw, o_conv_w[0], row(o_conv_b[0]), ssd_par)
    xt = _out_proj1(y, row(o_norm_g[0]), o_w_out[0], xt, gate)
    shift, scale, gate = mod_parts(3)
    xt = _moe(xt, row(norm_g[1, 1]), scale, shift, gate, moe_w_group[1], moe_b_group[1], moe_w_expert[1],
              moe_b_expert[1], moe_w1, moe_w3, moe_w2, row(final_norm_g), layer=1, final_norm=True)
    return xt[None]
```

```python
import functools

import jax
import jax.numpy as jnp
from jax import lax
from jax.experimental import pallas as pl
from jax.experimental.pallas import tpu as pltpu

F32 = jnp.float32
BF16 = jnp.bfloat16
I32 = jnp.int32
U32 = jnp.uint32

EPS = 1e-6
LANES = 128
CHUNK = 64
CHUNK_LOG2 = 6
GDN_HEADS = 8
HEAD_W = 128
CONF_CH = 1024
CONF_K = 31
SSM_GROUPS = 8
SSM_GROUP_W = 512
SSM_HEADDIM = 64
SSM_STATE = 128
N_EXPERTS = 32
EXPERTS_PER_GROUP = 8
N_GROUPS = 4
MOE_ROWS = 256
GATHER_AHEAD = 2
TOKEN_TILE_ROWS = 8
VMEM_LIMIT = 56 * 1024 * 1024
PROLOGUE_ROWS = 128
HALO_ROWS = 16
CONF_HALO_ROWS = 32

_HI = lax.Precision.HIGHEST


def _cparams(sem):
    return pltpu.CompilerParams(dimension_semantics=sem, vmem_limit_bytes=VMEM_LIMIT)


def _dot(a, b, precision=None):
    return jnp.dot(a, b, precision=precision, preferred_element_type=F32)


def _dot_nt(a, b, precision=None):
    return lax.dot_general(a, b, (((1,), (1,)), ((), ())), precision=precision, preferred_element_type=F32)


def _dot_tn(a, b, precision=None):
    return lax.dot_general(a, b, (((0,), (0,)), ((), ())), precision=precision, preferred_element_type=F32)


def _split_bf16(x):
    hi = x.astype(BF16)
    return hi, (x - hi.astype(F32)).astype(BF16)


def _dot3(a, b):
    ah, al = _split_bf16(a)
    bh, bl = _split_bf16(b)
    return _dot(ah, bh) + (_dot(ah, bl) + _dot(al, bh))


def _select_dot(sel, x, left=False):
    p1 = x.astype(BF16)
    r1 = x - p1.astype(F32)
    p2 = r1.astype(BF16)
    p3 = (r1 - p2.astype(F32)).astype(BF16)
    if left:
        return _dot(sel, p1) + (_dot(sel, p2) + _dot(sel, p3))
    return _dot(p1, sel) + (_dot(p2, sel) + _dot(p3, sel))


def _silu(x):
    return x * jax.nn.sigmoid(x)


def _softplus(x):
    return jnp.maximum(x, 0.0) + jnp.log1p(jnp.exp(-jnp.abs(x)))


def _iota(shape, dim):
    return lax.broadcasted_iota(I32, shape, dim)


def _mod_body(c_ref, w_ref, b_ref, o_ref):
    c = c_ref[...]
    o_ref[...] = jnp.sum(w_ref[...] * _silu(c), axis=0, keepdims=True) + b_ref[...]


def _modulation(c, w_mod, b_mod):
    nl, d, n = w_mod.shape
    tn = 768
    return pl.pallas_call(
        _mod_body,
        grid=(nl, n // tn),
        in_specs=[
            pl.BlockSpec((d, 1), lambda l, j: (0, 0)),
            pl.BlockSpec((None, d, tn), lambda l, j: (l, 0, j)),
            pl.BlockSpec((None, 1, tn), lambda l, j: (l, 0, j)),
        ],
        out_specs=pl.BlockSpec((None, 1, tn), lambda l, j: (l, 0, j)),
        out_shape=jax.ShapeDtypeStruct((nl, 1, n), F32),
        compiler_params=_cparams(("arbitrary", "arbitrary")),
        name="adaln_mod",
    )(c.reshape(d, 1), w_mod, b_mod.reshape(nl, 1, n))


def _adaln_rows(x, g, scale, shift):
    r = lax.rsqrt(jnp.mean(x * x, axis=-1, keepdims=True) + EPS)
    return x * r * (g * (1.0 + scale)) + shift


def _norm_mm_body(*refs, starts):
    x_ref, g_ref, sc_ref, sh_ref, ws_ref = refs[:5]
    w_refs = refs[5:5 + len(starts)]
    o_ref, os_ref, h_ref = refs[5 + len(starts):]
    j = pl.program_id(1)

    @pl.when(j == 0)
    def _():
        ws = ws_ref[...].astype(BF16)

        def norm_rows(c, carry):
            rs = pl.ds(pl.multiple_of(c * PROLOGUE_ROWS, PROLOGUE_ROWS), PROLOGUE_ROWS)
            h = _adaln_rows(x_ref[rs, :], g_ref[...], sc_ref[...], sh_ref[...]).astype(BF16)
            h_ref[rs, :] = h
            os_ref[rs, :] = _dot_nt(h, ws)
            return carry

        lax.fori_loop(0, x_ref.shape[0] // PROLOGUE_ROWS, norm_rows, 0)

    bounds = list(starts[1:]) + [None]
    for w_ref, lo, hi in zip(w_refs, starts, bounds):
        in_seg = (j >= lo) if hi is None else ((j >= lo) & (j < hi))

        @pl.when(in_seg)
        def _(w_ref=w_ref):
            o_ref[...] = _dot_nt(h_ref[...], w_ref[...].astype(BF16)).astype(o_ref.dtype)


def _norm_matmul(x, g, scale, shift, segments, w_small, *, tm=1024, tn=512):
    t, d = x.shape
    tm = min(tm, t)
    starts, specs, n_main = [], [], 0
    for w, nb in segments:
        lo = n_main // tn
        starts.append(lo)
        specs.append(pl.BlockSpec((tn, d), lambda i, j, lo=lo, nb=nb: (jnp.clip(j - lo, 0, nb - 1), 0)))
        n_main += nb * tn
    vec = pl.BlockSpec((1, d), lambda i, j: (0, 0))
    return pl.pallas_call(
        functools.partial(_norm_mm_body, starts=tuple(starts)),
        grid=(t // tm, n_main // tn),
        in_specs=[pl.BlockSpec((tm, d), lambda i, j: (i, 0)), vec, vec, vec,
                  pl.BlockSpec((LANES, d), lambda i, j: (0, 0))] + specs,
        out_specs=[pl.BlockSpec((tm, tn), lambda i, j: (i, j)),
                   pl.BlockSpec((tm, LANES), lambda i, j: (i, 0))],
        out_shape=[jax.ShapeDtypeStruct((t, n_main), BF16), jax.ShapeDtypeStruct((t, LANES), F32)],
        scratch_shapes=[pltpu.VMEM((tm, d), BF16)],
        compiler_params=_cparams(("arbitrary", "arbitrary")),
        name="adaln_in_proj",
    )(x, g, scale, shift, w_small, *[w for w, _ in segments])


def _causal_conv(halo, x, w, taps):
    hr = halo.shape[0]
    n = x.shape[0]
    xe = jnp.concatenate([halo, x], axis=0)
    rolled = {0: xe}
    acc = None
    for j in range(taps):
        s = taps - 1 - j
        a, b = divmod(s, 8)
        if b not in rolled:
            rolled[b] = pltpu.roll(xe, b, axis=0)
        term = rolled[b][hr - 8 * a:hr - 8 * a + n] * w[j:j + 1]
        acc = term if acc is None else acc + term
    return acc


def _short_conv_bf16(halo, x16, w):
    n = x16.shape[0]
    x = x16.astype(F32)
    lag = _iota((n, n), 0) - _iota((n, n), 1)
    shifts = jnp.concatenate([(lag == s).astype(BF16) for s in (1, 2, 3)], axis=0)
    moved = _dot(shifts, x16)
    acc = x * w[3:4] + moved[0:n] * w[2:3] + moved[n:2 * n] * w[1:2] + moved[2 * n:3 * n] * w[0:1]
    head = _causal_conv(halo, x[0:8], w, 4)
    return jnp.concatenate([head, acc[8:]], axis=0)


def _unit_lower_inverses(a_mats, row, col):
    n = a_mats[0].shape[0]
    eye = (row == col).astype(F32)
    level = ((row >> 1) == (col >> 1)) & (col < row)
    invs = [eye - jnp.where(level, a, 0.0) for a in a_mats]
    k = 1
    while (1 << k) < n:
        level = ((row >> (k + 1)) == (col >> (k + 1))) & (((row >> k) & 1) == 1) & (((col >> k) & 1) == 0)
        inv16 = [inv.astype(BF16) for inv in invs]
        ys = [_dot(i16, jnp.where(level, a, 0.0).astype(BF16)) for i16, a in zip(inv16, a_mats)]
        invs = [inv - _dot(y.astype(BF16), i16) for inv, y, i16 in zip(invs, ys, inv16)]
        k += 1
    return invs


def _gdn_body(q_ref, k_ref, v_ref, qh_ref, kh_ref, vh_ref, z_ref, ba_ref, cwq_ref, cwk_ref, cwv_ref,
              par_ref, hg_ref, o_ref, s_ref, *, hb, rows):
    hblk = pl.program_id(0)
    ir = pl.program_id(1)

    @pl.when(ir == 0)
    def _():
        s_ref[...] = jnp.zeros_like(s_ref)

    def conv_silu(x_ref, halo_ref, w_ref):
        halo = jnp.where(ir > 0, halo_ref[...].astype(F32), 0.0)
        return _silu(_short_conv_bf16(halo, x_ref[...], w_ref[...]))

    qc = conv_silu(q_ref, qh_ref, cwq_ref)
    kc = conv_silu(k_ref, kh_ref, cwk_ref)
    vc = conv_silu(v_ref, vh_ref, cwv_ref)

    ba = ba_ref[...]
    beta_all = jax.nn.sigmoid(ba)
    g_all = -jnp.exp(par_ref[0:1, :]) * _softplus(ba + par_ref[1:2, :])
    rr = _iota((rows, rows), 0)
    cc = _iota((rows, rows), 1)
    chunk_tril = ((rr >> CHUNK_LOG2) == (cc >> CHUNK_LOG2)) & (cc <= rr)
    gcs_all = _select_dot(chunk_tril.astype(BF16), g_all, left=True)
    gcs_t = gcs_all.T

    row = _iota((CHUNK, CHUNK), 0)
    col = _iota((CHUNK, CHUNK), 1)
    scale = HEAD_W ** -0.5
    hg = hg_ref[...]
    bf = lambda a: a.astype(BF16)
    lane = _iota((rows, LANES), 1)
    sub = _iota((LANES, rows), 0)
    nc = rows // CHUNK

    heads = []
    for h in range(hb):
        hs = slice(h * HEAD_W, (h + 1) * HEAD_W)
        head = hblk * hb + h
        q = qc[:, hs]
        k = kc[:, hs]
        heads.append(dict(
            q=q * lax.rsqrt(jnp.sum(q * q, axis=-1, keepdims=True) + EPS) * scale,
            k=k * lax.rsqrt(jnp.sum(k * k, axis=-1, keepdims=True) + EPS),
            v=vc[:, hs],
            beta=jnp.sum(jnp.where(lane == head, beta_all, 0.0), axis=-1, keepdims=True),
            gcs=jnp.sum(jnp.where(lane == head + GDN_HEADS, gcs_all, 0.0), axis=-1, keepdims=True),
            gcs_row=jnp.sum(jnp.where(sub == head + GDN_HEADS, gcs_t, 0.0), axis=0, keepdims=True)))

    items = []
    for c in range(nc):
        rs = slice(c * CHUNK, (c + 1) * CHUNK)
        for hd in heads:
            qq, kk, vv, beta, gcs = hd["q"][rs], hd["k"][rs], hd["v"][rs], hd["beta"][rs], hd["gcs"][rs]
            decay = jnp.where(col <= row, jnp.exp(jnp.minimum(gcs - hd["gcs_row"][:, rs], 0.0)), 0.0)
            kb = kk * beta
            eg = jnp.exp(gcs)
            g_last = gcs[CHUNK - 1:CHUNK]
            items.append(dict(decay=decay, kb16=bf(kb), k16=bf(kk), q16=bf(qq),
                              rhs16=bf(jnp.concatenate([vv * beta, kb * eg], axis=-1)),
                              q_dec16=bf(qq * eg), k_dec16=bf(kk * jnp.exp(g_last - gcs)), cd=jnp.exp(g_last)))
    a_mats = [jnp.where(col < row, _dot_nt(it["kb16"], it["k16"]) * it["decay"], 0.0) for it in items]
    qks = [bf(_dot_nt(it["q16"], it["k16"]) * it["decay"]) for it in items]
    t_invs = _unit_lower_inverses(a_mats, row, col)
    sols = [_dot(bf(t), it["rhs16"]) for t, it in zip(t_invs, items)]

    states = [s_ref[h] for h in range(hb)]
    for c in range(nc):
        rs = slice(c * CHUNK, (c + 1) * CHUNK)
        cur = slice(c * hb, (c + 1) * hb)
        s16 = [bf(s) for s in states]
        us = [so[:, :HEAD_W] - _dot(bf(so[:, HEAD_W:]), s) for so, s in zip(sols[cur], s16)]
        outs = [_dot(it["q_dec16"], s) + _dot(qk, bf(u)) for it, s, qk, u in zip(items[cur], s16, qks[cur], us)]
        states = [s * it["cd"] + _dot_tn(it["k_dec16"], bf(u)) for s, it, u in zip(states, items[cur], us)]
        for h, o in enumerate(outs):
            hs = slice(h * HEAD_W, (h + 1) * HEAD_W)
            o = o * lax.rsqrt(jnp.mean(o * o, axis=-1, keepdims=True) + EPS) * hg * _silu(z_ref[rs, hs].astype(F32))
            o_ref[rs, hs] = o.astype(o_ref.dtype)
    for h in range(hb):
        s_ref[h] = states[h]


def _gdn(proj, ba, conv_w, par, head_g, *, hb=GDN_HEADS, rows=256):
    t = proj.shape[0]
    rows = min(rows, t)
    w = hb * HEAD_W
    per = (GDN_HEADS * HEAD_W) // w

    def sec(k):
        return pl.BlockSpec((rows, w), lambda h, i, k=k: (i, k * per + h))

    def halo(k):
        return pl.BlockSpec((HALO_ROWS, w),
                            lambda h, i, k=k: (jnp.maximum(i * (rows // HALO_ROWS) - 1, 0), k * per + h))

    def cw(k):
        return pl.BlockSpec((4, w), lambda h, i, k=k: (0, k * per + h))

    return pl.pallas_call(
        functools.partial(_gdn_body, hb=hb, rows=rows),
        grid=(GDN_HEADS // hb, t // rows),
        in_specs=[sec(0), sec(1), sec(2), halo(0), halo(1), halo(2), sec(3),
                  pl.BlockSpec((rows, LANES), lambda h, i: (i, 0)),
                  cw(0), cw(1), cw(2),
                  pl.BlockSpec((8, LANES), lambda h, i: (0, 0)),
                  pl.BlockSpec((1, HEAD_W), lambda h, i: (0, 0))],
        out_specs=pl.BlockSpec((rows, w), lambda h, i: (i, h)),
        out_shape=jax.ShapeDtypeStruct((t, GDN_HEADS * HEAD_W), BF16),
        scratch_shapes=[pltpu.VMEM((hb, HEAD_W, HEAD_W), F32)],
        compiler_params=_cparams(("arbitrary", "arbitrary")),
        name="gated_deltanet",
    )(proj, proj, proj, proj, proj, proj, proj, ba, conv_w, conv_w, conv_w, par, head_g)


def _conf_body(x_ref, halo_ref, w_ref, b_ref, g_ref, lb_ref, o_ref):
    ir = pl.program_id(0)

    def glu(v):
        v = v.astype(F32)
        return v[:, :CONF_CH] * jax.nn.sigmoid(v[:, CONF_CH:])

    halo = jnp.where(ir > 0, glu(halo_ref[...]), 0.0)
    u = _causal_conv(halo, glu(x_ref[...]), w_ref[...], CONF_K) + b_ref[...]
    mu = jnp.mean(u, axis=-1, keepdims=True)
    uc = u - mu
    var = jnp.mean(uc * uc, axis=-1, keepdims=True)
    y = uc * lax.rsqrt(var + EPS) * g_ref[...] + lb_ref[...]
    o_ref[...] = _silu(y).astype(o_ref.dtype)


def _conformer(proj, dw, dw_b, ln_g, ln_b, *, rows=256):
    t = proj.shape[0]
    rows = min(rows, t)
    glu_blk = (4 * GDN_HEADS * HEAD_W) // (2 * CONF_CH)
    halo_rows = CONF_HALO_ROWS
    vec = pl.BlockSpec((1, CONF_CH), lambda i: (0, 0))
    return pl.pallas_call(
        _conf_body,
        grid=(t // rows,),
        in_specs=[pl.BlockSpec((rows, 2 * CONF_CH), lambda i: (i, glu_blk)),
                  pl.BlockSpec((halo_rows, 2 * CONF_CH),
                               lambda i: (jnp.maximum(i * (rows // halo_rows) - 1, 0), glu_blk)),
                  pl.BlockSpec((CONF_K, CONF_CH), lambda i: (0, 0)), vec, vec, vec],
        out_specs=pl.BlockSpec((rows, CONF_CH), lambda i: (i, 0)),
        out_shape=jax.ShapeDtypeStruct((t, CONF_CH), BF16),
        compiler_params=_cparams(("arbitrary",)),
        name="conformer_conv",
    )(proj, proj, dw, dw_b, ln_g, ln_b)


def _out0_body(a_ref, b_ref, wa_ref, wb_ref, x_ref, gate_ref, o_ref):
    mix = _dot(a_ref[...], wa_ref[...].astype(BF16)) + _dot(b_ref[...], wb_ref[...].astype(BF16))
    o_ref[...] = x_ref[...] + gate_ref[...] * mix


def _out_proj0(a, b, w, x, gate, *, tm=1024, tn=1024):
    t, d = x.shape
    tm = min(tm, t)
    ka, kb = a.shape[1], b.shape[1]
    return pl.pallas_call(
        _out0_body,
        grid=(t // tm, d // tn),
        in_specs=[pl.BlockSpec((tm, ka), lambda i, j: (i, 0)),
                  pl.BlockSpec((tm, kb), lambda i, j: (i, 0)),
                  pl.BlockSpec((ka, tn), lambda i, j: (0, j)),
                  pl.BlockSpec((kb, tn), lambda i, j: (ka // kb, j)),
                  pl.BlockSpec((tm, tn), lambda i, j: (i, j)),
                  pl.BlockSpec((1, tn), lambda i, j: (0, j))],
        out_specs=pl.BlockSpec((tm, tn), lambda i, j: (i, j)),
        out_shape=jax.ShapeDtypeStruct((t, d), F32),
        compiler_params=_cparams(("arbitrary", "arbitrary")),
        name="out_proj_even",
    )(a, b, w, w, x, gate)


def _out1_body(y_ref, g_ref, w_ref, x_ref, gate_ref, o_ref, h_ref):
    @pl.when(pl.program_id(1) == 0)
    def _():
        def norm_rows(c, carry):
            rs = pl.ds(pl.multiple_of(c * PROLOGUE_ROWS, PROLOGUE_ROWS), PROLOGUE_ROWS)
            y = y_ref[rs, :].astype(F32)
            r = lax.rsqrt(jnp.mean(y * y, axis=-1, keepdims=True) + EPS)
            h_ref[rs, :] = (y * r * g_ref[...]).astype(BF16)
            return carry

        lax.fori_loop(0, y_ref.shape[0] // PROLOGUE_ROWS, norm_rows, 0)

    o_ref[...] = x_ref[...] + gate_ref[...] * _dot(h_ref[...], w_ref[...].astype(BF16))


def _out_proj1(y, norm_g, w, x, gate, *, tm=1024, tn=512):
    t, d = x.shape
    tm = min(tm, t)
    k = y.shape[1]
    return pl.pallas_call(
        _out1_body,
        grid=(t // tm, d // tn),
        in_specs=[pl.BlockSpec((tm, k), lambda i, j: (i, 0)),
                  pl.BlockSpec((1, k), lambda i, j: (0, 0)),
                  pl.BlockSpec((k, tn), lambda i, j: (0, j)),
                  pl.BlockSpec((tm, tn), lambda i, j: (i, j)),
                  pl.BlockSpec((1, tn), lambda i, j: (0, j))],
        out_specs=pl.BlockSpec((tm, tn), lambda i, j: (i, j)),
        out_shape=jax.ShapeDtypeStruct((t, d), F32),
        scratch_shapes=[pltpu.VMEM((tm, k), BF16)],
        compiler_params=_cparams(("arbitrary", "arbitrary")),
        name="out_proj_odd",
    )(y, norm_g, w, x, gate)


def _ssd_body(x_ref, b_ref, c_ref, xh_ref, bh_ref, ch_ref, z_ref, dt_ref, cwx_ref, cwb_ref, cwc_ref,
              cbx_ref, cbb_ref, cbc_ref, par_ref, o_ref, s_ref, at_ref, *, rows, gps):
    ir = pl.program_id(1)
    gw = SSM_GROUP_W
    hpg = gw // SSM_HEADDIM

    @pl.when(ir == 0)
    def _():
        s_ref[...] = jnp.zeros_like(s_ref)

    def conv_silu(x_r, halo_r, w_r, bias_r):
        halo = jnp.where(ir > 0, halo_r[...].astype(F32), 0.0)
        return _silu(_causal_conv(halo, x_r[...].astype(F32), w_r[...], 4) + bias_r[...])

    xs_all = conv_silu(x_ref, xh_ref, cwx_ref, cbx_ref)
    bm_all = conv_silu(b_ref, bh_ref, cwb_ref, cbb_ref)
    cm_all = conv_silu(c_ref, ch_ref, cwc_ref, cbc_ref)

    dt = _softplus(dt_ref[...] + par_ref[0:1, :])
    da = dt * (-jnp.exp(par_ref[1:2, :]))
    rr = _iota((rows, rows), 0)
    cc = _iota((rows, rows), 1)
    chunk_tril = ((rr >> CHUNK_LOG2) == (cc >> CHUNK_LOG2)) & (cc <= rr)
    acs = _select_dot(chunk_tril.astype(BF16), da, left=True)
    at_ref[...] = acs.T

    causal = (_iota((CHUNK, gw), 1) & (CHUNK - 1)) <= _iota((CHUNK, gw), 0)
    same_head = (_iota((LANES, LANES), 0) >> CHUNK_LOG2) == (_iota((LANES, LANES), 1) >> CHUNK_LOG2)
    bf = lambda a: a.astype(BF16)
    nc = rows // CHUNK
    chunks = [slice(c * CHUNK, (c + 1) * CHUNK) for c in range(nc)]
    tiles = [slice(k * LANES, (k + 1) * LANES) for k in range(gw // LANES)]

    for gi in range(gps):
        grp = pl.program_id(0) * gps + gi
        cols = slice(gi * gw, (gi + 1) * gw)
        xs = xs_all[:, cols]
        bm = bm_all[:, gi * SSM_STATE:(gi + 1) * SSM_STATE]
        cm = cm_all[:, gi * SSM_STATE:(gi + 1) * SSM_STATE]
        acs_rows = at_ref[pl.ds(pl.multiple_of(grp * hpg, hpg), hpg), :]

        sel = (_iota((LANES, gw), 0) == grp * hpg + (_iota((LANES, gw), 1) >> CHUNK_LOG2)).astype(BF16)
        dt_e = _select_dot(sel, dt)
        acs_e = _select_dot(sel, acs)
        dskip_e = _select_dot(sel, par_ref[...])[2:3, :]

        a_es = [acs_e[rs] for rs in chunks]
        a_rows = [jnp.concatenate([acs_rows[h:h + 1, rs] for h in range(hpg)], axis=1) for rs in chunks]
        l_cats = [jnp.where(causal, jnp.exp(jnp.minimum(a_e - a_row, 0.0)), 0.0) for a_e, a_row in zip(a_es, a_rows)]
        c16 = [bf(cm[rs]) for rs in chunks]
        b16 = [bf(bm[rs]) for rs in chunks]
        cbs = [_dot_nt(c_c, jnp.concatenate([b_c] * hpg, axis=0)) for c_c, b_c in zip(c16, b16)]
        xdts = [xs[rs] * dt_e[rs] for rs in chunks]
        m16 = [bf(cb * l_cat) for cb, l_cat in zip(cbs, l_cats)]
        y_diags = []
        for m, xdt in zip(m16, xdts):
            parts = [_dot(m[:, ts], bf(jnp.where(same_head, jnp.concatenate([xdt[:, ts]] * 2, axis=0), 0.0)))
                     for ts in tiles]
            y_diags.append(jnp.concatenate(parts, axis=1))
        a_lasts = [a_e[CHUNK - 1:CHUNK] for a_e in a_es]
        updates = [_dot_tn(b_c, bf(xdt * jnp.exp(a_last - a_e)))
                   for b_c, xdt, a_last, a_e in zip(b16, xdts, a_lasts, a_es)]

        states = [s_ref[gi]]
        for a_last, upd in zip(a_lasts, updates):
            states.append(states[-1] * jnp.exp(a_last) + upd)
        s_ref[gi] = states[-1]

        y_offs = [_dot(c_c, bf(s)) * jnp.exp(a_e) for c_c, s, a_e in zip(c16, states[:-1], a_es)]
        for rs, y_diag, y_off in zip(chunks, y_diags, y_offs):
            y = y_diag + y_off + dskip_e * xs[rs]
            o_ref[rs, cols] = (y * _silu(z_ref[rs, cols].astype(F32))).astype(o_ref.dtype)


def _ssd(proj, dt_raw, conv_w, conv_b, par, *, rows=256, gps=8):
    t = proj.shape[0]
    rows = min(rows, t)
    xw = gps * SSM_GROUP_W
    bw = gps * SSM_STATE
    d_inner = SSM_GROUPS * SSM_GROUP_W
    xo = d_inner // xw
    bo = (2 * d_inner) // bw
    co = bo + SSM_GROUPS * SSM_STATE // bw
    hrow = lambda i: jnp.maximum(i * (rows // HALO_ROWS) - 1, 0)
    wbo = d_inner // bw
    wco = wbo + SSM_GROUPS * SSM_STATE // bw
    return pl.pallas_call(
        functools.partial(_ssd_body, rows=rows, gps=gps),
        grid=(SSM_GROUPS // gps, t // rows),
        in_specs=[pl.BlockSpec((rows, xw), lambda g, i: (i, xo + g)),
                  pl.BlockSpec((rows, bw), lambda g, i: (i, bo + g)),
                  pl.BlockSpec((rows, bw), lambda g, i: (i, co + g)),
                  pl.BlockSpec((HALO_ROWS, xw), lambda g, i: (hrow(i), xo + g)),
                  pl.BlockSpec((HALO_ROWS, bw), lambda g, i: (hrow(i), bo + g)),
                  pl.BlockSpec((HALO_ROWS, bw), lambda g, i: (hrow(i), co + g)),
                  pl.BlockSpec((rows, xw), lambda g, i: (i, g)),
                  pl.BlockSpec((rows, LANES), lambda g, i: (i, 0)),
                  pl.BlockSpec((4, xw), lambda g, i: (0, g)),
                  pl.BlockSpec((4, bw), lambda g, i: (0, wbo + g)),
                  pl.BlockSpec((4, bw), lambda g, i: (0, wco + g)),
                  pl.BlockSpec((1, xw), lambda g, i: (0, g)),
                  pl.BlockSpec((1, bw), lambda g, i: (0, wbo + g)),
                  pl.BlockSpec((1, bw), lambda g, i: (0, wco + g)),
                  pl.BlockSpec((8, LANES), lambda g, i: (0, 0))],
        out_specs=pl.BlockSpec((rows, xw), lambda g, i: (i, g)),
        out_shape=jax.ShapeDtypeStruct((t, d_inner), BF16),
        scratch_shapes=[pltpu.VMEM((gps, SSM_STATE, SSM_GROUP_W), F32), pltpu.VMEM((LANES, rows), F32)],
        compiler_params=_cparams(("arbitrary", "arbitrary")),
        name="ssd",
    )(proj, proj, proj, proj, proj, proj, proj, dt_raw, conv_w, conv_w, conv_w, conv_b, conv_b, conv_b, par)


def _store_token_tiles(tile_ref, x):
    rows, d = x.shape
    bits = lax.bitcast_convert_type(x.astype(BF16).astype(F32), U32)
    packed = (bits[:, :d // 2] >> 16) | (bits[:, d // 2:] & jnp.uint32(0xFFFF0000))
    for s in range(TOKEN_TILE_ROWS):
        tile_ref[pl.ds(s, rows, stride=TOKEN_TILE_ROWS), :] = packed[:, s * LANES:(s + 1) * LANES]


def _load_token_tiles(tile_ref, rows):
    lo, hi = [], []
    for s in range(TOKEN_TILE_ROWS):
        p = tile_ref[pl.ds(s, rows, stride=TOKEN_TILE_ROWS), :]
        lo.append(lax.bitcast_convert_type(p << 16, F32))
        hi.append(lax.bitcast_convert_type(p & jnp.uint32(0xFFFF0000), F32))
    return lo, hi


def _lane_pick(lane, idx, vals):
    return jnp.sum(jnp.where(lane == idx, vals, 0.0), axis=-1, keepdims=True)


def _router_body(x_ref, g_ref, sc_ref, sh_ref, w_ref, b_ref, hp_ref, route_ref, cnt_ref, carry_ref, *, tm):
    i = pl.program_id(0)

    @pl.when(i == 0)
    def _():
        carry_ref[...] = jnp.zeros_like(carry_ref)

    h = _adaln_rows(x_ref[...], g_ref[...], sc_ref[...], sh_ref[...])
    logits = _dot3(h, w_ref[...]) + b_ref[...]
    lane = _iota((tm, LANES), 1).astype(F32)
    neg = -jnp.inf
    gl = jnp.where(lane < N_GROUPS, logits, neg)
    gmax = jnp.max(gl, axis=-1, keepdims=True)
    grp_p = 1.0 / jnp.sum(jnp.exp(gl - gmax), axis=-1, keepdims=True)
    gidx = jnp.min(jnp.where(gl == gmax, lane, float(LANES)), axis=-1, keepdims=True)
    lo = N_GROUPS + EXPERTS_PER_GROUP * gidx
    el = jnp.where((lane >= lo) & (lane < lo + EXPERTS_PER_GROUP), logits, neg)
    m1 = jnp.max(el, axis=-1, keepdims=True)
    i1 = jnp.min(jnp.where(el == m1, lane, float(LANES)), axis=-1, keepdims=True)
    el2 = jnp.where(lane == i1, neg, el)
    m2 = jnp.max(el2, axis=-1, keepdims=True)
    i2 = jnp.min(jnp.where(el2 == m2, lane, float(LANES)), axis=-1, keepdims=True)
    tt = jnp.exp(m2 - m1)
    g1 = grp_p / (1.0 + tt)
    g2 = g1 * tt
    e1 = i1 - N_GROUPS
    e2 = i2 - N_GROUPS

    onehot = ((lane == e1) | (lane == e2)).astype(F32)
    strict = (_iota((tm, tm), 1) < _iota((tm, tm), 0)).astype(BF16)
    rank = _dot(strict, onehot.astype(BF16)) + carry_ref[0:1, :]
    r1 = _lane_pick(lane, e1, rank)
    r2 = _lane_pick(lane, e2, rank)
    carry_ref[...] = carry_ref[...] + jnp.sum(onehot, axis=0, keepdims=True)
    cnt_ref[...] = carry_ref[...]

    route = jnp.where(lane == 0, e1, jnp.where(lane == 1, e2, jnp.where(lane == 2, r1, jnp.where(
        lane == 3, r2, jnp.where(lane == 4, g1, jnp.where(lane == 5, g2, 0.0))))))
    route_ref[...] = route

    _store_token_tiles(hp_ref, h)


def _router(x, g, scale, shift, w_route, b_route, *, tm=512):
    t, d = x.shape
    vec = pl.BlockSpec((1, d), lambda i: (0, 0))
    return pl.pallas_call(
        functools.partial(_router_body, tm=tm),
        grid=(t // tm,),
        in_specs=[pl.BlockSpec((tm, d), lambda i: (i, 0)), vec, vec, vec,
                  pl.BlockSpec((d, LANES), lambda i: (0, 0)),
                  pl.BlockSpec((1, LANES), lambda i: (0, 0))],
        out_specs=[pl.BlockSpec((tm * TOKEN_TILE_ROWS, LANES), lambda i: (i, 0)),
                   pl.BlockSpec((tm, LANES), lambda i: (i, 0)),
                   pl.BlockSpec((8, LANES), lambda i: (0, 0))],
        out_shape=[jax.ShapeDtypeStruct((t * TOKEN_TILE_ROWS, LANES), U32),
                   jax.ShapeDtypeStruct((t, LANES), F32),
                   jax.ShapeDtypeStruct((8, LANES), F32)],
        scratch_shapes=[pltpu.VMEM((8, LANES), F32)],
        compiler_params=_cparams(("arbitrary",)),
        name="moe_router",
    )(x, g, scale, shift, w_route, b_route)


def _dest_body(route_ref, cnt_ref, dest_ref, blk_ref, *, tm, n_blocks_pad):
    cnt = cnt_ref[...]
    padded = jnp.floor((cnt + (MOE_ROWS - 1)) * (1.0 / MOE_ROWS)) * MOE_ROWS
    upper = (_iota((LANES, LANES), 0) <= _iota((LANES, LANES), 1)).astype(F32)
    pad_end = _dot(padded, upper, _HI)
    pad_start = (pad_end - padded)[0:1, :]
    route = route_ref[...]
    lane = _iota((tm, LANES), 1).astype(F32)
    e1, e2, r1, r2 = route[:, 0:1], route[:, 1:2], route[:, 2:3], route[:, 3:4]
    d1 = _lane_pick(lane, e1, pad_start) + r1
    d2 = _lane_pick(lane, e2, pad_start) + r2
    dest_ref[...] = jnp.where(lane == 0, d1, jnp.where(lane == 1, d2, 0.0)).astype(I32)

    lane_b = _iota((n_blocks_pad, LANES), 1)
    first_row = (_iota((n_blocks_pad, LANES), 0) * MOE_ROWS).astype(F32)
    done = ((pad_end[0:1, :] <= first_row) & (lane_b < N_EXPERTS)).astype(F32)
    blk_e = jnp.minimum(jnp.sum(done, axis=-1, keepdims=True), float(N_EXPERTS - 1))
    n_valid = pad_end[0:1, N_EXPERTS - 1:N_EXPERTS] * (1.0 / MOE_ROWS)
    on_diag = lane_b == _iota((n_blocks_pad, LANES), 0)
    seg_fill = jnp.sum(jnp.where(on_diag, pad_start + cnt[0:1, :], 0.0), axis=-1, keepdims=True)
    seg_end = jnp.sum(jnp.where(on_diag, pad_end[0:1, :], 0.0), axis=-1, keepdims=True)
    blk_ref[...] = jnp.where(lane_b == 0, blk_e, jnp.where(lane_b == 1, n_valid, jnp.where(
        lane_b == 2, seg_fill, jnp.where(lane_b == 3, seg_end, 0.0)))).astype(I32)


def _dest(route, counts, n_blocks_pad, *, tm=512):
    t = route.shape[0]
    assert t % tm == 0
    return pl.pallas_call(
        functools.partial(_dest_body, tm=tm, n_blocks_pad=n_blocks_pad),
        grid=(t // tm,),
        in_specs=[pl.BlockSpec((tm, LANES), lambda i: (i, 0)),
                  pl.BlockSpec((8, LANES), lambda i: (0, 0))],
        out_specs=[pl.BlockSpec((tm, LANES), lambda i: (i, 0)),
                   pl.BlockSpec((n_blocks_pad, LANES), lambda i: (0, 0))],
        out_shape=[jax.ShapeDtypeStruct((t, LANES), I32),
                   jax.ShapeDtypeStruct((n_blocks_pad, LANES), I32)],
        compiler_params=_cparams(("arbitrary",)),
        name="moe_dest",
    )(route, counts)


def _invert_body(dest_ref, fill_ref, end_ref, tok_ref, *, n_assign, n_rows):
    def fill8(g, c):
        for r in range(8):
            tok_ref[g * 8 + r] = 0
        return c

    def fill_expert(e, c):
        return lax.fori_loop(fill_ref[e] >> 3, end_ref[e] >> 3, fill8, c)

    lax.fori_loop(0, N_EXPERTS, fill_expert, 0)
    lax.fori_loop(end_ref[N_EXPERTS - 1] >> 3, n_rows // 8, fill8, 0)

    def scatter(a, c):
        tok_ref[dest_ref[a]] = a >> 1
        return c

    lax.fori_loop(0, n_assign, scatter, 0, unroll=8)


def _invert(dest_flat, seg_fill, seg_end, n_rows):
    n_assign = dest_flat.shape[0]
    return pl.pallas_call(
        functools.partial(_invert_body, n_assign=n_assign, n_rows=n_rows),
        grid_spec=pltpu.PrefetchScalarGridSpec(
            num_scalar_prefetch=3,
            grid=(1,),
            in_specs=[],
            out_specs=pl.BlockSpec(memory_space=pltpu.SMEM),
        ),
        out_shape=jax.ShapeDtypeStruct((n_rows,), I32),
        compiler_params=_cparams(("arbitrary",)),
        name="moe_invert",
    )(dest_flat, seg_fill, seg_end)


def _expert_body(be_ref, nv_ref, tok_ref, hp_ref, w1_ref, w3_ref, w2_ref, y_ref, xbuf_ref, sem, wst1_ref, wst3_ref,
                 wst2_ref, wsem, w1b_ref, w3b_ref, w2b_ref, *, layer):
    b = pl.program_id(0)
    n_blocks = pl.num_programs(0)
    wst_refs = (wst1_ref, wst3_ref, wst2_ref)
    n_valid = nv_ref[0]
    valid = b < n_valid
    tr = TOKEN_TILE_ROWS
    expert = be_ref[b]
    first_of_expert = valid & ((b == 0) | (expert != be_ref[jnp.maximum(b - 1, 0)]))

    def weight_copies(e):
        return [pltpu.make_async_copy(w_ref.at[layer, e], st_ref, wsem.at[i])
                for i, (w_ref, st_ref) in enumerate(zip((w1_ref, w3_ref, w2_ref), wst_refs))]

    @pl.when((b == 0) & valid)
    def _():
        for c in weight_copies(expert):
            c.start()

    def gather_rows(blk, slot):
        def one(j, c):
            tok = tok_ref[blk * MOE_ROWS + j]
            pltpu.make_async_copy(hp_ref.at[pl.ds(tok * tr, tr)], xbuf_ref.at[slot, pl.ds(j * tr, tr)],
                                  sem.at[slot]).start()
            return c

        lax.fori_loop(0, MOE_ROWS, one, 0, unroll=8)

    @pl.when(b == 0)
    def _():
        for ahead in range(GATHER_AHEAD):
            @pl.when(ahead < n_valid)
            def _(ahead=ahead):
                gather_rows(ahead, ahead)

    @pl.when(b + GATHER_AHEAD < n_valid)
    def _():
        gather_rows(b + GATHER_AHEAD, (b + GATHER_AHEAD) % (GATHER_AHEAD + 1))

    @pl.when(first_of_expert)
    def _():
        for c in weight_copies(expert):
            c.wait()
        for wb_ref, st_ref in zip((w1b_ref, w3b_ref, w2b_ref), wst_refs):
            wb_ref[...] = st_ref[...].astype(BF16)
        nxt = lax.while_loop(lambda j: (j < n_valid) & (be_ref[jnp.minimum(j, n_blocks - 1)] == expert),
                             lambda j: j + 1, b + 1)

        @pl.when(nxt < n_valid)
        def _():
            for c in weight_copies(be_ref[jnp.minimum(nxt, n_blocks - 1)]):
                c.start()

    @pl.when(jnp.logical_not(valid))
    def _():
        y_ref[...] = jnp.zeros_like(y_ref)

    @pl.when(valid)
    def _():
        slot = b % (GATHER_AHEAD + 1)
        pltpu.make_async_copy(hp_ref.at[pl.ds(0, MOE_ROWS * tr)], xbuf_ref.at[slot], sem.at[slot]).wait()
        lo, hi = _load_token_tiles(xbuf_ref.at[slot], MOE_ROWS)
        x = jnp.concatenate(lo + hi, axis=1).astype(BF16)
        a = (_silu(_dot(x, w1b_ref[...])) * _dot(x, w3b_ref[...])).astype(BF16)
        _store_token_tiles(y_ref, _dot(a, w2b_ref[...]))


def _experts(block_expert, n_valid, tok_rows, h_packed, w1, w3, w2, *, layer):
    n_blocks = block_expert.shape[0]
    _, _, d, f = w1.shape
    anywhere = pl.BlockSpec(memory_space=pl.ANY)
    return pl.pallas_call(
        functools.partial(_expert_body, layer=layer),
        grid_spec=pltpu.PrefetchScalarGridSpec(
            num_scalar_prefetch=3,
            grid=(n_blocks,),
            in_specs=[anywhere, anywhere, anywhere, anywhere],
            out_specs=pl.BlockSpec((MOE_ROWS * TOKEN_TILE_ROWS, LANES), lambda b, be, nv, tk: (b, 0)),
            scratch_shapes=[pltpu.VMEM((GATHER_AHEAD + 1, MOE_ROWS * TOKEN_TILE_ROWS, LANES), U32),
                            pltpu.SemaphoreType.DMA((GATHER_AHEAD + 1,)),
                            pltpu.VMEM((d, f), F32), pltpu.VMEM((d, f), F32), pltpu.VMEM((f, d), F32),
                            pltpu.SemaphoreType.DMA((3,)),
                            pltpu.VMEM((d, f), BF16), pltpu.VMEM((d, f), BF16), pltpu.VMEM((f, d), BF16)],
        ),
        out_shape=jax.ShapeDtypeStruct((n_blocks * MOE_ROWS * TOKEN_TILE_ROWS, LANES), U32),
        compiler_params=_cparams(("arbitrary",)),
        name="moe_experts",
    )(block_expert, n_valid, tok_rows, h_packed, w1, w3, w2)


def _combine_body(dest_ref, y_ref, x_ref, route_ref, gate_ref, fg_ref, o_ref, buf_ref, sem, *, tm, final_norm):
    i = pl.program_id(0)
    yr = TOKEN_TILE_ROWS

    def gather_rows(blk, slot):
        def one(j, c):
            for k in range(2):
                d = dest_ref[(blk * tm + j) * 2 + k]
                pltpu.make_async_copy(y_ref.at[pl.ds(d * yr, yr)], buf_ref.at[slot, k, pl.ds(j * yr, yr)],
                                      sem.at[slot, k]).start()
            return c

        lax.fori_loop(0, tm, one, 0, unroll=4)

    @pl.when(i == 0)
    def _():
        gather_rows(0, 0)

    @pl.when(i + 1 < pl.num_programs(0))
    def _():
        gather_rows(i + 1, (i + 1) % 2)

    slot = i % 2
    for k in range(2):
        pltpu.make_async_copy(y_ref.at[pl.ds(0, tm * yr)], buf_ref.at[slot, k], sem.at[slot, k]).wait()

    route = route_ref[...]
    g1, g2 = route[:, 4:5], route[:, 5:6]
    lo1, hi1 = _load_token_tiles(buf_ref.at[slot, 0], tm)
    lo2, hi2 = _load_token_tiles(buf_ref.at[slot, 1], tm)
    half = x_ref.shape[1] // 2
    for s in range(yr):
        for base, y1, y2 in ((0, lo1[s], lo2[s]), (half, hi1[s], hi2[s])):
            cs = slice(base + s * LANES, base + (s + 1) * LANES)
            o_ref[:, cs] = x_ref[:, cs] + gate_ref[:, cs] * (g1 * y1 + g2 * y2)
    if final_norm:
        xo = o_ref[...]
        o_ref[...] = xo * lax.rsqrt(jnp.mean(xo * xo, axis=-1, keepdims=True) + EPS) * fg_ref[...]


def _combine(dest_flat, y_rows, x, route, gate, final_g, *, final_norm, tm=128):
    t, d = x.shape
    return pl.pallas_call(
        functools.partial(_combine_body, tm=tm, final_norm=final_norm),
        grid_spec=pltpu.PrefetchScalarGridSpec(
            num_scalar_prefetch=1,
            grid=(t // tm,),
            in_specs=[pl.BlockSpec(memory_space=pl.ANY),
                      pl.BlockSpec((tm, d), lambda i, dr: (i, 0)),
                      pl.BlockSpec((tm, LANES), lambda i, dr: (i, 0)),
                      pl.BlockSpec((1, d), lambda i, dr: (0, 0)),
                      pl.BlockSpec((1, d), lambda i, dr: (0, 0))],
            out_specs=pl.BlockSpec((tm, d), lambda i, dr: (i, 0)),
            scratch_shapes=[pltpu.VMEM((2, 2, tm * TOKEN_TILE_ROWS, LANES), U32), pltpu.SemaphoreType.DMA((2, 2))],
        ),
        out_shape=jax.ShapeDtypeStruct((t, d), F32),
        compiler_params=_cparams(("arbitrary",)),
        name="moe_combine",
    )(dest_flat, y_rows, x, route, gate, final_g)


def _moe(x, g, scale, shift, gate, w_group, b_group, w_expert, b_expert, w1, w3, w2, final_g, *, layer, final_norm):
    t, d = x.shape
    pad = LANES - N_GROUPS - N_EXPERTS
    w_route = jnp.concatenate([w_group, w_expert, jnp.zeros((d, pad), F32)], axis=1)
    b_route = jnp.concatenate([b_group, b_expert, jnp.zeros((pad,), F32)]).reshape(1, LANES)
    h_packed, route, counts = _router(x, g, scale, shift, w_route, b_route)
    n_assign = 2 * t
    n_blocks = -(-(n_assign + N_EXPERTS * (MOE_ROWS - 1)) // MOE_ROWS)
    n_blocks_pad = -(-n_blocks // 8) * 8
    dest, blk = _dest(route, counts, n_blocks_pad)
    dest_flat = dest[:, :2].reshape(n_assign)
    block_expert = blk[:n_blocks, 0]
    n_valid = blk[0, 1:2]
    tok_rows = _invert(dest_flat, blk[:N_EXPERTS, 2], blk[:N_EXPERTS, 3], n_blocks * MOE_ROWS)
    y_rows = _experts(block_expert, n_valid, tok_rows, h_packed, w1, w3, w2, layer=layer)
    return _combine(dest_flat, y_rows, x, route, gate, final_g, final_norm=final_norm)


def _pad_rows(w, rows):
    return jnp.pad(w, ((0, rows - w.shape[0]), (0, 0)))


def _lane_row(v, offset):
    return jnp.zeros((LANES,), F32).at[offset:offset + v.shape[0]].set(v)


def kernel(x, c, w_mod, b_mod, norm_g, final_norm_g, e_w_in, e_conv_qkv, e_a_log, e_dt_bias, e_head_norm_g, e_conf_dw, e_conf_dw_b, e_conf_ln_g, e_conf_ln_b, e_w_out, o_w_in, o_conv_w, o_conv_b, o_dt_bias, o_a_log, o_d_skip, o_norm_g, o_w_out, moe_w_group, moe_b_group, moe_w_expert, moe_b_expert, moe_w1, moe_w3, moe_w2):
    bsz, t, d = x.shape
    assert bsz == 1 and c.shape == (1, d)
    xt = x[0]
    mod = _modulation(c[0], w_mod, b_mod)

    def mod_parts(l):
        return mod[l, :, :d], mod[l, :, d:2 * d], mod[l, :, 2 * d:]

    row = lambda v: v.reshape(1, -1)
    qkv_w = 3 * GDN_HEADS * HEAD_W
    z_end = qkv_w + GDN_HEADS * HEAD_W

    shift, scale, gate = mod_parts(0)
    w_in = e_w_in[0].T
    tn = 512
    w_glu = w_in[z_end + 2 * GDN_HEADS:]
    w_ba = _pad_rows(w_in[z_end:z_end + 2 * GDN_HEADS], LANES)
    proj0, ba = _norm_matmul(xt, row(norm_g[0, 0]), scale, shift,
                             [(w_in, z_end // tn), (w_glu, w_glu.shape[0] // tn)], w_ba, tn=tn)
    gdn_par = jnp.zeros((8, LANES), F32).at[0].set(_lane_row(e_a_log[0], GDN_HEADS)).at[1].set(
        _lane_row(e_dt_bias[0], GDN_HEADS))
    a_out = _gdn(proj0, ba, e_conv_qkv[0], gdn_par, row(e_head_norm_g[0]))
    b_out = _conformer(proj0, e_conf_dw[0], row(e_conf_dw_b[0]), row(e_conf_ln_g[0]), row(e_conf_ln_b[0]))
    xt = _out_proj0(a_out, b_out, e_w_out[0], xt, gate)
    shift, scale, gate = mod_parts(1)
    xt = _moe(xt, row(norm_g[0, 1]), scale, shift, gate, moe_w_group[0], moe_b_group[0], moe_w_expert[0],
              moe_b_expert[0], moe_w1, moe_w3, moe_w2, row(final_norm_g), layer=0, final_norm=False)

    shift, scale, gate = mod_parts(2)
    w_in = o_w_in[0].T
    n_main = 2 * SSM_GROUPS * SSM_GROUP_W + 2 * SSM_GROUPS * SSM_STATE
    tn = 1024
    proj1, dt_raw = _norm_matmul(xt, row(norm_g[1, 0]), scale, shift, [(w_in, n_main // tn)],
                                 _pad_rows(w_in[n_main:], LANES), tn=tn)
    ssd_par = jnp.zeros((8, LANES), F32).at[0].set(_lane_row(o_dt_bias[0], 0)).at[1].set(
        _lane_row(o_a_log[0], 0)).at[2].set(_lane_row(o_d_skip[0], 0))
    y = _ssd(proj1, dt_raw, o_conv_w[0], row(o_conv_b[0]), ssd_par)
    xt = _out_proj1(y, row(o_norm_g[0]), o_w_out[0], xt, gate)
    shift, scale, gate = mod_parts(3)
    xt = _moe(xt, row(norm_g[1, 1]), scale, shift, gate, moe_w_group[1], moe_b_group[1], moe_w_expert[1],
              moe_b_expert[1], moe_w1, moe_w3, moe_w2, row(final_norm_g), layer=1, final_norm=True)
    return xt[None]
```

```python
import functools

import jax
import jax.numpy as jnp
from jax import lax
from jax.experimental import pallas as pl
from jax.experimental.pallas import tpu as pltpu

F32 = jnp.float32
BF16 = jnp.bfloat16
I32 = jnp.int32
U32 = jnp.uint32

EPS = 1e-6
LANES = 128
CHUNK = 64
CHUNK_LOG2 = 6
GDN_HEADS = 8
HEAD_W = 128
CONF_CH = 1024
CONF_K = 31
SSM_GROUPS = 8
SSM_GROUP_W = 512
SSM_HEADDIM = 64
SSM_STATE = 128
N_EXPERTS = 32
EXPERTS_PER_GROUP = 8
N_GROUPS = 4
MOE_ROWS = 256
GATHER_AHEAD = 4
COMBINE_AHEAD = 3
TOKEN_TILE_ROWS = 8
VMEM_LIMIT = 56 * 1024 * 1024
PROLOGUE_ROWS = 128
HALO_ROWS = 16
CONF_HALO_ROWS = 32

_HI = lax.Precision.HIGHEST


def _cparams(sem):
    return pltpu.CompilerParams(dimension_semantics=sem, vmem_limit_bytes=VMEM_LIMIT)


def _dot(a, b, precision=None):
    return jnp.dot(a, b, precision=precision, preferred_element_type=F32)


def _dot_nt(a, b, precision=None):
    return lax.dot_general(a, b, (((1,), (1,)), ((), ())), precision=precision, preferred_element_type=F32)


def _dot_tn(a, b, precision=None):
    return lax.dot_general(a, b, (((0,), (0,)), ((), ())), precision=precision, preferred_element_type=F32)


def _split_bf16(x):
    hi = x.astype(BF16)
    return hi, (x - hi.astype(F32)).astype(BF16)


def _dot3(a, b):
    ah, al = _split_bf16(a)
    bh, bl = _split_bf16(b)
    return _dot(ah, bh) + (_dot(ah, bl) + _dot(al, bh))


def _select_dot(sel, x, left=False):
    p1 = x.astype(BF16)
    r1 = x - p1.astype(F32)
    p2 = r1.astype(BF16)
    p3 = (r1 - p2.astype(F32)).astype(BF16)
    if left:
        return _dot(sel, p1) + (_dot(sel, p2) + _dot(sel, p3))
    return _dot(p1, sel) + (_dot(p2, sel) + _dot(p3, sel))


def _silu(x):
    return x * jax.nn.sigmoid(x)


def _softplus(x):
    return jnp.maximum(x, 0.0) + jnp.log1p(jnp.exp(-jnp.abs(x)))


def _iota(shape, dim):
    return lax.broadcasted_iota(I32, shape, dim)


def _mod_body(c_ref, w_ref, b_ref, o_ref):
    c = c_ref[...]
    o_ref[...] = jnp.sum(w_ref[...] * _silu(c), axis=0, keepdims=True) + b_ref[...]


def _modulation(c, w_mod, b_mod):
    nl, d, n = w_mod.shape
    tn = 768
    return pl.pallas_call(
        _mod_body,
        grid=(nl, n // tn),
        in_specs=[
            pl.BlockSpec((d, 1), lambda l, j: (0, 0)),
            pl.BlockSpec((None, d, tn), lambda l, j: (l, 0, j)),
            pl.BlockSpec((None, 1, tn), lambda l, j: (l, 0, j)),
        ],
        out_specs=pl.BlockSpec((None, 1, tn), lambda l, j: (l, 0, j)),
        out_shape=jax.ShapeDtypeStruct((nl, 1, n), F32),
        compiler_params=_cparams(("arbitrary", "arbitrary")),
        name="adaln_mod",
    )(c.reshape(d, 1), w_mod, b_mod.reshape(nl, 1, n))


def _adaln_rows(x, g, scale, shift):
    r = lax.rsqrt(jnp.mean(x * x, axis=-1, keepdims=True) + EPS)
    return x * r * (g * (1.0 + scale)) + shift


def _norm_mm_body(*refs, starts):
    x_ref, g_ref, sc_ref, sh_ref, ws_ref = refs[:5]
    w_refs = refs[5:5 + len(starts)]
    o_ref, os_ref, h_ref = refs[5 + len(starts):]
    j = pl.program_id(1)

    @pl.when(j == 0)
    def _():
        ws = ws_ref[...].astype(BF16)

        def norm_rows(c, carry):
            rs = pl.ds(pl.multiple_of(c * PROLOGUE_ROWS, PROLOGUE_ROWS), PROLOGUE_ROWS)
            h = _adaln_rows(x_ref[rs, :], g_ref[...], sc_ref[...], sh_ref[...]).astype(BF16)
            h_ref[rs, :] = h
            os_ref[rs, :] = _dot_nt(h, ws)
            return carry

        lax.fori_loop(0, x_ref.shape[0] // PROLOGUE_ROWS, norm_rows, 0)

    bounds = list(starts[1:]) + [None]
    for w_ref, lo, hi in zip(w_refs, starts, bounds):
        in_seg = (j >= lo) if hi is None else ((j >= lo) & (j < hi))

        @pl.when(in_seg)
        def _(w_ref=w_ref):
            o_ref[...] = _dot_nt(h_ref[...], w_ref[...].astype(BF16)).astype(o_ref.dtype)


def _norm_matmul(x, g, scale, shift, segments, w_small, *, tm=1024, tn=512):
    t, d = x.shape
    tm = min(tm, t)
    starts, specs, n_main = [], [], 0
    for w, nb in segments:
        lo = n_main // tn
        starts.append(lo)
        specs.append(pl.BlockSpec((tn, d), lambda i, j, lo=lo, nb=nb: (jnp.clip(j - lo, 0, nb - 1), 0)))
        n_main += nb * tn
    vec = pl.BlockSpec((1, d), lambda i, j: (0, 0))
    return pl.pallas_call(
        functools.partial(_norm_mm_body, starts=tuple(starts)),
        grid=(t // tm, n_main // tn),
        in_specs=[pl.BlockSpec((tm, d), lambda i, j: (i, 0)), vec, vec, vec,
                  pl.BlockSpec((LANES, d), lambda i, j: (0, 0))] + specs,
        out_specs=[pl.BlockSpec((tm, tn), lambda i, j: (i, j)),
                   pl.BlockSpec((tm, LANES), lambda i, j: (i, 0))],
        out_shape=[jax.ShapeDtypeStruct((t, n_main), BF16), jax.ShapeDtypeStruct((t, LANES), F32)],
        scratch_shapes=[pltpu.VMEM((tm, d), BF16)],
        compiler_params=_cparams(("arbitrary", "arbitrary")),
        name="adaln_in_proj",
    )(x, g, scale, shift, w_small, *[w for w, _ in segments])


def _causal_conv(halo, x, w, taps):
    hr = halo.shape[0]
    n = x.shape[0]
    xe = jnp.concatenate([halo, x], axis=0)
    rolled = {0: xe}
    acc = None
    for j in range(taps):
        s = taps - 1 - j
        a, b = divmod(s, 8)
        if b not in rolled:
            rolled[b] = pltpu.roll(xe, b, axis=0)
        term = rolled[b][hr - 8 * a:hr - 8 * a + n] * w[j:j + 1]
        acc = term if acc is None else acc + term
    return acc


def _short_conv_bf16(halo, x16, w):
    n = x16.shape[0]
    x = x16.astype(F32)
    lag = _iota((n, n), 0) - _iota((n, n), 1)
    shifts = jnp.concatenate([(lag == s).astype(BF16) for s in (1, 2, 3)], axis=0)
    moved = _dot(shifts, x16)
    acc = x * w[3:4] + moved[0:n] * w[2:3] + moved[n:2 * n] * w[1:2] + moved[2 * n:3 * n] * w[0:1]
    head = _causal_conv(halo, x[0:8], w, 4)
    return jnp.concatenate([head, acc[8:]], axis=0)


def _unit_lower_inverses(a_mats, row, col):
    n = a_mats[0].shape[0]
    eye = (row == col).astype(F32)
    level = ((row >> 1) == (col >> 1)) & (col < row)
    invs = [eye - jnp.where(level, a, 0.0) for a in a_mats]
    k = 1
    while (1 << k) < n:
        level = ((row >> (k + 1)) == (col >> (k + 1))) & (((row >> k) & 1) == 1) & (((col >> k) & 1) == 0)
        inv16 = [inv.astype(BF16) for inv in invs]
        ys = [_dot(i16, jnp.where(level, a, 0.0).astype(BF16)) for i16, a in zip(inv16, a_mats)]
        invs = [inv - _dot(y.astype(BF16), i16) for inv, y, i16 in zip(invs, ys, inv16)]
        k += 1
    return invs


def _gdn_body(q_ref, k_ref, v_ref, qh_ref, kh_ref, vh_ref, z_ref, ba_ref, cwq_ref, cwk_ref, cwv_ref,
              par_ref, hg_ref, o_ref, s_ref, *, hb, rows):
    hblk = pl.program_id(0)
    ir = pl.program_id(1)

    @pl.when(ir == 0)
    def _():
        s_ref[...] = jnp.zeros_like(s_ref)

    def conv_silu(x_ref, halo_ref, w_ref):
        halo = jnp.where(ir > 0, halo_ref[...].astype(F32), 0.0)
        return _silu(_short_conv_bf16(halo, x_ref[...], w_ref[...]))

    qc = conv_silu(q_ref, qh_ref, cwq_ref)
    kc = conv_silu(k_ref, kh_ref, cwk_ref)
    vc = conv_silu(v_ref, vh_ref, cwv_ref)

    ba = ba_ref[...]
    beta_all = jax.nn.sigmoid(ba)
    g_all = -jnp.exp(par_ref[0:1, :]) * _softplus(ba + par_ref[1:2, :])
    rr = _iota((rows, rows), 0)
    cc = _iota((rows, rows), 1)
    chunk_tril = ((rr >> CHUNK_LOG2) == (cc >> CHUNK_LOG2)) & (cc <= rr)
    gcs_all = _select_dot(chunk_tril.astype(BF16), g_all, left=True)
    gcs_t = gcs_all.T

    row = _iota((CHUNK, CHUNK), 0)
    col = _iota((CHUNK, CHUNK), 1)
    scale = HEAD_W ** -0.5
    hg = hg_ref[...]
    bf = lambda a: a.astype(BF16)
    lane = _iota((rows, LANES), 1)
    sub = _iota((LANES, rows), 0)
    nc = rows // CHUNK

    heads = []
    for h in range(hb):
        hs = slice(h * HEAD_W, (h + 1) * HEAD_W)
        head = hblk * hb + h
        q = qc[:, hs]
        k = kc[:, hs]
        heads.append(dict(
            q=q * lax.rsqrt(jnp.sum(q * q, axis=-1, keepdims=True) + EPS) * scale,
            k=k * lax.rsqrt(jnp.sum(k * k, axis=-1, keepdims=True) + EPS),
            v=vc[:, hs],
            beta=jnp.sum(jnp.where(lane == head, beta_all, 0.0), axis=-1, keepdims=True),
            gcs=jnp.sum(jnp.where(lane == head + GDN_HEADS, gcs_all, 0.0), axis=-1, keepdims=True),
            gcs_row=jnp.sum(jnp.where(sub == head + GDN_HEADS, gcs_t, 0.0), axis=0, keepdims=True)))

    items = []
    for c in range(nc):
        rs = slice(c * CHUNK, (c + 1) * CHUNK)
        for hd in heads:
            qq, kk, vv, beta, gcs = hd["q"][rs], hd["k"][rs], hd["v"][rs], hd["beta"][rs], hd["gcs"][rs]
            decay = jnp.where(col <= row, jnp.exp(jnp.minimum(gcs - hd["gcs_row"][:, rs], 0.0)), 0.0)
            kb = kk * beta
            eg = jnp.exp(gcs)
            g_last = gcs[CHUNK - 1:CHUNK]
            items.append(dict(decay=decay, kb16=bf(kb), k16=bf(kk), q16=bf(qq),
                              rhs16=bf(jnp.concatenate([vv * beta, kb * eg], axis=-1)),
                              q_dec16=bf(qq * eg), k_dec16=bf(kk * jnp.exp(g_last - gcs)), cd=jnp.exp(g_last)))
    a_mats = [jnp.where(col < row, _dot_nt(it["kb16"], it["k16"]) * it["decay"], 0.0) for it in items]
    qks = [bf(_dot_nt(it["q16"], it["k16"]) * it["decay"]) for it in items]
    t_invs = _unit_lower_inverses(a_mats, row, col)
    sols = [_dot(bf(t), it["rhs16"]) for t, it in zip(t_invs, items)]

    states = [s_ref[h] for h in range(hb)]
    for c in range(nc):
        rs = slice(c * CHUNK, (c + 1) * CHUNK)
        cur = slice(c * hb, (c + 1) * hb)
        s16 = [bf(s) for s in states]
        us = [so[:, :HEAD_W] - _dot(bf(so[:, HEAD_W:]), s) for so, s in zip(sols[cur], s16)]
        outs = [_dot(it["q_dec16"], s) + _dot(qk, bf(u)) for it, s, qk, u in zip(items[cur], s16, qks[cur], us)]
        states = [s * it["cd"] + _dot_tn(it["k_dec16"], bf(u)) for s, it, u in zip(states, items[cur], us)]
        for h, o in enumerate(outs):
            hs = slice(h * HEAD_W, (h + 1) * HEAD_W)
            o = o * lax.rsqrt(jnp.mean(o * o, axis=-1, keepdims=True) + EPS) * hg * _silu(z_ref[rs, hs].astype(F32))
            o_ref[rs, hs] = o.astype(o_ref.dtype)
    for h in range(hb):
        s_ref[h] = states[h]


def _gdn(proj, ba, conv_w, par, head_g, *, hb=GDN_HEADS, rows=256):
    t = proj.shape[0]
    rows = min(rows, t)
    w = hb * HEAD_W
    per = (GDN_HEADS * HEAD_W) // w

    def sec(k):
        return pl.BlockSpec((rows, w), lambda h, i, k=k: (i, k * per + h))

    def halo(k):
        return pl.BlockSpec((HALO_ROWS, w),
                            lambda h, i, k=k: (jnp.maximum(i * (rows // HALO_ROWS) - 1, 0), k * per + h))

    def cw(k):
        return pl.BlockSpec((4, w), lambda h, i, k=k: (0, k * per + h))

    return pl.pallas_call(
        functools.partial(_gdn_body, hb=hb, rows=rows),
        grid=(GDN_HEADS // hb, t // rows),
        in_specs=[sec(0), sec(1), sec(2), halo(0), halo(1), halo(2), sec(3),
                  pl.BlockSpec((rows, LANES), lambda h, i: (i, 0)),
                  cw(0), cw(1), cw(2),
                  pl.BlockSpec((8, LANES), lambda h, i: (0, 0)),
                  pl.BlockSpec((1, HEAD_W), lambda h, i: (0, 0))],
        out_specs=pl.BlockSpec((rows, w), lambda h, i: (i, h)),
        out_shape=jax.ShapeDtypeStruct((t, GDN_HEADS * HEAD_W), BF16),
        scratch_shapes=[pltpu.VMEM((hb, HEAD_W, HEAD_W), F32)],
        compiler_params=_cparams(("arbitrary", "arbitrary")),
        name="gated_deltanet",
    )(proj, proj, proj, proj, proj, proj, proj, ba, conv_w, conv_w, conv_w, par, head_g)


def _conf_body(x_ref, halo_ref, w_ref, b_ref, g_ref, lb_ref, o_ref):
    ir = pl.program_id(0)

    def glu(v):
        v = v.astype(F32)
        return v[:, :CONF_CH] * jax.nn.sigmoid(v[:, CONF_CH:])

    halo = jnp.where(ir > 0, glu(halo_ref[...]), 0.0)
    u = _causal_conv(halo, glu(x_ref[...]), w_ref[...], CONF_K) + b_ref[...]
    mu = jnp.mean(u, axis=-1, keepdims=True)
    uc = u - mu
    var = jnp.mean(uc * uc, axis=-1, keepdims=True)
    y = uc * lax.rsqrt(var + EPS) * g_ref[...] + lb_ref[...]
    o_ref[...] = _silu(y).astype(o_ref.dtype)


def _conformer(proj, dw, dw_b, ln_g, ln_b, *, rows=256):
    t = proj.shape[0]
    rows = min(rows, t)
    glu_blk = (4 * GDN_HEADS * HEAD_W) // (2 * CONF_CH)
    halo_rows = CONF_HALO_ROWS
    vec = pl.BlockSpec((1, CONF_CH), lambda i: (0, 0))
    return pl.pallas_call(
        _conf_body,
        grid=(t // rows,),
        in_specs=[pl.BlockSpec((rows, 2 * CONF_CH), lambda i: (i, glu_blk)),
                  pl.BlockSpec((halo_rows, 2 * CONF_CH),
                               lambda i: (jnp.maximum(i * (rows // halo_rows) - 1, 0), glu_blk)),
                  pl.BlockSpec((CONF_K, CONF_CH), lambda i: (0, 0)), vec, vec, vec],
        out_specs=pl.BlockSpec((rows, CONF_CH), lambda i: (i, 0)),
        out_shape=jax.ShapeDtypeStruct((t, CONF_CH), BF16),
        compiler_params=_cparams(("arbitrary",)),
        name="conformer_conv",
    )(proj, proj, dw, dw_b, ln_g, ln_b)


def _out0_body(a_ref, b_ref, wa_ref, wb_ref, x_ref, gate_ref, o_ref):
    mix = _dot(a_ref[...], wa_ref[...].astype(BF16)) + _dot(b_ref[...], wb_ref[...].astype(BF16))
    o_ref[...] = x_ref[...] + gate_ref[...] * mix


def _out_proj0(a, b, w, x, gate, *, tm=1024, tn=1024):
    t, d = x.shape
    tm = min(tm, t)
    ka, kb = a.shape[1], b.shape[1]
    return pl.pallas_call(
        _out0_body,
        grid=(t // tm, d // tn),
        in_specs=[pl.BlockSpec((tm, ka), lambda i, j: (i, 0)),
                  pl.BlockSpec((tm, kb), lambda i, j: (i, 0)),
                  pl.BlockSpec((ka, tn), lambda i, j: (0, j)),
                  pl.BlockSpec((kb, tn), lambda i, j: (ka // kb, j)),
                  pl.BlockSpec((tm, tn), lambda i, j: (i, j)),
                  pl.BlockSpec((1, tn), lambda i, j: (0, j))],
        out_specs=pl.BlockSpec((tm, tn), lambda i, j: (i, j)),
        out_shape=jax.ShapeDtypeStruct((t, d), F32),
        compiler_params=_cparams(("arbitrary", "arbitrary")),
        name="out_proj_even",
    )(a, b, w, w, x, gate)


def _out1_body(y_ref, g_ref, w_ref, x_ref, gate_ref, o_ref, h_ref):
    @pl.when(pl.program_id(1) == 0)
    def _():
        def norm_rows(c, carry):
            rs = pl.ds(pl.multiple_of(c * PROLOGUE_ROWS, PROLOGUE_ROWS), PROLOGUE_ROWS)
            y = y_ref[rs, :].astype(F32)
            r = lax.rsqrt(jnp.mean(y * y, axis=-1, keepdims=True) + EPS)
            h_ref[rs, :] = (y * r * g_ref[...]).astype(BF16)
            return carry

        lax.fori_loop(0, y_ref.shape[0] // PROLOGUE_ROWS, norm_rows, 0)

    o_ref[...] = x_ref[...] + gate_ref[...] * _dot(h_ref[...], w_ref[...].astype(BF16))


def _out_proj1(y, norm_g, w, x, gate, *, tm=1024, tn=512):
    t, d = x.shape
    tm = min(tm, t)
    k = y.shape[1]
    return pl.pallas_call(
        _out1_body,
        grid=(t // tm, d // tn),
        in_specs=[pl.BlockSpec((tm, k), lambda i, j: (i, 0)),
                  pl.BlockSpec((1, k), lambda i, j: (0, 0)),
                  pl.BlockSpec((k, tn), lambda i, j: (0, j)),
                  pl.BlockSpec((tm, tn), lambda i, j: (i, j)),
                  pl.BlockSpec((1, tn), lambda i, j: (0, j))],
        out_specs=pl.BlockSpec((tm, tn), lambda i, j: (i, j)),
        out_shape=jax.ShapeDtypeStruct((t, d), F32),
        scratch_shapes=[pltpu.VMEM((tm, k), BF16)],
        compiler_params=_cparams(("arbitrary", "arbitrary")),
        name="out_proj_odd",
    )(y, norm_g, w, x, gate)


def _ssd_body(x_ref, b_ref, c_ref, xh_ref, bh_ref, ch_ref, z_ref, dt_ref, cwx_ref, cwb_ref, cwc_ref,
              cbx_ref, cbb_ref, cbc_ref, par_ref, o_ref, s_ref, at_ref, *, rows, gps):
    ir = pl.program_id(1)
    gw = SSM_GROUP_W
    hpg = gw // SSM_HEADDIM

    @pl.when(ir == 0)
    def _():
        s_ref[...] = jnp.zeros_like(s_ref)

    def conv_silu(x_r, halo_r, w_r, bias_r):
        halo = jnp.where(ir > 0, halo_r[...].astype(F32), 0.0)
        return _silu(_causal_conv(halo, x_r[...].astype(F32), w_r[...], 4) + bias_r[...])

    xs_all = conv_silu(x_ref, xh_ref, cwx_ref, cbx_ref)
    bm_all = conv_silu(b_ref, bh_ref, cwb_ref, cbb_ref)
    cm_all = conv_silu(c_ref, ch_ref, cwc_ref, cbc_ref)

    dt = _softplus(dt_ref[...] + par_ref[0:1, :])
    da = dt * (-jnp.exp(par_ref[1:2, :]))
    rr = _iota((rows, rows), 0)
    cc = _iota((rows, rows), 1)
    chunk_tril = ((rr >> CHUNK_LOG2) == (cc >> CHUNK_LOG2)) & (cc <= rr)
    acs = _select_dot(chunk_tril.astype(BF16), da, left=True)
    at_ref[...] = acs.T

    causal = (_iota((CHUNK, gw), 1) & (CHUNK - 1)) <= _iota((CHUNK, gw), 0)
    same_head = (_iota((LANES, LANES), 0) >> CHUNK_LOG2) == (_iota((LANES, LANES), 1) >> CHUNK_LOG2)
    bf = lambda a: a.astype(BF16)
    nc = rows // CHUNK
    chunks = [slice(c * CHUNK, (c + 1) * CHUNK) for c in range(nc)]
    tiles = [slice(k * LANES, (k + 1) * LANES) for k in range(gw // LANES)]

    for gi in range(gps):
        grp = pl.program_id(0) * gps + gi
        cols = slice(gi * gw, (gi + 1) * gw)
        xs = xs_all[:, cols]
        bm = bm_all[:, gi * SSM_STATE:(gi + 1) * SSM_STATE]
        cm = cm_all[:, gi * SSM_STATE:(gi + 1) * SSM_STATE]
        acs_rows = at_ref[pl.ds(pl.multiple_of(grp * hpg, hpg), hpg), :]

        sel = (_iota((LANES, gw), 0) == grp * hpg + (_iota((LANES, gw), 1) >> CHUNK_LOG2)).astype(BF16)
        dt_e = _select_dot(sel, dt)
        acs_e = _select_dot(sel, acs)
        dskip_e = _select_dot(sel, par_ref[...])[2:3, :]

        a_es = [acs_e[rs] for rs in chunks]
        a_rows = [jnp.concatenate([acs_rows[h:h + 1, rs] for h in range(hpg)], axis=1) for rs in chunks]
        l_cats = [jnp.where(causal, jnp.exp(jnp.minimum(a_e - a_row, 0.0)), 0.0) for a_e, a_row in zip(a_es, a_rows)]
        c16 = [bf(cm[rs]) for rs in chunks]
        b16 = [bf(bm[rs]) for rs in chunks]
        cbs = [_dot_nt(c_c, jnp.concatenate([b_c] * hpg, axis=0)) for c_c, b_c in zip(c16, b16)]
        xdts = [xs[rs] * dt_e[rs] for rs in chunks]
        m16 = [bf(cb * l_cat) for cb, l_cat in zip(cbs, l_cats)]
        y_diags = []
        for m, xdt in zip(m16, xdts):
            parts = [_dot(m[:, ts], bf(jnp.where(same_head, jnp.concatenate([xdt[:, ts]] * 2, axis=0), 0.0)))
                     for ts in tiles]
            y_diags.append(jnp.concatenate(parts, axis=1))
        a_lasts = [a_e[CHUNK - 1:CHUNK] for a_e in a_es]
        updates = [_dot_tn(b_c, bf(xdt * jnp.exp(a_last - a_e)))
                   for b_c, xdt, a_last, a_e in zip(b16, xdts, a_lasts, a_es)]

        states = [s_ref[gi]]
        for a_last, upd in zip(a_lasts, updates):
            states.append(states[-1] * jnp.exp(a_last) + upd)
        s_ref[gi] = states[-1]

        y_offs = [_dot(c_c, bf(s)) * jnp.exp(a_e) for c_c, s, a_e in zip(c16, states[:-1], a_es)]
        for rs, y_diag, y_off in zip(chunks, y_diags, y_offs):
            y = y_diag + y_off + dskip_e * xs[rs]
            o_ref[rs, cols] = (y * _silu(z_ref[rs, cols].astype(F32))).astype(o_ref.dtype)


def _ssd(proj, dt_raw, conv_w, conv_b, par, *, rows=256, gps=8):
    t = proj.shape[0]
    rows = min(rows, t)
    xw = gps * SSM_GROUP_W
    bw = gps * SSM_STATE
    d_inner = SSM_GROUPS * SSM_GROUP_W
    xo = d_inner // xw
    bo = (2 * d_inner) // bw
    co = bo + SSM_GROUPS * SSM_STATE // bw
    hrow = lambda i: jnp.maximum(i * (rows // HALO_ROWS) - 1, 0)
    wbo = d_inner // bw
    wco = wbo + SSM_GROUPS * SSM_STATE // bw
    return pl.pallas_call(
        functools.partial(_ssd_body, rows=rows, gps=gps),
        grid=(SSM_GROUPS // gps, t // rows),
        in_specs=[pl.BlockSpec((rows, xw), lambda g, i: (i, xo + g)),
                  pl.BlockSpec((rows, bw), lambda g, i: (i, bo + g)),
                  pl.BlockSpec((rows, bw), lambda g, i: (i, co + g)),
                  pl.BlockSpec((HALO_ROWS, xw), lambda g, i: (hrow(i), xo + g)),
                  pl.BlockSpec((HALO_ROWS, bw), lambda g, i: (hrow(i), bo + g)),
                  pl.BlockSpec((HALO_ROWS, bw), lambda g, i: (hrow(i), co + g)),
                  pl.BlockSpec((rows, xw), lambda g, i: (i, g)),
                  pl.BlockSpec((rows, LANES), lambda g, i: (i, 0)),
                  pl.BlockSpec((4, xw), lambda g, i: (0, g)),
                  pl.BlockSpec((4, bw), lambda g, i: (0, wbo + g)),
                  pl.BlockSpec((4, bw), lambda g, i: (0, wco + g)),
                  pl.BlockSpec((1, xw), lambda g, i: (0, g)),
                  pl.BlockSpec((1, bw), lambda g, i: (0, wbo + g)),
                  pl.BlockSpec((1, bw), lambda g, i: (0, wco + g)),
                  pl.BlockSpec((8, LANES), lambda g, i: (0, 0))],
        out_specs=pl.BlockSpec((rows, xw), lambda g, i: (i, g)),
        out_shape=jax.ShapeDtypeStruct((t, d_inner), BF16),
        scratch_shapes=[pltpu.VMEM((gps, SSM_STATE, SSM_GROUP_W), F32), pltpu.VMEM((LANES, rows), F32)],
        compiler_params=_cparams(("arbitrary", "arbitrary")),
        name="ssd",
    )(proj, proj, proj, proj, proj, proj, proj, dt_raw, conv_w, conv_w, conv_w, conv_b, conv_b, conv_b, par)


def _store_token_tiles(tile_ref, x):
    rows, d = x.shape
    bits = lax.bitcast_convert_type(x.astype(BF16).astype(F32), U32)
    packed = (bits[:, :d // 2] >> 16) | (bits[:, d // 2:] & jnp.uint32(0xFFFF0000))
    for s in range(TOKEN_TILE_ROWS):
        tile_ref[pl.ds(s, rows, stride=TOKEN_TILE_ROWS), :] = packed[:, s * LANES:(s + 1) * LANES]


def _load_token_tiles(tile_ref, rows):
    lo, hi = [], []
    for s in range(TOKEN_TILE_ROWS):
        p = tile_ref[pl.ds(s, rows, stride=TOKEN_TILE_ROWS), :]
        lo.append(lax.bitcast_convert_type(p << 16, F32))
        hi.append(lax.bitcast_convert_type(p & jnp.uint32(0xFFFF0000), F32))
    return lo, hi


def _lane_pick(lane, idx, vals):
    return jnp.sum(jnp.where(lane == idx, vals, 0.0), axis=-1, keepdims=True)


def _router_body(x_ref, g_ref, sc_ref, sh_ref, w_ref, b_ref, hp_ref, route_ref, cnt_ref, carry_ref, *, tm):
    i = pl.program_id(0)

    @pl.when(i == 0)
    def _():
        carry_ref[...] = jnp.zeros_like(carry_ref)

    h = _adaln_rows(x_ref[...], g_ref[...], sc_ref[...], sh_ref[...])
    logits = _dot3(h, w_ref[...]) + b_ref[...]
    lane = _iota((tm, LANES), 1).astype(F32)
    neg = -jnp.inf
    gl = jnp.where(lane < N_GROUPS, logits, neg)
    gmax = jnp.max(gl, axis=-1, keepdims=True)
    grp_p = 1.0 / jnp.sum(jnp.exp(gl - gmax), axis=-1, keepdims=True)
    gidx = jnp.min(jnp.where(gl == gmax, lane, float(LANES)), axis=-1, keepdims=True)
    lo = N_GROUPS + EXPERTS_PER_GROUP * gidx
    el = jnp.where((lane >= lo) & (lane < lo + EXPERTS_PER_GROUP), logits, neg)
    m1 = jnp.max(el, axis=-1, keepdims=True)
    i1 = jnp.min(jnp.where(el == m1, lane, float(LANES)), axis=-1, keepdims=True)
    el2 = jnp.where(lane == i1, neg, el)
    m2 = jnp.max(el2, axis=-1, keepdims=True)
    i2 = jnp.min(jnp.where(el2 == m2, lane, float(LANES)), axis=-1, keepdims=True)
    tt = jnp.exp(m2 - m1)
    g1 = grp_p / (1.0 + tt)
    g2 = g1 * tt
    e1 = i1 - N_GROUPS
    e2 = i2 - N_GROUPS

    onehot = ((lane == e1) | (lane == e2)).astype(F32)
    strict = (_iota((tm, tm), 1) < _iota((tm, tm), 0)).astype(BF16)
    rank = _dot(strict, onehot.astype(BF16)) + carry_ref[0:1, :]
    r1 = _lane_pick(lane, e1, rank)
    r2 = _lane_pick(lane, e2, rank)
    carry_ref[...] = carry_ref[...] + jnp.sum(onehot, axis=0, keepdims=True)
    cnt_ref[...] = carry_ref[...]

    route = jnp.where(lane == 0, e1, jnp.where(lane == 1, e2, jnp.where(lane == 2, r1, jnp.where(
        lane == 3, r2, jnp.where(lane == 4, g1, jnp.where(lane == 5, g2, 0.0))))))
    route_ref[...] = route

    _store_token_tiles(hp_ref, h)


def _router(x, g, scale, shift, w_route, b_route, *, tm=512):
    t, d = x.shape
    vec = pl.BlockSpec((1, d), lambda i: (0, 0))
    return pl.pallas_call(
        functools.partial(_router_body, tm=tm),
        grid=(t // tm,),
        in_specs=[pl.BlockSpec((tm, d), lambda i: (i, 0)), vec, vec, vec,
                  pl.BlockSpec((d, LANES), lambda i: (0, 0)),
                  pl.BlockSpec((1, LANES), lambda i: (0, 0))],
        out_specs=[pl.BlockSpec((tm * TOKEN_TILE_ROWS, LANES), lambda i: (i, 0)),
                   pl.BlockSpec((tm, LANES), lambda i: (i, 0)),
                   pl.BlockSpec((8, LANES), lambda i: (0, 0))],
        out_shape=[jax.ShapeDtypeStruct((t * TOKEN_TILE_ROWS, LANES), U32),
                   jax.ShapeDtypeStruct((t, LANES), F32),
                   jax.ShapeDtypeStruct((8, LANES), F32)],
        scratch_shapes=[pltpu.VMEM((8, LANES), F32)],
        compiler_params=_cparams(("arbitrary",)),
        name="moe_router",
    )(x, g, scale, shift, w_route, b_route)


def _dest_body(route_ref, cnt_ref, dest_ref, blk_ref, *, tm, n_blocks_pad):
    cnt = cnt_ref[...]
    padded = jnp.floor((cnt + (MOE_ROWS - 1)) * (1.0 / MOE_ROWS)) * MOE_ROWS
    upper = (_iota((LANES, LANES), 0) <= _iota((LANES, LANES), 1)).astype(F32)
    pad_end = _dot(padded, upper, _HI)
    pad_start = (pad_end - padded)[0:1, :]
    route = route_ref[...]
    lane = _iota((tm, LANES), 1).astype(F32)
    e1, e2, r1, r2 = route[:, 0:1], route[:, 1:2], route[:, 2:3], route[:, 3:4]
    d1 = _lane_pick(lane, e1, pad_start) + r1
    d2 = _lane_pick(lane, e2, pad_start) + r2
    dest_ref[...] = jnp.where(lane == 0, d1, jnp.where(lane == 1, d2, 0.0)).astype(I32)

    lane_b = _iota((n_blocks_pad, LANES), 1)
    first_row = (_iota((n_blocks_pad, LANES), 0) * MOE_ROWS).astype(F32)
    done = ((pad_end[0:1, :] <= first_row) & (lane_b < N_EXPERTS)).astype(F32)
    blk_e = jnp.minimum(jnp.sum(done, axis=-1, keepdims=True), float(N_EXPERTS - 1))
    n_valid = pad_end[0:1, N_EXPERTS - 1:N_EXPERTS] * (1.0 / MOE_ROWS)
    on_diag = lane_b == _iota((n_blocks_pad, LANES), 0)
    seg_fill = jnp.sum(jnp.where(on_diag, pad_start + cnt[0:1, :], 0.0), axis=-1, keepdims=True)
    seg_end = jnp.sum(jnp.where(on_diag, pad_end[0:1, :], 0.0), axis=-1, keepdims=True)
    blk_ref[...] = jnp.where(lane_b == 0, blk_e, jnp.where(lane_b == 1, n_valid, jnp.where(
        lane_b == 2, seg_fill, jnp.where(lane_b == 3, seg_end, 0.0)))).astype(I32)


def _dest(route, counts, n_blocks_pad, *, tm=512):
    t = route.shape[0]
    assert t % tm == 0
    return pl.pallas_call(
        functools.partial(_dest_body, tm=tm, n_blocks_pad=n_blocks_pad),
        grid=(t // tm,),
        in_specs=[pl.BlockSpec((tm, LANES), lambda i: (i, 0)),
                  pl.BlockSpec((8, LANES), lambda i: (0, 0))],
        out_specs=[pl.BlockSpec((tm, LANES), lambda i: (i, 0)),
                   pl.BlockSpec((n_blocks_pad, LANES), lambda i: (0, 0))],
        out_shape=[jax.ShapeDtypeStruct((t, LANES), I32),
                   jax.ShapeDtypeStruct((n_blocks_pad, LANES), I32)],
        compiler_params=_cparams(("arbitrary",)),
        name="moe_dest",
    )(route, counts)


def _invert_body(dest_ref, fill_ref, end_ref, tok_ref, *, n_assign, n_rows):
    def fill8(g, c):
        for r in range(8):
            tok_ref[g * 8 + r] = 0
        return c

    def fill_expert(e, c):
        return lax.fori_loop(fill_ref[e] >> 3, end_ref[e] >> 3, fill8, c)

    lax.fori_loop(0, N_EXPERTS, fill_expert, 0)
    lax.fori_loop(end_ref[N_EXPERTS - 1] >> 3, n_rows // 8, fill8, 0)

    def scatter(a, c):
        tok_ref[dest_ref[a]] = a >> 1
        return c

    lax.fori_loop(0, n_assign, scatter, 0, unroll=8)


def _invert(dest_flat, seg_fill, seg_end, n_rows):
    n_assign = dest_flat.shape[0]
    return pl.pallas_call(
        functools.partial(_invert_body, n_assign=n_assign, n_rows=n_rows),
        grid_spec=pltpu.PrefetchScalarGridSpec(
            num_scalar_prefetch=3,
            grid=(1,),
            in_specs=[],
            out_specs=pl.BlockSpec(memory_space=pltpu.SMEM),
        ),
        out_shape=jax.ShapeDtypeStruct((n_rows,), I32),
        compiler_params=_cparams(("arbitrary",)),
        name="moe_invert",
    )(dest_flat, seg_fill, seg_end)


def _expert_body(be_ref, nv_ref, tok_ref, hp_ref, w1_ref, w3_ref, w2_ref, y_ref, xbuf_ref, sem, wst1_ref, wst3_ref,
                 wst2_ref, wsem, w1b_ref, w3b_ref, w2b_ref, *, layer):
    b = pl.program_id(0)
    n_blocks = pl.num_programs(0)
    wst_refs = (wst1_ref, wst3_ref, wst2_ref)
    n_valid = nv_ref[0]
    valid = b < n_valid
    tr = TOKEN_TILE_ROWS
    expert = be_ref[b]
    first_of_expert = valid & ((b == 0) | (expert != be_ref[jnp.maximum(b - 1, 0)]))

    def weight_copies(e):
        return [pltpu.make_async_copy(w_ref.at[layer, e], st_ref, wsem.at[i])
                for i, (w_ref, st_ref) in enumerate(zip((w1_ref, w3_ref, w2_ref), wst_refs))]

    @pl.when((b == 0) & valid)
    def _():
        for c in weight_copies(expert):
            c.start()

    def gather_rows(blk, slot):
        def one(j, c):
            tok = tok_ref[blk * MOE_ROWS + j]
            pltpu.make_async_copy(hp_ref.at[pl.ds(tok * tr, tr)], xbuf_ref.at[slot, pl.ds(j * tr, tr)],
                                  sem.at[slot]).start()
            return c

        lax.fori_loop(0, MOE_ROWS, one, 0, unroll=8)

    @pl.when(b == 0)
    def _():
        for ahead in range(GATHER_AHEAD):
            @pl.when(ahead < n_valid)
            def _(ahead=ahead):
                gather_rows(ahead, ahead)

    @pl.when(b + GATHER_AHEAD < n_valid)
    def _():
        gather_rows(b + GATHER_AHEAD, (b + GATHER_AHEAD) % (GATHER_AHEAD + 1))

    @pl.when(first_of_expert)
    def _():
        for c in weight_copies(expert):
            c.wait()
        for wb_ref, st_ref in zip((w1b_ref, w3b_ref, w2b_ref), wst_refs):
            wb_ref[...] = st_ref[...].astype(BF16)
        nxt = lax.while_loop(lambda j: (j < n_valid) & (be_ref[jnp.minimum(j, n_blocks - 1)] == expert),
                             lambda j: j + 1, b + 1)

        @pl.when(nxt < n_valid)
        def _():
            for c in weight_copies(be_ref[jnp.minimum(nxt, n_blocks - 1)]):
                c.start()

    @pl.when(jnp.logical_not(valid))
    def _():
        y_ref[...] = jnp.zeros_like(y_ref)

    @pl.when(valid)
    def _():
        slot = b % (GATHER_AHEAD + 1)
        pltpu.make_async_copy(hp_ref.at[pl.ds(0, MOE_ROWS * tr)], xbuf_ref.at[slot], sem.at[slot]).wait()
        lo, hi = _load_token_tiles(xbuf_ref.at[slot], MOE_ROWS)
        x = jnp.concatenate(lo + hi, axis=1).astype(BF16)
        a = (_silu(_dot(x, w1b_ref[...])) * _dot(x, w3b_ref[...])).astype(BF16)
        _store_token_tiles(y_ref, _dot(a, w2b_ref[...]))


def _experts(block_expert, n_valid, tok_rows, h_packed, w1, w3, w2, *, layer):
    n_blocks = block_expert.shape[0]
    _, _, d, f = w1.shape
    anywhere = pl.BlockSpec(memory_space=pl.ANY)
    return pl.pallas_call(
        functools.partial(_expert_body, layer=layer),
        grid_spec=pltpu.PrefetchScalarGridSpec(
            num_scalar_prefetch=3,
            grid=(n_blocks,),
            in_specs=[anywhere, anywhere, anywhere, anywhere],
            out_specs=pl.BlockSpec((MOE_ROWS * TOKEN_TILE_ROWS, LANES), lambda b, be, nv, tk: (b, 0)),
            scratch_shapes=[pltpu.VMEM((GATHER_AHEAD + 1, MOE_ROWS * TOKEN_TILE_ROWS, LANES), U32),
                            pltpu.SemaphoreType.DMA((GATHER_AHEAD + 1,)),
                            pltpu.VMEM((d, f), F32), pltpu.VMEM((d, f), F32), pltpu.VMEM((f, d), F32),
                            pltpu.SemaphoreType.DMA((3,)),
                            pltpu.VMEM((d, f), BF16), pltpu.VMEM((d, f), BF16), pltpu.VMEM((f, d), BF16)],
        ),
        out_shape=jax.ShapeDtypeStruct((n_blocks * MOE_ROWS * TOKEN_TILE_ROWS, LANES), U32),
        compiler_params=_cparams(("arbitrary",)),
        name="moe_experts",
    )(block_expert, n_valid, tok_rows, h_packed, w1, w3, w2)


def _combine_body(dest_ref, y_ref, x_ref, route_ref, gate_ref, fg_ref, o_ref, buf_ref, sem, *, tm, final_norm):
    i = pl.program_id(0)
    yr = TOKEN_TILE_ROWS

    def gather_rows(blk, slot):
        def one(j, c):
            for k in range(2):
                d = dest_ref[(blk * tm + j) * 2 + k]
                pltpu.make_async_copy(y_ref.at[pl.ds(d * yr, yr)], buf_ref.at[slot, k, pl.ds(j * yr, yr)],
                                      sem.at[slot, k]).start()
            return c

        lax.fori_loop(0, tm, one, 0, unroll=4)

    n_steps = pl.num_programs(0)

    @pl.when(i == 0)
    def _():
        for ahead in range(COMBINE_AHEAD):
            @pl.when(ahead < n_steps)
            def _(ahead=ahead):
                gather_rows(ahead, ahead)

    @pl.when(i + COMBINE_AHEAD < n_steps)
    def _():
        gather_rows(i + COMBINE_AHEAD, (i + COMBINE_AHEAD) % (COMBINE_AHEAD + 1))

    slot = i % (COMBINE_AHEAD + 1)
    for k in range(2):
        pltpu.make_async_copy(y_ref.at[pl.ds(0, tm * yr)], buf_ref.at[slot, k], sem.at[slot, k]).wait()

    route = route_ref[...]
    g1, g2 = route[:, 4:5], route[:, 5:6]
    lo1, hi1 = _load_token_tiles(buf_ref.at[slot, 0], tm)
    lo2, hi2 = _load_token_tiles(buf_ref.at[slot, 1], tm)
    half = x_ref.shape[1] // 2
    for s in range(yr):
        for base, y1, y2 in ((0, lo1[s], lo2[s]), (half, hi1[s], hi2[s])):
            cs = slice(base + s * LANES, base + (s + 1) * LANES)
            o_ref[:, cs] = x_ref[:, cs] + gate_ref[:, cs] * (g1 * y1 + g2 * y2)
    if final_norm:
        xo = o_ref[...]
        o_ref[...] = xo * lax.rsqrt(jnp.mean(xo * xo, axis=-1, keepdims=True) + EPS) * fg_ref[...]


def _combine(dest_flat, y_rows, x, route, gate, final_g, *, final_norm, tm=128):
    t, d = x.shape
    return pl.pallas_call(
        functools.partial(_combine_body, tm=tm, final_norm=final_norm),
        grid_spec=pltpu.PrefetchScalarGridSpec(
            num_scalar_prefetch=1,
            grid=(t // tm,),
            in_specs=[pl.BlockSpec(memory_space=pl.ANY),
                      pl.BlockSpec((tm, d), lambda i, dr: (i, 0)),
                      pl.BlockSpec((tm, LANES), lambda i, dr: (i, 0)),
                      pl.BlockSpec((1, d), lambda i, dr: (0, 0)),
                      pl.BlockSpec((1, d), lambda i, dr: (0, 0))],
            out_specs=pl.BlockSpec((tm, d), lambda i, dr: (i, 0)),
            scratch_shapes=[pltpu.VMEM((COMBINE_AHEAD + 1, 2, tm * TOKEN_TILE_ROWS, LANES), U32),
                            pltpu.SemaphoreType.DMA((COMBINE_AHEAD + 1, 2))],
        ),
        out_shape=jax.ShapeDtypeStruct((t, d), F32),
        compiler_params=_cparams(("arbitrary",)),
        name="moe_combine",
    )(dest_flat, y_rows, x, route, gate, final_g)


def _moe(x, g, scale, shift, gate, w_group, b_group, w_expert, b_expert, w1, w3, w2, final_g, *, layer, final_norm):
    t, d = x.shape
    pad = LANES - N_GROUPS - N_EXPERTS
    w_route = jnp.concatenate([w_group, w_expert, jnp.zeros((d, pad), F32)], axis=1)
    b_route = jnp.concatenate([b_group, b_expert, jnp.zeros((pad,), F32)]).reshape(1, LANES)
    h_packed, route, counts = _router(x, g, scale, shift, w_route, b_route)
    n_assign = 2 * t
    n_blocks = -(-(n_assign + N_EXPERTS * (MOE_ROWS - 1)) // MOE_ROWS)
    n_blocks_pad = -(-n_blocks // 8) * 8
    dest, blk = _dest(route, counts, n_blocks_pad)
    dest_flat = dest[:, :2].reshape(n_assign)
    block_expert = blk[:n_blocks, 0]
    n_valid = blk[0, 1:2]
    tok_rows = _invert(dest_flat, blk[:N_EXPERTS, 2], blk[:N_EXPERTS, 3], n_blocks * MOE_ROWS)
    y_rows = _experts(block_expert, n_valid, tok_rows, h_packed, w1, w3, w2, layer=layer)
    return _combine(dest_flat, y_rows, x, route, gate, final_g, final_norm=final_norm)


def _pad_rows(w, rows):
    return jnp.pad(w, ((0, rows - w.shape[0]), (0, 0)))


def _lane_row(v, offset):
    return jnp.zeros((LANES,), F32).at[offset:offset + v.shape[0]].set(v)


def kernel(x, c, w_mod, b_mod, norm_g, final_norm_g, e_w_in, e_conv_qkv, e_a_log, e_dt_bias, e_head_norm_g, e_conf_dw, e_conf_dw_b, e_conf_ln_g, e_conf_ln_b, e_w_out, o_w_in, o_conv_w, o_conv_b, o_dt_bias, o_a_log, o_d_skip, o_norm_g, o_w_out, moe_w_group, moe_b_group, moe_w_expert, moe_b_expert, moe_w1, moe_w3, moe_w2):
    bsz, t, d = x.shape
    assert bsz == 1 and c.shape == (1, d)
    xt = x[0]
    mod = _modulation(c[0], w_mod, b_mod)

    def mod_parts(l):
        return mod[l, :, :d], mod[l, :, d:2 * d], mod[l, :, 2 * d:]

    row = lambda v: v.reshape(1, -1)
    qkv_w = 3 * GDN_HEADS * HEAD_W
    z_end = qkv_w + GDN_HEADS * HEAD_W

    shift, scale, gate = mod_parts(0)
    w_in = e_w_in[0].T
    tn = 512
    w_glu = w_in[z_end + 2 * GDN_HEADS:]
    w_ba = _pad_rows(w_in[z_end:z_end + 2 * GDN_HEADS], LANES)
    proj0, ba = _norm_matmul(xt, row(norm_g[0, 0]), scale, shift,
                             [(w_in, z_end // tn), (w_glu, w_glu.shape[0] // tn)], w_ba, tn=tn)
    gdn_par = jnp.zeros((8, LANES), F32).at[0].set(_lane_row(e_a_log[0], GDN_HEADS)).at[1].set(
        _lane_row(e_dt_bias[0], GDN_HEADS))
    a_out = _gdn(proj0, ba, e_conv_qkv[0], gdn_par, row(e_head_norm_g[0]))
    b_out = _conformer(proj0, e_conf_dw[0], row(e_conf_dw_b[0]), row(e_conf_ln_g[0]), row(e_conf_ln_b[0]))
    xt = _out_proj0(a_out, b_out, e_w_out[0], xt, gate)
    shift, scale, gate = mod_parts(1)
    xt = _moe(xt, row(norm_g[0, 1]), scale, shift, gate, moe_w_group[0], moe_b_group[0], moe_w_expert[0],
              moe_b_expert[0], moe_w1, moe_w3, moe_w2, row(final_norm_g), layer=0, final_norm=False)

    shift, scale, gate = mod_parts(2)
    w_in = o_w_in[0].T
    n_main = 2 * SSM_GROUPS * SSM_GROUP_W + 2 * SSM_GROUPS * SSM_STATE
    tn = 1024
    proj1, dt_raw = _norm_matmul(xt, row(norm_g[1, 0]), scale, shift, [(w_in, n_main // tn)],
                                 _pad_rows(w_in[n_main:], LANES), tn=tn)
    ssd_par = jnp.zeros((8, LANES), F32).at[0].set(_lane_row(o_dt_bias[0], 0)).at[1].set(
        _lane_row(o_a_log[0], 0)).at[2].set(_lane_row(o_d_skip[0], 0))
    y = _ssd(proj1, dt_raw, o_conv_w[0], row(o_conv_b[0]), ssd_par)
    xt = _out_proj1(y, row(o_norm_g[0]), o_w_out[0], xt, gate)
    shift, scale, gate = mod_parts(3)
    xt = _moe(xt, row(norm_g[1, 1]), scale, shift, gate, moe_w_group[1], moe_b_group[1], moe_w_expert[1],
              moe_b_expert[1], moe_w1, moe_w3, moe_w2, row(final_norm_g), layer=1, final_norm=True)
    return xt[None]
```

```python
import functools

import jax
import jax.numpy as jnp
from jax import lax
from jax.experimental import pallas as pl
from jax.experimental.pallas import tpu as pltpu

F32 = jnp.float32
BF16 = jnp.bfloat16
I32 = jnp.int32
U32 = jnp.uint32

EPS = 1e-6
LANES = 128
CHUNK = 64
CHUNK_LOG2 = 6
GDN_HEADS = 8
HEAD_W = 128
CONF_CH = 1024
CONF_K = 31
SSM_GROUPS = 8
SSM_GROUP_W = 512
SSM_HEADDIM = 64
SSM_STATE = 128
N_EXPERTS = 32
EXPERTS_PER_GROUP = 8
N_GROUPS = 4
MOE_ROWS = 256
GATHER_AHEAD = 6
COMBINE_AHEAD = 1
TOKEN_TILE_ROWS = 8
VMEM_LIMIT = 56 * 1024 * 1024
PROLOGUE_ROWS = 128
HALO_ROWS = 16
CONF_HALO_ROWS = 32

_HI = lax.Precision.HIGHEST


def _cparams(sem):
    return pltpu.CompilerParams(dimension_semantics=sem, vmem_limit_bytes=VMEM_LIMIT)


def _dot(a, b, precision=None):
    return jnp.dot(a, b, precision=precision, preferred_element_type=F32)


def _dot_nt(a, b, precision=None):
    return lax.dot_general(a, b, (((1,), (1,)), ((), ())), precision=precision, preferred_element_type=F32)


def _dot_tn(a, b, precision=None):
    return lax.dot_general(a, b, (((0,), (0,)), ((), ())), precision=precision, preferred_element_type=F32)


def _split_bf16(x):
    hi = x.astype(BF16)
    return hi, (x - hi.astype(F32)).astype(BF16)


def _dot3(a, b):
    ah, al = _split_bf16(a)
    bh, bl = _split_bf16(b)
    return _dot(ah, bh) + (_dot(ah, bl) + _dot(al, bh))


def _select_dot(sel, x, left=False):
    p1 = x.astype(BF16)
    r1 = x - p1.astype(F32)
    p2 = r1.astype(BF16)
    p3 = (r1 - p2.astype(F32)).astype(BF16)
    if left:
        return _dot(sel, p1) + (_dot(sel, p2) + _dot(sel, p3))
    return _dot(p1, sel) + (_dot(p2, sel) + _dot(p3, sel))


def _silu(x):
    return x * jax.nn.sigmoid(x)


def _softplus(x):
    return jnp.maximum(x, 0.0) + jnp.log1p(jnp.exp(-jnp.abs(x)))


def _iota(shape, dim):
    return lax.broadcasted_iota(I32, shape, dim)


def _mod_body(c_ref, w_ref, b_ref, o_ref):
    c = c_ref[...]
    o_ref[...] = jnp.sum(w_ref[...] * _silu(c), axis=0, keepdims=True) + b_ref[...]


def _modulation(c, w_mod, b_mod):
    nl, d, n = w_mod.shape
    tn = 768
    return pl.pallas_call(
        _mod_body,
        grid=(nl, n // tn),
        in_specs=[
            pl.BlockSpec((d, 1), lambda l, j: (0, 0)),
            pl.BlockSpec((None, d, tn), lambda l, j: (l, 0, j)),
            pl.BlockSpec((None, 1, tn), lambda l, j: (l, 0, j)),
        ],
        out_specs=pl.BlockSpec((None, 1, tn), lambda l, j: (l, 0, j)),
        out_shape=jax.ShapeDtypeStruct((nl, 1, n), F32),
        compiler_params=_cparams(("arbitrary", "arbitrary")),
        name="adaln_mod",
    )(c.reshape(d, 1), w_mod, b_mod.reshape(nl, 1, n))


def _adaln_rows(x, g, scale, shift):
    r = lax.rsqrt(jnp.mean(x * x, axis=-1, keepdims=True) + EPS)
    return x * r * (g * (1.0 + scale)) + shift


def _norm_mm_body(*refs, starts):
    x_ref, g_ref, sc_ref, sh_ref, ws_ref = refs[:5]
    w_refs = refs[5:5 + len(starts)]
    o_ref, os_ref, h_ref = refs[5 + len(starts):]
    j = pl.program_id(1)

    @pl.when(j == 0)
    def _():
        ws = ws_ref[...].astype(BF16)

        def norm_rows(c, carry):
            rs = pl.ds(pl.multiple_of(c * PROLOGUE_ROWS, PROLOGUE_ROWS), PROLOGUE_ROWS)
            h = _adaln_rows(x_ref[rs, :], g_ref[...], sc_ref[...], sh_ref[...]).astype(BF16)
            h_ref[rs, :] = h
            os_ref[rs, :] = _dot_nt(h, ws)
            return carry

        lax.fori_loop(0, x_ref.shape[0] // PROLOGUE_ROWS, norm_rows, 0)

    bounds = list(starts[1:]) + [None]
    for w_ref, lo, hi in zip(w_refs, starts, bounds):
        in_seg = (j >= lo) if hi is None else ((j >= lo) & (j < hi))

        @pl.when(in_seg)
        def _(w_ref=w_ref):
            o_ref[...] = _dot_nt(h_ref[...], w_ref[...].astype(BF16)).astype(o_ref.dtype)


def _norm_matmul(x, g, scale, shift, segments, w_small, *, tm=1024, tn=512):
    t, d = x.shape
    tm = min(tm, t)
    starts, specs, n_main = [], [], 0
    for w, nb in segments:
        lo = n_main // tn
        starts.append(lo)
        specs.append(pl.BlockSpec((tn, d), lambda i, j, lo=lo, nb=nb: (jnp.clip(j - lo, 0, nb - 1), 0)))
        n_main += nb * tn
    vec = pl.BlockSpec((1, d), lambda i, j: (0, 0))
    return pl.pallas_call(
        functools.partial(_norm_mm_body, starts=tuple(starts)),
        grid=(t // tm, n_main // tn),
        in_specs=[pl.BlockSpec((tm, d), lambda i, j: (i, 0)), vec, vec, vec,
                  pl.BlockSpec((LANES, d), lambda i, j: (0, 0))] + specs,
        out_specs=[pl.BlockSpec((tm, tn), lambda i, j: (i, j)),
                   pl.BlockSpec((tm, LANES), lambda i, j: (i, 0))],
        out_shape=[jax.ShapeDtypeStruct((t, n_main), BF16), jax.ShapeDtypeStruct((t, LANES), F32)],
        scratch_shapes=[pltpu.VMEM((tm, d), BF16)],
        compiler_params=_cparams(("arbitrary", "arbitrary")),
        name="adaln_in_proj",
    )(x, g, scale, shift, w_small, *[w for w, _ in segments])


def _causal_conv(halo, x, w, taps):
    hr = halo.shape[0]
    n = x.shape[0]
    xe = jnp.concatenate([halo, x], axis=0)
    rolled = {0: xe}
    acc = None
    for j in range(taps):
        s = taps - 1 - j
        a, b = divmod(s, 8)
        if b not in rolled:
            rolled[b] = pltpu.roll(xe, b, axis=0)
        term = rolled[b][hr - 8 * a:hr - 8 * a + n] * w[j:j + 1]
        acc = term if acc is None else acc + term
    return acc


def _short_conv_bf16(halo, x16, w):
    n = x16.shape[0]
    x = x16.astype(F32)
    lag = _iota((n, n), 0) - _iota((n, n), 1)
    shifts = jnp.concatenate([(lag == s).astype(BF16) for s in (1, 2, 3)], axis=0)
    moved = _dot(shifts, x16)
    acc = x * w[3:4] + moved[0:n] * w[2:3] + moved[n:2 * n] * w[1:2] + moved[2 * n:3 * n] * w[0:1]
    head = _causal_conv(halo, x[0:8], w, 4)
    return jnp.concatenate([head, acc[8:]], axis=0)


def _unit_lower_inverses(a_mats, row, col):
    n = a_mats[0].shape[0]
    eye = (row == col).astype(F32)
    level = ((row >> 1) == (col >> 1)) & (col < row)
    invs = [eye - jnp.where(level, a, 0.0) for a in a_mats]
    k = 1
    while (1 << k) < n:
        level = ((row >> (k + 1)) == (col >> (k + 1))) & (((row >> k) & 1) == 1) & (((col >> k) & 1) == 0)
        inv16 = [inv.astype(BF16) for inv in invs]
        ys = [_dot(i16, jnp.where(level, a, 0.0).astype(BF16)) for i16, a in zip(inv16, a_mats)]
        invs = [inv - _dot(y.astype(BF16), i16) for inv, y, i16 in zip(invs, ys, inv16)]
        k += 1
    return invs


def _gdn_body(q_ref, k_ref, v_ref, qh_ref, kh_ref, vh_ref, z_ref, ba_ref, cwq_ref, cwk_ref, cwv_ref,
              par_ref, hg_ref, o_ref, s_ref, *, hb, rows):
    hblk = pl.program_id(0)
    ir = pl.program_id(1)

    @pl.when(ir == 0)
    def _():
        s_ref[...] = jnp.zeros_like(s_ref)

    def conv_silu(x_ref, halo_ref, w_ref):
        halo = jnp.where(ir > 0, halo_ref[...].astype(F32), 0.0)
        return _silu(_short_conv_bf16(halo, x_ref[...], w_ref[...]))

    qc = conv_silu(q_ref, qh_ref, cwq_ref)
    kc = conv_silu(k_ref, kh_ref, cwk_ref)
    vc = conv_silu(v_ref, vh_ref, cwv_ref)

    ba = ba_ref[...]
    beta_all = jax.nn.sigmoid(ba)
    g_all = -jnp.exp(par_ref[0:1, :]) * _softplus(ba + par_ref[1:2, :])
    rr = _iota((rows, rows), 0)
    cc = _iota((rows, rows), 1)
    chunk_tril = ((rr >> CHUNK_LOG2) == (cc >> CHUNK_LOG2)) & (cc <= rr)
    gcs_all = _select_dot(chunk_tril.astype(BF16), g_all, left=True)
    gcs_t = gcs_all.T

    row = _iota((CHUNK, CHUNK), 0)
    col = _iota((CHUNK, CHUNK), 1)
    scale = HEAD_W ** -0.5
    hg = hg_ref[...]
    bf = lambda a: a.astype(BF16)
    lane = _iota((rows, LANES), 1)
    sub = _iota((LANES, rows), 0)
    nc = rows // CHUNK

    heads = []
    for h in range(hb):
        hs = slice(h * HEAD_W, (h + 1) * HEAD_W)
        head = hblk * hb + h
        q = qc[:, hs]
        k = kc[:, hs]
        heads.append(dict(
            q=q * lax.rsqrt(jnp.sum(q * q, axis=-1, keepdims=True) + EPS) * scale,
            k=k * lax.rsqrt(jnp.sum(k * k, axis=-1, keepdims=True) + EPS),
            v=vc[:, hs],
            beta=jnp.sum(jnp.where(lane == head, beta_all, 0.0), axis=-1, keepdims=True),
            gcs=jnp.sum(jnp.where(lane == head + GDN_HEADS, gcs_all, 0.0), axis=-1, keepdims=True),
            gcs_row=jnp.sum(jnp.where(sub == head + GDN_HEADS, gcs_t, 0.0), axis=0, keepdims=True)))

    items = []
    for c in range(nc):
        rs = slice(c * CHUNK, (c + 1) * CHUNK)
        for hd in heads:
            qq, kk, vv, beta, gcs = hd["q"][rs], hd["k"][rs], hd["v"][rs], hd["beta"][rs], hd["gcs"][rs]
            decay = jnp.where(col <= row, jnp.exp(jnp.minimum(gcs - hd["gcs_row"][:, rs], 0.0)), 0.0)
            kb = kk * beta
            eg = jnp.exp(gcs)
            g_last = gcs[CHUNK - 1:CHUNK]
            items.append(dict(decay=decay, kb16=bf(kb), k16=bf(kk), q16=bf(qq),
                              rhs16=bf(jnp.concatenate([vv * beta, kb * eg], axis=-1)),
                              q_dec16=bf(qq * eg), k_dec16=bf(kk * jnp.exp(g_last - gcs)), cd=jnp.exp(g_last)))
    a_mats = [jnp.where(col < row, _dot_nt(it["kb16"], it["k16"]) * it["decay"], 0.0) for it in items]
    qks = [bf(_dot_nt(it["q16"], it["k16"]) * it["decay"]) for it in items]
    t_invs = _unit_lower_inverses(a_mats, row, col)
    sols = [_dot(bf(t), it["rhs16"]) for t, it in zip(t_invs, items)]

    states = [s_ref[h] for h in range(hb)]
    for c in range(nc):
        rs = slice(c * CHUNK, (c + 1) * CHUNK)
        cur = slice(c * hb, (c + 1) * hb)
        s16 = [bf(s) for s in states]
        us = [so[:, :HEAD_W] - _dot(bf(so[:, HEAD_W:]), s) for so, s in zip(sols[cur], s16)]
        outs = [_dot(it["q_dec16"], s) + _dot(qk, bf(u)) for it, s, qk, u in zip(items[cur], s16, qks[cur], us)]
        states = [s * it["cd"] + _dot_tn(it["k_dec16"], bf(u)) for s, it, u in zip(states, items[cur], us)]
        for h, o in enumerate(outs):
            hs = slice(h * HEAD_W, (h + 1) * HEAD_W)
            o = o * lax.rsqrt(jnp.mean(o * o, axis=-1, keepdims=True) + EPS) * hg * _silu(z_ref[rs, hs].astype(F32))
            o_ref[rs, hs] = o.astype(o_ref.dtype)
    for h in range(hb):
        s_ref[h] = states[h]


def _gdn(proj, ba, conv_w, par, head_g, *, hb=GDN_HEADS, rows=256):
    t = proj.shape[0]
    rows = min(rows, t)
    w = hb * HEAD_W
    per = (GDN_HEADS * HEAD_W) // w

    def sec(k):
        return pl.BlockSpec((rows, w), lambda h, i, k=k: (i, k * per + h))

    def halo(k):
        return pl.BlockSpec((HALO_ROWS, w),
                            lambda h, i, k=k: (jnp.maximum(i * (rows // HALO_ROWS) - 1, 0), k * per + h))

    def cw(k):
        return pl.BlockSpec((4, w), lambda h, i, k=k: (0, k * per + h))

    return pl.pallas_call(
        functools.partial(_gdn_body, hb=hb, rows=rows),
        grid=(GDN_HEADS // hb, t // rows),
        in_specs=[sec(0), sec(1), sec(2), halo(0), halo(1), halo(2), sec(3),
                  pl.BlockSpec((rows, LANES), lambda h, i: (i, 0)),
                  cw(0), cw(1), cw(2),
                  pl.BlockSpec((8, LANES), lambda h, i: (0, 0)),
                  pl.BlockSpec((1, HEAD_W), lambda h, i: (0, 0))],
        out_specs=pl.BlockSpec((rows, w), lambda h, i: (i, h)),
        out_shape=jax.ShapeDtypeStruct((t, GDN_HEADS * HEAD_W), BF16),
        scratch_shapes=[pltpu.VMEM((hb, HEAD_W, HEAD_W), F32)],
        compiler_params=_cparams(("arbitrary", "arbitrary")),
        name="gated_deltanet",
    )(proj, proj, proj, proj, proj, proj, proj, ba, conv_w, conv_w, conv_w, par, head_g)


def _conf_body(x_ref, halo_ref, w_ref, b_ref, g_ref, lb_ref, o_ref):
    ir = pl.program_id(0)

    def glu(v):
        v = v.astype(F32)
        return v[:, :CONF_CH] * jax.nn.sigmoid(v[:, CONF_CH:])

    halo = jnp.where(ir > 0, glu(halo_ref[...]), 0.0)
    u = _causal_conv(halo, glu(x_ref[...]), w_ref[...], CONF_K) + b_ref[...]
    mu = jnp.mean(u, axis=-1, keepdims=True)
    uc = u - mu
    var = jnp.mean(uc * uc, axis=-1, keepdims=True)
    y = uc * lax.rsqrt(var + EPS) * g_ref[...] + lb_ref[...]
    o_ref[...] = _silu(y).astype(o_ref.dtype)


def _conformer(proj, dw, dw_b, ln_g, ln_b, *, rows=256):
    t = proj.shape[0]
    rows = min(rows, t)
    glu_blk = (4 * GDN_HEADS * HEAD_W) // (2 * CONF_CH)
    halo_rows = CONF_HALO_ROWS
    vec = pl.BlockSpec((1, CONF_CH), lambda i: (0, 0))
    return pl.pallas_call(
        _conf_body,
        grid=(t // rows,),
        in_specs=[pl.BlockSpec((rows, 2 * CONF_CH), lambda i: (i, glu_blk)),
                  pl.BlockSpec((halo_rows, 2 * CONF_CH),
                               lambda i: (jnp.maximum(i * (rows // halo_rows) - 1, 0), glu_blk)),
                  pl.BlockSpec((CONF_K, CONF_CH), lambda i: (0, 0)), vec, vec, vec],
        out_specs=pl.BlockSpec((rows, CONF_CH), lambda i: (i, 0)),
        out_shape=jax.ShapeDtypeStruct((t, CONF_CH), BF16),
        compiler_params=_cparams(("arbitrary",)),
        name="conformer_conv",
    )(proj, proj, dw, dw_b, ln_g, ln_b)


def _out0_body(a_ref, b_ref, wa_ref, wb_ref, x_ref, gate_ref, o_ref):
    mix = _dot(a_ref[...], wa_ref[...].astype(BF16)) + _dot(b_ref[...], wb_ref[...].astype(BF16))
    o_ref[...] = x_ref[...] + gate_ref[...] * mix


def _out_proj0(a, b, w, x, gate, *, tm=1024, tn=1024):
    t, d = x.shape
    tm = min(tm, t)
    ka, kb = a.shape[1], b.shape[1]
    return pl.pallas_call(
        _out0_body,
        grid=(t // tm, d // tn),
        in_specs=[pl.BlockSpec((tm, ka), lambda i, j: (i, 0)),
                  pl.BlockSpec((tm, kb), lambda i, j: (i, 0)),
                  pl.BlockSpec((ka, tn), lambda i, j: (0, j)),
                  pl.BlockSpec((kb, tn), lambda i, j: (ka // kb, j)),
                  pl.BlockSpec((tm, tn), lambda i, j: (i, j)),
                  pl.BlockSpec((1, tn), lambda i, j: (0, j))],
        out_specs=pl.BlockSpec((tm, tn), lambda i, j: (i, j)),
        out_shape=jax.ShapeDtypeStruct((t, d), F32),
        compiler_params=_cparams(("arbitrary", "arbitrary")),
        name="out_proj_even",
    )(a, b, w, w, x, gate)


def _out1_body(y_ref, g_ref, w_ref, x_ref, gate_ref, o_ref, h_ref):
    @pl.when(pl.program_id(1) == 0)
    def _():
        def norm_rows(c, carry):
            rs = pl.ds(pl.multiple_of(c * PROLOGUE_ROWS, PROLOGUE_ROWS), PROLOGUE_ROWS)
            y = y_ref[rs, :].astype(F32)
            r = lax.rsqrt(jnp.mean(y * y, axis=-1, keepdims=True) + EPS)
            h_ref[rs, :] = (y * r * g_ref[...]).astype(BF16)
            return carry

        lax.fori_loop(0, y_ref.shape[0] // PROLOGUE_ROWS, norm_rows, 0)

    o_ref[...] = x_ref[...] + gate_ref[...] * _dot(h_ref[...], w_ref[...].astype(BF16))


def _out_proj1(y, norm_g, w, x, gate, *, tm=1024, tn=512):
    t, d = x.shape
    tm = min(tm, t)
    k = y.shape[1]
    return pl.pallas_call(
        _out1_body,
        grid=(t // tm, d // tn),
        in_specs=[pl.BlockSpec((tm, k), lambda i, j: (i, 0)),
                  pl.BlockSpec((1, k), lambda i, j: (0, 0)),
                  pl.BlockSpec((k, tn), lambda i, j: (0, j)),
                  pl.BlockSpec((tm, tn), lambda i, j: (i, j)),
                  pl.BlockSpec((1, tn), lambda i, j: (0, j))],
        out_specs=pl.BlockSpec((tm, tn), lambda i, j: (i, j)),
        out_shape=jax.ShapeDtypeStruct((t, d), F32),
        scratch_shapes=[pltpu.VMEM((tm, k), BF16)],
        compiler_params=_cparams(("arbitrary", "arbitrary")),
        name="out_proj_odd",
    )(y, norm_g, w, x, gate)


def _ssd_body(x_ref, b_ref, c_ref, xh_ref, bh_ref, ch_ref, z_ref, dt_ref, cwx_ref, cwb_ref, cwc_ref,
              cbx_ref, cbb_ref, cbc_ref, par_ref, o_ref, s_ref, at_ref, *, rows, gps):
    ir = pl.program_id(1)
    gw = SSM_GROUP_W
    hpg = gw // SSM_HEADDIM

    @pl.when(ir == 0)
    def _():
        s_ref[...] = jnp.zeros_like(s_ref)

    def conv_silu(x_r, halo_r, w_r, bias_r):
        halo = jnp.where(ir > 0, halo_r[...].astype(F32), 0.0)
        return _silu(_causal_conv(halo, x_r[...].astype(F32), w_r[...], 4) + bias_r[...])

    xs_all = conv_silu(x_ref, xh_ref, cwx_ref, cbx_ref)
    bm_all = conv_silu(b_ref, bh_ref, cwb_ref, cbb_ref)
    cm_all = conv_silu(c_ref, ch_ref, cwc_ref, cbc_ref)

    dt = _softplus(dt_ref[...] + par_ref[0:1, :])
    da = dt * (-jnp.exp(par_ref[1:2, :]))
    rr = _iota((rows, rows), 0)
    cc = _iota((rows, rows), 1)
    chunk_tril = ((rr >> CHUNK_LOG2) == (cc >> CHUNK_LOG2)) & (cc <= rr)
    acs = _select_dot(chunk_tril.astype(BF16), da, left=True)
    at_ref[...] = acs.T

    causal = (_iota((CHUNK, gw), 1) & (CHUNK - 1)) <= _iota((CHUNK, gw), 0)
    same_head = (_iota((LANES, LANES), 0) >> CHUNK_LOG2) == (_iota((LANES, LANES), 1) >> CHUNK_LOG2)
    bf = lambda a: a.astype(BF16)
    nc = rows // CHUNK
    chunks = [slice(c * CHUNK, (c + 1) * CHUNK) for c in range(nc)]
    tiles = [slice(k * LANES, (k + 1) * LANES) for k in range(gw // LANES)]

    for gi in range(gps):
        grp = pl.program_id(0) * gps + gi
        cols = slice(gi * gw, (gi + 1) * gw)
        xs = xs_all[:, cols]
        bm = bm_all[:, gi * SSM_STATE:(gi + 1) * SSM_STATE]
        cm = cm_all[:, gi * SSM_STATE:(gi + 1) * SSM_STATE]
        acs_rows = at_ref[pl.ds(pl.multiple_of(grp * hpg, hpg), hpg), :]

        sel = (_iota((LANES, gw), 0) == grp * hpg + (_iota((LANES, gw), 1) >> CHUNK_LOG2)).astype(BF16)
        dt_e = _select_dot(sel, dt)
        acs_e = _select_dot(sel, acs)
        dskip_e = _select_dot(sel, par_ref[...])[2:3, :]

        a_es = [acs_e[rs] for rs in chunks]
        a_rows = [jnp.concatenate([acs_rows[h:h + 1, rs] for h in range(hpg)], axis=1) for rs in chunks]
        l_cats = [jnp.where(causal, jnp.exp(jnp.minimum(a_e - a_row, 0.0)), 0.0) for a_e, a_row in zip(a_es, a_rows)]
        c16 = [bf(cm[rs]) for rs in chunks]
        b16 = [bf(bm[rs]) for rs in chunks]
        cbs = [_dot_nt(c_c, jnp.concatenate([b_c] * hpg, axis=0)) for c_c, b_c in zip(c16, b16)]
        xdts = [xs[rs] * dt_e[rs] for rs in chunks]
        m16 = [bf(cb * l_cat) for cb, l_cat in zip(cbs, l_cats)]
        y_diags = []
        for m, xdt in zip(m16, xdts):
            parts = [_dot(m[:, ts], bf(jnp.where(same_head, jnp.concatenate([xdt[:, ts]] * 2, axis=0), 0.0)))
                     for ts in tiles]
            y_diags.append(jnp.concatenate(parts, axis=1))
        a_lasts = [a_e[CHUNK - 1:CHUNK] for a_e in a_es]
        updates = [_dot_tn(b_c, bf(xdt * jnp.exp(a_last - a_e)))
                   for b_c, xdt, a_last, a_e in zip(b16, xdts, a_lasts, a_es)]

        states = [s_ref[gi]]
        for a_last, upd in zip(a_lasts, updates):
            states.append(states[-1] * jnp.exp(a_last) + upd)
        s_ref[gi] = states[-1]

        y_offs = [_dot(c_c, bf(s)) * jnp.exp(a_e) for c_c, s, a_e in zip(c16, states[:-1], a_es)]
        for rs, y_diag, y_off in zip(chunks, y_diags, y_offs):
            y = y_diag + y_off + dskip_e * xs[rs]
            o_ref[rs, cols] = (y * _silu(z_ref[rs, cols].astype(F32))).astype(o_ref.dtype)


def _ssd(proj, dt_raw, conv_w, conv_b, par, *, rows=256, gps=8):
    t = proj.shape[0]
    rows = min(rows, t)
    xw = gps * SSM_GROUP_W
    bw = gps * SSM_STATE
    d_inner = SSM_GROUPS * SSM_GROUP_W
    xo = d_inner // xw
    bo = (2 * d_inner) // bw
    co = bo + SSM_GROUPS * SSM_STATE // bw
    hrow = lambda i: jnp.maximum(i * (rows // HALO_ROWS) - 1, 0)
    wbo = d_inner // bw
    wco = wbo + SSM_GROUPS * SSM_STATE // bw
    return pl.pallas_call(
        functools.partial(_ssd_body, rows=rows, gps=gps),
        grid=(SSM_GROUPS // gps, t // rows),
        in_specs=[pl.BlockSpec((rows, xw), lambda g, i: (i, xo + g)),
                  pl.BlockSpec((rows, bw), lambda g, i: (i, bo + g)),
                  pl.BlockSpec((rows, bw), lambda g, i: (i, co + g)),
                  pl.BlockSpec((HALO_ROWS, xw), lambda g, i: (hrow(i), xo + g)),
                  pl.BlockSpec((HALO_ROWS, bw), lambda g, i: (hrow(i), bo + g)),
                  pl.BlockSpec((HALO_ROWS, bw), lambda g, i: (hrow(i), co + g)),
                  pl.BlockSpec((rows, xw), lambda g, i: (i, g)),
                  pl.BlockSpec((rows, LANES), lambda g, i: (i, 0)),
                  pl.BlockSpec((4, xw), lambda g, i: (0, g)),
                  pl.BlockSpec((4, bw), lambda g, i: (0, wbo + g)),
                  pl.BlockSpec((4, bw), lambda g, i: (0, wco + g)),
                  pl.BlockSpec((1, xw), lambda g, i: (0, g)),
                  pl.BlockSpec((1, bw), lambda g, i: (0, wbo + g)),
                  pl.BlockSpec((1, bw), lambda g, i: (0, wco + g)),
                  pl.BlockSpec((8, LANES), lambda g, i: (0, 0))],
        out_specs=pl.BlockSpec((rows, xw), lambda g, i: (i, g)),
        out_shape=jax.ShapeDtypeStruct((t, d_inner), BF16),
        scratch_shapes=[pltpu.VMEM((gps, SSM_STATE, SSM_GROUP_W), F32), pltpu.VMEM((LANES, rows), F32)],
        compiler_params=_cparams(("arbitrary", "arbitrary")),
        name="ssd",
    )(proj, proj, proj, proj, proj, proj, proj, dt_raw, conv_w, conv_w, conv_w, conv_b, conv_b, conv_b, par)


def _store_token_tiles(tile_ref, x):
    rows, d = x.shape
    bits = lax.bitcast_convert_type(x.astype(BF16).astype(F32), U32)
    packed = (bits[:, :d // 2] >> 16) | (bits[:, d // 2:] & jnp.uint32(0xFFFF0000))
    for s in range(TOKEN_TILE_ROWS):
        tile_ref[pl.ds(s, rows, stride=TOKEN_TILE_ROWS), :] = packed[:, s * LANES:(s + 1) * LANES]


def _load_token_tiles(tile_ref, rows):
    lo, hi = [], []
    for s in range(TOKEN_TILE_ROWS):
        p = tile_ref[pl.ds(s, rows, stride=TOKEN_TILE_ROWS), :]
        lo.append(lax.bitcast_convert_type(p << 16, F32))
        hi.append(lax.bitcast_convert_type(p & jnp.uint32(0xFFFF0000), F32))
    return lo, hi


def _lane_pick(lane, idx, vals):
    return jnp.sum(jnp.where(lane == idx, vals, 0.0), axis=-1, keepdims=True)


def _router_body(x_ref, g_ref, sc_ref, sh_ref, w_ref, b_ref, hp_ref, route_ref, cnt_ref, carry_ref, *, tm):
    i = pl.program_id(0)

    @pl.when(i == 0)
    def _():
        carry_ref[...] = jnp.zeros_like(carry_ref)

    h = _adaln_rows(x_ref[...], g_ref[...], sc_ref[...], sh_ref[...])
    logits = _dot3(h, w_ref[...]) + b_ref[...]
    lane = _iota((tm, LANES), 1).astype(F32)
    neg = -jnp.inf
    gl = jnp.where(lane < N_GROUPS, logits, neg)
    gmax = jnp.max(gl, axis=-1, keepdims=True)
    grp_p = 1.0 / jnp.sum(jnp.exp(gl - gmax), axis=-1, keepdims=True)
    gidx = jnp.min(jnp.where(gl == gmax, lane, float(LANES)), axis=-1, keepdims=True)
    lo = N_GROUPS + EXPERTS_PER_GROUP * gidx
    el = jnp.where((lane >= lo) & (lane < lo + EXPERTS_PER_GROUP), logits, neg)
    m1 = jnp.max(el, axis=-1, keepdims=True)
    i1 = jnp.min(jnp.where(el == m1, lane, float(LANES)), axis=-1, keepdims=True)
    el2 = jnp.where(lane == i1, neg, el)
    m2 = jnp.max(el2, axis=-1, keepdims=True)
    i2 = jnp.min(jnp.where(el2 == m2, lane, float(LANES)), axis=-1, keepdims=True)
    tt = jnp.exp(m2 - m1)
    g1 = grp_p / (1.0 + tt)
    g2 = g1 * tt
    e1 = i1 - N_GROUPS
    e2 = i2 - N_GROUPS

    onehot = ((lane == e1) | (lane == e2)).astype(F32)
    strict = (_iota((tm, tm), 1) < _iota((tm, tm), 0)).astype(BF16)
    rank = _dot(strict, onehot.astype(BF16)) + carry_ref[0:1, :]
    r1 = _lane_pick(lane, e1, rank)
    r2 = _lane_pick(lane, e2, rank)
    carry_ref[...] = carry_ref[...] + jnp.sum(onehot, axis=0, keepdims=True)
    cnt_ref[...] = carry_ref[...]

    route = jnp.where(lane == 0, e1, jnp.where(lane == 1, e2, jnp.where(lane == 2, r1, jnp.where(
        lane == 3, r2, jnp.where(lane == 4, g1, jnp.where(lane == 5, g2, 0.0))))))
    route_ref[...] = route

    _store_token_tiles(hp_ref, h)


def _router(x, g, scale, shift, w_route, b_route, *, tm=512):
    t, d = x.shape
    vec = pl.BlockSpec((1, d), lambda i: (0, 0))
    return pl.pallas_call(
        functools.partial(_router_body, tm=tm),
        grid=(t // tm,),
        in_specs=[pl.BlockSpec((tm, d), lambda i: (i, 0)), vec, vec, vec,
                  pl.BlockSpec((d, LANES), lambda i: (0, 0)),
                  pl.BlockSpec((1, LANES), lambda i: (0, 0))],
        out_specs=[pl.BlockSpec((tm * TOKEN_TILE_ROWS, LANES), lambda i: (i, 0)),
                   pl.BlockSpec((tm, LANES), lambda i: (i, 0)),
                   pl.BlockSpec((8, LANES), lambda i: (0, 0))],
        out_shape=[jax.ShapeDtypeStruct((t * TOKEN_TILE_ROWS, LANES), U32),
                   jax.ShapeDtypeStruct((t, LANES), F32),
                   jax.ShapeDtypeStruct((8, LANES), F32)],
        scratch_shapes=[pltpu.VMEM((8, LANES), F32)],
        compiler_params=_cparams(("arbitrary",)),
        name="moe_router",
    )(x, g, scale, shift, w_route, b_route)


def _dest_body(route_ref, cnt_ref, dest_ref, blk_ref, *, tm, n_blocks_pad):
    cnt = cnt_ref[...]
    padded = jnp.floor((cnt + (MOE_ROWS - 1)) * (1.0 / MOE_ROWS)) * MOE_ROWS
    upper = (_iota((LANES, LANES), 0) <= _iota((LANES, LANES), 1)).astype(F32)
    pad_end = _dot(padded, upper, _HI)
    pad_start = (pad_end - padded)[0:1, :]
    route = route_ref[...]
    lane = _iota((tm, LANES), 1).astype(F32)
    e1, e2, r1, r2 = route[:, 0:1], route[:, 1:2], route[:, 2:3], route[:, 3:4]
    d1 = _lane_pick(lane, e1, pad_start) + r1
    d2 = _lane_pick(lane, e2, pad_start) + r2
    dest_ref[...] = jnp.where(lane == 0, d1, jnp.where(lane == 1, d2, 0.0)).astype(I32)

    lane_b = _iota((n_blocks_pad, LANES), 1)
    first_row = (_iota((n_blocks_pad, LANES), 0) * MOE_ROWS).astype(F32)
    done = ((pad_end[0:1, :] <= first_row) & (lane_b < N_EXPERTS)).astype(F32)
    blk_e = jnp.minimum(jnp.sum(done, axis=-1, keepdims=True), float(N_EXPERTS - 1))
    n_valid = pad_end[0:1, N_EXPERTS - 1:N_EXPERTS] * (1.0 / MOE_ROWS)
    on_diag = lane_b == _iota((n_blocks_pad, LANES), 0)
    seg_fill = jnp.sum(jnp.where(on_diag, pad_start + cnt[0:1, :], 0.0), axis=-1, keepdims=True)
    seg_end = jnp.sum(jnp.where(on_diag, pad_end[0:1, :], 0.0), axis=-1, keepdims=True)
    blk_ref[...] = jnp.where(lane_b == 0, blk_e, jnp.where(lane_b == 1, n_valid, jnp.where(
        lane_b == 2, seg_fill, jnp.where(lane_b == 3, seg_end, 0.0)))).astype(I32)


def _dest(route, counts, n_blocks_pad, *, tm=512):
    t = route.shape[0]
    assert t % tm == 0
    return pl.pallas_call(
        functools.partial(_dest_body, tm=tm, n_blocks_pad=n_blocks_pad),
        grid=(t // tm,),
        in_specs=[pl.BlockSpec((tm, LANES), lambda i: (i, 0)),
                  pl.BlockSpec((8, LANES), lambda i: (0, 0))],
        out_specs=[pl.BlockSpec((tm, LANES), lambda i: (i, 0)),
                   pl.BlockSpec((n_blocks_pad, LANES), lambda i: (0, 0))],
        out_shape=[jax.ShapeDtypeStruct((t, LANES), I32),
                   jax.ShapeDtypeStruct((n_blocks_pad, LANES), I32)],
        compiler_params=_cparams(("arbitrary",)),
        name="moe_dest",
    )(route, counts)


def _invert_body(dest_ref, fill_ref, end_ref, tok_ref, *, n_assign, n_rows):
    def fill8(g, c):
        for r in range(8):
            tok_ref[g * 8 + r] = 0
        return c

    def fill_expert(e, c):
        return lax.fori_loop(fill_ref[e] >> 3, end_ref[e] >> 3, fill8, c)

    lax.fori_loop(0, N_EXPERTS, fill_expert, 0)
    lax.fori_loop(end_ref[N_EXPERTS - 1] >> 3, n_rows // 8, fill8, 0)

    def scatter(a, c):
        tok_ref[dest_ref[a]] = a >> 1
        return c

    lax.fori_loop(0, n_assign, scatter, 0, unroll=8)


def _invert(dest_flat, seg_fill, seg_end, n_rows):
    n_assign = dest_flat.shape[0]
    return pl.pallas_call(
        functools.partial(_invert_body, n_assign=n_assign, n_rows=n_rows),
        grid_spec=pltpu.PrefetchScalarGridSpec(
            num_scalar_prefetch=3,
            grid=(1,),
            in_specs=[],
            out_specs=pl.BlockSpec(memory_space=pltpu.SMEM),
        ),
        out_shape=jax.ShapeDtypeStruct((n_rows,), I32),
        compiler_params=_cparams(("arbitrary",)),
        name="moe_invert",
    )(dest_flat, seg_fill, seg_end)


def _expert_body(be_ref, nv_ref, tok_ref, hp_ref, w1_ref, w3_ref, w2_ref, y_ref, xbuf_ref, sem, wst1_ref, wst3_ref,
                 wst2_ref, wsem, w1b_ref, w3b_ref, w2b_ref, *, layer):
    b = pl.program_id(0)
    n_blocks = pl.num_programs(0)
    wst_refs = (wst1_ref, wst3_ref, wst2_ref)
    n_valid = nv_ref[0]
    valid = b < n_valid
    tr = TOKEN_TILE_ROWS
    expert = be_ref[b]
    first_of_expert = valid & ((b == 0) | (expert != be_ref[jnp.maximum(b - 1, 0)]))

    def weight_copies(e):
        return [pltpu.make_async_copy(w_ref.at[layer, e], st_ref, wsem.at[i])
                for i, (w_ref, st_ref) in enumerate(zip((w1_ref, w3_ref, w2_ref), wst_refs))]

    @pl.when((b == 0) & valid)
    def _():
        for c in weight_copies(expert):
            c.start()

    def gather_rows(blk, slot):
        def one(j, c):
            tok = tok_ref[blk * MOE_ROWS + j]
            pltpu.make_async_copy(hp_ref.at[pl.ds(tok * tr, tr)], xbuf_ref.at[slot, pl.ds(j * tr, tr)],
                                  sem.at[slot]).start()
            return c

        lax.fori_loop(0, MOE_ROWS, one, 0, unroll=8)

    @pl.when(b == 0)
    def _():
        for ahead in range(GATHER_AHEAD):
            @pl.when(ahead < n_valid)
            def _(ahead=ahead):
                gather_rows(ahead, ahead)

    @pl.when(b + GATHER_AHEAD < n_valid)
    def _():
        gather_rows(b + GATHER_AHEAD, (b + GATHER_AHEAD) % (GATHER_AHEAD + 1))

    @pl.when(first_of_expert)
    def _():
        for c in weight_copies(expert):
            c.wait()
        for wb_ref, st_ref in zip((w1b_ref, w3b_ref, w2b_ref), wst_refs):
            wb_ref[...] = st_ref[...].astype(BF16)
        nxt = lax.while_loop(lambda j: (j < n_valid) & (be_ref[jnp.minimum(j, n_blocks - 1)] == expert),
                             lambda j: j + 1, b + 1)

        @pl.when(nxt < n_valid)
        def _():
            for c in weight_copies(be_ref[jnp.minimum(nxt, n_blocks - 1)]):
                c.start()

    @pl.when(jnp.logical_not(valid))
    def _():
        y_ref[...] = jnp.zeros_like(y_ref)

    @pl.when(valid)
    def _():
        slot = b % (GATHER_AHEAD + 1)
        pltpu.make_async_copy(hp_ref.at[pl.ds(0, MOE_ROWS * tr)], xbuf_ref.at[slot], sem.at[slot]).wait()
        lo, hi = _load_token_tiles(xbuf_ref.at[slot], MOE_ROWS)
        x = jnp.concatenate(lo + hi, axis=1).astype(BF16)
        a = (_silu(_dot(x, w1b_ref[...])) * _dot(x, w3b_ref[...])).astype(BF16)
        _store_token_tiles(y_ref, _dot(a, w2b_ref[...]))


def _experts(block_expert, n_valid, tok_rows, h_packed, w1, w3, w2, *, layer):
    n_blocks = block_expert.shape[0]
    _, _, d, f = w1.shape
    anywhere = pl.BlockSpec(memory_space=pl.ANY)
    return pl.pallas_call(
        functools.partial(_expert_body, layer=layer),
        grid_spec=pltpu.PrefetchScalarGridSpec(
            num_scalar_prefetch=3,
            grid=(n_blocks,),
            in_specs=[anywhere, anywhere, anywhere, anywhere],
            out_specs=pl.BlockSpec((MOE_ROWS * TOKEN_TILE_ROWS, LANES), lambda b, be, nv, tk: (b, 0)),
            scratch_shapes=[pltpu.VMEM((GATHER_AHEAD + 1, MOE_ROWS * TOKEN_TILE_ROWS, LANES), U32),
                            pltpu.SemaphoreType.DMA((GATHER_AHEAD + 1,)),
                            pltpu.VMEM((d, f), F32), pltpu.VMEM((d, f), F32), pltpu.VMEM((f, d), F32),
                            pltpu.SemaphoreType.DMA((3,)),
                            pltpu.VMEM((d, f), BF16), pltpu.VMEM((d, f), BF16), pltpu.VMEM((f, d), BF16)],
        ),
        out_shape=jax.ShapeDtypeStruct((n_blocks * MOE_ROWS * TOKEN_TILE_ROWS, LANES), U32),
        compiler_params=_cparams(("arbitrary",)),
        name="moe_experts",
    )(block_expert, n_valid, tok_rows, h_packed, w1, w3, w2)


def _combine_body(dest_ref, y_ref, x_ref, route_ref, gate_ref, fg_ref, o_ref, buf_ref, sem, *, tm, final_norm):
    i = pl.program_id(0)
    yr = TOKEN_TILE_ROWS

    def gather_rows(blk, slot):
        def one(j, c):
            for k in range(2):
                d = dest_ref[(blk * tm + j) * 2 + k]
                pltpu.make_async_copy(y_ref.at[pl.ds(d * yr, yr)], buf_ref.at[slot, k, pl.ds(j * yr, yr)],
                                      sem.at[slot, k]).start()
            return c

        lax.fori_loop(0, tm, one, 0, unroll=4)

    n_steps = pl.num_programs(0)

    @pl.when(i == 0)
    def _():
        for ahead in range(COMBINE_AHEAD):
            @pl.when(ahead < n_steps)
            def _(ahead=ahead):
                gather_rows(ahead, ahead)

    @pl.when(i + COMBINE_AHEAD < n_steps)
    def _():
        gather_rows(i + COMBINE_AHEAD, (i + COMBINE_AHEAD) % (COMBINE_AHEAD + 1))

    slot = i % (COMBINE_AHEAD + 1)
    for k in range(2):
        pltpu.make_async_copy(y_ref.at[pl.ds(0, tm * yr)], buf_ref.at[slot, k], sem.at[slot, k]).wait()

    route = route_ref[...]
    g1, g2 = route[:, 4:5], route[:, 5:6]
    lo1, hi1 = _load_token_tiles(buf_ref.at[slot, 0], tm)
    lo2, hi2 = _load_token_tiles(buf_ref.at[slot, 1], tm)
    half = x_ref.shape[1] // 2
    for s in range(yr):
        for base, y1, y2 in ((0, lo1[s], lo2[s]), (half, hi1[s], hi2[s])):
            cs = slice(base + s * LANES, base + (s + 1) * LANES)
            o_ref[:, cs] = x_ref[:, cs] + gate_ref[:, cs] * (g1 * y1 + g2 * y2)
    if final_norm:
        xo = o_ref[...]
        o_ref[...] = xo * lax.rsqrt(jnp.mean(xo * xo, axis=-1, keepdims=True) + EPS) * fg_ref[...]


def _combine(dest_flat, y_rows, x, route, gate, final_g, *, final_norm, tm=256):
    t, d = x.shape
    return pl.pallas_call(
        functools.partial(_combine_body, tm=tm, final_norm=final_norm),
        grid_spec=pltpu.PrefetchScalarGridSpec(
            num_scalar_prefetch=1,
            grid=(t // tm,),
            in_specs=[pl.BlockSpec(memory_space=pl.ANY),
                      pl.BlockSpec((tm, d), lambda i, dr: (i, 0)),
                      pl.BlockSpec((tm, LANES), lambda i, dr: (i, 0)),
                      pl.BlockSpec((1, d), lambda i, dr: (0, 0)),
                      pl.BlockSpec((1, d), lambda i, dr: (0, 0))],
            out_specs=pl.BlockSpec((tm, d), lambda i, dr: (i, 0)),
            scratch_shapes=[pltpu.VMEM((COMBINE_AHEAD + 1, 2, tm * TOKEN_TILE_ROWS, LANES), U32),
                            pltpu.SemaphoreType.DMA((COMBINE_AHEAD + 1, 2))],
        ),
        out_shape=jax.ShapeDtypeStruct((t, d), F32),
        compiler_params=_cparams(("arbitrary",)),
        name="moe_combine",
    )(dest_flat, y_rows, x, route, gate, final_g)


def _moe(x, g, scale, shift, gate, w_group, b_group, w_expert, b_expert, w1, w3, w2, final_g, *, layer, final_norm):
    t, d = x.shape
    pad = LANES - N_GROUPS - N_EXPERTS
    w_route = jnp.concatenate([w_group, w_expert, jnp.zeros((d, pad), F32)], axis=1)
    b_route = jnp.concatenate([b_group, b_expert, jnp.zeros((pad,), F32)]).reshape(1, LANES)
    h_packed, route, counts = _router(x, g, scale, shift, w_route, b_route)
    n_assign = 2 * t
    n_blocks = -(-(n_assign + N_EXPERTS * (MOE_ROWS - 1)) // MOE_ROWS)
    n_blocks_pad = -(-n_blocks // 8) * 8
    dest, blk = _dest(route, counts, n_blocks_pad)
    dest_flat = dest[:, :2].reshape(n_assign)
    block_expert = blk[:n_blocks, 0]
    n_valid = blk[0, 1:2]
    tok_rows = _invert(dest_flat, blk[:N_EXPERTS, 2], blk[:N_EXPERTS, 3], n_blocks * MOE_ROWS)
    y_rows = _experts(block_expert, n_valid, tok_rows, h_packed, w1, w3, w2, layer=layer)
    return _combine(dest_flat, y_rows, x, route, gate, final_g, final_norm=final_norm)


def _pad_rows(w, rows):
    return jnp.pad(w, ((0, rows - w.shape[0]), (0, 0)))


def _lane_row(v, offset):
    return jnp.zeros((LANES,), F32).at[offset:offset + v.shape[0]].set(v)


def kernel(x, c, w_mod, b_mod, norm_g, final_norm_g, e_w_in, e_conv_qkv, e_a_log, e_dt_bias, e_head_norm_g, e_conf_dw, e_conf_dw_b, e_conf_ln_g, e_conf_ln_b, e_w_out, o_w_in, o_conv_w, o_conv_b, o_dt_bias, o_a_log, o_d_skip, o_norm_g, o_w_out, moe_w_group, moe_b_group, moe_w_expert, moe_b_expert, moe_w1, moe_w3, moe_w2):
    bsz, t, d = x.shape
    assert bsz == 1 and c.shape == (1, d)
    xt = x[0]
    mod = _modulation(c[0], w_mod, b_mod)

    def mod_parts(l):
        return mod[l, :, :d], mod[l, :, d:2 * d], mod[l, :, 2 * d:]

    row = lambda v: v.reshape(1, -1)
    qkv_w = 3 * GDN_HEADS * HEAD_W
    z_end = qkv_w + GDN_HEADS * HEAD_W

    shift, scale, gate = mod_parts(0)
    w_in = e_w_in[0].T
    tn = 512
    w_glu = w_in[z_end + 2 * GDN_HEADS:]
    w_ba = _pad_rows(w_in[z_end:z_end + 2 * GDN_HEADS], LANES)
    proj0, ba = _norm_matmul(xt, row(norm_g[0, 0]), scale, shift,
                             [(w_in, z_end // tn), (w_glu, w_glu.shape[0] // tn)], w_ba, tn=tn)
    gdn_par = jnp.zeros((8, LANES), F32).at[0].set(_lane_row(e_a_log[0], GDN_HEADS)).at[1].set(
        _lane_row(e_dt_bias[0], GDN_HEADS))
    a_out = _gdn(proj0, ba, e_conv_qkv[0], gdn_par, row(e_head_norm_g[0]))
    b_out = _conformer(proj0, e_conf_dw[0], row(e_conf_dw_b[0]), row(e_conf_ln_g[0]), row(e_conf_ln_b[0]))
    xt = _out_proj0(a_out, b_out, e_w_out[0], xt, gate)
    shift, scale, gate = mod_parts(1)
    xt = _moe(xt, row(norm_g[0, 1]), scale, shift, gate, moe_w_group[0], moe_b_group[0], moe_w_expert[0],
              moe_b_expert[0], moe_w1, moe_w3, moe_w2, row(final_norm_g), layer=0, final_norm=False)

    shift, scale, gate = mod_parts(2)
    w_in = o_w_in[0].T
    n_main = 2 * SSM_GROUPS * SSM_GROUP_W + 2 * SSM_GROUPS * SSM_STATE
    tn = 1024
    proj1, dt_raw = _norm_matmul(xt, row(norm_g[1, 0]), scale, shift, [(w_in, n_main // tn)],
                                 _pad_rows(w_in[n_main:], LANES), tn=tn)
    ssd_par = jnp.zeros((8, LANES), F32).at[0].set(_lane_row(o_dt_bias[0], 0)).at[1].set(
        _lane_row(o_a_log[0], 0)).at[2].set(_lane_row(o_d_skip[0], 0))
    y = _ssd(proj1, dt_raw, o_conv_w[0], row(o_conv_b[0]), ssd_par)
    xt = _out_proj1(y, row(o_norm_g[0]), o_w_out[0], xt, gate)
    shift, scale, gate = mod_parts(3)
    xt = _moe(xt, row(norm_g[1, 1]), scale, shift, gate, moe_w_group[1], moe_b_group[1], moe_w_expert[1],
              moe_b_expert[1], moe_w1, moe_w3, moe_w2, row(final_norm_g), layer=1, final_norm=True)
    return xt[None]
```

```python
import functools

import jax
import jax.numpy as jnp
from jax import lax
from jax.experimental import pallas as pl
from jax.experimental.pallas import tpu as pltpu

F32 = jnp.float32
BF16 = jnp.bfloat16
I32 = jnp.int32
U32 = jnp.uint32

EPS = 1e-6
LANES = 128
CHUNK = 64
CHUNK_LOG2 = 6
GDN_HEADS = 8
HEAD_W = 128
CONF_CH = 1024
CONF_K = 31
SSM_GROUPS = 8
SSM_GROUP_W = 512
SSM_HEADDIM = 64
SSM_STATE = 128
N_EXPERTS = 32
EXPERTS_PER_GROUP = 8
N_GROUPS = 4
MOE_ROWS = 256
GATHER_AHEAD = 6
COMBINE_AHEAD = 1
TOKEN_TILE_ROWS = 8
VMEM_LIMIT = 56 * 1024 * 1024
PROLOGUE_ROWS = 128
HALO_ROWS = 16
CONF_HALO_ROWS = 32

_HI = lax.Precision.HIGHEST


def _cparams(sem):
    return pltpu.CompilerParams(dimension_semantics=sem, vmem_limit_bytes=VMEM_LIMIT)


def _dot(a, b, precision=None):
    return jnp.dot(a, b, precision=precision, preferred_element_type=F32)


def _dot_nt(a, b, precision=None):
    return lax.dot_general(a, b, (((1,), (1,)), ((), ())), precision=precision, preferred_element_type=F32)


def _dot_tn(a, b, precision=None):
    return lax.dot_general(a, b, (((0,), (0,)), ((), ())), precision=precision, preferred_element_type=F32)


def _split_bf16(x):
    hi = x.astype(BF16)
    return hi, (x - hi.astype(F32)).astype(BF16)


def _dot3(a, b):
    ah, al = _split_bf16(a)
    bh, bl = _split_bf16(b)
    return _dot(ah, bh) + (_dot(ah, bl) + _dot(al, bh))


def _select_dot(sel, x, left=False):
    p1 = x.astype(BF16)
    r1 = x - p1.astype(F32)
    p2 = r1.astype(BF16)
    p3 = (r1 - p2.astype(F32)).astype(BF16)
    if left:
        return _dot(sel, p1) + (_dot(sel, p2) + _dot(sel, p3))
    return _dot(p1, sel) + (_dot(p2, sel) + _dot(p3, sel))


def _silu(x):
    return x * jax.nn.sigmoid(x)


def _softplus(x):
    return jnp.maximum(x, 0.0) + jnp.log1p(jnp.exp(-jnp.abs(x)))


def _iota(shape, dim):
    return lax.broadcasted_iota(I32, shape, dim)


def _mod_body(c_ref, w_ref, b_ref, o_ref):
    c = c_ref[...]
    o_ref[...] = jnp.sum(w_ref[...] * _silu(c), axis=0, keepdims=True) + b_ref[...]


def _modulation(c, w_mod, b_mod):
    nl, d, n = w_mod.shape
    tn = 768
    return pl.pallas_call(
        _mod_body,
        grid=(nl, n // tn),
        in_specs=[
            pl.BlockSpec((d, 1), lambda l, j: (0, 0)),
            pl.BlockSpec((None, d, tn), lambda l, j: (l, 0, j)),
            pl.BlockSpec((None, 1, tn), lambda l, j: (l, 0, j)),
        ],
        out_specs=pl.BlockSpec((None, 1, tn), lambda l, j: (l, 0, j)),
        out_shape=jax.ShapeDtypeStruct((nl, 1, n), F32),
        compiler_params=_cparams(("arbitrary", "arbitrary")),
        name="adaln_mod",
    )(c.reshape(d, 1), w_mod, b_mod.reshape(nl, 1, n))


def _adaln_rows(x, g, scale, shift):
    r = lax.rsqrt(jnp.mean(x * x, axis=-1, keepdims=True) + EPS)
    return x * r * (g * (1.0 + scale)) + shift


def _norm_mm_body(*refs, starts):
    x_ref, g_ref, sc_ref, sh_ref, ws_ref = refs[:5]
    w_refs = refs[5:5 + len(starts)]
    o_ref, os_ref, h_ref = refs[5 + len(starts):]
    j = pl.program_id(1)

    @pl.when(j == 0)
    def _():
        ws = ws_ref[...].astype(BF16)

        def norm_rows(c, carry):
            rs = pl.ds(pl.multiple_of(c * PROLOGUE_ROWS, PROLOGUE_ROWS), PROLOGUE_ROWS)
            h = _adaln_rows(x_ref[rs, :], g_ref[...], sc_ref[...], sh_ref[...]).astype(BF16)
            h_ref[rs, :] = h
            os_ref[rs, :] = _dot_nt(h, ws)
            return carry

        lax.fori_loop(0, x_ref.shape[0] // PROLOGUE_ROWS, norm_rows, 0)

    bounds = list(starts[1:]) + [None]
    for w_ref, lo, hi in zip(w_refs, starts, bounds):
        in_seg = (j >= lo) if hi is None else ((j >= lo) & (j < hi))

        @pl.when(in_seg)
        def _(w_ref=w_ref):
            o_ref[...] = _dot_nt(h_ref[...], w_ref[...].astype(BF16)).astype(o_ref.dtype)


def _norm_matmul(x, g, scale, shift, segments, w_small, *, tm=1024, tn=512):
    t, d = x.shape
    tm = min(tm, t)
    starts, specs, n_main = [], [], 0
    for w, nb in segments:
        lo = n_main // tn
        starts.append(lo)
        specs.append(pl.BlockSpec((tn, d), lambda i, j, lo=lo, nb=nb: (jnp.clip(j - lo, 0, nb - 1), 0)))
        n_main += nb * tn
    vec = pl.BlockSpec((1, d), lambda i, j: (0, 0))
    return pl.pallas_call(
        functools.partial(_norm_mm_body, starts=tuple(starts)),
        grid=(t // tm, n_main // tn),
        in_specs=[pl.BlockSpec((tm, d), lambda i, j: (i, 0), pipeline_mode=pl.Buffered(1)), vec, vec, vec,
                  pl.BlockSpec((LANES, d), lambda i, j: (0, 0))] + specs,
        out_specs=[pl.BlockSpec((tm, tn), lambda i, j: (i, j)),
                   pl.BlockSpec((tm, LANES), lambda i, j: (i, 0))],
        out_shape=[jax.ShapeDtypeStruct((t, n_main), BF16), jax.ShapeDtypeStruct((t, LANES), F32)],
        scratch_shapes=[pltpu.VMEM((tm, d), BF16)],
        compiler_params=_cparams(("arbitrary", "arbitrary")),
        name="adaln_in_proj",
    )(x, g, scale, shift, w_small, *[w for w, _ in segments])


def _causal_conv(halo, x, w, taps):
    hr = halo.shape[0]
    n = x.shape[0]
    xe = jnp.concatenate([halo, x], axis=0)
    rolled = {0: xe}
    acc = None
    for j in range(taps):
        s = taps - 1 - j
        a, b = divmod(s, 8)
        if b not in rolled:
            rolled[b] = pltpu.roll(xe, b, axis=0)
        term = rolled[b][hr - 8 * a:hr - 8 * a + n] * w[j:j + 1]
        acc = term if acc is None else acc + term
    return acc


def _short_conv_bf16(halo, x16, w):
    n = x16.shape[0]
    x = x16.astype(F32)
    lag = _iota((n, n), 0) - _iota((n, n), 1)
    shifts = jnp.concatenate([(lag == s).astype(BF16) for s in (1, 2, 3)], axis=0)
    moved = _dot(shifts, x16)
    acc = x * w[3:4] + moved[0:n] * w[2:3] + moved[n:2 * n] * w[1:2] + moved[2 * n:3 * n] * w[0:1]
    head = _causal_conv(halo, x[0:8], w, 4)
    return jnp.concatenate([head, acc[8:]], axis=0)


def _unit_lower_inverses(a_mats, row, col):
    n = a_mats[0].shape[0]
    eye = (row == col).astype(F32)
    level = ((row >> 1) == (col >> 1)) & (col < row)
    invs = [eye - jnp.where(level, a, 0.0) for a in a_mats]
    k = 1
    while (1 << k) < n:
        level = ((row >> (k + 1)) == (col >> (k + 1))) & (((row >> k) & 1) == 1) & (((col >> k) & 1) == 0)
        inv16 = [inv.astype(BF16) for inv in invs]
        ys = [_dot(i16, jnp.where(level, a, 0.0).astype(BF16)) for i16, a in zip(inv16, a_mats)]
        invs = [inv - _dot(y.astype(BF16), i16) for inv, y, i16 in zip(invs, ys, inv16)]
        k += 1
    return invs


def _gdn_body(q_ref, k_ref, v_ref, qh_ref, kh_ref, vh_ref, z_ref, ba_ref, cwq_ref, cwk_ref, cwv_ref,
              par_ref, hg_ref, o_ref, s_ref, *, hb, rows):
    hblk = pl.program_id(0)
    ir = pl.program_id(1)

    @pl.when(ir == 0)
    def _():
        s_ref[...] = jnp.zeros_like(s_ref)

    def conv_silu(x_ref, halo_ref, w_ref):
        halo = jnp.where(ir > 0, halo_ref[...].astype(F32), 0.0)
        return _silu(_short_conv_bf16(halo, x_ref[...], w_ref[...]))

    qc = conv_silu(q_ref, qh_ref, cwq_ref)
    kc = conv_silu(k_ref, kh_ref, cwk_ref)
    vc = conv_silu(v_ref, vh_ref, cwv_ref)

    ba = ba_ref[...]
    beta_all = jax.nn.sigmoid(ba)
    g_all = -jnp.exp(par_ref[0:1, :]) * _softplus(ba + par_ref[1:2, :])
    rr = _iota((rows, rows), 0)
    cc = _iota((rows, rows), 1)
    chunk_tril = ((rr >> CHUNK_LOG2) == (cc >> CHUNK_LOG2)) & (cc <= rr)
    gcs_all = _select_dot(chunk_tril.astype(BF16), g_all, left=True)
    gcs_t = gcs_all.T

    row = _iota((CHUNK, CHUNK), 0)
    col = _iota((CHUNK, CHUNK), 1)
    scale = HEAD_W ** -0.5
    hg = hg_ref[...]
    bf = lambda a: a.astype(BF16)
    lane = _iota((rows, LANES), 1)
    sub = _iota((LANES, rows), 0)
    nc = rows // CHUNK

    heads = []
    for h in range(hb):
        hs = slice(h * HEAD_W, (h + 1) * HEAD_W)
        head = hblk * hb + h
        q = qc[:, hs]
        k = kc[:, hs]
        heads.append(dict(
            q=q * lax.rsqrt(jnp.sum(q * q, axis=-1, keepdims=True) + EPS) * scale,
            k=k * lax.rsqrt(jnp.sum(k * k, axis=-1, keepdims=True) + EPS),
            v=vc[:, hs],
            beta=jnp.sum(jnp.where(lane == head, beta_all, 0.0), axis=-1, keepdims=True),
            gcs=jnp.sum(jnp.where(lane == head + GDN_HEADS, gcs_all, 0.0), axis=-1, keepdims=True),
            gcs_row=jnp.sum(jnp.where(sub == head + GDN_HEADS, gcs_t, 0.0), axis=0, keepdims=True)))

    items = []
    for c in range(nc):
        rs = slice(c * CHUNK, (c + 1) * CHUNK)
        for hd in heads:
            qq, kk, vv, beta, gcs = hd["q"][rs], hd["k"][rs], hd["v"][rs], hd["beta"][rs], hd["gcs"][rs]
            decay = jnp.where(col <= row, jnp.exp(jnp.minimum(gcs - hd["gcs_row"][:, rs], 0.0)), 0.0)
            kb = kk * beta
            eg = jnp.exp(gcs)
            g_last = gcs[CHUNK - 1:CHUNK]
            items.append(dict(decay=decay, kb16=bf(kb), k16=bf(kk), q16=bf(qq),
                              rhs16=bf(jnp.concatenate([vv * beta, kb * eg], axis=-1)),
                              q_dec16=bf(qq * eg), k_dec16=bf(kk * jnp.exp(g_last - gcs)), cd=jnp.exp(g_last)))
    a_mats = [jnp.where(col < row, _dot_nt(it["kb16"], it["k16"]) * it["decay"], 0.0) for it in items]
    qks = [bf(_dot_nt(it["q16"], it["k16"]) * it["decay"]) for it in items]
    t_invs = _unit_lower_inverses(a_mats, row, col)
    sols = [_dot(bf(t), it["rhs16"]) for t, it in zip(t_invs, items)]

    states = [s_ref[h] for h in range(hb)]
    for c in range(nc):
        rs = slice(c * CHUNK, (c + 1) * CHUNK)
        cur = slice(c * hb, (c + 1) * hb)
        s16 = [bf(s) for s in states]
        us = [so[:, :HEAD_W] - _dot(bf(so[:, HEAD_W:]), s) for so, s in zip(sols[cur], s16)]
        outs = [_dot(it["q_dec16"], s) + _dot(qk, bf(u)) for it, s, qk, u in zip(items[cur], s16, qks[cur], us)]
        states = [s * it["cd"] + _dot_tn(it["k_dec16"], bf(u)) for s, it, u in zip(states, items[cur], us)]
        for h, o in enumerate(outs):
            hs = slice(h * HEAD_W, (h + 1) * HEAD_W)
            o = o * lax.rsqrt(jnp.mean(o * o, axis=-1, keepdims=True) + EPS) * hg * _silu(z_ref[rs, hs].astype(F32))
            o_ref[rs, hs] = o.astype(o_ref.dtype)
    for h in range(hb):
        s_ref[h] = states[h]


def _gdn(proj, ba, conv_w, par, head_g, *, hb=GDN_HEADS, rows=256):
    t = proj.shape[0]
    rows = min(rows, t)
    w = hb * HEAD_W
    per = (GDN_HEADS * HEAD_W) // w

    def sec(k):
        return pl.BlockSpec((rows, w), lambda h, i, k=k: (i, k * per + h))

    def halo(k):
        return pl.BlockSpec((HALO_ROWS, w),
                            lambda h, i, k=k: (jnp.maximum(i * (rows // HALO_ROWS) - 1, 0), k * per + h))

    def cw(k):
        return pl.BlockSpec((4, w), lambda h, i, k=k: (0, k * per + h))

    return pl.pallas_call(
        functools.partial(_gdn_body, hb=hb, rows=rows),
        grid=(GDN_HEADS // hb, t // rows),
        in_specs=[sec(0), sec(1), sec(2), halo(0), halo(1), halo(2), sec(3),
                  pl.BlockSpec((rows, LANES), lambda h, i: (i, 0)),
                  cw(0), cw(1), cw(2),
                  pl.BlockSpec((8, LANES), lambda h, i: (0, 0)),
                  pl.BlockSpec((1, HEAD_W), lambda h, i: (0, 0))],
        out_specs=pl.BlockSpec((rows, w), lambda h, i: (i, h)),
        out_shape=jax.ShapeDtypeStruct((t, GDN_HEADS * HEAD_W), BF16),
        scratch_shapes=[pltpu.VMEM((hb, HEAD_W, HEAD_W), F32)],
        compiler_params=_cparams(("arbitrary", "arbitrary")),
        name="gated_deltanet",
    )(proj, proj, proj, proj, proj, proj, proj, ba, conv_w, conv_w, conv_w, par, head_g)


def _conf_body(x_ref, halo_ref, w_ref, b_ref, g_ref, lb_ref, o_ref):
    ir = pl.program_id(0)

    def glu(v):
        v = v.astype(F32)
        return v[:, :CONF_CH] * jax.nn.sigmoid(v[:, CONF_CH:])

    halo = jnp.where(ir > 0, glu(halo_ref[...]), 0.0)
    u = _causal_conv(halo, glu(x_ref[...]), w_ref[...], CONF_K) + b_ref[...]
    mu = jnp.mean(u, axis=-1, keepdims=True)
    uc = u - mu
    var = jnp.mean(uc * uc, axis=-1, keepdims=True)
    y = uc * lax.rsqrt(var + EPS) * g_ref[...] + lb_ref[...]
    o_ref[...] = _silu(y).astype(o_ref.dtype)


def _conformer(proj, dw, dw_b, ln_g, ln_b, *, rows=256):
    t = proj.shape[0]
    rows = min(rows, t)
    glu_blk = (4 * GDN_HEADS * HEAD_W) // (2 * CONF_CH)
    halo_rows = CONF_HALO_ROWS
    vec = pl.BlockSpec((1, CONF_CH), lambda i: (0, 0))
    return pl.pallas_call(
        _conf_body,
        grid=(t // rows,),
        in_specs=[pl.BlockSpec((rows, 2 * CONF_CH), lambda i: (i, glu_blk)),
                  pl.BlockSpec((halo_rows, 2 * CONF_CH),
                               lambda i: (jnp.maximum(i * (rows // halo_rows) - 1, 0), glu_blk)),
                  pl.BlockSpec((CONF_K, CONF_CH), lambda i: (0, 0)), vec, vec, vec],
        out_specs=pl.BlockSpec((rows, CONF_CH), lambda i: (i, 0)),
        out_shape=jax.ShapeDtypeStruct((t, CONF_CH), BF16),
        compiler_params=_cparams(("arbitrary",)),
        name="conformer_conv",
    )(proj, proj, dw, dw_b, ln_g, ln_b)


def _out0_body(a_ref, b_ref, wa_ref, wb_ref, x_ref, gate_ref, o_ref):
    mix = _dot(a_ref[...], wa_ref[...].astype(BF16)) + _dot(b_ref[...], wb_ref[...].astype(BF16))
    o_ref[...] = x_ref[...] + gate_ref[...] * mix


def _out_proj0(a, b, w, x, gate, *, tm=1024, tn=1024):
    t, d = x.shape
    tm = min(tm, t)
    ka, kb = a.shape[1], b.shape[1]
    return pl.pallas_call(
        _out0_body,
        grid=(t // tm, d // tn),
        in_specs=[pl.BlockSpec((tm, ka), lambda i, j: (i, 0)),
                  pl.BlockSpec((tm, kb), lambda i, j: (i, 0)),
                  pl.BlockSpec((ka, tn), lambda i, j: (0, j)),
                  pl.BlockSpec((kb, tn), lambda i, j: (ka // kb, j)),
                  pl.BlockSpec((tm, tn), lambda i, j: (i, j)),
                  pl.BlockSpec((1, tn), lambda i, j: (0, j))],
        out_specs=pl.BlockSpec((tm, tn), lambda i, j: (i, j)),
        out_shape=jax.ShapeDtypeStruct((t, d), F32),
        compiler_params=_cparams(("arbitrary", "arbitrary")),
        name="out_proj_even",
    )(a, b, w, w, x, gate)


def _out1_body(y_ref, g_ref, w_ref, x_ref, gate_ref, o_ref, h_ref):
    @pl.when(pl.program_id(1) == 0)
    def _():
        def norm_rows(c, carry):
            rs = pl.ds(pl.multiple_of(c * PROLOGUE_ROWS, PROLOGUE_ROWS), PROLOGUE_ROWS)
            y = y_ref[rs, :].astype(F32)
            r = lax.rsqrt(jnp.mean(y * y, axis=-1, keepdims=True) + EPS)
            h_ref[rs, :] = (y * r * g_ref[...]).astype(BF16)
            return carry

        lax.fori_loop(0, y_ref.shape[0] // PROLOGUE_ROWS, norm_rows, 0)

    o_ref[...] = x_ref[...] + gate_ref[...] * _dot(h_ref[...], w_ref[...].astype(BF16))


def _out_proj1(y, norm_g, w, x, gate, *, tm=1024, tn=512):
    t, d = x.shape
    tm = min(tm, t)
    k = y.shape[1]
    return pl.pallas_call(
        _out1_body,
        grid=(t // tm, d // tn),
        in_specs=[pl.BlockSpec((tm, k), lambda i, j: (i, 0)),
                  pl.BlockSpec((1, k), lambda i, j: (0, 0)),
                  pl.BlockSpec((k, tn), lambda i, j: (0, j)),
                  pl.BlockSpec((tm, tn), lambda i, j: (i, j)),
                  pl.BlockSpec((1, tn), lambda i, j: (0, j))],
        out_specs=pl.BlockSpec((tm, tn), lambda i, j: (i, j)),
        out_shape=jax.ShapeDtypeStruct((t, d), F32),
        scratch_shapes=[pltpu.VMEM((tm, k), BF16)],
        compiler_params=_cparams(("arbitrary", "arbitrary")),
        name="out_proj_odd",
    )(y, norm_g, w, x, gate)


def _ssd_body(x_ref, b_ref, c_ref, xh_ref, bh_ref, ch_ref, z_ref, dt_ref, cwx_ref, cwb_ref, cwc_ref,
              cbx_ref, cbb_ref, cbc_ref, par_ref, o_ref, s_ref, at_ref, *, rows, gps):
    ir = pl.program_id(1)
    gw = SSM_GROUP_W
    hpg = gw // SSM_HEADDIM

    @pl.when(ir == 0)
    def _():
        s_ref[...] = jnp.zeros_like(s_ref)

    def conv_silu(x_r, halo_r, w_r, bias_r):
        halo = jnp.where(ir > 0, halo_r[...].astype(F32), 0.0)
        return _silu(_causal_conv(halo, x_r[...].astype(F32), w_r[...], 4) + bias_r[...])

    xs_all = conv_silu(x_ref, xh_ref, cwx_ref, cbx_ref)
    bm_all = conv_silu(b_ref, bh_ref, cwb_ref, cbb_ref)
    cm_all = conv_silu(c_ref, ch_ref, cwc_ref, cbc_ref)

    dt = _softplus(dt_ref[...] + par_ref[0:1, :])
    da = dt * (-jnp.exp(par_ref[1:2, :]))
    rr = _iota((rows, rows), 0)
    cc = _iota((rows, rows), 1)
    chunk_tril = ((rr >> CHUNK_LOG2) == (cc >> CHUNK_LOG2)) & (cc <= rr)
    acs = _select_dot(chunk_tril.astype(BF16), da, left=True)
    at_ref[...] = acs.T

    causal = (_iota((CHUNK, gw), 1) & (CHUNK - 1)) <= _iota((CHUNK, gw), 0)
    same_head = (_iota((LANES, LANES), 0) >> CHUNK_LOG2) == (_iota((LANES, LANES), 1) >> CHUNK_LOG2)
    bf = lambda a: a.astype(BF16)
    nc = rows // CHUNK
    chunks = [slice(c * CHUNK, (c + 1) * CHUNK) for c in range(nc)]
    tiles = [slice(k * LANES, (k + 1) * LANES) for k in range(gw // LANES)]

    for gi in range(gps):
        grp = pl.program_id(0) * gps + gi
        cols = slice(gi * gw, (gi + 1) * gw)
        xs = xs_all[:, cols]
        bm = bm_all[:, gi * SSM_STATE:(gi + 1) * SSM_STATE]
        cm = cm_all[:, gi * SSM_STATE:(gi + 1) * SSM_STATE]
        acs_rows = at_ref[pl.ds(pl.multiple_of(grp * hpg, hpg), hpg), :]

        sel = (_iota((LANES, gw), 0) == grp * hpg + (_iota((LANES, gw), 1) >> CHUNK_LOG2)).astype(BF16)
        dt_e = _select_dot(sel, dt)
        acs_e = _select_dot(sel, acs)
        dskip_e = _select_dot(sel, par_ref[...])[2:3, :]

        a_es = [acs_e[rs] for rs in chunks]
        a_rows = [jnp.concatenate([acs_rows[h:h + 1, rs] for h in range(hpg)], axis=1) for rs in chunks]
        l_cats = [jnp.where(causal, jnp.exp(jnp.minimum(a_e - a_row, 0.0)), 0.0) for a_e, a_row in zip(a_es, a_rows)]
        c16 = [bf(cm[rs]) for rs in chunks]
        b16 = [bf(bm[rs]) for rs in chunks]
        cbs = [_dot_nt(c_c, jnp.concatenate([b_c] * hpg, axis=0)) for c_c, b_c in zip(c16, b16)]
        xdts = [xs[rs] * dt_e[rs] for rs in chunks]
        m16 = [bf(cb * l_cat) for cb, l_cat in zip(cbs, l_cats)]
        y_diags = []
        for m, xdt in zip(m16, xdts):
            parts = [_dot(m[:, ts], bf(jnp.where(same_head, jnp.concatenate([xdt[:, ts]] * 2, axis=0), 0.0)))
                     for ts in tiles]
            y_diags.append(jnp.concatenate(parts, axis=1))
        a_lasts = [a_e[CHUNK - 1:CHUNK] for a_e in a_es]
        updates = [_dot_tn(b_c, bf(xdt * jnp.exp(a_last - a_e)))
                   for b_c, xdt, a_last, a_e in zip(b16, xdts, a_lasts, a_es)]

        states = [s_ref[gi]]
        for a_last, upd in zip(a_lasts, updates):
            states.append(states[-1] * jnp.exp(a_last) + upd)
        s_ref[gi] = states[-1]

        y_offs = [_dot(c_c, bf(s)) * jnp.exp(a_e) for c_c, s, a_e in zip(c16, states[:-1], a_es)]
        for rs, y_diag, y_off in zip(chunks, y_diags, y_offs):
            y = y_diag + y_off + dskip_e * xs[rs]
            o_ref[rs, cols] = (y * _silu(z_ref[rs, cols].astype(F32))).astype(o_ref.dtype)


def _ssd(proj, dt_raw, conv_w, conv_b, par, *, rows=256, gps=8):
    t = proj.shape[0]
    rows = min(rows, t)
    xw = gps * SSM_GROUP_W
    bw = gps * SSM_STATE
    d_inner = SSM_GROUPS * SSM_GROUP_W
    xo = d_inner // xw
    bo = (2 * d_inner) // bw
    co = bo + SSM_GROUPS * SSM_STATE // bw
    hrow = lambda i: jnp.maximum(i * (rows // HALO_ROWS) - 1, 0)
    wbo = d_inner // bw
    wco = wbo + SSM_GROUPS * SSM_STATE // bw
    return pl.pallas_call(
        functools.partial(_ssd_body, rows=rows, gps=gps),
        grid=(SSM_GROUPS // gps, t // rows),
        in_specs=[pl.BlockSpec((rows, xw), lambda g, i: (i, xo + g)),
                  pl.BlockSpec((rows, bw), lambda g, i: (i, bo + g)),
                  pl.BlockSpec((rows, bw), lambda g, i: (i, co + g)),
                  pl.BlockSpec((HALO_ROWS, xw), lambda g, i: (hrow(i), xo + g)),
                  pl.BlockSpec((HALO_ROWS, bw), lambda g, i: (hrow(i), bo + g)),
                  pl.BlockSpec((HALO_ROWS, bw), lambda g, i: (hrow(i), co + g)),
                  pl.BlockSpec((rows, xw), lambda g, i: (i, g)),
                  pl.BlockSpec((rows, LANES), lambda g, i: (i, 0)),
                  pl.BlockSpec((4, xw), lambda g, i: (0, g)),
                  pl.BlockSpec((4, bw), lambda g, i: (0, wbo + g)),
                  pl.BlockSpec((4, bw), lambda g, i: (0, wco + g)),
                  pl.BlockSpec((1, xw), lambda g, i: (0, g)),
                  pl.BlockSpec((1, bw), lambda g, i: (0, wbo + g)),
                  pl.BlockSpec((1, bw), lambda g, i: (0, wco + g)),
                  pl.BlockSpec((8, LANES), lambda g, i: (0, 0))],
        out_specs=pl.BlockSpec((rows, xw), lambda g, i: (i, g)),
        out_shape=jax.ShapeDtypeStruct((t, d_inner), BF16),
        scratch_shapes=[pltpu.VMEM((gps, SSM_STATE, SSM_GROUP_W), F32), pltpu.VMEM((LANES, rows), F32)],
        compiler_params=_cparams(("arbitrary", "arbitrary")),
        name="ssd",
    )(proj, proj, proj, proj, proj, proj, proj, dt_raw, conv_w, conv_w, conv_w, conv_b, conv_b, conv_b, par)


def _store_token_tiles(tile_ref, x):
    rows, d = x.shape
    bits = lax.bitcast_convert_type(x.astype(BF16).astype(F32), U32)
    packed = (bits[:, :d // 2] >> 16) | (bits[:, d // 2:] & jnp.uint32(0xFFFF0000))
    for s in range(TOKEN_TILE_ROWS):
        tile_ref[pl.ds(s, rows, stride=TOKEN_TILE_ROWS), :] = packed[:, s * LANES:(s + 1) * LANES]


def _load_token_tiles(tile_ref, rows):
    lo, hi = [], []
    for s in range(TOKEN_TILE_ROWS):
        p = tile_ref[pl.ds(s, rows, stride=TOKEN_TILE_ROWS), :]
        lo.append(lax.bitcast_convert_type(p << 16, F32))
        hi.append(lax.bitcast_convert_type(p & jnp.uint32(0xFFFF0000), F32))
    return lo, hi


def _lane_pick(lane, idx, vals):
    return jnp.sum(jnp.where(lane == idx, vals, 0.0), axis=-1, keepdims=True)


def _router_body(x_ref, g_ref, sc_ref, sh_ref, w_ref, b_ref, hp_ref, route_ref, cnt_ref, carry_ref, *, tm):
    i = pl.program_id(0)

    @pl.when(i == 0)
    def _():
        carry_ref[...] = jnp.zeros_like(carry_ref)

    h = _adaln_rows(x_ref[...], g_ref[...], sc_ref[...], sh_ref[...])
    logits = _dot3(h, w_ref[...]) + b_ref[...]
    lane = _iota((tm, LANES), 1).astype(F32)
    neg = -jnp.inf
    gl = jnp.where(lane < N_GROUPS, logits, neg)
    gmax = jnp.max(gl, axis=-1, keepdims=True)
    grp_p = 1.0 / jnp.sum(jnp.exp(gl - gmax), axis=-1, keepdims=True)
    gidx = jnp.min(jnp.where(gl == gmax, lane, float(LANES)), axis=-1, keepdims=True)
    lo = N_GROUPS + EXPERTS_PER_GROUP * gidx
    el = jnp.where((lane >= lo) & (lane < lo + EXPERTS_PER_GROUP), logits, neg)
    m1 = jnp.max(el, axis=-1, keepdims=True)
    i1 = jnp.min(jnp.where(el == m1, lane, float(LANES)), axis=-1, keepdims=True)
    el2 = jnp.where(lane == i1, neg, el)
    m2 = jnp.max(el2, axis=-1, keepdims=True)
    i2 = jnp.min(jnp.where(el2 == m2, lane, float(LANES)), axis=-1, keepdims=True)
    tt = jnp.exp(m2 - m1)
    g1 = grp_p / (1.0 + tt)
    g2 = g1 * tt
    e1 = i1 - N_GROUPS
    e2 = i2 - N_GROUPS

    onehot = ((lane == e1) | (lane == e2)).astype(F32)
    strict = (_iota((tm, tm), 1) < _iota((tm, tm), 0)).astype(BF16)
    rank = _dot(strict, onehot.astype(BF16)) + carry_ref[0:1, :]
    r1 = _lane_pick(lane, e1, rank)
    r2 = _lane_pick(lane, e2, rank)
    carry_ref[...] = carry_ref[...] + jnp.sum(onehot, axis=0, keepdims=True)
    cnt_ref[...] = carry_ref[...]

    route = jnp.where(lane == 0, e1, jnp.where(lane == 1, e2, jnp.where(lane == 2, r1, jnp.where(
        lane == 3, r2, jnp.where(lane == 4, g1, jnp.where(lane == 5, g2, 0.0))))))
    route_ref[...] = route

    _store_token_tiles(hp_ref, h)


def _router(x, g, scale, shift, w_route, b_route, *, tm=512):
    t, d = x.shape
    vec = pl.BlockSpec((1, d), lambda i: (0, 0))
    return pl.pallas_call(
        functools.partial(_router_body, tm=tm),
        grid=(t // tm,),
        in_specs=[pl.BlockSpec((tm, d), lambda i: (i, 0)), vec, vec, vec,
                  pl.BlockSpec((d, LANES), lambda i: (0, 0)),
                  pl.BlockSpec((1, LANES), lambda i: (0, 0))],
        out_specs=[pl.BlockSpec((tm * TOKEN_TILE_ROWS, LANES), lambda i: (i, 0)),
                   pl.BlockSpec((tm, LANES), lambda i: (i, 0)),
                   pl.BlockSpec((8, LANES), lambda i: (0, 0))],
        out_shape=[jax.ShapeDtypeStruct((t * TOKEN_TILE_ROWS, LANES), U32),
                   jax.ShapeDtypeStruct((t, LANES), F32),
                   jax.ShapeDtypeStruct((8, LANES), F32)],
        scratch_shapes=[pltpu.VMEM((8, LANES), F32)],
        compiler_params=_cparams(("arbitrary",)),
        name="moe_router",
    )(x, g, scale, shift, w_route, b_route)


def _dest_body(route_ref, cnt_ref, dest_ref, blk_ref, *, tm, n_blocks_pad):
    cnt = cnt_ref[...]
    padded = jnp.floor((cnt + (MOE_ROWS - 1)) * (1.0 / MOE_ROWS)) * MOE_ROWS
    upper = (_iota((LANES, LANES), 0) <= _iota((LANES, LANES), 1)).astype(F32)
    pad_end = _dot(padded, upper, _HI)
    pad_start = (pad_end - padded)[0:1, :]
    route = route_ref[...]
    lane = _iota((tm, LANES), 1).astype(F32)
    e1, e2, r1, r2 = route[:, 0:1], route[:, 1:2], route[:, 2:3], route[:, 3:4]
    d1 = _lane_pick(lane, e1, pad_start) + r1
    d2 = _lane_pick(lane, e2, pad_start) + r2
    dest_ref[...] = jnp.where(lane == 0, d1, jnp.where(lane == 1, d2, 0.0)).astype(I32)

    lane_b = _iota((n_blocks_pad, LANES), 1)
    first_row = (_iota((n_blocks_pad, LANES), 0) * MOE_ROWS).astype(F32)
    done = ((pad_end[0:1, :] <= first_row) & (lane_b < N_EXPERTS)).astype(F32)
    blk_e = jnp.minimum(jnp.sum(done, axis=-1, keepdims=True), float(N_EXPERTS - 1))
    n_valid = pad_end[0:1, N_EXPERTS - 1:N_EXPERTS] * (1.0 / MOE_ROWS)
    on_diag = lane_b == _iota((n_blocks_pad, LANES), 0)
    seg_fill = jnp.sum(jnp.where(on_diag, pad_start + cnt[0:1, :], 0.0), axis=-1, keepdims=True)
    seg_end = jnp.sum(jnp.where(on_diag, pad_end[0:1, :], 0.0), axis=-1, keepdims=True)
    blk_ref[...] = jnp.where(lane_b == 0, blk_e, jnp.where(lane_b == 1, n_valid, jnp.where(
        lane_b == 2, seg_fill, jnp.where(lane_b == 3, seg_end, 0.0)))).astype(I32)


def _dest(route, counts, n_blocks_pad, *, tm=512):
    t = route.shape[0]
    assert t % tm == 0
    return pl.pallas_call(
        functools.partial(_dest_body, tm=tm, n_blocks_pad=n_blocks_pad),
        grid=(t // tm,),
        in_specs=[pl.BlockSpec((tm, LANES), lambda i: (i, 0)),
                  pl.BlockSpec((8, LANES), lambda i: (0, 0))],
        out_specs=[pl.BlockSpec((tm, LANES), lambda i: (i, 0)),
                   pl.BlockSpec((n_blocks_pad, LANES), lambda i: (0, 0))],
        out_shape=[jax.ShapeDtypeStruct((t, LANES), I32),
                   jax.ShapeDtypeStruct((n_blocks_pad, LANES), I32)],
        compiler_params=_cparams(("arbitrary",)),
        name="moe_dest",
    )(route, counts)


def _invert_body(dest_ref, fill_ref, end_ref, tok_ref, *, n_assign, n_rows):
    def fill8(g, c):
        for r in range(8):
            tok_ref[g * 8 + r] = 0
        return c

    def fill_expert(e, c):
        return lax.fori_loop(fill_ref[e] >> 3, end_ref[e] >> 3, fill8, c)

    lax.fori_loop(0, N_EXPERTS, fill_expert, 0)
    lax.fori_loop(end_ref[N_EXPERTS - 1] >> 3, n_rows // 8, fill8, 0)

    def scatter(a, c):
        tok_ref[dest_ref[a]] = a >> 1
        return c

    lax.fori_loop(0, n_assign, scatter, 0, unroll=8)


def _invert(dest_flat, seg_fill, seg_end, n_rows):
    n_assign = dest_flat.shape[0]
    return pl.pallas_call(
        functools.partial(_invert_body, n_assign=n_assign, n_rows=n_rows),
        grid_spec=pltpu.PrefetchScalarGridSpec(
            num_scalar_prefetch=3,
            grid=(1,),
            in_specs=[],
            out_specs=pl.BlockSpec(memory_space=pltpu.SMEM),
        ),
        out_shape=jax.ShapeDtypeStruct((n_rows,), I32),
        compiler_params=_cparams(("arbitrary",)),
        name="moe_invert",
    )(dest_flat, seg_fill, seg_end)


def _expert_body(be_ref, nv_ref, tok_ref, hp_ref, w1_ref, w3_ref, w2_ref, y_ref, xbuf_ref, sem, wst1_ref, wst3_ref,
                 wst2_ref, wsem, w1b_ref, w3b_ref, w2b_ref, *, layer):
    b = pl.program_id(0)
    n_blocks = pl.num_programs(0)
    wst_refs = (wst1_ref, wst3_ref, wst2_ref)
    n_valid = nv_ref[0]
    valid = b < n_valid
    tr = TOKEN_TILE_ROWS
    expert = be_ref[b]
    first_of_expert = valid & ((b == 0) | (expert != be_ref[jnp.maximum(b - 1, 0)]))

    def weight_copies(e):
        return [pltpu.make_async_copy(w_ref.at[layer, e], st_ref, wsem.at[i])
                for i, (w_ref, st_ref) in enumerate(zip((w1_ref, w3_ref, w2_ref), wst_refs))]

    @pl.when((b == 0) & valid)
    def _():
        for c in weight_copies(expert):
            c.start()

    def gather_rows(blk, slot):
        def one(j, c):
            tok = tok_ref[blk * MOE_ROWS + j]
            pltpu.make_async_copy(hp_ref.at[pl.ds(tok * tr, tr)], xbuf_ref.at[slot, pl.ds(j * tr, tr)],
                                  sem.at[slot]).start()
            return c

        lax.fori_loop(0, MOE_ROWS, one, 0, unroll=8)

    @pl.when(b == 0)
    def _():
        for ahead in range(GATHER_AHEAD):
            @pl.when(ahead < n_valid)
            def _(ahead=ahead):
                gather_rows(ahead, ahead)

    @pl.when(b + GATHER_AHEAD < n_valid)
    def _():
        gather_rows(b + GATHER_AHEAD, (b + GATHER_AHEAD) % (GATHER_AHEAD + 1))

    @pl.when(first_of_expert)
    def _():
        for c in weight_copies(expert):
            c.wait()
        for wb_ref, st_ref in zip((w1b_ref, w3b_ref, w2b_ref), wst_refs):
            wb_ref[...] = st_ref[...].astype(BF16)
        nxt = lax.while_loop(lambda j: (j < n_valid) & (be_ref[jnp.minimum(j, n_blocks - 1)] == expert),
                             lambda j: j + 1, b + 1)

        @pl.when(nxt < n_valid)
        def _():
            for c in weight_copies(be_ref[jnp.minimum(nxt, n_blocks - 1)]):
                c.start()

    @pl.when(jnp.logical_not(valid))
    def _():
        y_ref[...] = jnp.zeros_like(y_ref)

    @pl.when(valid)
    def _():
        slot = b % (GATHER_AHEAD + 1)
        pltpu.make_async_copy(hp_ref.at[pl.ds(0, MOE_ROWS * tr)], xbuf_ref.at[slot], sem.at[slot]).wait()
        lo, hi = _load_token_tiles(xbuf_ref.at[slot], MOE_ROWS)
        x = jnp.concatenate(lo + hi, axis=1).astype(BF16)
        a = (_silu(_dot(x, w1b_ref[...])) * _dot(x, w3b_ref[...])).astype(BF16)
        _store_token_tiles(y_ref, _dot(a, w2b_ref[...]))


def _experts(block_expert, n_valid, tok_rows, h_packed, w1, w3, w2, *, layer):
    n_blocks = block_expert.shape[0]
    _, _, d, f = w1.shape
    anywhere = pl.BlockSpec(memory_space=pl.ANY)
    return pl.pallas_call(
        functools.partial(_expert_body, layer=layer),
        grid_spec=pltpu.PrefetchScalarGridSpec(
            num_scalar_prefetch=3,
            grid=(n_blocks,),
            in_specs=[anywhere, anywhere, anywhere, anywhere],
            out_specs=pl.BlockSpec((MOE_ROWS * TOKEN_TILE_ROWS, LANES), lambda b, be, nv, tk: (b, 0)),
            scratch_shapes=[pltpu.VMEM((GATHER_AHEAD + 1, MOE_ROWS * TOKEN_TILE_ROWS, LANES), U32),
                            pltpu.SemaphoreType.DMA((GATHER_AHEAD + 1,)),
                            pltpu.VMEM((d, f), F32), pltpu.VMEM((d, f), F32), pltpu.VMEM((f, d), F32),
                            pltpu.SemaphoreType.DMA((3,)),
                            pltpu.VMEM((d, f), BF16), pltpu.VMEM((d, f), BF16), pltpu.VMEM((f, d), BF16)],
        ),
        out_shape=jax.ShapeDtypeStruct((n_blocks * MOE_ROWS * TOKEN_TILE_ROWS, LANES), U32),
        compiler_params=_cparams(("arbitrary",)),
        name="moe_experts",
    )(block_expert, n_valid, tok_rows, h_packed, w1, w3, w2)


def _combine_body(dest_ref, y_ref, x_ref, route_ref, gate_ref, fg_ref, o_ref, buf_ref, sem, *, tm, final_norm):
    i = pl.program_id(0)
    yr = TOKEN_TILE_ROWS

    def gather_rows(blk, slot):
        def one(j, c):
            for k in range(2):
                d = dest_ref[(blk * tm + j) * 2 + k]
                pltpu.make_async_copy(y_ref.at[pl.ds(d * yr, yr)], buf_ref.at[slot, k, pl.ds(j * yr, yr)],
                                      sem.at[slot, k]).start()
            return c

        lax.fori_loop(0, tm, one, 0, unroll=4)

    n_steps = pl.num_programs(0)

    @pl.when(i == 0)
    def _():
        for ahead in range(COMBINE_AHEAD):
            @pl.when(ahead < n_steps)
            def _(ahead=ahead):
                gather_rows(ahead, ahead)

    @pl.when(i + COMBINE_AHEAD < n_steps)
    def _():
        gather_rows(i + COMBINE_AHEAD, (i + COMBINE_AHEAD) % (COMBINE_AHEAD + 1))

    slot = i % (COMBINE_AHEAD + 1)
    for k in range(2):
        pltpu.make_async_copy(y_ref.at[pl.ds(0, tm * yr)], buf_ref.at[slot, k], sem.at[slot, k]).wait()

    route = route_ref[...]
    g1, g2 = route[:, 4:5], route[:, 5:6]
    lo1, hi1 = _load_token_tiles(buf_ref.at[slot, 0], tm)
    lo2, hi2 = _load_token_tiles(buf_ref.at[slot, 1], tm)
    half = x_ref.shape[1] // 2
    for s in range(yr):
        for base, y1, y2 in ((0, lo1[s], lo2[s]), (half, hi1[s], hi2[s])):
            cs = slice(base + s * LANES, base + (s + 1) * LANES)
            o_ref[:, cs] = x_ref[:, cs] + gate_ref[:, cs] * (g1 * y1 + g2 * y2)
    if final_norm:
        xo = o_ref[...]
        o_ref[...] = xo * lax.rsqrt(jnp.mean(xo * xo, axis=-1, keepdims=True) + EPS) * fg_ref[...]


def _combine(dest_flat, y_rows, x, route, gate, final_g, *, final_norm, tm=256):
    t, d = x.shape
    return pl.pallas_call(
        functools.partial(_combine_body, tm=tm, final_norm=final_norm),
        grid_spec=pltpu.PrefetchScalarGridSpec(
            num_scalar_prefetch=1,
            grid=(t // tm,),
            in_specs=[pl.BlockSpec(memory_space=pl.ANY),
                      pl.BlockSpec((tm, d), lambda i, dr: (i, 0)),
                      pl.BlockSpec((tm, LANES), lambda i, dr: (i, 0)),
                      pl.BlockSpec((1, d), lambda i, dr: (0, 0)),
                      pl.BlockSpec((1, d), lambda i, dr: (0, 0))],
            out_specs=pl.BlockSpec((tm, d), lambda i, dr: (i, 0)),
            scratch_shapes=[pltpu.VMEM((COMBINE_AHEAD + 1, 2, tm * TOKEN_TILE_ROWS, LANES), U32),
                            pltpu.SemaphoreType.DMA((COMBINE_AHEAD + 1, 2))],
        ),
        out_shape=jax.ShapeDtypeStruct((t, d), F32),
        compiler_params=_cparams(("arbitrary",)),
        name="moe_combine",
    )(dest_flat, y_rows, x, route, gate, final_g)


def _moe(x, g, scale, shift, gate, w_group, b_group, w_expert, b_expert, w1, w3, w2, final_g, *, layer, final_norm):
    t, d = x.shape
    pad = LANES - N_GROUPS - N_EXPERTS
    w_route = jnp.concatenate([w_group, w_expert, jnp.zeros((d, pad), F32)], axis=1)
    b_route = jnp.concatenate([b_group, b_expert, jnp.zeros((pad,), F32)]).reshape(1, LANES)
    h_packed, route, counts = _router(x, g, scale, shift, w_route, b_route)
    n_assign = 2 * t
    n_blocks = -(-(n_assign + N_EXPERTS * (MOE_ROWS - 1)) // MOE_ROWS)
    n_blocks_pad = -(-n_blocks // 8) * 8
    dest, blk = _dest(route, counts, n_blocks_pad)
    dest_flat = dest[:, :2].reshape(n_assign)
    block_expert = blk[:n_blocks, 0]
    n_valid = blk[0, 1:2]
    tok_rows = _invert(dest_flat, blk[:N_EXPERTS, 2], blk[:N_EXPERTS, 3], n_blocks * MOE_ROWS)
    y_rows = _experts(block_expert, n_valid, tok_rows, h_packed, w1, w3, w2, layer=layer)
    return _combine(dest_flat, y_rows, x, route, gate, final_g, final_norm=final_norm)


def _pad_rows(w, rows):
    return jnp.pad(w, ((0, rows - w.shape[0]), (0, 0)))


def _lane_row(v, offset):
    return jnp.zeros((LANES,), F32).at[offset:offset + v.shape[0]].set(v)


def kernel(x, c, w_mod, b_mod, norm_g, final_norm_g, e_w_in, e_conv_qkv, e_a_log, e_dt_bias, e_head_norm_g, e_conf_dw, e_conf_dw_b, e_conf_ln_g, e_conf_ln_b, e_w_out, o_w_in, o_conv_w, o_conv_b, o_dt_bias, o_a_log, o_d_skip, o_norm_g, o_w_out, moe_w_group, moe_b_group, moe_w_expert, moe_b_expert, moe_w1, moe_w3, moe_w2):
    bsz, t, d = x.shape
    assert bsz == 1 and c.shape == (1, d)
    xt = x[0]
    mod = _modulation(c[0], w_mod, b_mod)

    def mod_parts(l):
        return mod[l, :, :d], mod[l, :, d:2 * d], mod[l, :, 2 * d:]

    row = lambda v: v.reshape(1, -1)
    qkv_w = 3 * GDN_HEADS * HEAD_W
    z_end = qkv_w + GDN_HEADS * HEAD_W

    shift, scale, gate = mod_parts(0)
    w_in = e_w_in[0].T
    tn = 1024
    w_glu = w_in[z_end + 2 * GDN_HEADS:]
    w_ba = _pad_rows(w_in[z_end:z_end + 2 * GDN_HEADS], LANES)
    proj0, ba = _norm_matmul(xt, row(norm_g[0, 0]), scale, shift,
                             [(w_in, z_end // tn), (w_glu, w_glu.shape[0] // tn)], w_ba, tn=tn)
    gdn_par = jnp.zeros((8, LANES), F32).at[0].set(_lane_row(e_a_log[0], GDN_HEADS)).at[1].set(
        _lane_row(e_dt_bias[0], GDN_HEADS))
    a_out = _gdn(proj0, ba, e_conv_qkv[0], gdn_par, row(e_head_norm_g[0]))
    b_out = _conformer(proj0, e_conf_dw[0], row(e_conf_dw_b[0]), row(e_conf_ln_g[0]), row(e_conf_ln_b[0]))
    xt = _out_proj0(a_out, b_out, e_w_out[0], xt, gate)
    shift, scale, gate = mod_parts(1)
    xt = _moe(xt, row(norm_g[0, 1]), scale, shift, gate, moe_w_group[0], moe_b_group[0], moe_w_expert[0],
              moe_b_expert[0], moe_w1, moe_w3, moe_w2, row(final_norm_g), layer=0, final_norm=False)

    shift, scale, gate = mod_parts(2)
    w_in = o_w_in[0].T
    n_main = 2 * SSM_GROUPS * SSM_GROUP_W + 2 * SSM_GROUPS * SSM_STATE
    tn = 1024
    proj1, dt_raw = _norm_matmul(xt, row(norm_g[1, 0]), scale, shift, [(w_in, n_main // tn)],
                                 _pad_rows(w_in[n_main:], LANES), tn=tn)
    ssd_par = jnp.zeros((8, LANES), F32).at[0].set(_lane_row(o_dt_bias[0], 0)).at[1].set(
        _lane_row(o_a_log[0], 0)).at[2].set(_lane_row(o_d_skip[0], 0))
    y = _ssd(proj1, dt_raw, o_conv_w[0], row(o_conv_b[0]), ssd_par)
    xt = _out_proj1(y, row(o_norm_g[0]), o_w_out[0], xt, gate)
    shift, scale, gate = mod_parts(3)
    xt = _moe(xt, row(norm_g[1, 1]), scale, shift, gate, moe_w_group[1], moe_b_group[1], moe_w_expert[1],
              moe_b_expert[1], moe_w1, moe_w3, moe_w2, row(final_norm_g), layer=1, final_norm=True)
    return xt[None]
```
